```python
import math
import jax, jax.numpy as jnp
from jax import lax
import numpy as np

D_MODEL = 2048
BATCH = 4
SEQ = 2048
DEPTH = 1
DEC_BATCH = 128
DEC_SEQ = 8
PAST_LEN = 16384
PAGE_SIZE = 128

PLE_DIM = 256
S5_GROUP = 16
S5_WIDTH = D_MODEL // 2
S5_GROUPS = S5_WIDTH // S5_GROUP
S5_STATE = 64
DN_HEAD_DIM = 128
DN_WIDTH = D_MODEL // 2
DN_HEADS = DN_WIDTH // DN_HEAD_DIM
CONV_WIDTH = 4
CONV_CH = 3 * DN_WIDTH
CHUNK = 64
D_FF = ((-(-(8 * D_MODEL) // 3) + 255) // 256) * 256
OFF_U = S5_WIDTH
OFF_QKV = OFF_U + CONV_CH
OFF_Z = OFF_QKV + DN_WIDTH
OFF_BETA = OFF_Z + DN_HEADS
OFF_A = OFF_BETA + DN_HEADS
OFF_GA = OFF_A + D_MODEL
N_IN = OFF_GA + D_MODEL
EPS = 1e-6

kernel_name = "hybrid_s5_gdn_parallel_decode_step"


def rms_norm(x, g):
    xf = x.astype(jnp.float32)
    y = xf * lax.rsqrt(jnp.mean(xf * xf, axis=-1, keepdims=True) + EPS)
    return (y * g.astype(jnp.float32)).astype(x.dtype)


def l2_norm(x):
    return x * lax.rsqrt(jnp.sum(x * x, axis=-1, keepdims=True) + EPS)


def causal_conv_silu(xc, buf, w):
    L = xc.shape[1]
    xp = jnp.concatenate([buf.astype(xc.dtype), xc], axis=1)
    out = xp[:, 0:L] * w[0]
    for j in range(1, CONV_WIDTH):
        out = out + xp[:, j:j + L] * w[j]
    return jax.nn.silu(out), xp[:, L:]


def s5_mixer(u, h0_re, h0_im, a_re, a_im, b_re, b_im, c_re, c_im, d, log_dt):
    f32 = jnp.float32
    Bsz, L, _ = u.shape
    uf = u.astype(f32)
    ug = uf.reshape(Bsz, L, S5_GROUPS, S5_GROUP)
    dt = jnp.exp(log_dt.astype(f32))[:, None]
    ar = a_re.astype(f32)
    ai = a_im.astype(f32)
    mag = jnp.exp(ar * dt)
    lb_re = mag * jnp.cos(ai * dt)
    lb_im = mag * jnp.sin(ai * dt)
    nr = lb_re - 1.0
    ni = lb_im
    den = ar * ar + ai * ai
    cr = (nr * ar + ni * ai) / den
    ci = (ni * ar - nr * ai) / den
    br = b_re.astype(f32)
    bi = b_im.astype(f32)
    bb_re = cr[..., None] * br - ci[..., None] * bi
    bb_im = cr[..., None] * bi + ci[..., None] * br
    bu_re = jnp.einsum('blgp,gnp->blgn', ug, bb_re)
    bu_im = jnp.einsum('blgp,gnp->blgn', ug, bb_im)
    h0r = h0_re.astype(f32)
    h0i = h0_im.astype(f32)
    bu_re = bu_re.at[:, 0].add(lb_re * h0r - lb_im * h0i)
    bu_im = bu_im.at[:, 0].add(lb_re * h0i + lb_im * h0r)
    a_r = jnp.broadcast_to(lb_re, bu_re.shape)
    a_i = jnp.broadcast_to(lb_im, bu_im.shape)

    def combine(e1, e2):
        a1r, a1i, b1r, b1i = e1
        a2r, a2i, b2r, b2i = e2
        return (a2r * a1r - a2i * a1i,
                a2r * a1i + a2i * a1r,
                a2r * b1r - a2i * b1i + b2r,
                a2r * b1i + a2i * b1r + b2i)

    _, _, hr, hi = lax.associative_scan(combine, (a_r, a_i, bu_re, bu_im), axis=1)
    y = (jnp.einsum('blgn,gpn->blgp', hr, c_re.astype(f32))
         - jnp.einsum('blgn,gpn->blgp', hi, c_im.astype(f32)))
    y = y.reshape(Bsz, L, S5_WIDTH) + d.astype(f32) * uf
    return y, hr[:, -1], hi[:, -1]


def gated_delta_rule(q, k, v, g, beta, S0):
    Bsz, L, H, DK = q.shape
    DV = v.shape[-1]
    C = min(CHUNK, L)
    n = -(-L // C)
    pad = n * C - L
    if pad:
        p4 = ((0, 0), (0, pad), (0, 0), (0, 0))
        p3 = ((0, 0), (0, pad), (0, 0))
        q = jnp.pad(q, p4)
        k = jnp.pad(k, p4)
        v = jnp.pad(v, p4)
        g = jnp.pad(g, p3)
        beta = jnp.pad(beta, p3)

    def to_chunks(t):
        t = t.reshape((Bsz, n, C) + t.shape[2:])
        return jnp.moveaxis(t, (1, 3), (0, 2))

    qc = to_chunks(q)
    kc = to_chunks(k)
    vc = to_chunks(v)
    bc = to_chunks(beta)
    gc = jnp.cumsum(to_chunks(g), axis=-1)
    idx = jnp.arange(C)
    incl = idx[:, None] >= idx[None, :]
    strict = idx[:, None] > idx[None, :]
    decay = jnp.exp(jnp.where(incl, gc[..., :, None] - gc[..., None, :], -jnp.inf))
    kk = jnp.einsum('nbhid,nbhjd->nbhij', kc, kc)
    lower = jnp.where(strict, bc[..., :, None] * kk * decay, 0.0)
    eye = jnp.eye(C, dtype=jnp.float32)
    tmat = lax.linalg.triangular_solve(eye + lower, jnp.broadcast_to(eye, lower.shape),
                                       left_side=True, lower=True)
    u = tmat @ (vc * bc[..., None])
    w = tmat @ (kc * (bc * jnp.exp(gc))[..., None])
    qk = jnp.einsum('nbhid,nbhjd->nbhij', qc, kc) * decay

    def step(S, xs):
        q_i, k_i, u_i, w_i, qk_i, g_i = xs
        v_new = u_i - jnp.einsum('bhcd,bhde->bhce', w_i, S)
        o = (jnp.einsum('bhcd,bhde->bhce', q_i * jnp.exp(g_i)[..., None], S)
             + jnp.einsum('bhij,bhje->bhie', qk_i, v_new))
        g_last = g_i[..., -1:]
        S = (S * jnp.exp(g_last)[..., None]
             + jnp.einsum('bhcd,bhce->bhde', k_i * jnp.exp(g_last - g_i)[..., None], v_new))
        return S, o

    S, o = lax.scan(step, S0, (qc, kc, u, w, qk, gc))
    o = jnp.moveaxis(o, (0, 2), (1, 3)).reshape(Bsz, n * C, H, DV)[:, :L]
    return o, S


def deltanet_mixer(qkv, z, beta_l, a_l, conv_buf, S0, conv_w, a_log, dt_bias, onorm_w):
    f32 = jnp.float32
    qkv, conv_new = causal_conv_silu(qkv, conv_buf, conv_w)
    Bsz, L, _ = qkv.shape
    qkv = qkv.astype(f32)
    q = qkv[..., :DN_WIDTH].reshape(Bsz, L, DN_HEADS, DN_HEAD_DIM)
    k = qkv[..., DN_WIDTH:2 * DN_WIDTH].reshape(Bsz, L, DN_HEADS, DN_HEAD_DIM)
    v = qkv[..., 2 * DN_WIDTH:].reshape(Bsz, L, DN_HEADS, DN_HEAD_DIM)
    q = l2_norm(q) * (DN_HEAD_DIM ** -0.5)
    k = l2_norm(k)
    beta = jax.nn.sigmoid(beta_l.astype(f32))
    g = -jnp.exp(a_log.astype(f32)) * jax.nn.softplus(a_l.astype(f32) + dt_bias.astype(f32))
    o, S_new = gated_delta_rule(q, k, v, g, beta, S0.astype(f32))
    o = o * lax.rsqrt(jnp.mean(o * o, axis=-1, keepdims=True) + EPS) * onorm_w.astype(f32)
    o = o * jax.nn.silu(z.astype(f32).reshape(Bsz, L, DN_HEADS, DN_HEAD_DIM))
    return o.reshape(Bsz, L, DN_WIDTH), conv_new, S_new


def decoder_layer(x, p, conv_buf, S0, h_re, h_im, lw):
    dtype = x.dtype
    h = rms_norm(x, lw['g_mix'])
    proj = h @ lw['w_in']
    u = proj[..., :OFF_U]
    qkv = proj[..., OFF_U:OFF_QKV]
    z = proj[..., OFF_QKV:OFF_Z]
    beta_l = proj[..., OFF_Z:OFF_BETA]
    a_l = proj[..., OFF_BETA:OFF_A]
    gate_a = proj[..., OFF_A:OFF_GA]
    gate_b = proj[..., OFF_GA:]
    y_s5, hr_new, hi_new = s5_mixer(u, h_re, h_im, lw['s5_a_re'], lw['s5_a_im'], lw['s5_b_re'],
                                    lw['s5_b_im'], lw['s5_c_re'], lw['s5_c_im'], lw['s5_d'], lw['s5_log_dt'])
    y_s5 = jax.nn.gelu(y_s5).astype(dtype)
    y_s5 = y_s5 * jax.nn.sigmoid(y_s5 @ lw['w_glu'] + lw['b_glu'])
    y_dn, conv_new, S_new = deltanet_mixer(qkv, z, beta_l, a_l, conv_buf, S0, lw['conv_w'],
                                           lw['a_log'], lw['dt_bias'], lw['onorm_w'])
    y_dn = y_dn.astype(dtype)
    mix = jax.nn.sigmoid(gate_a) * (y_s5 @ lw['w_a']) + jax.nn.sigmoid(gate_b) * (y_dn @ lw['w_b'])
    x = x + mix @ lw['w_out']
    h2 = rms_norm(x, lw['g_ffn'])
    x = x + (jax.nn.silu(h2 @ lw['w_gate']) * (h2 @ lw['w_up'])) @ lw['w_down']
    h3 = rms_norm(x, lw['g_ple'])
    x = x + (p.astype(dtype) @ lw['w_ple']) * jax.nn.sigmoid(h3 @ lw['w_ple_gate'])
    return x, conv_new.astype(dtype), S_new.astype(dtype), hr_new.astype(dtype), hi_new.astype(dtype)


def setup_inputs(seed: int = 0) -> dict:
    key = jax.random.key(seed)
    ks = list(jax.random.split(key, 48))
    f32 = jnp.float32

    def nrm(shape, scale):
        return jax.random.normal(ks.pop(), shape, f32) * scale

    def unif(shape, lo, hi):
        return jax.random.uniform(ks.pop(), shape, f32, lo, hi)

    out = {}
    out['x_prompt'] = nrm((BATCH, SEQ, D_MODEL), 1.0)
    out['x_sample'] = nrm((DEC_BATCH, DEC_SEQ, D_MODEL), 1.0)
    out['state_conv'] = nrm((DEPTH, DEC_BATCH, CONV_WIDTH - 1, CONV_CH), 1.0)
    out['state_delta'] = nrm((DEPTH, DEC_BATCH, DN_HEADS, DN_HEAD_DIM, DN_HEAD_DIM), 0.1)
    out['state_s5_re'] = nrm((DEPTH, DEC_BATCH, S5_GROUPS, S5_STATE), 0.5)
    out['state_s5_im'] = nrm((DEPTH, DEC_BATCH, S5_GROUPS, S5_STATE), 0.5)
    out['p_prompt'] = nrm((DEPTH, BATCH, SEQ, PLE_DIM), 1.0)
    out['p_sample'] = nrm((DEPTH, DEC_BATCH, DEC_SEQ, PLE_DIM), 1.0)
    out['g_mix'] = 1.0 + nrm((DEPTH, D_MODEL), 0.02)
    out['w_in'] = nrm((DEPTH, D_MODEL, N_IN), D_MODEL ** -0.5)
    out['conv_w'] = nrm((DEPTH, CONV_WIDTH, CONV_CH), CONV_WIDTH ** -0.5)
    out['a_log'] = jnp.log(unif((DEPTH, DN_HEADS), 1.0, 16.0))
    dt = jnp.exp(unif((DEPTH, DN_HEADS), math.log(1e-3), math.log(1e-1)))
    out['dt_bias'] = dt + jnp.log(-jnp.expm1(-dt))
    out['onorm_w'] = 1.0 + nrm((DEPTH, DN_HEAD_DIM), 0.02)
    out['s5_a_re'] = -0.5 + nrm((DEPTH, S5_GROUPS, S5_STATE), 0.01)
    out['s5_a_im'] = jnp.pi * jnp.arange(S5_STATE, dtype=f32) + nrm((DEPTH, S5_GROUPS, S5_STATE), 0.01)
    out['s5_b_re'] = nrm((DEPTH, S5_GROUPS, S5_STATE, S5_GROUP), (2 * S5_GROUP) ** -0.5)
    out['s5_b_im'] = nrm((DEPTH, S5_GROUPS, S5_STATE, S5_GROUP), (2 * S5_GROUP) ** -0.5)
    out['s5_c_re'] = nrm((DEPTH, S5_GROUPS, S5_GROUP, S5_STATE), (2 * S5_STATE) ** -0.5)
    out['s5_c_im'] = nrm((DEPTH, S5_GROUPS, S5_GROUP, S5_STATE), (2 * S5_STATE) ** -0.5)
    out['s5_d'] = nrm((DEPTH, S5_WIDTH), 1.0)
    out['s5_log_dt'] = unif((DEPTH, S5_GROUPS), math.log(1e-3), math.log(1e-1))
    out['w_glu'] = nrm((DEPTH, S5_WIDTH, S5_WIDTH), S5_WIDTH ** -0.5)
    out['b_glu'] = nrm((DEPTH, S5_WIDTH), 0.01)
    out['w_a'] = nrm((DEPTH, S5_WIDTH, D_MODEL), S5_WIDTH ** -0.5)
    out['w_b'] = nrm((DEPTH, DN_WIDTH, D_MODEL), DN_WIDTH ** -0.5)
    out['w_out'] = nrm((DEPTH, D_MODEL, D_MODEL), D_MODEL ** -0.5)
    out['g_ffn'] = 1.0 + nrm((DEPTH, D_MODEL), 0.02)
    out['w_gate'] = nrm((DEPTH, D_MODEL, D_FF), D_MODEL ** -0.5)
    out['w_up'] = nrm((DEPTH, D_MODEL, D_FF), D_MODEL ** -0.5)
    out['w_down'] = nrm((DEPTH, D_FF, D_MODEL), D_FF ** -0.5)
    out['g_ple'] = 1.0 + nrm((DEPTH, D_MODEL), 0.02)
    out['w_ple'] = nrm((DEPTH, PLE_DIM, D_MODEL), PLE_DIM ** -0.5)
    out['w_ple_gate'] = nrm((DEPTH, D_MODEL, D_MODEL), D_MODEL ** -0.5)
    out['g_final'] = 1.0 + nrm((D_MODEL,), 0.02)
    return out


def reference(x_prompt, x_sample, state_conv, state_delta, state_s5_re, state_s5_im, p_prompt, p_sample,
              g_mix, w_in, conv_w, a_log, dt_bias, onorm_w, s5_a_re, s5_a_im, s5_b_re, s5_b_im,
              s5_c_re, s5_c_im, s5_d, s5_log_dt, w_glu, b_glu, w_a, w_b, w_out, g_ffn, w_gate, w_up,
              w_down, g_ple, w_ple, w_ple_gate, g_final):
    dt_x = x_prompt.dtype
    zc = jnp.zeros((BATCH, CONV_WIDTH - 1, CONV_CH), dt_x)
    zd = jnp.zeros((BATCH, DN_HEADS, DN_HEAD_DIM, DN_HEAD_DIM), dt_x)
    zs = jnp.zeros((BATCH, S5_GROUPS, S5_STATE), dt_x)
    yp = x_prompt
    ys = x_sample
    pc, pd, pr, pi_ = [], [], [], []
    sc, sd, sr, si = [], [], [], []
    for i in range(DEPTH):
        lw = dict(g_mix=g_mix[i], w_in=w_in[i], conv_w=conv_w[i], a_log=a_log[i], dt_bias=dt_bias[i],
                  onorm_w=onorm_w[i], s5_a_re=s5_a_re[i], s5_a_im=s5_a_im[i], s5_b_re=s5_b_re[i],
                  s5_b_im=s5_b_im[i], s5_c_re=s5_c_re[i], s5_c_im=s5_c_im[i], s5_d=s5_d[i],
                  s5_log_dt=s5_log_dt[i], w_glu=w_glu[i], b_glu=b_glu[i], w_a=w_a[i], w_b=w_b[i],
                  w_out=w_out[i], g_ffn=g_ffn[i], w_gate=w_gate[i], w_up=w_up[i], w_down=w_down[i],
                  g_ple=g_ple[i], w_ple=w_ple[i], w_ple_gate=w_ple_gate[i])
        yp, c1, d1, r1, m1 = decoder_layer(yp, p_prompt[i], zc, zd, zs, zs, lw)
        ys, c2, d2, r2, m2 = decoder_layer(ys, p_sample[i], state_conv[i], state_delta[i],
                                           state_s5_re[i], state_s5_im[i], lw)
        pc.append(c1); pd.append(d1); pr.append(r1); pi_.append(m1)
        sc.append(c2); sd.append(d2); sr.append(r2); si.append(m2)
    y_prompt = rms_norm(yp, g_final)
    y_sample = rms_norm(ys, g_final)
    return (y_prompt, y_sample,
            jnp.stack(pc), jnp.stack(pd), jnp.stack(pr), jnp.stack(pi_),
            jnp.stack(sc), jnp.stack(sd), jnp.stack(sr), jnp.stack(si))
```

```python
import functools
import math

import jax
import jax.numpy as jnp
from jax import lax
from jax.experimental import pallas as pl
from jax.experimental.pallas import tpu as pltpu

F32 = jnp.float32
BF16 = jnp.bfloat16

EPS = 1e-6
LANE = 128
SUBLANE = 8
VMEM_LIMIT_BYTES = 56 * 1024 * 1024

S5_GROUP = 16
S5_STATE = 64
S5_ROW = SUBLANE
S5_TILE_GROUPS = LANE // S5_GROUP
S5_TILE_STATE = S5_TILE_GROUPS * S5_STATE
DN_HEAD_DIM = 128
CONV_WIDTH = 4
DN_BLOCK = 64


def _cparams(*sem):
    return pltpu.CompilerParams(dimension_semantics=sem, vmem_limit_bytes=VMEM_LIMIT_BYTES)


def _mm(a, b):
    return jnp.dot(a.astype(BF16), b.astype(BF16), preferred_element_type=F32)


def _mm_nt(a, b):
    return lax.dot_general(a.astype(BF16), b.astype(BF16), (((1,), (1,)), ((), ())),
                           preferred_element_type=F32)


def _mm_tn(a, b):
    return lax.dot_general(a.astype(BF16), b.astype(BF16), (((0,), (0,)), ((), ())),
                           preferred_element_type=F32)


def _split3(x):
    hi = x.astype(BF16)
    r1 = x - hi.astype(F32)
    mid = r1.astype(BF16)
    lo = (r1 - mid.astype(F32)).astype(BF16)
    return hi, mid, lo


def _sigmoid(x):
    return 1.0 / (1.0 + jnp.exp(-x))


def _rms(x, g):
    ms = jnp.mean(x * x, axis=-1, keepdims=True)
    return x * lax.rsqrt(ms + EPS) * g


def _inproj_kernel(x_ref, g_ref, w_ref, wba_ref, out_ref, ba_ref, h_scr):
    @pl.when(pl.program_id(1) == 0)
    def _():
        h = _rms(x_ref[...], g_ref[...]).astype(BF16)
        h_scr[...] = h
        ba_ref[...] = jnp.dot(h, wba_ref[...], preferred_element_type=F32)

    out_ref[...] = jnp.dot(h_scr[...], w_ref[...], preferred_element_type=F32)


def _inproj(x, g, w_main, w_ba, tm, tn):
    t, d = x.shape
    n = w_main.shape[1]
    return pl.pallas_call(
        _inproj_kernel,
        grid=(t // tm, n // tn),
        in_specs=[
            pl.BlockSpec((tm, d), lambda i, j: (i, 0)),
            pl.BlockSpec((1, d), lambda i, j: (0, 0)),
            pl.BlockSpec((d, tn), lambda i, j: (0, j)),
            pl.BlockSpec((d, LANE), lambda i, j: (0, 0)),
        ],
        out_specs=[
            pl.BlockSpec((tm, tn), lambda i, j: (i, j)),
            pl.BlockSpec((tm, LANE), lambda i, j: (i, 0)),
        ],
        out_shape=[jax.ShapeDtypeStruct((t, n), F32), jax.ShapeDtypeStruct((t, LANE), F32)],
        scratch_shapes=[pltpu.VMEM((tm, d), BF16)],
        compiler_params=_cparams("parallel", "arbitrary"),
        name="inproj",
    )(x, g, w_main, w_ba)


def _s5_operators(a_re, a_im, b_re, b_im, c_re, c_im, log_dt):
    g, n = a_re.shape
    p = S5_GROUP
    nt = g // S5_TILE_GROUPS
    dt = jnp.exp(log_dt)[:, None]
    k = jnp.arange(S5_ROW + 1, dtype=F32)[:, None, None]
    magk = jnp.exp(a_re * dt * k)
    lr = magk * jnp.cos(a_im * dt * k)
    li = magk * jnp.sin(a_im * dt * k)
    nr = lr[1] - 1.0
    ni = li[1]
    den = a_re * a_re + a_im * a_im
    cr = (nr * a_re + ni * a_im) / den
    ci = (ni * a_re - nr * a_im) / den
    bb_re = cr[..., None] * b_re - ci[..., None] * b_im
    bb_im = cr[..., None] * b_im + ci[..., None] * b_re
    m_re = c_re[None] * lr[:, :, None, :] - c_im[None] * li[:, :, None, :]
    m_im = c_re[None] * li[:, :, None, :] + c_im[None] * lr[:, :, None, :]
    hp = lax.Precision.HIGHEST
    ktau = (jnp.einsum('kgon,gni->kgoi', m_re[:S5_ROW], bb_re, precision=hp)
            - jnp.einsum('kgon,gni->kgoi', m_im[:S5_ROW], bb_im, precision=hp))
    lrev_r = lr[S5_ROW - 1::-1][:S5_ROW]
    lrev_i = li[S5_ROW - 1::-1][:S5_ROW]
    e_re = lrev_r[..., None] * bb_re[None] - lrev_i[..., None] * bb_im[None]
    e_im = lrev_r[..., None] * bb_im[None] + lrev_i[..., None] * bb_re[None]
    we = jnp.stack([e_re, e_im], axis=0)
    we = we.transpose(2, 1, 4, 0, 3)
    wy = jnp.stack([m_re[1:], -m_im[1:]], axis=0)
    wy = wy.transpose(2, 0, 4, 1, 3)
    ii = jnp.arange(S5_ROW)
    lag = ii[None, :] - ii[:, None]
    kt = ktau[jnp.clip(lag, 0, S5_ROW - 1)]
    kt = jnp.where((lag >= 0)[:, :, None, None, None], kt, 0.0)
    wk = kt.transpose(2, 0, 4, 1, 3)

    eye = jnp.eye(S5_TILE_GROUPS, dtype=F32)
    we = we.reshape(nt, S5_TILE_GROUPS, S5_ROW, p, 2, n)
    we_t = jnp.einsum('tgjprn,gh->tjgprhn', we, eye).reshape(nt, S5_ROW * LANE, 2 * S5_TILE_STATE)
    wy = wy.reshape(nt, S5_TILE_GROUPS, 2, n, S5_ROW, p)
    wy_t = jnp.einsum('tgrnjp,gh->trgnjhp', wy, eye).reshape(nt, 2 * S5_TILE_STATE, S5_ROW * LANE)
    wk = wk.reshape(nt, S5_TILE_GROUPS, S5_ROW, p, S5_ROW, p)
    wk_t = jnp.einsum('tgiqjp,gh->tigqjhp', wk, eye).reshape(nt, S5_ROW * LANE, S5_ROW * LANE)
    lam = jnp.concatenate([lr[S5_ROW].reshape(nt, 1, S5_TILE_STATE),
                           li[S5_ROW].reshape(nt, 1, S5_TILE_STATE)], axis=-1)
    return we_t.astype(BF16), wy_t.astype(BF16), wk_t.astype(BF16), lam


def _s5_kernel(*refs, scan, rows):
    nu = S5_ROW
    u_refs = refs[:nu]
    we_ref, wy_ref, wk_ref, d_ref, lam_ref = refs[nu:nu + 5]
    pos = nu + 5
    if not scan:
        h0r_ref, h0i_ref = refs[pos:pos + 2]
        pos += 2
    y_ref, hre_ref, him_ref = refs[pos:pos + 3]
    scratch = refs[pos + 3:]

    ns = S5_TILE_STATE
    u = jnp.concatenate([r[...] for r in u_refs], axis=1)
    ub = u.astype(BF16)
    e = jnp.dot(ub, we_ref[...], preferred_element_type=F32)
    lam = lam_ref[...]
    lr = lam[:, :ns]
    li = lam[:, ns:]
    if scan:
        e_scr, h_scr = scratch
        e_scr[...] = e

        def body(c, carry):
            hr, hi = carry
            h_scr[pl.ds(c, 1), :] = jnp.concatenate([hr, hi], axis=1)
            ec = e_scr[pl.ds(c, 1), :]
            return (lr * hr - li * hi + ec[:, :ns], lr * hi + li * hr + ec[:, ns:])

        zero = jnp.zeros((1, ns), F32)
        hr, hi = lax.fori_loop(0, rows, body, (zero, zero))
        hre_ref[...] = hr
        him_ref[...] = hi
        hin = h_scr[...]
    else:
        h0r = h0r_ref[...]
        h0i = h0i_ref[...]
        hre_ref[...] = lr * h0r - li * h0i + e[:, :ns]
        him_ref[...] = lr * h0i + li * h0r + e[:, ns:]
        hin = jnp.concatenate([h0r, h0i], axis=1)
    y = (jnp.dot(hin.astype(BF16), wy_ref[...], preferred_element_type=F32)
         + jnp.dot(ub, wk_ref[...], preferred_element_type=F32))
    d = d_ref[...]
    for j in range(nu):
        y_ref[j] = y[:, j * LANE:(j + 1) * LANE] + d * u_refs[j][...]


def _s5_apply(proj, ncols, nseq, seqlen, ops, d, h0=None):
    we_t, wy_t, wk_t, lam = ops
    nt = we_t.shape[0]
    width = nt * LANE
    t = nseq * seqlen
    rows_total = t // S5_ROW
    pv = proj.reshape(rows_total, S5_ROW * ncols)
    cb = ncols // LANE
    scan = h0 is None
    if scan:
        rows = seqlen // S5_ROW
        grid = (nt, nseq)
        rmap = lambda tt, b: b
    else:
        assert seqlen == S5_ROW
        rows = nseq
        grid = (nt,)
        rmap = lambda tt: 0

    def spec(shape, fn):
        if scan:
            return pl.BlockSpec(shape, lambda tt, b: fn(tt, b))
        return pl.BlockSpec(shape, lambda tt: fn(tt, 0))

    in_specs = [spec((rows, LANE), functools.partial(lambda tt, b, j: (b if scan else 0, j * cb + tt), j=j))
                for j in range(S5_ROW)]
    in_specs += [
        spec((None, S5_ROW * LANE, 2 * S5_TILE_STATE), lambda tt, b: (tt, 0, 0)),
        spec((None, 2 * S5_TILE_STATE, S5_ROW * LANE), lambda tt, b: (tt, 0, 0)),
        spec((None, S5_ROW * LANE, S5_ROW * LANE), lambda tt, b: (tt, 0, 0)),
        spec((None, 1, LANE), lambda tt, b: (tt, 0, 0)),
        spec((None, 1, 2 * S5_TILE_STATE), lambda tt, b: (tt, 0, 0)),
    ]
    args = [pv] * S5_ROW + [we_t, wy_t, wk_t, d.reshape(nt, 1, LANE), lam]
    if not scan:
        h0r, h0i = h0
        in_specs += [spec((rows, S5_TILE_STATE), lambda tt, b: (0, tt))] * 2
        args += [h0r.reshape(nseq, nt * S5_TILE_STATE), h0i.reshape(nseq, nt * S5_TILE_STATE)]
    out_specs = [spec((S5_ROW, rows, LANE), lambda tt, b: (0, b if scan else 0, tt))]
    out_shape = [jax.ShapeDtypeStruct((S5_ROW, rows_total, width), F32)]
    if scan:
        out_specs += [spec((None, 1, S5_TILE_STATE), lambda tt, b: (b, 0, tt))] * 2
        out_shape += [jax.ShapeDtypeStruct((nseq, 1, nt * S5_TILE_STATE), F32)] * 2
        scratch = [pltpu.VMEM((rows, 2 * S5_TILE_STATE), F32)] * 2
        sem = ("parallel", "arbitrary")
    else:
        out_specs += [spec((rows, S5_TILE_STATE), lambda tt, b: (0, tt))] * 2
        out_shape += [jax.ShapeDtypeStruct((nseq, nt * S5_TILE_STATE), F32)] * 2
        scratch = []
        sem = ("arbitrary",)
    outs = pl.pallas_call(
        functools.partial(_s5_kernel, scan=scan, rows=rows),
        grid=grid,
        in_specs=in_specs,
        out_specs=out_specs,
        out_shape=out_shape,
        scratch_shapes=scratch,
        compiler_params=_cparams(*sem),
        name="s5_scan" if scan else "s5_step",
    )(*args)
    return outs


def _delta_kernel(qkv_ref, z_ref, ba_ref, cs_ref, s0_ref, convw_ref, gpar_ref, onw_ref,
                  y_ref, cnew_ref, snew_ref, ext_scr, csx_scr, s_scr, *, nsb, nheads):
    rb = DN_BLOCK
    lt = rb // nsb
    hd = DN_HEAD_DIM
    width = nheads * hd
    c = pl.program_id(1)
    nc = pl.num_programs(1)
    tail = CONV_WIDTH - 1

    @pl.when(c == 0)
    def _init():
        s_scr[...] = s0_ref[...]
        if nsb == 1:
            ext_scr[0:SUBLANE, :] = cs_ref[0]
        else:
            ext_scr[0:SUBLANE, :] = jnp.zeros((SUBLANE, 3 * width), F32)

    x = qkv_ref[...]
    ext_scr[SUBLANE:SUBLANE + rb, :] = x
    cw = convw_ref[...]
    acc = x * cw[tail:tail + 1, :]
    if nsb > 1:
        csx_scr[0:rb, :] = cs_ref[...].reshape(rb, 3 * width)
        csx_scr[rb:rb + SUBLANE, :] = jnp.zeros((SUBLANE, 3 * width), F32)
        tl = lax.broadcasted_iota(jnp.int32, (rb, 1), 0) % lt
    for k in range(1, CONV_WIDTH):
        xk = ext_scr[SUBLANE - k:SUBLANE - k + rb, :]
        if nsb > 1:
            xk = jnp.where(tl < k, csx_scr[SUBLANE - k:SUBLANE - k + rb, :], xk)
        acc = acc + xk * cw[tail - k:tail - k + 1, :]
    xs = acc * _sigmoid(acc)

    @pl.when(c == nc - 1)
    def _conv_out():
        for s in range(nsb):
            r0 = SUBLANE + (s + 1) * lt - tail
            cnew_ref[s] = ext_scr[r0:r0 + tail, :]

    if nsb == 1:
        ext_scr[0:SUBLANE, :] = ext_scr[rb:rb + SUBLANE, :]

    ba = ba_ref[...]
    gpar = gpar_ref[...]
    beta_all = _sigmoid(ba)
    xg = ba + gpar[1:2, :]
    g_all = gpar[0:1, :] * (jnp.maximum(xg, 0.0) + jnp.log(1.0 + jnp.exp(-jnp.abs(xg))))

    ri = lax.broadcasted_iota(jnp.int32, (rb, rb), 0)
    ci = lax.broadcasted_iota(jnp.int32, (rb, rb), 1)
    same = (ri // lt) == (ci // lt)
    incl = (ri >= ci) & same
    strict = (ri > ci) & same
    tri = jnp.where(incl, 1.0, 0.0).astype(BF16)
    ghi, gmid, glo = _split3(g_all)
    gc_col = (jnp.dot(tri, ghi, preferred_element_type=F32)
              + jnp.dot(tri, gmid, preferred_element_type=F32)
              + jnp.dot(tri, glo, preferred_element_type=F32))
    sel = jnp.where(lax.broadcasted_iota(jnp.int32, (2 * SUBLANE, LANE), 1)
                    == lax.broadcasted_iota(jnp.int32, (2 * SUBLANE, LANE), 0) + nheads, 1.0, 0.0).astype(BF16)
    chi, cmid, clo = _split3(gc_col)
    nt_dims = (((1,), (1,)), ((), ()))
    gc_row = (lax.dot_general(sel, chi, nt_dims, preferred_element_type=F32)
              + lax.dot_general(sel, cmid, nt_dims, preferred_element_type=F32)
              + lax.dot_general(sel, clo, nt_dims, preferred_element_type=F32))
    lastsel = jnp.where(same & ((ci % lt) == lt - 1), 1.0, 0.0).astype(BF16)
    glast_col = (jnp.dot(lastsel, chi, preferred_element_type=F32)
                 + jnp.dot(lastsel, cmid, preferred_element_type=F32)
                 + jnp.dot(lastsel, clo, preferred_element_type=F32))

    eye = jnp.where(ri == ci, 1.0, 0.0)
    onw = onw_ref[...]
    n_sq = max(int(math.log2(lt)) - 1, 0)
    for h in range(nheads):
        q = xs[:, h * hd:(h + 1) * hd]
        k = xs[:, width + h * hd:width + (h + 1) * hd]
        v = xs[:, 2 * width + h * hd:2 * width + (h + 1) * hd]
        q = q * lax.rsqrt(jnp.sum(q * q, axis=-1, keepdims=True) + EPS) * (hd ** -0.5)
        k = k * lax.rsqrt(jnp.sum(k * k, axis=-1, keepdims=True) + EPS)
        beta = beta_all[:, h:h + 1]
        gcc = gc_col[:, nheads + h:nheads + h + 1]
        gcr = gc_row[h:h + 1, :]
        glc = glast_col[:, nheads + h:nheads + h + 1]
        decay = jnp.where(incl, jnp.exp(jnp.where(incl, gcc - gcr, 0.0)), 0.0)
        qk_kk = _mm_nt(jnp.concatenate([q, k], axis=0), k)
        qk = qk_kk[:rb] * decay
        a = jnp.where(strict, beta * qk_kk[rb:] * decay, 0.0)
        tm = eye - a
        if n_sq > 0:
            bpow = _mm(a, a)
        for r in range(n_sq):
            if r == n_sq - 1:
                tm = tm + _mm(tm, bpow)
            else:
                nxt = _mm(jnp.concatenate([tm, bpow], axis=0), bpow)
                bpow = nxt[rb:]
                tm = tm + nxt[:rb]
        egc = jnp.exp(gcc)
        uw = _mm(tm, jnp.concatenate([v * beta, k * (beta * egc)], axis=1))
        u = uw[:, :hd]
        w = uw[:, hd:]
        qe = q * egc
        ks = k * jnp.exp(glc - gcc)
        if nsb == 1:
            st = s_scr[0, h]
            vnew = u - _mm(w, st)
            o = _mm(qe, st) + _mm(qk, vnew)
            s_scr[0, h] = st * jnp.exp(glc[0:1, :]) + _mm_tn(ks, vnew)
        else:
            ws_parts = []
            qs_parts = []
            for s in range(nsb):
                lhs = jnp.concatenate([w[s * lt:(s + 1) * lt], qe[s * lt:(s + 1) * lt]], axis=0)
                r = _mm(lhs, s_scr[s, h])
                ws_parts.append(r[:lt])
                qs_parts.append(r[lt:])
            vnew = u - jnp.concatenate(ws_parts, axis=0)
            o = jnp.concatenate(qs_parts, axis=0) + _mm(qk, vnew)
            rowseq = lax.broadcasted_iota(jnp.int32, (rb, 1), 0) // lt
            for s in range(nsb):
                ksm = jnp.where(rowseq == s, ks, 0.0)
                s_scr[s, h] = s_scr[s, h] * jnp.exp(glc[s * lt:s * lt + 1, :]) + _mm_tn(ksm, vnew)
        o = o * lax.rsqrt(jnp.mean(o * o, axis=-1, keepdims=True) + EPS) * onw
        zh = z_ref[:, h * hd:(h + 1) * hd]
        y_ref[:, h * hd:(h + 1) * hd] = o * (zh * _sigmoid(zh))

    @pl.when(c == nc - 1)
    def _state_out():
        snew_ref[...] = s_scr[...]


def _delta_apply(proj, ba, ncols, qkv_col, z_col, nseq, seqlen, cstate, s0, conv_w, a_log, dt_bias, onorm_w):
    nheads = s0.shape[1]
    width = nheads * DN_HEAD_DIM
    rb = DN_BLOCK
    tail = CONV_WIDTH - 1
    if seqlen >= rb:
        nsb = 1
        nc = seqlen // rb
        grid = (nseq, nc)
        row = lambda b, c: b * nc + c
    else:
        nsb = rb // seqlen
        assert seqlen == SUBLANE and nseq % nsb == 0
        nc = 1
        grid = (nseq // nsb, 1)
        row = lambda b, c: b
    cs8 = jnp.pad(cstate, ((0, 0), (SUBLANE - tail, 0), (0, 0)))
    gpar = jnp.zeros((SUBLANE, LANE), F32)
    gpar = gpar.at[0, nheads:2 * nheads].set(-jnp.exp(a_log))
    gpar = gpar.at[1, nheads:2 * nheads].set(dt_bias)
    t = nseq * seqlen
    return pl.pallas_call(
        functools.partial(_delta_kernel, nsb=nsb, nheads=nheads),
        grid=grid,
        in_specs=[
            pl.BlockSpec((rb, 3 * width), lambda b, c: (row(b, c), qkv_col // (3 * width))),
            pl.BlockSpec((rb, width), lambda b, c: (row(b, c), z_col // width)),
            pl.BlockSpec((rb, LANE), lambda b, c: (row(b, c), 0)),
            pl.BlockSpec((nsb, SUBLANE, 3 * width), lambda b, c: (b, 0, 0)),
            pl.BlockSpec((nsb, nheads, DN_HEAD_DIM, DN_HEAD_DIM), lambda b, c: (b, 0, 0, 0)),
            pl.BlockSpec((CONV_WIDTH, 3 * width), lambda b, c: (0, 0)),
            pl.BlockSpec((SUBLANE, LANE), lambda b, c: (0, 0)),
            pl.BlockSpec((1, DN_HEAD_DIM), lambda b, c: (0, 0)),
        ],
        out_specs=[
            pl.BlockSpec((rb, width), lambda b, c: (row(b, c), 0)),
            pl.BlockSpec((nsb, tail, 3 * width), lambda b, c: (b, 0, 0)),
            pl.BlockSpec((nsb, nheads, DN_HEAD_DIM, DN_HEAD_DIM), lambda b, c: (b, 0, 0, 0)),
        ],
        out_shape=[
            jax.ShapeDtypeStruct((t, width), F32),
            jax.ShapeDtypeStruct((nseq, tail, 3 * width), F32),
            jax.ShapeDtypeStruct(s0.shape, F32),
        ],
        scratch_shapes=[
            pltpu.VMEM((rb + 2 * SUBLANE, 3 * width), F32),
            pltpu.VMEM((rb + SUBLANE, 3 * width), F32),
            pltpu.VMEM((nsb, nheads, DN_HEAD_DIM, DN_HEAD_DIM), F32),
        ],
        compiler_params=_cparams("parallel", "arbitrary"),
        name="delta_chunk" if nsb == 1 else "delta_step",
    )(proj, proj, ba, cs8, s0, conv_w, gpar, onorm_w.reshape(1, DN_HEAD_DIM))


def _mix_kernel(y5_ref, ydn_ref, ga_ref, gb_ref, wglu_ref, bglu_ref, wa_ref, wb_ref, out_ref, glu_scr, dn_scr):
    @pl.when(pl.program_id(2) == 0)
    def _():
        y = y5_ref[...]
        y = 0.5 * y * (1.0 + jnp.tanh(math.sqrt(2.0 / math.pi) * (y + 0.044715 * (y * y * y))))
        lin = jnp.dot(y.astype(BF16), wglu_ref[...], preferred_element_type=F32) + bglu_ref[...]
        glu_scr[...] = (y * _sigmoid(lin)).astype(BF16)
        dn_scr[...] = ydn_ref[...].astype(BF16)

    a = jnp.dot(glu_scr[...], wa_ref[...], preferred_element_type=F32)
    b = jnp.dot(dn_scr[...], wb_ref[...], preferred_element_type=F32)
    out_ref[...] = (_sigmoid(ga_ref[...]) * a + _sigmoid(gb_ref[...]) * b).astype(out_ref.dtype)


def _mix(y5, ydn, proj, ncols, ga_col, gb_col, w_glu, b_glu, w_a, w_b, tr, tn):
    nrow, rtot, w5 = y5.shape
    dm = w_a.shape[1]
    ydn_v = ydn.reshape(rtot, nrow * w5)
    proj_v = proj.reshape(rtot, nrow * ncols)
    cb = ncols // tn
    out = pl.pallas_call(
        _mix_kernel,
        grid=(nrow, rtot // tr, dm // tn),
        in_specs=[
            pl.BlockSpec((None, tr, w5), lambda j, r, n: (j, r, 0)),
            pl.BlockSpec((tr, w5), lambda j, r, n: (r, j)),
            pl.BlockSpec((tr, tn), lambda j, r, n: (r, j * cb + ga_col // tn + n)),
            pl.BlockSpec((tr, tn), lambda j, r, n: (r, j * cb + gb_col // tn + n)),
            pl.BlockSpec((w5, w5), lambda j, r, n: (0, 0)),
            pl.BlockSpec((1, w5), lambda j, r, n: (0, 0)),
            pl.BlockSpec((w5, tn), lambda j, r, n: (0, n)),
            pl.BlockSpec((w5, tn), lambda j, r, n: (0, n)),
        ],
        out_specs=pl.BlockSpec((tr, tn), lambda j, r, n: (r, j * (dm // tn) + n)),
        out_shape=jax.ShapeDtypeStruct((rtot, nrow * dm), BF16),
        scratch_shapes=[pltpu.VMEM((tr, w5), BF16), pltpu.VMEM((tr, w5), BF16)],
        compiler_params=_cparams("parallel", "parallel", "arbitrary"),
        name="mix",
    )(y5, ydn_v, proj_v, proj_v, w_glu, b_glu, w_a, w_b)
    return out.reshape(rtot * nrow, dm)


def _outproj_kernel(x_ref, mix_ref, w_ref, out_ref):
    out_ref[...] = x_ref[...] + jnp.dot(mix_ref[...], w_ref[...], preferred_element_type=F32)


def _outproj(x, mix, w_out, tm, tn):
    t, d = x.shape
    return pl.pallas_call(
        _outproj_kernel,
        grid=(t // tm, d // tn),
        in_specs=[
            pl.BlockSpec((tm, tn), lambda i, j: (i, j)),
            pl.BlockSpec((tm, d), lambda i, j: (i, 0)),
            pl.BlockSpec((d, tn), lambda i, j: (0, j)),
        ],
        out_specs=pl.BlockSpec((tm, tn), lambda i, j: (i, j)),
        out_shape=jax.ShapeDtypeStruct((t, d), F32),
        compiler_params=_cparams("parallel", "arbitrary"),
        name="outproj",
    )(x, mix, w_out)


def _ffn_kernel(x_ref, g_ref, wg_ref, wu_ref, wd_ref, out_ref, h_scr):
    k = pl.program_id(1)

    @pl.when(k == 0)
    def _():
        h_scr[...] = _rms(x_ref[...], g_ref[...]).astype(BF16)

    h = h_scr[...]
    gate = jnp.dot(h, wg_ref[...], preferred_element_type=F32)
    up = jnp.dot(h, wu_ref[...], preferred_element_type=F32)
    act = (gate * _sigmoid(gate) * up).astype(BF16)
    contrib = jnp.dot(act, wd_ref[...], preferred_element_type=F32)

    @pl.when(k == 0)
    def _():
        out_ref[...] = x_ref[...] + contrib

    @pl.when(k > 0)
    def _():
        out_ref[...] += contrib


def _ffn(x, g, w_gate, w_up, w_down, tm, tk):
    t, d = x.shape
    dff = w_gate.shape[1]
    return pl.pallas_call(
        _ffn_kernel,
        grid=(t // tm, dff // tk),
        in_specs=[
            pl.BlockSpec((tm, d), lambda i, k: (i, 0)),
            pl.BlockSpec((1, d), lambda i, k: (0, 0)),
            pl.BlockSpec((d, tk), lambda i, k: (0, k)),
            pl.BlockSpec((d, tk), lambda i, k: (0, k)),
            pl.BlockSpec((tk, d), lambda i, k: (k, 0)),
        ],
        out_specs=pl.BlockSpec((tm, d), lambda i, k: (i, 0)),
        out_shape=jax.ShapeDtypeStruct((t, d), F32),
        scratch_shapes=[pltpu.VMEM((tm, d), BF16)],
        compiler_params=_cparams("parallel", "arbitrary"),
        name="ffn",
    )(x, g, w_gate, w_up, w_down)


def _ple_kernel(x_ref, xj_ref, p_ref, gple_ref, gfin_ref, wple_ref, wpg_ref, out_ref, h_scr, x3_scr, *, nj, tn,
                final):
    j = pl.program_id(1)

    @pl.when(j == 0)
    def _():
        h_scr[...] = _rms(x_ref[...], gple_ref[...]).astype(BF16)

    gate = _sigmoid(jnp.dot(h_scr[...], wpg_ref[...], preferred_element_type=F32))
    emb = jnp.dot(p_ref[...].astype(BF16), wple_ref[...], preferred_element_type=F32)
    x3_scr[j] = xj_ref[...] + emb * gate

    @pl.when(j == nj - 1)
    def _():
        if final:
            ss = jnp.sum(x3_scr[0] * x3_scr[0], axis=-1, keepdims=True)
            for jj in range(1, nj):
                ss = ss + jnp.sum(x3_scr[jj] * x3_scr[jj], axis=-1, keepdims=True)
            inv = lax.rsqrt(ss / (nj * tn) + EPS)
        for jj in range(nj):
            if final:
                out_ref[:, jj * tn:(jj + 1) * tn] = x3_scr[jj] * inv * gfin_ref[:, jj * tn:(jj + 1) * tn]
            else:
                out_ref[:, jj * tn:(jj + 1) * tn] = x3_scr[jj]


def _ple_final(x, p, g_ple, g_final, w_ple, w_ple_gate, tm, tn, final):
    t, d = x.shape
    pd = p.shape[1]
    nj = d // tn
    return pl.pallas_call(
        functools.partial(_ple_kernel, nj=nj, tn=tn, final=final),
        grid=(t // tm, nj),
        in_specs=[
            pl.BlockSpec((tm, d), lambda i, j: (i, 0)),
            pl.BlockSpec((tm, tn), lambda i, j: (i, j)),
            pl.BlockSpec((tm, pd), lambda i, j: (i, 0)),
            pl.BlockSpec((1, d), lambda i, j: (0, 0)),
            pl.BlockSpec((1, d), lambda i, j: (0, 0)),
            pl.BlockSpec((pd, tn), lambda i, j: (0, j)),
            pl.BlockSpec((d, tn), lambda i, j: (0, j)),
        ],
        out_specs=pl.BlockSpec((tm, d), lambda i, j: (i, 0)),
        out_shape=jax.ShapeDtypeStruct((t, d), F32),
        scratch_shapes=[pltpu.VMEM((tm, d), BF16), pltpu.VMEM((nj, tm, tn), F32)],
        compiler_params=_cparams("parallel", "arbitrary"),
        name="ple_final",
    )(x, x, p, g_ple, g_final, w_ple, w_ple_gate)


COL_U, COL_Z, COL_GA, COL_GB, COL_QKV = 0, 1024, 2048, 4096, 6144


def _prep_w_in(w_in, d_model, nheads):
    s5w = d_model // 2
    dnw = d_model // 2
    off_u = s5w
    off_qkv = off_u + 3 * dnw
    off_z = off_qkv + dnw
    off_a = off_z + 2 * nheads
    off_ga = off_a + d_model
    w_main = jnp.concatenate([w_in[:, :off_u], w_in[:, off_qkv:off_z], w_in[:, off_a:off_ga],
                              w_in[:, off_ga:], w_in[:, off_u:off_qkv]], axis=1).astype(BF16)
    w_ba = jnp.pad(w_in[:, off_z:off_a], ((0, 0), (0, LANE - 2 * nheads))).astype(BF16)
    return w_main, w_ba


def _layer(x3, p3, cstate, s0, h0, lw, tm, final):
    nseq, seqlen, d = x3.shape
    t = nseq * seqlen
    x = x3.reshape(t, d)
    ncols = lw['w_main'].shape[1]
    proj, ba = _inproj(x, lw['g_mix'], lw['w_main'], lw['w_ba'], tm, 1152)
    y5, hre, him = _s5_apply(proj, ncols, nseq, seqlen, lw['s5_ops'], lw['s5_d'], h0)
    ydn, cnew, snew = _delta_apply(proj, ba, ncols, COL_QKV, COL_Z, nseq, seqlen, cstate, s0,
                                   lw['conv_w'], lw['a_log'], lw['dt_bias'], lw['onorm_w'])
    rtot = t // S5_ROW
    mix = _mix(y5, ydn, proj, ncols, COL_GA, COL_GB, lw['w_glu'], lw['b_glu'], lw['w_a'], lw['w_b'],
               min(512, rtot), 1024)
    x1 = _outproj(x, mix, lw['w_out'], tm, 1024)
    x2 = _ffn(x1, lw['g_ffn'], lw['w_gate'], lw['w_up'], lw['w_down'], min(512, tm), 512)
    y = _ple_final(x2, p3.reshape(t, -1), lw['g_ple'], lw['g_final'], lw['w_ple'], lw['w_ple_gate'],
                   min(512, tm), 512, final)
    ng = hre.shape[-1] // S5_STATE
    return (y.reshape(nseq, seqlen, d), cnew, snew,
            hre.reshape(nseq, ng, S5_STATE), him.reshape(nseq, ng, S5_STATE))


def kernel(x_prompt, x_sample, state_conv, state_delta, state_s5_re, state_s5_im, p_prompt, p_sample, g_mix, w_in, conv_w, a_log, dt_bias, onorm_w, s5_a_re, s5_a_im, s5_b_re, s5_b_im, s5_c_re, s5_c_im, s5_d, s5_log_dt, w_glu, b_glu, w_a, w_b, w_out, g_ffn, w_gate, w_up, w_down, g_ple, w_ple, w_ple_gate, g_final):
    depth = w_in.shape[0]
    d_model = x_prompt.shape[-1]
    nheads = state_delta.shape[2]
    nb_p = x_prompt.shape[0]
    f32z = functools.partial(jnp.zeros, dtype=F32)
    yp, ys = x_prompt, x_sample
    outs_p, outs_s = [], []
    for i in range(depth):
        w_main, w_ba = _prep_w_in(w_in[i], d_model, nheads)
        lw = dict(
            g_mix=g_mix[i][None], w_main=w_main, w_ba=w_ba,
            conv_w=conv_w[i], a_log=a_log[i], dt_bias=dt_bias[i], onorm_w=onorm_w[i],
            s5_ops=_s5_operators(s5_a_re[i], s5_a_im[i], s5_b_re[i], s5_b_im[i], s5_c_re[i], s5_c_im[i],
                                 s5_log_dt[i]),
            s5_d=s5_d[i],
            w_glu=w_glu[i].astype(BF16), b_glu=b_glu[i][None], w_a=w_a[i].astype(BF16), w_b=w_b[i].astype(BF16),
            w_out=w_out[i].astype(BF16), g_ffn=g_ffn[i][None],
            w_gate=w_gate[i].astype(BF16), w_up=w_up[i].astype(BF16), w_down=w_down[i].astype(BF16),
            g_ple=g_ple[i][None], w_ple=w_ple[i].astype(BF16), w_ple_gate=w_ple_gate[i].astype(BF16),
            g_final=g_final[None],
        )
        final = i == depth - 1
        yp, c1, d1, r1, m1 = _layer(yp, p_prompt[i], f32z((nb_p,) + state_conv.shape[2:]),
                                    f32z((nb_p,) + state_delta.shape[2:]), None, lw, 1024, final)
        ys, c2, d2, r2, m2 = _layer(ys, p_sample[i], state_conv[i], state_delta[i],
                                    (state_s5_re[i], state_s5_im[i]), lw, 1024, final)
        outs_p.append((c1, d1, r1, m1))
        outs_s.append((c2, d2, r2, m2))
    stack = lambda outs, k: jnp.stack([o[k] for o in outs])
    return (yp, ys,
            stack(outs_p, 0), stack(outs_p, 1), stack(outs_p, 2), stack(outs_p, 3),
            stack(outs_s, 0), stack(outs_s, 1), stack(outs_s, 2), stack(outs_s, 3))
```

```python
import functools
import math

import jax
import jax.numpy as jnp
from jax import lax
from jax.experimental import pallas as pl
from jax.experimental.pallas import tpu as pltpu

F32 = jnp.float32
BF16 = jnp.bfloat16

EPS = 1e-6
LANE = 128
SUBLANE = 8
VMEM_LIMIT_BYTES = 56 * 1024 * 1024

S5_GROUP = 16
S5_STATE = 64
S5_ROW = SUBLANE
S5_TILE_GROUPS = LANE // S5_GROUP
S5_TILE_STATE = S5_TILE_GROUPS * S5_STATE
DN_HEAD_DIM = 128
CONV_WIDTH = 4
DN_BLOCK = 64


def _cparams(*sem):
    return pltpu.CompilerParams(dimension_semantics=sem, vmem_limit_bytes=VMEM_LIMIT_BYTES)


def _mm(a, b):
    return jnp.dot(a.astype(BF16), b.astype(BF16), preferred_element_type=F32)


def _mm_nt(a, b):
    return lax.dot_general(a.astype(BF16), b.astype(BF16), (((1,), (1,)), ((), ())),
                           preferred_element_type=F32)


def _mm_tn(a, b):
    return lax.dot_general(a.astype(BF16), b.astype(BF16), (((0,), (0,)), ((), ())),
                           preferred_element_type=F32)


def _split3(x):
    hi = x.astype(BF16)
    r1 = x - hi.astype(F32)
    mid = r1.astype(BF16)
    lo = (r1 - mid.astype(F32)).astype(BF16)
    return hi, mid, lo


def _sigmoid(x):
    return 1.0 / (1.0 + jnp.exp(-x))


def _rms(x, g):
    ms = jnp.mean(x * x, axis=-1, keepdims=True)
    return x * lax.rsqrt(ms + EPS) * g


def _inproj_kernel(x_ref, g_ref, w_ref, wba_ref, out_ref, ba_ref, h_scr):
    @pl.when(pl.program_id(1) == 0)
    def _():
        h = _rms(x_ref[...], g_ref[...]).astype(BF16)
        h_scr[...] = h
        ba_ref[...] = jnp.dot(h, wba_ref[...], preferred_element_type=F32)

    out_ref[...] = jnp.dot(h_scr[...], w_ref[...], preferred_element_type=F32)


def _inproj(x, g, w_main, w_ba, tm, tn):
    t, d = x.shape
    n = w_main.shape[1]
    return pl.pallas_call(
        _inproj_kernel,
        grid=(t // tm, n // tn),
        in_specs=[
            pl.BlockSpec((tm, d), lambda i, j: (i, 0)),
            pl.BlockSpec((1, d), lambda i, j: (0, 0)),
            pl.BlockSpec((d, tn), lambda i, j: (0, j)),
            pl.BlockSpec((d, LANE), lambda i, j: (0, 0)),
        ],
        out_specs=[
            pl.BlockSpec((tm, tn), lambda i, j: (i, j)),
            pl.BlockSpec((tm, LANE), lambda i, j: (i, 0)),
        ],
        out_shape=[jax.ShapeDtypeStruct((t, n), F32), jax.ShapeDtypeStruct((t, LANE), F32)],
        scratch_shapes=[pltpu.VMEM((tm, d), BF16)],
        compiler_params=_cparams("parallel", "arbitrary"),
        name="inproj",
    )(x, g, w_main, w_ba)


def _s5_operators(a_re, a_im, b_re, b_im, c_re, c_im, log_dt):
    g, n = a_re.shape
    p = S5_GROUP
    nt = g // S5_TILE_GROUPS
    dt = jnp.exp(log_dt)[:, None]
    k = jnp.arange(S5_ROW + 1, dtype=F32)[:, None, None]
    magk = jnp.exp(a_re * dt * k)
    lr = magk * jnp.cos(a_im * dt * k)
    li = magk * jnp.sin(a_im * dt * k)
    nr = lr[1] - 1.0
    ni = li[1]
    den = a_re * a_re + a_im * a_im
    cr = (nr * a_re + ni * a_im) / den
    ci = (ni * a_re - nr * a_im) / den
    bb_re = cr[..., None] * b_re - ci[..., None] * b_im
    bb_im = cr[..., None] * b_im + ci[..., None] * b_re
    m_re = c_re[None] * lr[:, :, None, :] - c_im[None] * li[:, :, None, :]
    m_im = c_re[None] * li[:, :, None, :] + c_im[None] * lr[:, :, None, :]
    hp = lax.Precision.HIGHEST
    ktau = (jnp.einsum('kgon,gni->kgoi', m_re[:S5_ROW], bb_re, precision=hp)
            - jnp.einsum('kgon,gni->kgoi', m_im[:S5_ROW], bb_im, precision=hp))
    lrev_r = lr[S5_ROW - 1::-1][:S5_ROW]
    lrev_i = li[S5_ROW - 1::-1][:S5_ROW]
    e_re = lrev_r[..., None] * bb_re[None] - lrev_i[..., None] * bb_im[None]
    e_im = lrev_r[..., None] * bb_im[None] + lrev_i[..., None] * bb_re[None]
    we = jnp.stack([e_re, e_im], axis=0)
    we = we.transpose(2, 1, 4, 0, 3)
    wy = jnp.stack([m_re[1:], -m_im[1:]], axis=0)
    wy = wy.transpose(2, 0, 4, 1, 3)
    ii = jnp.arange(S5_ROW)
    lag = ii[None, :] - ii[:, None]
    kt = ktau[jnp.clip(lag, 0, S5_ROW - 1)]
    kt = jnp.where((lag >= 0)[:, :, None, None, None], kt, 0.0)
    wk = kt.transpose(2, 0, 4, 1, 3)

    eye = jnp.eye(S5_TILE_GROUPS, dtype=F32)
    we = we.reshape(nt, S5_TILE_GROUPS, S5_ROW, p, 2, n)
    we_t = jnp.einsum('tgjprn,gh->tjgprhn', we, eye).reshape(nt, S5_ROW * LANE, 2 * S5_TILE_STATE)
    wy = wy.reshape(nt, S5_TILE_GROUPS, 2, n, S5_ROW, p)
    wy_t = jnp.einsum('tgrnjp,gh->trgnjhp', wy, eye).reshape(nt, 2 * S5_TILE_STATE, S5_ROW * LANE)
    wk = wk.reshape(nt, S5_TILE_GROUPS, S5_ROW, p, S5_ROW, p)
    wk_t = jnp.einsum('tgiqjp,gh->tigqjhp', wk, eye).reshape(nt, S5_ROW * LANE, S5_ROW * LANE)
    lam = jnp.concatenate([lr[S5_ROW].reshape(nt, 1, S5_TILE_STATE),
                           li[S5_ROW].reshape(nt, 1, S5_TILE_STATE)], axis=-1)
    return we_t.astype(BF16), wy_t.astype(BF16), wk_t.astype(BF16), lam


def _s5_kernel(*refs, scan, rows):
    nu = S5_ROW
    u_ref, we_ref, wy_ref, wk_ref, d_ref, lam_ref = refs[:6]
    pos = 6
    if not scan:
        h0r_ref, h0i_ref = refs[pos:pos + 2]
        pos += 2
    y_ref, hre_ref, him_ref = refs[pos:pos + 3]
    scratch = refs[pos + 3:]

    ns = S5_TILE_STATE
    us = [u_ref[pl.ds(j, rows, stride=nu), :] for j in range(nu)]
    u = jnp.concatenate(us, axis=1)
    ub = u.astype(BF16)
    e = jnp.dot(ub, we_ref[...], preferred_element_type=F32)
    lam = lam_ref[...]
    lr = lam[:, :ns]
    li = lam[:, ns:]
    if scan:
        e_scr, h_scr = scratch
        e_scr[...] = e

        def body(c, carry):
            hr, hi = carry
            h_scr[pl.ds(c, 1), :] = jnp.concatenate([hr, hi], axis=1)
            ec = e_scr[pl.ds(c, 1), :]
            return (lr * hr - li * hi + ec[:, :ns], lr * hi + li * hr + ec[:, ns:])

        zero = jnp.zeros((1, ns), F32)
        hr, hi = lax.fori_loop(0, rows, body, (zero, zero))
        hre_ref[...] = hr
        him_ref[...] = hi
        hin = h_scr[...]
    else:
        h0r = h0r_ref[...]
        h0i = h0i_ref[...]
        hre_ref[...] = lr * h0r - li * h0i + e[:, :ns]
        him_ref[...] = lr * h0i + li * h0r + e[:, ns:]
        hin = jnp.concatenate([h0r, h0i], axis=1)
    y = (jnp.dot(hin.astype(BF16), wy_ref[...], preferred_element_type=F32)
         + jnp.dot(ub, wk_ref[...], preferred_element_type=F32))
    d = d_ref[...]
    for j in range(nu):
        y_ref[pl.ds(j, rows, stride=nu), :] = y[:, j * LANE:(j + 1) * LANE] + d * us[j]


def _s5_apply(proj, nseq, seqlen, ops, d, h0=None):
    we_t, wy_t, wk_t, lam = ops
    nt = we_t.shape[0]
    width = nt * LANE
    t = nseq * seqlen
    scan = h0 is None
    if scan:
        rows = seqlen // S5_ROW
        grid = (nt, nseq)
    else:
        assert seqlen == S5_ROW
        rows = nseq
        grid = (nt,)

    def spec(shape, fn):
        if scan:
            return pl.BlockSpec(shape, lambda tt, b: fn(tt, b))
        return pl.BlockSpec(shape, lambda tt: fn(tt, 0))

    in_specs = [
        spec((rows * S5_ROW, LANE), lambda tt, b: (b, tt)),
        spec((None, S5_ROW * LANE, 2 * S5_TILE_STATE), lambda tt, b: (tt, 0, 0)),
        spec((None, 2 * S5_TILE_STATE, S5_ROW * LANE), lambda tt, b: (tt, 0, 0)),
        spec((None, S5_ROW * LANE, S5_ROW * LANE), lambda tt, b: (tt, 0, 0)),
        spec((None, 1, LANE), lambda tt, b: (tt, 0, 0)),
        spec((None, 1, 2 * S5_TILE_STATE), lambda tt, b: (tt, 0, 0)),
    ]
    args = [proj, we_t, wy_t, wk_t, d.reshape(nt, 1, LANE), lam]
    if not scan:
        h0r, h0i = h0
        in_specs += [spec((rows, S5_TILE_STATE), lambda tt, b: (0, tt))] * 2
        args += [h0r.reshape(nseq, nt * S5_TILE_STATE), h0i.reshape(nseq, nt * S5_TILE_STATE)]
    out_specs = [spec((rows * S5_ROW, LANE), lambda tt, b: (b, tt))]
    out_shape = [jax.ShapeDtypeStruct((t, width), F32)]
    if scan:
        out_specs += [spec((None, 1, S5_TILE_STATE), lambda tt, b: (b, 0, tt))] * 2
        out_shape += [jax.ShapeDtypeStruct((nseq, 1, nt * S5_TILE_STATE), F32)] * 2
        scratch = [pltpu.VMEM((rows, 2 * S5_TILE_STATE), F32)] * 2
        sem = ("parallel", "arbitrary")
    else:
        out_specs += [spec((rows, S5_TILE_STATE), lambda tt, b: (0, tt))] * 2
        out_shape += [jax.ShapeDtypeStruct((nseq, nt * S5_TILE_STATE), F32)] * 2
        scratch = []
        sem = ("arbitrary",)
    outs = pl.pallas_call(
        functools.partial(_s5_kernel, scan=scan, rows=rows),
        grid=grid,
        in_specs=in_specs,
        out_specs=out_specs,
        out_shape=out_shape,
        scratch_shapes=scratch,
        compiler_params=_cparams(*sem),
        name="s5_scan" if scan else "s5_step",
    )(*args)
    return outs


def _delta_kernel(qkv_ref, z_ref, ba_ref, cs_ref, s0_ref, convw_ref, gpar_ref, onw_ref,
                  y_ref, cnew_ref, snew_ref, ext_scr, csx_scr, s_scr, *, nsb, nheads):
    rb = DN_BLOCK
    lt = rb // nsb
    hd = DN_HEAD_DIM
    width = nheads * hd
    c = pl.program_id(1)
    nc = pl.num_programs(1)
    tail = CONV_WIDTH - 1

    @pl.when(c == 0)
    def _init():
        s_scr[...] = s0_ref[...]
        if nsb == 1:
            ext_scr[0:SUBLANE, :] = cs_ref[0]
        else:
            ext_scr[0:SUBLANE, :] = jnp.zeros((SUBLANE, 3 * width), F32)

    x = qkv_ref[...]
    ext_scr[SUBLANE:SUBLANE + rb, :] = x
    cw = convw_ref[...]
    acc = x * cw[tail:tail + 1, :]
    if nsb > 1:
        csx_scr[0:rb, :] = cs_ref[...].reshape(rb, 3 * width)
        csx_scr[rb:rb + SUBLANE, :] = jnp.zeros((SUBLANE, 3 * width), F32)
        tl = lax.broadcasted_iota(jnp.int32, (rb, 1), 0) % lt
    for k in range(1, CONV_WIDTH):
        xk = ext_scr[SUBLANE - k:SUBLANE - k + rb, :]
        if nsb > 1:
            xk = jnp.where(tl < k, csx_scr[SUBLANE - k:SUBLANE - k + rb, :], xk)
        acc = acc + xk * cw[tail - k:tail - k + 1, :]
    xs = acc * _sigmoid(acc)

    @pl.when(c == nc - 1)
    def _conv_out():
        for s in range(nsb):
            r0 = SUBLANE + (s + 1) * lt - tail
            cnew_ref[s] = ext_scr[r0:r0 + tail, :]

    if nsb == 1:
        ext_scr[0:SUBLANE, :] = ext_scr[rb:rb + SUBLANE, :]

    ba = ba_ref[...]
    gpar = gpar_ref[...]
    beta_all = _sigmoid(ba)
    xg = ba + gpar[1:2, :]
    g_all = gpar[0:1, :] * (jnp.maximum(xg, 0.0) + jnp.log(1.0 + jnp.exp(-jnp.abs(xg))))

    ri = lax.broadcasted_iota(jnp.int32, (rb, rb), 0)
    ci = lax.broadcasted_iota(jnp.int32, (rb, rb), 1)
    same = (ri // lt) == (ci // lt)
    incl = (ri >= ci) & same
    strict = (ri > ci) & same
    tri = jnp.where(incl, 1.0, 0.0).astype(BF16)
    ghi, gmid, glo = _split3(g_all)
    gc_col = (jnp.dot(tri, ghi, preferred_element_type=F32)
              + jnp.dot(tri, gmid, preferred_element_type=F32)
              + jnp.dot(tri, glo, preferred_element_type=F32))
    sel = jnp.where(lax.broadcasted_iota(jnp.int32, (2 * SUBLANE, LANE), 1)
                    == lax.broadcasted_iota(jnp.int32, (2 * SUBLANE, LANE), 0) + nheads, 1.0, 0.0).astype(BF16)
    chi, cmid, clo = _split3(gc_col)
    nt_dims = (((1,), (1,)), ((), ()))
    gc_row = (lax.dot_general(sel, chi, nt_dims, preferred_element_type=F32)
              + lax.dot_general(sel, cmid, nt_dims, preferred_element_type=F32)
              + lax.dot_general(sel, clo, nt_dims, preferred_element_type=F32))
    lastsel = jnp.where(same & ((ci % lt) == lt - 1), 1.0, 0.0).astype(BF16)
    glast_col = (jnp.dot(lastsel, chi, preferred_element_type=F32)
                 + jnp.dot(lastsel, cmid, preferred_element_type=F32)
                 + jnp.dot(lastsel, clo, preferred_element_type=F32))

    eye = jnp.where(ri == ci, 1.0, 0.0)
    onw = onw_ref[...]
    n_sq = max(int(math.log2(lt)) - 1, 0)
    heads = range(nheads)
    rowseq = lax.broadcasted_iota(jnp.int32, (rb, 1), 0) // lt
    q, k, v, beta, gcc, glc, decay = [], [], [], [], [], [], []
    for h in heads:
        qh = xs[:, h * hd:(h + 1) * hd]
        kh = xs[:, width + h * hd:width + (h + 1) * hd]
        q.append(qh * lax.rsqrt(jnp.sum(qh * qh, axis=-1, keepdims=True) + EPS) * (hd ** -0.5))
        k.append(kh * lax.rsqrt(jnp.sum(kh * kh, axis=-1, keepdims=True) + EPS))
        v.append(xs[:, 2 * width + h * hd:2 * width + (h + 1) * hd])
        beta.append(beta_all[:, h:h + 1])
        gcc.append(gc_col[:, nheads + h:nheads + h + 1])
        glc.append(glast_col[:, nheads + h:nheads + h + 1])
        gcr = gc_row[h:h + 1, :]
        decay.append(jnp.where(incl, jnp.exp(jnp.where(incl, gcc[h] - gcr, 0.0)), 0.0))
    qk_kk = [_mm_nt(jnp.concatenate([q[h], k[h]], axis=0), k[h]) for h in heads]
    qk = [qk_kk[h][:rb] * decay[h] for h in heads]
    a = [jnp.where(strict, beta[h] * qk_kk[h][rb:] * decay[h], 0.0) for h in heads]
    tm = [eye - a[h] for h in heads]
    if n_sq > 0:
        bpow = [_mm(a[h], a[h]) for h in heads]
    for r in range(n_sq):
        if r == n_sq - 1:
            tm = [tm[h] + _mm(tm[h], bpow[h]) for h in heads]
        else:
            nxt = [_mm(jnp.concatenate([tm[h], bpow[h]], axis=0), bpow[h]) for h in heads]
            bpow = [nxt[h][rb:] for h in heads]
            tm = [tm[h] + nxt[h][:rb] for h in heads]
    egc = [jnp.exp(gcc[h]) for h in heads]
    uw = [_mm(tm[h], jnp.concatenate([v[h] * beta[h], k[h] * (beta[h] * egc[h])], axis=1)) for h in heads]
    u = [uw[h][:, :hd] for h in heads]
    w = [uw[h][:, hd:] for h in heads]
    qe = [q[h] * egc[h] for h in heads]
    ks = [k[h] * jnp.exp(glc[h] - gcc[h]) for h in heads]
    if nsb == 1:
        st = [s_scr[0, h] for h in heads]
        vnew = [u[h] - _mm(w[h], st[h]) for h in heads]
        o = [_mm(qe[h], st[h]) + _mm(qk[h], vnew[h]) for h in heads]
        for h in heads:
            s_scr[0, h] = st[h] * jnp.exp(glc[h][0:1, :]) + _mm_tn(ks[h], vnew[h])
    else:
        ws, qs = [], []
        for h in heads:
            parts = [_mm(jnp.concatenate([w[h][s * lt:(s + 1) * lt], qe[h][s * lt:(s + 1) * lt]], axis=0),
                         s_scr[s, h]) for s in range(nsb)]
            ws.append(jnp.concatenate([p[:lt] for p in parts], axis=0))
            qs.append(jnp.concatenate([p[lt:] for p in parts], axis=0))
        vnew = [u[h] - ws[h] for h in heads]
        o = [qs[h] + _mm(qk[h], vnew[h]) for h in heads]
        for h in heads:
            for s in range(nsb):
                ksm = jnp.where(rowseq == s, ks[h], 0.0)
                s_scr[s, h] = s_scr[s, h] * jnp.exp(glc[h][s * lt:s * lt + 1, :]) + _mm_tn(ksm, vnew[h])
    for h in heads:
        oh = o[h] * lax.rsqrt(jnp.mean(o[h] * o[h], axis=-1, keepdims=True) + EPS) * onw
        zh = z_ref[:, h * hd:(h + 1) * hd]
        y_ref[:, h * hd:(h + 1) * hd] = oh * (zh * _sigmoid(zh))

    @pl.when(c == nc - 1)
    def _state_out():
        snew_ref[...] = s_scr[...]


def _delta_apply(proj, ba, qkv_col, z_col, nseq, seqlen, cstate, s0, conv_w, a_log, dt_bias, onorm_w):
    nheads = s0.shape[1]
    width = nheads * DN_HEAD_DIM
    rb = DN_BLOCK
    tail = CONV_WIDTH - 1
    if seqlen >= rb:
        nsb = 1
        nc = seqlen // rb
        grid = (nseq, nc)
        row = lambda b, c: b * nc + c
    else:
        nsb = rb // seqlen
        assert seqlen == SUBLANE and nseq % nsb == 0
        nc = 1
        grid = (nseq // nsb, 1)
        row = lambda b, c: b
    cs8 = jnp.pad(cstate, ((0, 0), (SUBLANE - tail, 0), (0, 0)))
    gpar = jnp.zeros((SUBLANE, LANE), F32)
    gpar = gpar.at[0, nheads:2 * nheads].set(-jnp.exp(a_log))
    gpar = gpar.at[1, nheads:2 * nheads].set(dt_bias)
    t = nseq * seqlen
    return pl.pallas_call(
        functools.partial(_delta_kernel, nsb=nsb, nheads=nheads),
        grid=grid,
        in_specs=[
            pl.BlockSpec((rb, 3 * width), lambda b, c: (row(b, c), qkv_col // (3 * width))),
            pl.BlockSpec((rb, width), lambda b, c: (row(b, c), z_col // width)),
            pl.BlockSpec((rb, LANE), lambda b, c: (row(b, c), 0)),
            pl.BlockSpec((nsb, SUBLANE, 3 * width), lambda b, c: (b, 0, 0)),
            pl.BlockSpec((nsb, nheads, DN_HEAD_DIM, DN_HEAD_DIM), lambda b, c: (b, 0, 0, 0)),
            pl.BlockSpec((CONV_WIDTH, 3 * width), lambda b, c: (0, 0)),
            pl.BlockSpec((SUBLANE, LANE), lambda b, c: (0, 0)),
            pl.BlockSpec((1, DN_HEAD_DIM), lambda b, c: (0, 0)),
        ],
        out_specs=[
            pl.BlockSpec((rb, width), lambda b, c: (row(b, c), 0)),
            pl.BlockSpec((nsb, tail, 3 * width), lambda b, c: (b, 0, 0)),
            pl.BlockSpec((nsb, nheads, DN_HEAD_DIM, DN_HEAD_DIM), lambda b, c: (b, 0, 0, 0)),
        ],
        out_shape=[
            jax.ShapeDtypeStruct((t, width), F32),
            jax.ShapeDtypeStruct((nseq, tail, 3 * width), F32),
            jax.ShapeDtypeStruct(s0.shape, F32),
        ],
        scratch_shapes=[
            pltpu.VMEM((rb + 2 * SUBLANE, 3 * width), F32),
            pltpu.VMEM((rb + SUBLANE, 3 * width), F32),
            pltpu.VMEM((nsb, nheads, DN_HEAD_DIM, DN_HEAD_DIM), F32),
        ],
        compiler_params=_cparams("parallel", "arbitrary"),
        name="delta_chunk" if nsb == 1 else "delta_step",
    )(proj, proj, ba, cs8, s0, conv_w, gpar, onorm_w.reshape(1, DN_HEAD_DIM))


def _mix_kernel(y5_ref, ydn_ref, ga_ref, gb_ref, wglu_ref, bglu_ref, wa_ref, wb_ref, out_ref, glu_scr, dn_scr):
    @pl.when(pl.program_id(1) == 0)
    def _():
        y = y5_ref[...]
        y = 0.5 * y * (1.0 + jnp.tanh(math.sqrt(2.0 / math.pi) * (y + 0.044715 * (y * y * y))))
        lin = jnp.dot(y.astype(BF16), wglu_ref[...], preferred_element_type=F32) + bglu_ref[...]
        glu_scr[...] = (y * _sigmoid(lin)).astype(BF16)
        dn_scr[...] = ydn_ref[...].astype(BF16)

    a = jnp.dot(glu_scr[...], wa_ref[...], preferred_element_type=F32)
    b = jnp.dot(dn_scr[...], wb_ref[...], preferred_element_type=F32)
    out_ref[...] = (_sigmoid(ga_ref[...]) * a + _sigmoid(gb_ref[...]) * b).astype(out_ref.dtype)


def _mix(y5, ydn, proj, ga_col, gb_col, w_glu, b_glu, w_a, w_b, tm, tn):
    t, w5 = y5.shape
    dm = w_a.shape[1]
    return pl.pallas_call(
        _mix_kernel,
        grid=(t // tm, dm // tn),
        in_specs=[
            pl.BlockSpec((tm, w5), lambda i, n: (i, 0)),
            pl.BlockSpec((tm, w5), lambda i, n: (i, 0)),
            pl.BlockSpec((tm, tn), lambda i, n: (i, ga_col // tn + n)),
            pl.BlockSpec((tm, tn), lambda i, n: (i, gb_col // tn + n)),
            pl.BlockSpec((w5, w5), lambda i, n: (0, 0)),
            pl.BlockSpec((1, w5), lambda i, n: (0, 0)),
            pl.BlockSpec((w5, tn), lambda i, n: (0, n)),
            pl.BlockSpec((w5, tn), lambda i, n: (0, n)),
        ],
        out_specs=pl.BlockSpec((tm, tn), lambda i, n: (i, n)),
        out_shape=jax.ShapeDtypeStruct((t, dm), BF16),
        scratch_shapes=[pltpu.VMEM((tm, w5), BF16), pltpu.VMEM((tm, w5), BF16)],
        compiler_params=_cparams("parallel", "arbitrary"),
        name="mix",
    )(y5, ydn, proj, proj, w_glu, b_glu, w_a, w_b)


def _outproj_kernel(x_ref, mix_ref, w_ref, out_ref):
    out_ref[...] = x_ref[...] + jnp.dot(mix_ref[...], w_ref[...], preferred_element_type=F32)


def _outproj(x, mix, w_out, tm, tn):
    t, d = x.shape
    return pl.pallas_call(
        _outproj_kernel,
        grid=(t // tm, d // tn),
        in_specs=[
            pl.BlockSpec((tm, tn), lambda i, j: (i, j)),
            pl.BlockSpec((tm, d), lambda i, j: (i, 0)),
            pl.BlockSpec((d, tn), lambda i, j: (0, j)),
        ],
        out_specs=pl.BlockSpec((tm, tn), lambda i, j: (i, j)),
        out_shape=jax.ShapeDtypeStruct((t, d), F32),
        compiler_params=_cparams("parallel", "arbitrary"),
        name="outproj",
    )(x, mix, w_out)


def _ffn_kernel(x_ref, g_ref, wg_ref, wu_ref, wd_ref, out_ref, h_scr):
    k = pl.program_id(1)

    @pl.when(k == 0)
    def _():
        h_scr[...] = _rms(x_ref[...], g_ref[...]).astype(BF16)

    h = h_scr[...]
    gate = jnp.dot(h, wg_ref[...], preferred_element_type=F32)
    up = jnp.dot(h, wu_ref[...], preferred_element_type=F32)
    act = (gate * _sigmoid(gate) * up).astype(BF16)
    contrib = jnp.dot(act, wd_ref[...], preferred_element_type=F32)

    @pl.when(k == 0)
    def _():
        out_ref[...] = x_ref[...] + contrib

    @pl.when(k > 0)
    def _():
        out_ref[...] += contrib


def _ffn(x, g, w_gate, w_up, w_down, tm, tk):
    t, d = x.shape
    dff = w_gate.shape[1]
    return pl.pallas_call(
        _ffn_kernel,
        grid=(t // tm, dff // tk),
        in_specs=[
            pl.BlockSpec((tm, d), lambda i, k: (i, 0)),
            pl.BlockSpec((1, d), lambda i, k: (0, 0)),
            pl.BlockSpec((d, tk), lambda i, k: (0, k)),
            pl.BlockSpec((d, tk), lambda i, k: (0, k)),
            pl.BlockSpec((tk, d), lambda i, k: (k, 0)),
        ],
        out_specs=pl.BlockSpec((tm, d), lambda i, k: (i, 0)),
        out_shape=jax.ShapeDtypeStruct((t, d), F32),
        scratch_shapes=[pltpu.VMEM((tm, d), BF16)],
        compiler_params=_cparams("parallel", "arbitrary"),
        name="ffn",
    )(x, g, w_gate, w_up, w_down)


def _ple_kernel(x_ref, xj_ref, p_ref, gple_ref, gfin_ref, wple_ref, wpg_ref, out_ref, h_scr, x3_scr, *, nj, tn,
                final):
    j = pl.program_id(1)

    @pl.when(j == 0)
    def _():
        h_scr[...] = _rms(x_ref[...], gple_ref[...]).astype(BF16)

    gate = _sigmoid(jnp.dot(h_scr[...], wpg_ref[...], preferred_element_type=F32))
    emb = jnp.dot(p_ref[...].astype(BF16), wple_ref[...], preferred_element_type=F32)
    x3_scr[j] = xj_ref[...] + emb * gate

    @pl.when(j == nj - 1)
    def _():
        if final:
            ss = jnp.sum(x3_scr[0] * x3_scr[0], axis=-1, keepdims=True)
            for jj in range(1, nj):
                ss = ss + jnp.sum(x3_scr[jj] * x3_scr[jj], axis=-1, keepdims=True)
            inv = lax.rsqrt(ss / (nj * tn) + EPS)
        for jj in range(nj):
            if final:
                out_ref[:, jj * tn:(jj + 1) * tn] = x3_scr[jj] * inv * gfin_ref[:, jj * tn:(jj + 1) * tn]
            else:
                out_ref[:, jj * tn:(jj + 1) * tn] = x3_scr[jj]


def _ple_final(x, p, g_ple, g_final, w_ple, w_ple_gate, tm, tn, final):
    t, d = x.shape
    pd = p.shape[1]
    nj = d // tn
    return pl.pallas_call(
        functools.partial(_ple_kernel, nj=nj, tn=tn, final=final),
        grid=(t // tm, nj),
        in_specs=[
            pl.BlockSpec((tm, d), lambda i, j: (i, 0)),
            pl.BlockSpec((tm, tn), lambda i, j: (i, j)),
            pl.BlockSpec((tm, pd), lambda i, j: (i, 0)),
            pl.BlockSpec((1, d), lambda i, j: (0, 0)),
            pl.BlockSpec((1, d), lambda i, j: (0, 0)),
            pl.BlockSpec((pd, tn), lambda i, j: (0, j)),
            pl.BlockSpec((d, tn), lambda i, j: (0, j)),
        ],
        out_specs=pl.BlockSpec((tm, d), lambda i, j: (i, 0)),
        out_shape=jax.ShapeDtypeStruct((t, d), F32),
        scratch_shapes=[pltpu.VMEM((tm, d), BF16), pltpu.VMEM((nj, tm, tn), F32)],
        compiler_params=_cparams("parallel", "arbitrary"),
        name="ple_final",
    )(x, x, p, g_ple, g_final, w_ple, w_ple_gate)


COL_U, COL_Z, COL_GA, COL_GB, COL_QKV = 0, 1024, 2048, 4096, 6144


def _prep_w_in(w_in, d_model, nheads):
    s5w = d_model // 2
    dnw = d_model // 2
    off_u = s5w
    off_qkv = off_u + 3 * dnw
    off_z = off_qkv + dnw
    off_a = off_z + 2 * nheads
    off_ga = off_a + d_model
    w_main = jnp.concatenate([w_in[:, :off_u], w_in[:, off_qkv:off_z], w_in[:, off_a:off_ga],
                              w_in[:, off_ga:], w_in[:, off_u:off_qkv]], axis=1).astype(BF16)
    w_ba = jnp.pad(w_in[:, off_z:off_a], ((0, 0), (0, LANE - 2 * nheads))).astype(BF16)
    return w_main, w_ba


def _layer(x3, p3, cstate, s0, h0, lw, tm, final):
    nseq, seqlen, d = x3.shape
    t = nseq * seqlen
    x = x3.reshape(t, d)
    proj, ba = _inproj(x, lw['g_mix'], lw['w_main'], lw['w_ba'], tm, 1152)
    y5, hre, him = _s5_apply(proj, nseq, seqlen, lw['s5_ops'], lw['s5_d'], h0)
    ydn, cnew, snew = _delta_apply(proj, ba, COL_QKV, COL_Z, nseq, seqlen, cstate, s0,
                                   lw['conv_w'], lw['a_log'], lw['dt_bias'], lw['onorm_w'])
    mix = _mix(y5, ydn, proj, COL_GA, COL_GB, lw['w_glu'], lw['b_glu'], lw['w_a'], lw['w_b'],
               min(512, tm), 1024)
    x1 = _outproj(x, mix, lw['w_out'], tm, 1024)
    x2 = _ffn(x1, lw['g_ffn'], lw['w_gate'], lw['w_up'], lw['w_down'], min(512, tm), 512)
    y = _ple_final(x2, p3.reshape(t, -1), lw['g_ple'], lw['g_final'], lw['w_ple'], lw['w_ple_gate'],
                   min(512, tm), 512, final)
    ng = hre.shape[-1] // S5_STATE
    return (y.reshape(nseq, seqlen, d), cnew, snew,
            hre.reshape(nseq, ng, S5_STATE), him.reshape(nseq, ng, S5_STATE))


def kernel(x_prompt, x_sample, state_conv, state_delta, state_s5_re, state_s5_im, p_prompt, p_sample, g_mix, w_in, conv_w, a_log, dt_bias, onorm_w, s5_a_re, s5_a_im, s5_b_re, s5_b_im, s5_c_re, s5_c_im, s5_d, s5_log_dt, w_glu, b_glu, w_a, w_b, w_out, g_ffn, w_gate, w_up, w_down, g_ple, w_ple, w_ple_gate, g_final):
    depth = w_in.shape[0]
    d_model = x_prompt.shape[-1]
    nheads = state_delta.shape[2]
    nb_p = x_prompt.shape[0]
    f32z = functools.partial(jnp.zeros, dtype=F32)
    yp, ys = x_prompt, x_sample
    outs_p, outs_s = [], []
    for i in range(depth):
        w_main, w_ba = _prep_w_in(w_in[i], d_model, nheads)
        lw = dict(
            g_mix=g_mix[i][None], w_main=w_main, w_ba=w_ba,
            conv_w=conv_w[i], a_log=a_log[i], dt_bias=dt_bias[i], onorm_w=onorm_w[i],
            s5_ops=_s5_operators(s5_a_re[i], s5_a_im[i], s5_b_re[i], s5_b_im[i], s5_c_re[i], s5_c_im[i],
                                 s5_log_dt[i]),
            s5_d=s5_d[i],
            w_glu=w_glu[i].astype(BF16), b_glu=b_glu[i][None], w_a=w_a[i].astype(BF16), w_b=w_b[i].astype(BF16),
            w_out=w_out[i].astype(BF16), g_ffn=g_ffn[i][None],
            w_gate=w_gate[i].astype(BF16), w_up=w_up[i].astype(BF16), w_down=w_down[i].astype(BF16),
            g_ple=g_ple[i][None], w_ple=w_ple[i].astype(BF16), w_ple_gate=w_ple_gate[i].astype(BF16),
            g_final=g_final[None],
        )
        final = i == depth - 1
        yp, c1, d1, r1, m1 = _layer(yp, p_prompt[i], f32z((nb_p,) + state_conv.shape[2:]),
                                    f32z((nb_p,) + state_delta.shape[2:]), None, lw, 1024, final)
        ys, c2, d2, r2, m2 = _layer(ys, p_sample[i], state_conv[i], state_delta[i],
                                    (state_s5_re[i], state_s5_im[i]), lw, 1024, final)
        outs_p.append((c1, d1, r1, m1))
        outs_s.append((c2, d2, r2, m2))
    stack = lambda outs, k: jnp.stack([o[k] for o in outs])
    return (yp, ys,
            stack(outs_p, 0), stack(outs_p, 1), stack(outs_p, 2), stack(outs_p, 3),
            stack(outs_s, 0), stack(outs_s, 1), stack(outs_s, 2), stack(outs_s, 3))
```

```python
import functools
import math

import jax
import jax.numpy as jnp
from jax import lax
from jax.experimental import pallas as pl
from jax.experimental.pallas import tpu as pltpu

F32 = jnp.float32
BF16 = jnp.bfloat16

EPS = 1e-6
LANE = 128
SUBLANE = 8
VMEM_LIMIT_BYTES = 56 * 1024 * 1024

S5_GROUP = 16
S5_STATE = 64
S5_ROW = SUBLANE
S5_TILE_GROUPS = LANE // S5_GROUP
S5_TILE_STATE = S5_TILE_GROUPS * S5_STATE
DN_HEAD_DIM = 128
CONV_WIDTH = 4
DN_BLOCK = 64


def _cparams(*sem):
    return pltpu.CompilerParams(dimension_semantics=sem, vmem_limit_bytes=VMEM_LIMIT_BYTES)


def _mm(a, b):
    return jnp.dot(a.astype(BF16), b.astype(BF16), preferred_element_type=F32)


def _mm_nt(a, b):
    return lax.dot_general(a.astype(BF16), b.astype(BF16), (((1,), (1,)), ((), ())),
                           preferred_element_type=F32)


def _mm_tn(a, b):
    return lax.dot_general(a.astype(BF16), b.astype(BF16), (((0,), (0,)), ((), ())),
                           preferred_element_type=F32)


def _split3(x):
    hi = x.astype(BF16)
    r1 = x - hi.astype(F32)
    mid = r1.astype(BF16)
    lo = (r1 - mid.astype(F32)).astype(BF16)
    return hi, mid, lo


def _sigmoid(x):
    return 1.0 / (1.0 + jnp.exp(-x))


def _rms(x, g):
    ms = jnp.mean(x * x, axis=-1, keepdims=True)
    return x * lax.rsqrt(ms + EPS) * g


def _inproj_kernel(x_ref, g_ref, w_ref, wba_ref, out_ref, ba_ref, h_scr):
    @pl.when(pl.program_id(1) == 0)
    def _():
        h = _rms(x_ref[...], g_ref[...]).astype(BF16)
        h_scr[...] = h
        ba_ref[...] = jnp.dot(h, wba_ref[...], preferred_element_type=F32)

    out_ref[...] = jnp.dot(h_scr[...], w_ref[...], preferred_element_type=F32)


def _inproj(x, g, w_main, w_ba, tm, tn):
    t, d = x.shape
    n = w_main.shape[1]
    return pl.pallas_call(
        _inproj_kernel,
        grid=(t // tm, n // tn),
        in_specs=[
            pl.BlockSpec((tm, d), lambda i, j: (i, 0)),
            pl.BlockSpec((1, d), lambda i, j: (0, 0)),
            pl.BlockSpec((d, tn), lambda i, j: (0, j)),
            pl.BlockSpec((d, LANE), lambda i, j: (0, 0)),
        ],
        out_specs=[
            pl.BlockSpec((tm, tn), lambda i, j: (i, j)),
            pl.BlockSpec((tm, LANE), lambda i, j: (i, 0)),
        ],
        out_shape=[jax.ShapeDtypeStruct((t, n), F32), jax.ShapeDtypeStruct((t, LANE), F32)],
        scratch_shapes=[pltpu.VMEM((tm, d), BF16)],
        compiler_params=_cparams("parallel", "arbitrary"),
        name="inproj",
    )(x, g, w_main, w_ba)


def _s5_operators(a_re, a_im, b_re, b_im, c_re, c_im, log_dt):
    g, n = a_re.shape
    nt = g // S5_TILE_GROUPS
    ns = S5_TILE_STATE
    wide = S5_ROW * LANE
    row = lambda x: x.reshape(1, g * n)
    bt = lambda x: x.transpose(2, 0, 1).reshape(S5_GROUP, g * n)
    ct = lambda x: x.transpose(1, 0, 2).reshape(S5_GROUP, g * n)
    vec = pl.BlockSpec((1, ns), lambda t: (0, t))
    mat = pl.BlockSpec((S5_GROUP, ns), lambda t: (0, t))
    return pl.pallas_call(
        _s5_ops_kernel,
        grid=(nt,),
        in_specs=[vec, vec, vec, mat, mat, mat, mat],
        out_specs=[
            pl.BlockSpec((None, wide, 2 * ns), lambda t: (t, 0, 0)),
            pl.BlockSpec((None, wide, 2 * ns), lambda t: (t, 0, 0)),
            pl.BlockSpec((None, wide, wide), lambda t: (t, 0, 0)),
            pl.BlockSpec((None, 1, 2 * ns), lambda t: (t, 0, 0)),
        ],
        out_shape=[
            jax.ShapeDtypeStruct((nt, wide, 2 * ns), BF16),
            jax.ShapeDtypeStruct((nt, wide, 2 * ns), BF16),
            jax.ShapeDtypeStruct((nt, wide, wide), BF16),
            jax.ShapeDtypeStruct((nt, 1, 2 * ns), F32),
        ],
        compiler_params=_cparams("parallel"),
        name="s5_ops",
    )(row(a_re), row(a_im), row(jnp.repeat(log_dt, n)), bt(b_re), bt(b_im), ct(c_re), ct(c_im))


def _mm_nt_split(a, b):
    ah = a.astype(BF16)
    al = (a - ah.astype(F32)).astype(BF16)
    bh = b.astype(BF16)
    bl = (b - bh.astype(F32)).astype(BF16)
    dims = (((1,), (1,)), ((), ()))
    return (lax.dot_general(ah, bh, dims, preferred_element_type=F32)
            + lax.dot_general(ah, bl, dims, preferred_element_type=F32)
            + lax.dot_general(al, bh, dims, preferred_element_type=F32))


def _s5_ops_kernel(ar_ref, ai_ref, ldt_ref, btr_ref, bti_ref, ctr_ref, cti_ref, we_ref, wyt_ref, wk_ref, lam_ref):
    ns = S5_TILE_STATE
    ar = ar_ref[...]
    ai = ai_ref[...]
    dt = jnp.exp(ldt_ref[...])
    kk = lax.broadcasted_iota(jnp.int32, (2 * SUBLANE, ns), 0).astype(F32)
    mag = jnp.exp(ar * dt * kk)
    lr = mag * jnp.cos(ai * dt * kk)
    li = mag * jnp.sin(ai * dt * kk)
    nr = lr[1:2] - 1.0
    ni = li[1:2]
    den = ar * ar + ai * ai
    cr = (nr * ar + ni * ai) / den
    ci = (ni * ar - nr * ai) / den
    btr = btr_ref[...]
    bti = bti_ref[...]
    bbr = cr * btr - ci * bti
    bbi = cr * bti + ci * btr
    ctr = ctr_ref[...]
    cti = cti_ref[...]
    same_group = (lax.broadcasted_iota(jnp.int32, (LANE, ns), 0) // S5_GROUP
                  == lax.broadcasted_iota(jnp.int32, (LANE, ns), 1) // S5_STATE)

    def blockdiag(x):
        return jnp.where(same_group, jnp.concatenate([x] * S5_TILE_GROUPS, axis=0), 0.0)

    for j in range(S5_ROW):
        k = S5_ROW - 1 - j
        er = lr[k:k + 1] * bbr - li[k:k + 1] * bbi
        ei = lr[k:k + 1] * bbi + li[k:k + 1] * bbr
        we_ref[j * LANE:(j + 1) * LANE, :] = jnp.concatenate([blockdiag(er), blockdiag(ei)], axis=1).astype(BF16)
    bq = jnp.concatenate([blockdiag(bbr), blockdiag(bbi)], axis=1)
    kblocks = []
    for k in range(S5_ROW + 1):
        mr = ctr * lr[k:k + 1] - cti * li[k:k + 1]
        mi = ctr * li[k:k + 1] + cti * lr[k:k + 1]
        wy_k = jnp.concatenate([blockdiag(mr), -blockdiag(mi)], axis=1)
        if k >= 1:
            wyt_ref[(k - 1) * LANE:k * LANE, :] = wy_k.astype(BF16)
        if k < S5_ROW:
            kblocks.append(_mm_nt_split(bq, wy_k).astype(BF16))
    zero = jnp.zeros((LANE, LANE), BF16)
    for i in range(S5_ROW):
        for j in range(S5_ROW):
            wk_ref[i * LANE:(i + 1) * LANE, j * LANE:(j + 1) * LANE] = kblocks[j - i] if j >= i else zero
    lam_ref[...] = jnp.concatenate([lr[S5_ROW:S5_ROW + 1], li[S5_ROW:S5_ROW + 1]], axis=1)


def _s5_kernel(*refs, scan, rows):
    nu = S5_ROW
    u_ref, we_ref, wy_ref, wk_ref, d_ref, lam_ref = refs[:6]
    pos = 6
    if not scan:
        h0r_ref, h0i_ref = refs[pos:pos + 2]
        pos += 2
    y_ref, hre_ref, him_ref = refs[pos:pos + 3]
    scratch = refs[pos + 3:]

    ns = S5_TILE_STATE
    us = [u_ref[pl.ds(j, rows, stride=nu), :] for j in range(nu)]
    u = jnp.concatenate(us, axis=1)
    ub = u.astype(BF16)
    e = jnp.dot(ub, we_ref[...], preferred_element_type=F32)
    lam = lam_ref[...]
    lr = lam[:, :ns]
    li = lam[:, ns:]
    if scan:
        e_scr, h_scr = scratch
        e_scr[...] = e

        def body(c, carry):
            hr, hi = carry
            h_scr[pl.ds(c, 1), :] = jnp.concatenate([hr, hi], axis=1)
            ec = e_scr[pl.ds(c, 1), :]
            return (lr * hr - li * hi + ec[:, :ns], lr * hi + li * hr + ec[:, ns:])

        zero = jnp.zeros((1, ns), F32)
        hr, hi = lax.fori_loop(0, rows, body, (zero, zero))
        hre_ref[...] = hr
        him_ref[...] = hi
        hin = h_scr[...]
    else:
        h0r = h0r_ref[...]
        h0i = h0i_ref[...]
        hre_ref[...] = lr * h0r - li * h0i + e[:, :ns]
        him_ref[...] = lr * h0i + li * h0r + e[:, ns:]
        hin = jnp.concatenate([h0r, h0i], axis=1)
    y = (lax.dot_general(hin.astype(BF16), wy_ref[...], (((1,), (1,)), ((), ())), preferred_element_type=F32)
         + jnp.dot(ub, wk_ref[...], preferred_element_type=F32))
    d = d_ref[...]
    for j in range(nu):
        y_ref[pl.ds(j, rows, stride=nu), :] = y[:, j * LANE:(j + 1) * LANE] + d * us[j]


def _s5_apply(proj, nseq, seqlen, ops, d, h0=None):
    we_t, wy_t, wk_t, lam = ops
    nt = we_t.shape[0]
    width = nt * LANE
    t = nseq * seqlen
    scan = h0 is None
    if scan:
        rows = seqlen // S5_ROW
        grid = (nt, nseq)
    else:
        assert seqlen == S5_ROW
        rows = nseq
        grid = (nt,)

    def spec(shape, fn):
        if scan:
            return pl.BlockSpec(shape, lambda tt, b: fn(tt, b))
        return pl.BlockSpec(shape, lambda tt: fn(tt, 0))

    in_specs = [
        spec((rows * S5_ROW, LANE), lambda tt, b: (b, tt)),
        spec((None, S5_ROW * LANE, 2 * S5_TILE_STATE), lambda tt, b: (tt, 0, 0)),
        spec((None, 2 * S5_TILE_STATE, S5_ROW * LANE), lambda tt, b: (tt, 0, 0)),
        spec((None, S5_ROW * LANE, S5_ROW * LANE), lambda tt, b: (tt, 0, 0)),
        spec((None, 1, LANE), lambda tt, b: (tt, 0, 0)),
        spec((None, 1, 2 * S5_TILE_STATE), lambda tt, b: (tt, 0, 0)),
    ]
    args = [proj, we_t, wy_t, wk_t, d.reshape(nt, 1, LANE), lam]
    if not scan:
        h0r, h0i = h0
        in_specs += [spec((rows, S5_TILE_STATE), lambda tt, b: (0, tt))] * 2
        args += [h0r.reshape(nseq, nt * S5_TILE_STATE), h0i.reshape(nseq, nt * S5_TILE_STATE)]
    out_specs = [spec((rows * S5_ROW, LANE), lambda tt, b: (b, tt))]
    out_shape = [jax.ShapeDtypeStruct((t, width), F32)]
    if scan:
        out_specs += [spec((None, 1, S5_TILE_STATE), lambda tt, b: (b, 0, tt))] * 2
        out_shape += [jax.ShapeDtypeStruct((nseq, 1, nt * S5_TILE_STATE), F32)] * 2
        scratch = [pltpu.VMEM((rows, 2 * S5_TILE_STATE), F32)] * 2
        sem = ("parallel", "arbitrary")
    else:
        out_specs += [spec((rows, S5_TILE_STATE), lambda tt, b: (0, tt))] * 2
        out_shape += [jax.ShapeDtypeStruct((nseq, nt * S5_TILE_STATE), F32)] * 2
        scratch = []
        sem = ("arbitrary",)
    outs = pl.pallas_call(
        functools.partial(_s5_kernel, scan=scan, rows=rows),
        grid=grid,
        in_specs=in_specs,
        out_specs=out_specs,
        out_shape=out_shape,
        scratch_shapes=scratch,
        compiler_params=_cparams(*sem),
        name="s5_scan" if scan else "s5_step",
    )(*args)
    return outs


def _delta_kernel(qkv_ref, z_ref, ba_ref, cs_ref, s0_ref, convw_ref, gpar_ref, onw_ref,
                  y_ref, cnew_ref, snew_ref, ext_scr, csx_scr, s_scr, *, nsb, nheads):
    rb = DN_BLOCK
    lt = rb // nsb
    hd = DN_HEAD_DIM
    width = nheads * hd
    c = pl.program_id(1)
    nc = pl.num_programs(1)
    tail = CONV_WIDTH - 1

    @pl.when(c == 0)
    def _init():
        s_scr[...] = s0_ref[...]
        if nsb == 1:
            ext_scr[0:SUBLANE, :] = cs_ref[0]
        else:
            ext_scr[0:SUBLANE, :] = jnp.zeros((SUBLANE, 3 * width), F32)

    x = qkv_ref[...]
    ext_scr[SUBLANE:SUBLANE + rb, :] = x
    cw = convw_ref[...]
    acc = x * cw[tail:tail + 1, :]
    if nsb > 1:
        csx_scr[0:rb, :] = cs_ref[...].reshape(rb, 3 * width)
        csx_scr[rb:rb + SUBLANE, :] = jnp.zeros((SUBLANE, 3 * width), F32)
        tl = lax.broadcasted_iota(jnp.int32, (rb, 1), 0) % lt
    for k in range(1, CONV_WIDTH):
        xk = ext_scr[SUBLANE - k:SUBLANE - k + rb, :]
        if nsb > 1:
            xk = jnp.where(tl < k, csx_scr[SUBLANE - k:SUBLANE - k + rb, :], xk)
        acc = acc + xk * cw[tail - k:tail - k + 1, :]
    xs = acc * _sigmoid(acc)

    @pl.when(c == nc - 1)
    def _conv_out():
        for s in range(nsb):
            r0 = SUBLANE + (s + 1) * lt - tail
            cnew_ref[s] = ext_scr[r0:r0 + tail, :]

    if nsb == 1:
        ext_scr[0:SUBLANE, :] = ext_scr[rb:rb + SUBLANE, :]

    ba = ba_ref[...]
    gpar = gpar_ref[...]
    beta_all = _sigmoid(ba)
    xg = ba + gpar[1:2, :]
    g_all = gpar[0:1, :] * (jnp.maximum(xg, 0.0) + jnp.log(1.0 + jnp.exp(-jnp.abs(xg))))

    ri = lax.broadcasted_iota(jnp.int32, (rb, rb), 0)
    ci = lax.broadcasted_iota(jnp.int32, (rb, rb), 1)
    same = (ri // lt) == (ci // lt)
    incl = (ri >= ci) & same
    strict = (ri > ci) & same
    tri = jnp.where(incl, 1.0, 0.0).astype(BF16)
    ghi, gmid, glo = _split3(g_all)
    gc_col = (jnp.dot(tri, ghi, preferred_element_type=F32)
              + jnp.dot(tri, gmid, preferred_element_type=F32)
              + jnp.dot(tri, glo, preferred_element_type=F32))
    sel = jnp.where(lax.broadcasted_iota(jnp.int32, (2 * SUBLANE, LANE), 1)
                    == lax.broadcasted_iota(jnp.int32, (2 * SUBLANE, LANE), 0) + nheads, 1.0, 0.0).astype(BF16)
    chi, cmid, clo = _split3(gc_col)
    nt_dims = (((1,), (1,)), ((), ()))
    gc_row = (lax.dot_general(sel, chi, nt_dims, preferred_element_type=F32)
              + lax.dot_general(sel, cmid, nt_dims, preferred_element_type=F32)
              + lax.dot_general(sel, clo, nt_dims, preferred_element_type=F32))
    lastsel = jnp.where(same & ((ci % lt) == lt - 1), 1.0, 0.0).astype(BF16)
    glast_col = (jnp.dot(lastsel, chi, preferred_element_type=F32)
                 + jnp.dot(lastsel, cmid, preferred_element_type=F32)
                 + jnp.dot(lastsel, clo, preferred_element_type=F32))

    eye = jnp.where(ri == ci, 1.0, 0.0)
    onw = onw_ref[...]
    n_sq = max(int(math.log2(lt)) - 1, 0)
    heads = range(nheads)
    rowseq = lax.broadcasted_iota(jnp.int32, (rb, 1), 0) // lt
    q, k, v, beta, gcc, glc, decay = [], [], [], [], [], [], []
    for h in heads:
        qh = xs[:, h * hd:(h + 1) * hd]
        kh = xs[:, width + h * hd:width + (h + 1) * hd]
        q.append(qh * lax.rsqrt(jnp.sum(qh * qh, axis=-1, keepdims=True) + EPS) * (hd ** -0.5))
        k.append(kh * lax.rsqrt(jnp.sum(kh * kh, axis=-1, keepdims=True) + EPS))
        v.append(xs[:, 2 * width + h * hd:2 * width + (h + 1) * hd])
        beta.append(beta_all[:, h:h + 1])
        gcc.append(gc_col[:, nheads + h:nheads + h + 1])
        glc.append(glast_col[:, nheads + h:nheads + h + 1])
        gcr = gc_row[h:h + 1, :]
        decay.append(jnp.where(incl, jnp.exp(jnp.where(incl, gcc[h] - gcr, 0.0)), 0.0))
    qk_kk = [_mm_nt(jnp.concatenate([q[h], k[h]], axis=0), k[h]) for h in heads]
    qk = [qk_kk[h][:rb] * decay[h] for h in heads]
    a = [jnp.where(strict, beta[h] * qk_kk[h][rb:] * decay[h], 0.0) for h in heads]
    tm = [eye - a[h] for h in heads]
    if n_sq > 0:
        bpow = [_mm(a[h], a[h]) for h in heads]
    for r in range(n_sq):
        if r == n_sq - 1:
            tm = [tm[h] + _mm(tm[h], bpow[h]) for h in heads]
        else:
            nxt = [_mm(jnp.concatenate([tm[h], bpow[h]], axis=0), bpow[h]) for h in heads]
            bpow = [nxt[h][rb:] for h in heads]
            tm = [tm[h] + nxt[h][:rb] for h in heads]
    egc = [jnp.exp(gcc[h]) for h in heads]
    uw = [_mm(tm[h], jnp.concatenate([v[h] * beta[h], k[h] * (beta[h] * egc[h])], axis=1)) for h in heads]
    u = [uw[h][:, :hd] for h in heads]
    w = [uw[h][:, hd:] for h in heads]
    qe = [q[h] * egc[h] for h in heads]
    ks = [k[h] * jnp.exp(glc[h] - gcc[h]) for h in heads]
    if nsb == 1:
        st = [s_scr[0, h] for h in heads]
        vnew = [u[h] - _mm(w[h], st[h]) for h in heads]
        o = [_mm(qe[h], st[h]) + _mm(qk[h], vnew[h]) for h in heads]
        for h in heads:
            s_scr[0, h] = st[h] * jnp.exp(glc[h][0:1, :]) + _mm_tn(ks[h], vnew[h])
    else:
        ws, qs = [], []
        for h in heads:
            parts = [_mm(jnp.concatenate([w[h][s * lt:(s + 1) * lt], qe[h][s * lt:(s + 1) * lt]], axis=0),
                         s_scr[s, h]) for s in range(nsb)]
            ws.append(jnp.concatenate([p[:lt] for p in parts], axis=0))
            qs.append(jnp.concatenate([p[lt:] for p in parts], axis=0))
        vnew = [u[h] - ws[h] for h in heads]
        o = [qs[h] + _mm(qk[h], vnew[h]) for h in heads]
        for h in heads:
            for s in range(nsb):
                ksm = jnp.where(rowseq == s, ks[h], 0.0)
                s_scr[s, h] = s_scr[s, h] * jnp.exp(glc[h][s * lt:s * lt + 1, :]) + _mm_tn(ksm, vnew[h])
    for h in heads:
        oh = o[h] * lax.rsqrt(jnp.mean(o[h] * o[h], axis=-1, keepdims=True) + EPS) * onw
        zh = z_ref[:, h * hd:(h + 1) * hd]
        y_ref[:, h * hd:(h + 1) * hd] = oh * (zh * _sigmoid(zh))

    @pl.when(c == nc - 1)
    def _state_out():
        snew_ref[...] = s_scr[...]


def _delta_apply(proj, ba, qkv_col, z_col, nseq, seqlen, cstate, s0, conv_w, a_log, dt_bias, onorm_w):
    nheads = s0.shape[1]
    width = nheads * DN_HEAD_DIM
    rb = DN_BLOCK
    tail = CONV_WIDTH - 1
    if seqlen >= rb:
        nsb = 1
        nc = seqlen // rb
        grid = (nseq, nc)
        row = lambda b, c: b * nc + c
    else:
        nsb = rb // seqlen
        assert seqlen == SUBLANE and nseq % nsb == 0
        nc = 1
        grid = (nseq // nsb, 1)
        row = lambda b, c: b
    cs8 = jnp.pad(cstate, ((0, 0), (SUBLANE - tail, 0), (0, 0)))
    gpar = jnp.zeros((SUBLANE, LANE), F32)
    gpar = gpar.at[0, nheads:2 * nheads].set(-jnp.exp(a_log))
    gpar = gpar.at[1, nheads:2 * nheads].set(dt_bias)
    t = nseq * seqlen
    return pl.pallas_call(
        functools.partial(_delta_kernel, nsb=nsb, nheads=nheads),
        grid=grid,
        in_specs=[
            pl.BlockSpec((rb, 3 * width), lambda b, c: (row(b, c), qkv_col // (3 * width))),
            pl.BlockSpec((rb, width), lambda b, c: (row(b, c), z_col // width)),
            pl.BlockSpec((rb, LANE), lambda b, c: (row(b, c), 0)),
            pl.BlockSpec((nsb, SUBLANE, 3 * width), lambda b, c: (b, 0, 0)),
            pl.BlockSpec((nsb, nheads, DN_HEAD_DIM, DN_HEAD_DIM), lambda b, c: (b, 0, 0, 0)),
            pl.BlockSpec((CONV_WIDTH, 3 * width), lambda b, c: (0, 0)),
            pl.BlockSpec((SUBLANE, LANE), lambda b, c: (0, 0)),
            pl.BlockSpec((1, DN_HEAD_DIM), lambda b, c: (0, 0)),
        ],
        out_specs=[
            pl.BlockSpec((rb, width), lambda b, c: (row(b, c), 0)),
            pl.BlockSpec((nsb, tail, 3 * width), lambda b, c: (b, 0, 0)),
            pl.BlockSpec((nsb, nheads, DN_HEAD_DIM, DN_HEAD_DIM), lambda b, c: (b, 0, 0, 0)),
        ],
        out_shape=[
            jax.ShapeDtypeStruct((t, width), F32),
            jax.ShapeDtypeStruct((nseq, tail, 3 * width), F32),
            jax.ShapeDtypeStruct(s0.shape, F32),
        ],
        scratch_shapes=[
            pltpu.VMEM((rb + 2 * SUBLANE, 3 * width), F32),
            pltpu.VMEM((rb + SUBLANE, 3 * width), F32),
            pltpu.VMEM((nsb, nheads, DN_HEAD_DIM, DN_HEAD_DIM), F32),
        ],
        compiler_params=_cparams("parallel", "arbitrary"),
        name="delta_chunk" if nsb == 1 else "delta_step",
    )(proj, proj, ba, cs8, s0, conv_w, gpar, onorm_w.reshape(1, DN_HEAD_DIM))


def _mix_kernel(y5_ref, ydn_ref, ga_ref, gb_ref, wglu_ref, bglu_ref, wa_ref, wb_ref, out_ref, glu_scr, dn_scr):
    @pl.when(pl.program_id(1) == 0)
    def _():
        y = y5_ref[...]
        y = 0.5 * y * (1.0 + jnp.tanh(math.sqrt(2.0 / math.pi) * (y + 0.044715 * (y * y * y))))
        lin = jnp.dot(y.astype(BF16), wglu_ref[...], preferred_element_type=F32) + bglu_ref[...]
        glu_scr[...] = (y * _sigmoid(lin)).astype(BF16)
        dn_scr[...] = ydn_ref[...].astype(BF16)

    a = jnp.dot(glu_scr[...], wa_ref[...], preferred_element_type=F32)
    b = jnp.dot(dn_scr[...], wb_ref[...], preferred_element_type=F32)
    out_ref[...] = (_sigmoid(ga_ref[...]) * a + _sigmoid(gb_ref[...]) * b).astype(out_ref.dtype)


def _mix(y5, ydn, proj, ga_col, gb_col, w_glu, b_glu, w_a, w_b, tm, tn):
    t, w5 = y5.shape
    dm = w_a.shape[1]
    return pl.pallas_call(
        _mix_kernel,
        grid=(t // tm, dm // tn),
        in_specs=[
            pl.BlockSpec((tm, w5), lambda i, n: (i, 0)),
            pl.BlockSpec((tm, w5), lambda i, n: (i, 0)),
            pl.BlockSpec((tm, tn), lambda i, n: (i, ga_col // tn + n)),
            pl.BlockSpec((tm, tn), lambda i, n: (i, gb_col // tn + n)),
            pl.BlockSpec((w5, w5), lambda i, n: (0, 0)),
            pl.BlockSpec((1, w5), lambda i, n: (0, 0)),
            pl.BlockSpec((w5, tn), lambda i, n: (0, n)),
            pl.BlockSpec((w5, tn), lambda i, n: (0, n)),
        ],
        out_specs=pl.BlockSpec((tm, tn), lambda i, n: (i, n)),
        out_shape=jax.ShapeDtypeStruct((t, dm), BF16),
        scratch_shapes=[pltpu.VMEM((tm, w5), BF16), pltpu.VMEM((tm, w5), BF16)],
        compiler_params=_cparams("parallel", "arbitrary"),
        name="mix",
    )(y5, ydn, proj, proj, w_glu, b_glu, w_a, w_b)


def _outproj_kernel(x_ref, mix_ref, w_ref, out_ref):
    out_ref[...] = x_ref[...] + jnp.dot(mix_ref[...], w_ref[...], preferred_element_type=F32)


def _outproj(x, mix, w_out, tm, tn):
    t, d = x.shape
    return pl.pallas_call(
        _outproj_kernel,
        grid=(t // tm, d // tn),
        in_specs=[
            pl.BlockSpec((tm, tn), lambda i, j: (i, j)),
            pl.BlockSpec((tm, d), lambda i, j: (i, 0)),
            pl.BlockSpec((d, tn), lambda i, j: (0, j)),
        ],
        out_specs=pl.BlockSpec((tm, tn), lambda i, j: (i, j)),
        out_shape=jax.ShapeDtypeStruct((t, d), F32),
        compiler_params=_cparams("parallel", "arbitrary"),
        name="outproj",
    )(x, mix, w_out)


def _ffn_kernel(x_ref, xn_ref, g_ref, wg_ref, wu_ref, wd_ref, out_ref, h_scr, act_scr, *, nk, tk):
    s = pl.program_id(1)

    @pl.when(s == 0)
    def _():
        h_scr[...] = _rms(x_ref[...], g_ref[...]).astype(BF16)

    @pl.when(s < nk)
    def _():
        h = h_scr[...]
        gate = jnp.dot(h, wg_ref[...], preferred_element_type=F32)
        up = jnp.dot(h, wu_ref[...], preferred_element_type=F32)
        act_scr[s] = (gate * _sigmoid(gate) * up).astype(BF16)

    @pl.when(s >= nk)
    def _():
        acc = xn_ref[...]
        for kk in range(nk):
            acc = acc + jnp.dot(act_scr[kk], wd_ref[kk * tk:(kk + 1) * tk, :], preferred_element_type=F32)
        out_ref[...] = acc


def _ffn(x, g, w_gate, w_up, w_down, tm, tk, tn):
    t, d = x.shape
    dff = w_gate.shape[1]
    nk = dff // tk
    up_blk = lambda i, s: (0, jnp.minimum(s, nk - 1))
    down_blk = lambda i, s: (0, jnp.maximum(s - nk, 0))
    out_blk = lambda i, s: (i, jnp.maximum(s - nk, 0))
    return pl.pallas_call(
        functools.partial(_ffn_kernel, nk=nk, tk=tk),
        grid=(t // tm, nk + d // tn),
        in_specs=[
            pl.BlockSpec((tm, d), lambda i, s: (i, 0)),
            pl.BlockSpec((tm, tn), out_blk),
            pl.BlockSpec((1, d), lambda i, s: (0, 0)),
            pl.BlockSpec((d, tk), up_blk),
            pl.BlockSpec((d, tk), up_blk),
            pl.BlockSpec((dff, tn), down_blk),
        ],
        out_specs=pl.BlockSpec((tm, tn), out_blk),
        out_shape=jax.ShapeDtypeStruct((t, d), F32),
        scratch_shapes=[pltpu.VMEM((tm, d), BF16), pltpu.VMEM((nk, tm, tk), BF16)],
        compiler_params=_cparams("parallel", "arbitrary"),
        name="ffn",
    )(x, x, g, w_gate, w_up, w_down)


def _ple_kernel(x_ref, p_ref, gple_ref, gfin_ref, wple_ref, wpg_ref, out_ref, h_scr, *, nj, tn, final):
    j = pl.program_id(1)

    @pl.when(j == 0)
    def _():
        h_scr[...] = _rms(x_ref[...], gple_ref[...]).astype(BF16)

    gate = _sigmoid(jnp.dot(h_scr[...], wpg_ref[...], preferred_element_type=F32))
    emb = jnp.dot(p_ref[...].astype(BF16), wple_ref[...], preferred_element_type=F32)
    upd = emb * gate
    for jj in range(nj):
        @pl.when(j == jj)
        def _(jj=jj):
            out_ref[:, jj * tn:(jj + 1) * tn] = x_ref[:, jj * tn:(jj + 1) * tn] + upd

    if final:
        @pl.when(j == nj - 1)
        def _():
            out_ref[...] = _rms(out_ref[...], gfin_ref[...])


def _ple_final(x, p, g_ple, g_final, w_ple, w_ple_gate, tm, tn, final):
    t, d = x.shape
    pd = p.shape[1]
    nj = d // tn
    return pl.pallas_call(
        functools.partial(_ple_kernel, nj=nj, tn=tn, final=final),
        grid=(t // tm, nj),
        in_specs=[
            pl.BlockSpec((tm, d), lambda i, j: (i, 0)),
            pl.BlockSpec((tm, pd), lambda i, j: (i, 0)),
            pl.BlockSpec((1, d), lambda i, j: (0, 0)),
            pl.BlockSpec((1, d), lambda i, j: (0, 0)),
            pl.BlockSpec((pd, tn), lambda i, j: (0, j)),
            pl.BlockSpec((d, tn), lambda i, j: (0, j)),
        ],
        out_specs=pl.BlockSpec((tm, d), lambda i, j: (i, 0)),
        out_shape=jax.ShapeDtypeStruct((t, d), F32),
        scratch_shapes=[pltpu.VMEM((tm, d), BF16)],
        compiler_params=_cparams("parallel", "arbitrary"),
        name="ple_final",
    )(x, p, g_ple, g_final, w_ple, w_ple_gate)


COL_U, COL_Z, COL_GA, COL_GB, COL_QKV = 0, 1024, 2048, 4096, 6144


def _prep_w_in(w_in, d_model, nheads):
    s5w = d_model // 2
    dnw = d_model // 2
    off_u = s5w
    off_qkv = off_u + 3 * dnw
    off_z = off_qkv + dnw
    off_a = off_z + 2 * nheads
    off_ga = off_a + d_model
    w_main = jnp.concatenate([w_in[:, :off_u], w_in[:, off_qkv:off_z], w_in[:, off_a:off_ga],
                              w_in[:, off_ga:], w_in[:, off_u:off_qkv]], axis=1).astype(BF16)
    w_ba = jnp.pad(w_in[:, off_z:off_a], ((0, 0), (0, LANE - 2 * nheads))).astype(BF16)
    return w_main, w_ba


def _layer(x3, p3, cstate, s0, h0, lw, tm, final):
    nseq, seqlen, d = x3.shape
    t = nseq * seqlen
    x = x3.reshape(t, d)
    proj, ba = _inproj(x, lw['g_mix'], lw['w_main'], lw['w_ba'], tm, 1536)
    y5, hre, him = _s5_apply(proj, nseq, seqlen, lw['s5_ops'], lw['s5_d'], h0)
    ydn, cnew, snew = _delta_apply(proj, ba, COL_QKV, COL_Z, nseq, seqlen, cstate, s0,
                                   lw['conv_w'], lw['a_log'], lw['dt_bias'], lw['onorm_w'])
    mix = _mix(y5, ydn, proj, COL_GA, COL_GB, lw['w_glu'], lw['b_glu'], lw['w_a'], lw['w_b'],
               min(256, tm), d)
    x1 = _outproj(x, mix, lw['w_out'], tm, 1024)
    x2 = _ffn(x1, lw['g_ffn'], lw['w_gate'], lw['w_up'], lw['w_down'], min(512, tm), 512, 512)
    y = _ple_final(x2, p3.reshape(t, -1), lw['g_ple'], lw['g_final'], lw['w_ple'], lw['w_ple_gate'],
                   tm, 512, final)
    ng = hre.shape[-1] // S5_STATE
    return (y.reshape(nseq, seqlen, d), cnew, snew,
            hre.reshape(nseq, ng, S5_STATE), him.reshape(nseq, ng, S5_STATE))


def kernel(x_prompt, x_sample, state_conv, state_delta, state_s5_re, state_s5_im, p_prompt, p_sample, g_mix, w_in, conv_w, a_log, dt_bias, onorm_w, s5_a_re, s5_a_im, s5_b_re, s5_b_im, s5_c_re, s5_c_im, s5_d, s5_log_dt, w_glu, b_glu, w_a, w_b, w_out, g_ffn, w_gate, w_up, w_down, g_ple, w_ple, w_ple_gate, g_final):
    depth = w_in.shape[0]
    d_model = x_prompt.shape[-1]
    nheads = state_delta.shape[2]
    nb_p = x_prompt.shape[0]
    f32z = functools.partial(jnp.zeros, dtype=F32)
    yp, ys = x_prompt, x_sample
    outs_p, outs_s = [], []
    for i in range(depth):
        w_main, w_ba = _prep_w_in(w_in[i], d_model, nheads)
        lw = dict(
            g_mix=g_mix[i][None], w_main=w_main, w_ba=w_ba,
            conv_w=conv_w[i], a_log=a_log[i], dt_bias=dt_bias[i], onorm_w=onorm_w[i],
            s5_ops=_s5_operators(s5_a_re[i], s5_a_im[i], s5_b_re[i], s5_b_im[i], s5_c_re[i], s5_c_im[i],
                                 s5_log_dt[i]),
            s5_d=s5_d[i],
            w_glu=w_glu[i].astype(BF16), b_glu=b_glu[i][None], w_a=w_a[i].astype(BF16), w_b=w_b[i].astype(BF16),
            w_out=w_out[i].astype(BF16), g_ffn=g_ffn[i][None],
            w_gate=w_gate[i].astype(BF16), w_up=w_up[i].astype(BF16), w_down=w_down[i].astype(BF16),
            g_ple=g_ple[i][None], w_ple=w_ple[i].astype(BF16), w_ple_gate=w_ple_gate[i].astype(BF16),
            g_final=g_final[None],
        )
        final = i == depth - 1
        yp, c1, d1, r1, m1 = _layer(yp, p_prompt[i], f32z((nb_p,) + state_conv.shape[2:]),
                                    f32z((nb_p,) + state_delta.shape[2:]), None, lw, 1024, final)
        ys, c2, d2, r2, m2 = _layer(ys, p_sample[i], state_conv[i], state_delta[i],
                                    (state_s5_re[i], state_s5_im[i]), lw, 1024, final)
        outs_p.append((c1, d1, r1, m1))
        outs_s.append((c2, d2, r2, m2))
    stack = lambda outs, k: jnp.stack([o[k] for o in outs])
    return (yp, ys,
            stack(outs_p, 0), stack(outs_p, 1), stack(outs_p, 2), stack(outs_p, 3),
            stack(outs_s, 0), stack(outs_s, 1), stack(outs_s, 2), stack(outs_s, 3))
```

```python
import functools
import math

import jax
import jax.numpy as jnp
from jax import lax
from jax.experimental import pallas as pl
from jax.experimental.pallas import tpu as pltpu

F32 = jnp.float32
BF16 = jnp.bfloat16

EPS = 1e-6
LANE = 128
SUBLANE = 8
VMEM_LIMIT_BYTES = 56 * 1024 * 1024

S5_GROUP = 16
S5_STATE = 64
S5_ROW = SUBLANE
S5_TILE_GROUPS = LANE // S5_GROUP
S5_TILE_STATE = S5_TILE_GROUPS * S5_STATE
DN_HEAD_DIM = 128
CONV_WIDTH = 4
DN_CHUNK = 64
DN_CHUNKS_PER_STEP = 2


def _cparams(*sem):
    return pltpu.CompilerParams(dimension_semantics=sem, vmem_limit_bytes=VMEM_LIMIT_BYTES)


def _mm(a, b):
    return jnp.dot(a.astype(BF16), b.astype(BF16), preferred_element_type=F32)


def _mm_nt(a, b):
    return lax.dot_general(a.astype(BF16), b.astype(BF16), (((1,), (1,)), ((), ())),
                           preferred_element_type=F32)


def _mm_tn(a, b):
    return lax.dot_general(a.astype(BF16), b.astype(BF16), (((0,), (0,)), ((), ())),
                           preferred_element_type=F32)


def _split3(x):
    hi = x.astype(BF16)
    r1 = x - hi.astype(F32)
    mid = r1.astype(BF16)
    lo = (r1 - mid.astype(F32)).astype(BF16)
    return hi, mid, lo


def _sigmoid(x):
    return 1.0 / (1.0 + jnp.exp(-x))


def _rms(x, g):
    ms = jnp.mean(x * x, axis=-1, keepdims=True)
    return x * lax.rsqrt(ms + EPS) * g


def _inproj_kernel(x_ref, g_ref, w_ref, wba_ref, out_ref, ba_ref, h_scr):
    @pl.when(pl.program_id(1) == 0)
    def _():
        h = _rms(x_ref[...], g_ref[...]).astype(BF16)
        h_scr[...] = h
        ba_ref[...] = jnp.dot(h, wba_ref[...], preferred_element_type=F32)

    out_ref[...] = jnp.dot(h_scr[...], w_ref[...], preferred_element_type=F32)


def _inproj(x, g, w_main, w_ba, tm, tn):
    t, d = x.shape
    n = w_main.shape[1]
    return pl.pallas_call(
        _inproj_kernel,
        grid=(t // tm, n // tn),
        in_specs=[
            pl.BlockSpec((tm, d), lambda i, j: (i, 0)),
            pl.BlockSpec((1, d), lambda i, j: (0, 0)),
            pl.BlockSpec((d, tn), lambda i, j: (0, j)),
            pl.BlockSpec((d, LANE), lambda i, j: (0, 0)),
        ],
        out_specs=[
            pl.BlockSpec((tm, tn), lambda i, j: (i, j)),
            pl.BlockSpec((tm, LANE), lambda i, j: (i, 0)),
        ],
        out_shape=[jax.ShapeDtypeStruct((t, n), F32), jax.ShapeDtypeStruct((t, LANE), F32)],
        scratch_shapes=[pltpu.VMEM((tm, d), BF16)],
        compiler_params=_cparams("parallel", "arbitrary"),
        name="inproj",
    )(x, g, w_main, w_ba)


def _s5_operators(a_re, a_im, b_re, b_im, c_re, c_im, log_dt):
    g, n = a_re.shape
    nt = g // S5_TILE_GROUPS
    ns = S5_TILE_STATE
    wide = S5_ROW * LANE
    row = lambda x: x.reshape(1, g * n)
    bt = lambda x: x.transpose(2, 0, 1).reshape(S5_GROUP, g * n)
    ct = lambda x: x.transpose(1, 0, 2).reshape(S5_GROUP, g * n)
    vec = pl.BlockSpec((1, ns), lambda t: (0, t))
    mat = pl.BlockSpec((S5_GROUP, ns), lambda t: (0, t))
    return pl.pallas_call(
        _s5_ops_kernel,
        grid=(nt,),
        in_specs=[vec, vec, vec, mat, mat, mat, mat],
        out_specs=[
            pl.BlockSpec((None, wide, 2 * ns), lambda t: (t, 0, 0)),
            pl.BlockSpec((None, wide, 2 * ns), lambda t: (t, 0, 0)),
            pl.BlockSpec((None, wide, wide), lambda t: (t, 0, 0)),
            pl.BlockSpec((None, 1, 2 * ns), lambda t: (t, 0, 0)),
        ],
        out_shape=[
            jax.ShapeDtypeStruct((nt, wide, 2 * ns), BF16),
            jax.ShapeDtypeStruct((nt, wide, 2 * ns), BF16),
            jax.ShapeDtypeStruct((nt, wide, wide), BF16),
            jax.ShapeDtypeStruct((nt, 1, 2 * ns), F32),
        ],
        compiler_params=_cparams("parallel"),
        name="s5_ops",
    )(row(a_re), row(a_im), row(jnp.repeat(log_dt, n)), bt(b_re), bt(b_im), ct(c_re), ct(c_im))


def _mm_nt_split(a, b):
    ah = a.astype(BF16)
    al = (a - ah.astype(F32)).astype(BF16)
    bh = b.astype(BF16)
    bl = (b - bh.astype(F32)).astype(BF16)
    dims = (((1,), (1,)), ((), ()))
    return (lax.dot_general(ah, bh, dims, preferred_element_type=F32)
            + lax.dot_general(ah, bl, dims, preferred_element_type=F32)
            + lax.dot_general(al, bh, dims, preferred_element_type=F32))


def _s5_ops_kernel(ar_ref, ai_ref, ldt_ref, btr_ref, bti_ref, ctr_ref, cti_ref, we_ref, wyt_ref, wk_ref, lam_ref):
    ns = S5_TILE_STATE
    ar = ar_ref[...]
    ai = ai_ref[...]
    dt = jnp.exp(ldt_ref[...])
    kk = lax.broadcasted_iota(jnp.int32, (2 * SUBLANE, ns), 0).astype(F32)
    mag = jnp.exp(ar * dt * kk)
    lr = mag * jnp.cos(ai * dt * kk)
    li = mag * jnp.sin(ai * dt * kk)
    nr = lr[1:2] - 1.0
    ni = li[1:2]
    den = ar * ar + ai * ai
    cr = (nr * ar + ni * ai) / den
    ci = (ni * ar - nr * ai) / den
    btr = btr_ref[...]
    bti = bti_ref[...]
    bbr = cr * btr - ci * bti
    bbi = cr * bti + ci * btr
    ctr = ctr_ref[...]
    cti = cti_ref[...]
    same_group = (lax.broadcasted_iota(jnp.int32, (LANE, ns), 0) // S5_GROUP
                  == lax.broadcasted_iota(jnp.int32, (LANE, ns), 1) // S5_STATE)

    def blockdiag(x):
        return jnp.where(same_group, jnp.concatenate([x] * S5_TILE_GROUPS, axis=0), 0.0)

    for j in range(S5_ROW):
        k = S5_ROW - 1 - j
        er = lr[k:k + 1] * bbr - li[k:k + 1] * bbi
        ei = lr[k:k + 1] * bbi + li[k:k + 1] * bbr
        we_ref[j * LANE:(j + 1) * LANE, :] = jnp.concatenate([blockdiag(er), blockdiag(ei)], axis=1).astype(BF16)
    bq = jnp.concatenate([blockdiag(bbr), blockdiag(bbi)], axis=1)
    kblocks = []
    for k in range(S5_ROW + 1):
        mr = ctr * lr[k:k + 1] - cti * li[k:k + 1]
        mi = ctr * li[k:k + 1] + cti * lr[k:k + 1]
        wy_k = jnp.concatenate([blockdiag(mr), -blockdiag(mi)], axis=1)
        if k >= 1:
            wyt_ref[(k - 1) * LANE:k * LANE, :] = wy_k.astype(BF16)
        if k < S5_ROW:
            kblocks.append(_mm_nt_split(bq, wy_k).astype(BF16))
    zero = jnp.zeros((LANE, LANE), BF16)
    for i in range(S5_ROW):
        for j in range(S5_ROW):
            wk_ref[i * LANE:(i + 1) * LANE, j * LANE:(j + 1) * LANE] = kblocks[j - i] if j >= i else zero
    lam_ref[...] = jnp.concatenate([lr[S5_ROW:S5_ROW + 1], li[S5_ROW:S5_ROW + 1]], axis=1)


def _s5_kernel(*refs, scan, rows):
    nu = S5_ROW
    u_ref, we_ref, wy_ref, wk_ref, d_ref, lam_ref = refs[:6]
    pos = 6
    if not scan:
        h0r_ref, h0i_ref = refs[pos:pos + 2]
        pos += 2
    y_ref, hre_ref, him_ref = refs[pos:pos + 3]
    scratch = refs[pos + 3:]

    ns = S5_TILE_STATE
    us = [u_ref[pl.ds(j, rows, stride=nu), :] for j in range(nu)]
    u = jnp.concatenate(us, axis=1)
    ub = u.astype(BF16)
    e = jnp.dot(ub, we_ref[...], preferred_element_type=F32)
    lam = lam_ref[...]
    lr = lam[:, :ns]
    li = lam[:, ns:]
    if scan:
        e_scr, h_scr = scratch
        e_scr[...] = e

        def body(c, carry):
            hr, hi = carry
            h_scr[pl.ds(c, 1), :] = jnp.concatenate([hr, hi], axis=1)
            ec = e_scr[pl.ds(c, 1), :]
            return (lr * hr - li * hi + ec[:, :ns], lr * hi + li * hr + ec[:, ns:])

        zero = jnp.zeros((1, ns), F32)
        hr, hi = lax.fori_loop(0, rows, body, (zero, zero))
        hre_ref[...] = hr
        him_ref[...] = hi
        hin = h_scr[...]
    else:
        h0r = h0r_ref[...]
        h0i = h0i_ref[...]
        hre_ref[...] = lr * h0r - li * h0i + e[:, :ns]
        him_ref[...] = lr * h0i + li * h0r + e[:, ns:]
        hin = jnp.concatenate([h0r, h0i], axis=1)
    y = (lax.dot_general(hin.astype(BF16), wy_ref[...], (((1,), (1,)), ((), ())), preferred_element_type=F32)
         + jnp.dot(ub, wk_ref[...], preferred_element_type=F32))
    d = d_ref[...]
    for j in range(nu):
        y_ref[pl.ds(j, rows, stride=nu), :] = y[:, j * LANE:(j + 1) * LANE] + d * us[j]


def _s5_apply(proj, nseq, seqlen, ops, d, h0=None):
    we_t, wy_t, wk_t, lam = ops
    nt = we_t.shape[0]
    width = nt * LANE
    t = nseq * seqlen
    scan = h0 is None
    if scan:
        rows = seqlen // S5_ROW
        grid = (nt, nseq)
    else:
        assert seqlen == S5_ROW
        rows = nseq
        grid = (nt,)

    def spec(shape, fn):
        if scan:
            return pl.BlockSpec(shape, lambda tt, b: fn(tt, b))
        return pl.BlockSpec(shape, lambda tt: fn(tt, 0))

    in_specs = [
        spec((rows * S5_ROW, LANE), lambda tt, b: (b, tt)),
        spec((None, S5_ROW * LANE, 2 * S5_TILE_STATE), lambda tt, b: (tt, 0, 0)),
        spec((None, 2 * S5_TILE_STATE, S5_ROW * LANE), lambda tt, b: (tt, 0, 0)),
        spec((None, S5_ROW * LANE, S5_ROW * LANE), lambda tt, b: (tt, 0, 0)),
        spec((None, 1, LANE), lambda tt, b: (tt, 0, 0)),
        spec((None, 1, 2 * S5_TILE_STATE), lambda tt, b: (tt, 0, 0)),
    ]
    args = [proj, we_t, wy_t, wk_t, d.reshape(nt, 1, LANE), lam]
    if not scan:
        h0r, h0i = h0
        in_specs += [spec((rows, S5_TILE_STATE), lambda tt, b: (0, tt))] * 2
        args += [h0r.reshape(nseq, nt * S5_TILE_STATE), h0i.reshape(nseq, nt * S5_TILE_STATE)]
    out_specs = [spec((rows * S5_ROW, LANE), lambda tt, b: (b, tt))]
    out_shape = [jax.ShapeDtypeStruct((t, width), F32)]
    if scan:
        out_specs += [spec((None, 1, S5_TILE_STATE), lambda tt, b: (b, 0, tt))] * 2
        out_shape += [jax.ShapeDtypeStruct((nseq, 1, nt * S5_TILE_STATE), F32)] * 2
        scratch = [pltpu.VMEM((rows, 2 * S5_TILE_STATE), F32)] * 2
        sem = ("parallel", "arbitrary")
    else:
        out_specs += [spec((rows, S5_TILE_STATE), lambda tt, b: (0, tt))] * 2
        out_shape += [jax.ShapeDtypeStruct((nseq, nt * S5_TILE_STATE), F32)] * 2
        scratch = []
        sem = ("arbitrary",)
    outs = pl.pallas_call(
        functools.partial(_s5_kernel, scan=scan, rows=rows),
        grid=grid,
        in_specs=in_specs,
        out_specs=out_specs,
        out_shape=out_shape,
        scratch_shapes=scratch,
        compiler_params=_cparams(*sem),
        name="s5_scan" if scan else "s5_step",
    )(*args)
    return outs


def _delta_kernel(qkv_ref, z_ref, ba_ref, cs_ref, s0_ref, convw_ref, gpar_ref, onw_ref,
                  y_ref, cnew_ref, snew_ref, ext_scr, csx_scr, s_scr, *, nseg, lt, chained, nheads):
    rb = nseg * lt
    nsb = 1 if chained else nseg
    hd = DN_HEAD_DIM
    width = nheads * hd
    c = pl.program_id(1)
    nc = pl.num_programs(1)
    tail = CONV_WIDTH - 1

    @pl.when(c == 0)
    def _init():
        s_scr[...] = s0_ref[...]
        if nsb == 1:
            ext_scr[0:SUBLANE, :] = cs_ref[0]
        else:
            ext_scr[0:SUBLANE, :] = jnp.zeros((SUBLANE, 3 * width), F32)

    x = qkv_ref[...]
    ext_scr[SUBLANE:SUBLANE + rb, :] = x
    cw = convw_ref[...]
    acc = x * cw[tail:tail + 1, :]
    if nsb > 1:
        csx_scr[0:rb, :] = cs_ref[...].reshape(rb, 3 * width)
        csx_scr[rb:rb + SUBLANE, :] = jnp.zeros((SUBLANE, 3 * width), F32)
        tl = lax.broadcasted_iota(jnp.int32, (rb, 1), 0) % lt
    for k in range(1, CONV_WIDTH):
        xk = ext_scr[SUBLANE - k:SUBLANE - k + rb, :]
        if nsb > 1:
            xk = jnp.where(tl < k, csx_scr[SUBLANE - k:SUBLANE - k + rb, :], xk)
        acc = acc + xk * cw[tail - k:tail - k + 1, :]
    xs = acc * _sigmoid(acc)

    @pl.when(c == nc - 1)
    def _conv_out():
        for s in range(nsb):
            r0 = SUBLANE + (s + 1) * (rb // nsb) - tail
            cnew_ref[s] = ext_scr[r0:r0 + tail, :]

    if nsb == 1:
        ext_scr[0:SUBLANE, :] = ext_scr[rb:rb + SUBLANE, :]

    ba = ba_ref[...]
    gpar = gpar_ref[...]
    beta_all = _sigmoid(ba)
    xg = ba + gpar[1:2, :]
    g_all = gpar[0:1, :] * (jnp.maximum(xg, 0.0) + jnp.log(1.0 + jnp.exp(-jnp.abs(xg))))

    ri = lax.broadcasted_iota(jnp.int32, (rb, rb), 0)
    ci = lax.broadcasted_iota(jnp.int32, (rb, rb), 1)
    same = (ri // lt) == (ci // lt)
    incl = (ri >= ci) & same
    strict = (ri > ci) & same
    tri = jnp.where(incl, 1.0, 0.0).astype(BF16)
    ghi, gmid, glo = _split3(g_all)
    gc_col = (jnp.dot(tri, ghi, preferred_element_type=F32)
              + jnp.dot(tri, gmid, preferred_element_type=F32)
              + jnp.dot(tri, glo, preferred_element_type=F32))
    sel = jnp.where(lax.broadcasted_iota(jnp.int32, (2 * SUBLANE, LANE), 1)
                    == lax.broadcasted_iota(jnp.int32, (2 * SUBLANE, LANE), 0) + nheads, 1.0, 0.0).astype(BF16)
    chi, cmid, clo = _split3(gc_col)
    nt_dims = (((1,), (1,)), ((), ()))
    gc_row = (lax.dot_general(sel, chi, nt_dims, preferred_element_type=F32)
              + lax.dot_general(sel, cmid, nt_dims, preferred_element_type=F32)
              + lax.dot_general(sel, clo, nt_dims, preferred_element_type=F32))
    lastsel = jnp.where(same & ((ci % lt) == lt - 1), 1.0, 0.0).astype(BF16)
    glast_col = (jnp.dot(lastsel, chi, preferred_element_type=F32)
                 + jnp.dot(lastsel, cmid, preferred_element_type=F32)
                 + jnp.dot(lastsel, clo, preferred_element_type=F32))

    eye = jnp.where(ri == ci, 1.0, 0.0)
    onw = onw_ref[...]
    n_sq = max(int(math.log2(lt)) - 1, 0)
    heads = range(nheads)
    rowseq = lax.broadcasted_iota(jnp.int32, (rb, 1), 0) // lt
    q, k, v, beta, gcc, glc, decay = [], [], [], [], [], [], []
    for h in heads:
        qh = xs[:, h * hd:(h + 1) * hd]
        kh = xs[:, width + h * hd:width + (h + 1) * hd]
        q.append(qh * lax.rsqrt(jnp.sum(qh * qh, axis=-1, keepdims=True) + EPS) * (hd ** -0.5))
        k.append(kh * lax.rsqrt(jnp.sum(kh * kh, axis=-1, keepdims=True) + EPS))
        v.append(xs[:, 2 * width + h * hd:2 * width + (h + 1) * hd])
        beta.append(beta_all[:, h:h + 1])
        gcc.append(gc_col[:, nheads + h:nheads + h + 1])
        glc.append(glast_col[:, nheads + h:nheads + h + 1])
        gcr = gc_row[h:h + 1, :]
        decay.append(jnp.where(incl, jnp.exp(jnp.where(incl, gcc[h] - gcr, 0.0)), 0.0))
    qk_kk = [_mm_nt(jnp.concatenate([q[h], k[h]], axis=0), k[h]) for h in heads]
    qk = [qk_kk[h][:rb] * decay[h] for h in heads]
    a = [jnp.where(strict, beta[h] * qk_kk[h][rb:] * decay[h], 0.0) for h in heads]
    tm = [eye - a[h] for h in heads]
    if n_sq > 0:
        bpow = [_mm(a[h], a[h]) for h in heads]
    for r in range(n_sq):
        if r == n_sq - 1:
            tm = [tm[h] + _mm(tm[h], bpow[h]) for h in heads]
        else:
            nxt = [_mm(jnp.concatenate([tm[h], bpow[h]], axis=0), bpow[h]) for h in heads]
            bpow = [nxt[h][rb:] for h in heads]
            tm = [tm[h] + nxt[h][:rb] for h in heads]
    egc = [jnp.exp(gcc[h]) for h in heads]
    uw = [_mm(tm[h], jnp.concatenate([v[h] * beta[h], k[h] * (beta[h] * egc[h])], axis=1)) for h in heads]
    u = [uw[h][:, :hd] for h in heads]
    w = [uw[h][:, hd:] for h in heads]
    qe = [q[h] * egc[h] for h in heads]
    ks = [k[h] * jnp.exp(glc[h] - gcc[h]) for h in heads]
    if chained:
        st = [s_scr[0, h] for h in heads]
        o_parts = [[] for _ in heads]
        for sg in range(nseg):
            rows = slice(sg * lt, (sg + 1) * lt)
            vnew = [u[h][rows] - _mm(w[h][rows], st[h]) for h in heads]
            above = [jnp.zeros((sg * lt, hd), F32)] if sg > 0 else []
            below = [jnp.zeros(((nseg - 1 - sg) * lt, hd), F32)] if sg < nseg - 1 else []
            vpad = [jnp.concatenate(above + [vnew[h]] + below, axis=0) if nseg > 1 else vnew[h] for h in heads]
            for h in heads:
                o_parts[h].append(_mm(qe[h][rows], st[h]) + _mm(qk[h][rows], vpad[h]))
            st = [st[h] * jnp.exp(glc[h][sg * lt:sg * lt + 1, :]) + _mm_tn(ks[h][rows], vnew[h]) for h in heads]
        for h in heads:
            s_scr[0, h] = st[h]
        o = [jnp.concatenate(o_parts[h], axis=0) if nseg > 1 else o_parts[h][0] for h in heads]
    else:
        ws, qs = [], []
        for h in heads:
            parts = [_mm(jnp.concatenate([w[h][s * lt:(s + 1) * lt], qe[h][s * lt:(s + 1) * lt]], axis=0),
                         s_scr[s, h]) for s in range(nsb)]
            ws.append(jnp.concatenate([p[:lt] for p in parts], axis=0))
            qs.append(jnp.concatenate([p[lt:] for p in parts], axis=0))
        vnew = [u[h] - ws[h] for h in heads]
        o = [qs[h] + _mm(qk[h], vnew[h]) for h in heads]
        for h in heads:
            for s in range(nsb):
                ksm = jnp.where(rowseq == s, ks[h], 0.0)
                s_scr[s, h] = s_scr[s, h] * jnp.exp(glc[h][s * lt:s * lt + 1, :]) + _mm_tn(ksm, vnew[h])
    for h in heads:
        oh = o[h] * lax.rsqrt(jnp.mean(o[h] * o[h], axis=-1, keepdims=True) + EPS) * onw
        zh = z_ref[:, h * hd:(h + 1) * hd]
        y_ref[:, h * hd:(h + 1) * hd] = oh * (zh * _sigmoid(zh))

    @pl.when(c == nc - 1)
    def _state_out():
        snew_ref[...] = s_scr[...]


def _delta_apply(proj, ba, qkv_col, z_col, nseq, seqlen, cstate, s0, conv_w, a_log, dt_bias, onorm_w):
    nheads = s0.shape[1]
    width = nheads * DN_HEAD_DIM
    tail = CONV_WIDTH - 1
    chained = seqlen >= DN_CHUNK
    if chained:
        lt, nseg, nsb = DN_CHUNK, DN_CHUNKS_PER_STEP, 1
        rb = nseg * lt
        nc = seqlen // rb
        grid = (nseq, nc)
        row = lambda b, c: b * nc + c
    else:
        assert seqlen == SUBLANE
        lt, nseg = seqlen, DN_CHUNK // seqlen
        nsb = nseg
        rb = nseg * lt
        assert nseq % nsb == 0
        nc = 1
        grid = (nseq // nsb, 1)
        row = lambda b, c: b
    cs8 = jnp.pad(cstate, ((0, 0), (SUBLANE - tail, 0), (0, 0)))
    gpar = jnp.zeros((SUBLANE, LANE), F32)
    gpar = gpar.at[0, nheads:2 * nheads].set(-jnp.exp(a_log))
    gpar = gpar.at[1, nheads:2 * nheads].set(dt_bias)
    t = nseq * seqlen
    return pl.pallas_call(
        functools.partial(_delta_kernel, nseg=nseg, lt=lt, chained=chained, nheads=nheads),
        grid=grid,
        in_specs=[
            pl.BlockSpec((rb, 3 * width), lambda b, c: (row(b, c), qkv_col // (3 * width))),
            pl.BlockSpec((rb, width), lambda b, c: (row(b, c), z_col // width)),
            pl.BlockSpec((rb, LANE), lambda b, c: (row(b, c), 0)),
            pl.BlockSpec((nsb, SUBLANE, 3 * width), lambda b, c: (b, 0, 0)),
            pl.BlockSpec((nsb, nheads, DN_HEAD_DIM, DN_HEAD_DIM), lambda b, c: (b, 0, 0, 0)),
            pl.BlockSpec((CONV_WIDTH, 3 * width), lambda b, c: (0, 0)),
            pl.BlockSpec((SUBLANE, LANE), lambda b, c: (0, 0)),
            pl.BlockSpec((1, DN_HEAD_DIM), lambda b, c: (0, 0)),
        ],
        out_specs=[
            pl.BlockSpec((rb, width), lambda b, c: (row(b, c), 0)),
            pl.BlockSpec((nsb, tail, 3 * width), lambda b, c: (b, 0, 0)),
            pl.BlockSpec((nsb, nheads, DN_HEAD_DIM, DN_HEAD_DIM), lambda b, c: (b, 0, 0, 0)),
        ],
        out_shape=[
            jax.ShapeDtypeStruct((t, width), F32),
            jax.ShapeDtypeStruct((nseq, tail, 3 * width), F32),
            jax.ShapeDtypeStruct(s0.shape, F32),
        ],
        scratch_shapes=[
            pltpu.VMEM((rb + 2 * SUBLANE, 3 * width), F32),
            pltpu.VMEM((rb + SUBLANE, 3 * width), F32),
            pltpu.VMEM((nsb, nheads, DN_HEAD_DIM, DN_HEAD_DIM), F32),
        ],
        compiler_params=_cparams("parallel", "arbitrary"),
        name="delta_chunk" if chained else "delta_step",
    )(proj, proj, ba, cs8, s0, conv_w, gpar, onorm_w.reshape(1, DN_HEAD_DIM))


def _mix_kernel(y5_ref, ydn_ref, ga_ref, gb_ref, wglu_ref, bglu_ref, wa_ref, wb_ref, out_ref, glu_scr, dn_scr):
    @pl.when(pl.program_id(1) == 0)
    def _():
        y = y5_ref[...]
        y = 0.5 * y * (1.0 + jnp.tanh(math.sqrt(2.0 / math.pi) * (y + 0.044715 * (y * y * y))))
        lin = jnp.dot(y.astype(BF16), wglu_ref[...], preferred_element_type=F32) + bglu_ref[...]
        glu_scr[...] = (y * _sigmoid(lin)).astype(BF16)
        dn_scr[...] = ydn_ref[...].astype(BF16)

    a = jnp.dot(glu_scr[...], wa_ref[...], preferred_element_type=F32)
    b = jnp.dot(dn_scr[...], wb_ref[...], preferred_element_type=F32)
    out_ref[...] = (_sigmoid(ga_ref[...]) * a + _sigmoid(gb_ref[...]) * b).astype(out_ref.dtype)


def _mix(y5, ydn, proj, ga_col, gb_col, w_glu, b_glu, w_a, w_b, tm, tn):
    t, w5 = y5.shape
    dm = w_a.shape[1]
    return pl.pallas_call(
        _mix_kernel,
        grid=(t // tm, dm // tn),
        in_specs=[
            pl.BlockSpec((tm, w5), lambda i, n: (i, 0)),
            pl.BlockSpec((tm, w5), lambda i, n: (i, 0)),
            pl.BlockSpec((tm, tn), lambda i, n: (i, ga_col // tn + n)),
            pl.BlockSpec((tm, tn), lambda i, n: (i, gb_col // tn + n)),
            pl.BlockSpec((w5, w5), lambda i, n: (0, 0)),
            pl.BlockSpec((1, w5), lambda i, n: (0, 0)),
            pl.BlockSpec((w5, tn), lambda i, n: (0, n)),
            pl.BlockSpec((w5, tn), lambda i, n: (0, n)),
        ],
        out_specs=pl.BlockSpec((tm, tn), lambda i, n: (i, n)),
        out_shape=jax.ShapeDtypeStruct((t, dm), BF16),
        scratch_shapes=[pltpu.VMEM((tm, w5), BF16), pltpu.VMEM((tm, w5), BF16)],
        compiler_params=_cparams("parallel", "arbitrary"),
        name="mix",
    )(y5, ydn, proj, proj, w_glu, b_glu, w_a, w_b)


def _outproj_kernel(x_ref, mix_ref, w_ref, out_ref):
    out_ref[...] = x_ref[...] + jnp.dot(mix_ref[...], w_ref[...], preferred_element_type=F32)


def _outproj(x, mix, w_out, tm, tn):
    t, d = x.shape
    return pl.pallas_call(
        _outproj_kernel,
        grid=(t // tm, d // tn),
        in_specs=[
            pl.BlockSpec((tm, tn), lambda i, j: (i, j)),
            pl.BlockSpec((tm, d), lambda i, j: (i, 0)),
            pl.BlockSpec((d, tn), lambda i, j: (0, j)),
        ],
        out_specs=pl.BlockSpec((tm, tn), lambda i, j: (i, j)),
        out_shape=jax.ShapeDtypeStruct((t, d), F32),
        compiler_params=_cparams("parallel", "arbitrary"),
        name="outproj",
    )(x, mix, w_out)


def _ffn_kernel(x_ref, xn_ref, g_ref, wg_ref, wu_ref, wd_ref, out_ref, h_scr, act_scr, *, nk, tk):
    s = pl.program_id(1)

    @pl.when(s == 0)
    def _():
        h_scr[...] = _rms(x_ref[...], g_ref[...]).astype(BF16)

    @pl.when(s < nk)
    def _():
        h = h_scr[...]
        gate = jnp.dot(h, wg_ref[...], preferred_element_type=F32)
        up = jnp.dot(h, wu_ref[...], preferred_element_type=F32)
        act_scr[s] = (gate * _sigmoid(gate) * up).astype(BF16)

    @pl.when(s >= nk)
    def _():
        acc = xn_ref[...]
        for kk in range(nk):
            acc = acc + jnp.dot(act_scr[kk], wd_ref[kk * tk:(kk + 1) * tk, :], preferred_element_type=F32)
        out_ref[...] = acc


def _ffn(x, g, w_gate, w_up, w_down, tm, tk, tn):
    t, d = x.shape
    dff = w_gate.shape[1]
    nk = dff // tk
    up_blk = lambda i, s: (0, jnp.minimum(s, nk - 1))
    down_blk = lambda i, s: (0, jnp.maximum(s - nk, 0))
    out_blk = lambda i, s: (i, jnp.maximum(s - nk, 0))
    return pl.pallas_call(
        functools.partial(_ffn_kernel, nk=nk, tk=tk),
        grid=(t // tm, nk + d // tn),
        in_specs=[
            pl.BlockSpec((tm, d), lambda i, s: (i, 0), pipeline_mode=pl.Buffered(1)),
            pl.BlockSpec((tm, tn), out_blk),
            pl.BlockSpec((1, d), lambda i, s: (0, 0)),
            pl.BlockSpec((d, tk), up_blk),
            pl.BlockSpec((d, tk), up_blk),
            pl.BlockSpec((dff, tn), down_blk),
        ],
        out_specs=pl.BlockSpec((tm, tn), out_blk),
        out_shape=jax.ShapeDtypeStruct((t, d), F32),
        scratch_shapes=[pltpu.VMEM((tm, d), BF16), pltpu.VMEM((nk, tm, tk), BF16)],
        compiler_params=_cparams("parallel", "arbitrary"),
        name="ffn",
    )(x, x, g, w_gate, w_up, w_down)


def _ple_kernel(x_ref, p_ref, gple_ref, gfin_ref, wple_ref, wpg_ref, out_ref, h_scr, *, nj, tn, final):
    j = pl.program_id(1)

    @pl.when(j == 0)
    def _():
        h_scr[...] = _rms(x_ref[...], gple_ref[...]).astype(BF16)

    gate = _sigmoid(jnp.dot(h_scr[...], wpg_ref[...], preferred_element_type=F32))
    emb = jnp.dot(p_ref[...].astype(BF16), wple_ref[...], preferred_element_type=F32)
    upd = emb * gate
    for jj in range(nj):
        @pl.when(j == jj)
        def _(jj=jj):
            out_ref[:, jj * tn:(jj + 1) * tn] = x_ref[:, jj * tn:(jj + 1) * tn] + upd

    if final:
        @pl.when(j == nj - 1)
        def _():
            out_ref[...] = _rms(out_ref[...], gfin_ref[...])


def _ple_final(x, p, g_ple, g_final, w_ple, w_ple_gate, tm, tn, final):
    t, d = x.shape
    pd = p.shape[1]
    nj = d // tn
    return pl.pallas_call(
        functools.partial(_ple_kernel, nj=nj, tn=tn, final=final),
        grid=(t // tm, nj),
        in_specs=[
            pl.BlockSpec((tm, d), lambda i, j: (i, 0)),
            pl.BlockSpec((tm, pd), lambda i, j: (i, 0)),
            pl.BlockSpec((1, d), lambda i, j: (0, 0)),
            pl.BlockSpec((1, d), lambda i, j: (0, 0)),
            pl.BlockSpec((pd, tn), lambda i, j: (0, j)),
            pl.BlockSpec((d, tn), lambda i, j: (0, j)),
        ],
        out_specs=pl.BlockSpec((tm, d), lambda i, j: (i, 0)),
        out_shape=jax.ShapeDtypeStruct((t, d), F32),
        scratch_shapes=[pltpu.VMEM((tm, d), BF16)],
        compiler_params=_cparams("parallel", "arbitrary"),
        name="ple_final",
    )(x, p, g_ple, g_final, w_ple, w_ple_gate)


COL_U, COL_Z, COL_GA, COL_GB, COL_QKV = 0, 1024, 2048, 4096, 6144


def _prep_w_in(w_in, d_model, nheads):
    s5w = d_model // 2
    dnw = d_model // 2
    off_u = s5w
    off_qkv = off_u + 3 * dnw
    off_z = off_qkv + dnw
    off_a = off_z + 2 * nheads
    off_ga = off_a + d_model
    w_main = jnp.concatenate([w_in[:, :off_u], w_in[:, off_qkv:off_z], w_in[:, off_a:off_ga],
                              w_in[:, off_ga:], w_in[:, off_u:off_qkv]], axis=1).astype(BF16)
    w_ba = jnp.pad(w_in[:, off_z:off_a], ((0, 0), (0, LANE - 2 * nheads))).astype(BF16)
    return w_main, w_ba


def _layer(x3, p3, cstate, s0, h0, lw, tm, final):
    nseq, seqlen, d = x3.shape
    t = nseq * seqlen
    x = x3.reshape(t, d)
    proj, ba = _inproj(x, lw['g_mix'], lw['w_main'], lw['w_ba'], tm, 1536)
    y5, hre, him = _s5_apply(proj, nseq, seqlen, lw['s5_ops'], lw['s5_d'], h0)
    ydn, cnew, snew = _delta_apply(proj, ba, COL_QKV, COL_Z, nseq, seqlen, cstate, s0,
                                   lw['conv_w'], lw['a_log'], lw['dt_bias'], lw['onorm_w'])
    mix = _mix(y5, ydn, proj, COL_GA, COL_GB, lw['w_glu'], lw['b_glu'], lw['w_a'], lw['w_b'],
               min(256, tm), d)
    x1 = _outproj(x, mix, lw['w_out'], tm, 1024)
    x2 = _ffn(x1, lw['g_ffn'], lw['w_gate'], lw['w_up'], lw['w_down'], tm, 512, 256)
    y = _ple_final(x2, p3.reshape(t, -1), lw['g_ple'], lw['g_final'], lw['w_ple'], lw['w_ple_gate'],
                   tm, 512, final)
    ng = hre.shape[-1] // S5_STATE
    return (y.reshape(nseq, seqlen, d), cnew, snew,
            hre.reshape(nseq, ng, S5_STATE), him.reshape(nseq, ng, S5_STATE))


def kernel(x_prompt, x_sample, state_conv, state_delta, state_s5_re, state_s5_im, p_prompt, p_sample, g_mix, w_in, conv_w, a_log, dt_bias, onorm_w, s5_a_re, s5_a_im, s5_b_re, s5_b_im, s5_c_re, s5_c_im, s5_d, s5_log_dt, w_glu, b_glu, w_a, w_b, w_out, g_ffn, w_gate, w_up, w_down, g_ple, w_ple, w_ple_gate, g_final):
    depth = w_in.shape[0]
    d_model = x_prompt.shape[-1]
    nheads = state_delta.shape[2]
    nb_p = x_prompt.shape[0]
    f32z = functools.partial(jnp.zeros, dtype=F32)
    yp, ys = x_prompt, x_sample
    outs_p, outs_s = [], []
    for i in range(depth):
        w_main, w_ba = _prep_w_in(w_in[i], d_model, nheads)
        lw = dict(
            g_mix=g_mix[i][None], w_main=w_main, w_ba=w_ba,
            conv_w=conv_w[i], a_log=a_log[i], dt_bias=dt_bias[i], onorm_w=onorm_w[i],
            s5_ops=_s5_operators(s5_a_re[i], s5_a_im[i], s5_b_re[i], s5_b_im[i], s5_c_re[i], s5_c_im[i],
                                 s5_log_dt[i]),
            s5_d=s5_d[i],
            w_glu=w_glu[i].astype(BF16), b_glu=b_glu[i][None], w_a=w_a[i].astype(BF16), w_b=w_b[i].astype(BF16),
            w_out=w_out[i].astype(BF16), g_ffn=g_ffn[i][None],
            w_gate=w_gate[i].astype(BF16), w_up=w_up[i].astype(BF16), w_down=w_down[i].astype(BF16),
            g_ple=g_ple[i][None], w_ple=w_ple[i].astype(BF16), w_ple_gate=w_ple_gate[i].astype(BF16),
            g_final=g_final[None],
        )
        final = i == depth - 1
        yp, c1, d1, r1, m1 = _layer(yp, p_prompt[i], f32z((nb_p,) + state_conv.shape[2:]),
                                    f32z((nb_p,) + state_delta.shape[2:]), None, lw, 1024, final)
        ys, c2, d2, r2, m2 = _layer(ys, p_sample[i], state_conv[i], state_delta[i],
                                    (state_s5_re[i], state_s5_im[i]), lw, 1024, final)
        outs_p.append((c1, d1, r1, m1))
        outs_s.append((c2, d2, r2, m2))
    stack = lambda outs, k: jnp.stack([o[k] for o in outs])
    return (yp, ys,
            stack(outs_p, 0), stack(outs_p, 1), stack(outs_p, 2), stack(outs_p, 3),
            stack(outs_s, 0), stack(outs_s, 1), stack(outs_s, 2), stack(outs_s, 3))
```

```python
import functools
import math

import jax
import jax.numpy as jnp
from jax import lax
from jax.experimental import pallas as pl
from jax.experimental.pallas import tpu as pltpu

F32 = jnp.float32
BF16 = jnp.bfloat16

EPS = 1e-6
LANE = 128
SUBLANE = 8
VMEM_LIMIT_BYTES = 56 * 1024 * 1024

S5_GROUP = 16
S5_STATE = 64
S5_ROW = SUBLANE
S5_TILE_GROUPS = LANE // S5_GROUP
S5_TILE_STATE = S5_TILE_GROUPS * S5_STATE
DN_HEAD_DIM = 128
CONV_WIDTH = 4
DN_CHUNK = 64
DN_CHUNKS_PER_STEP = 2


def _cparams(*sem):
    return pltpu.CompilerParams(dimension_semantics=sem, vmem_limit_bytes=VMEM_LIMIT_BYTES)


def _mm(a, b):
    return jnp.dot(a.astype(BF16), b.astype(BF16), preferred_element_type=F32)


def _mm_nt(a, b):
    return lax.dot_general(a.astype(BF16), b.astype(BF16), (((1,), (1,)), ((), ())),
                           preferred_element_type=F32)


def _mm_tn(a, b):
    return lax.dot_general(a.astype(BF16), b.astype(BF16), (((0,), (0,)), ((), ())),
                           preferred_element_type=F32)


def _split3(x):
    hi = x.astype(BF16)
    r1 = x - hi.astype(F32)
    mid = r1.astype(BF16)
    lo = (r1 - mid.astype(F32)).astype(BF16)
    return hi, mid, lo


def _sigmoid(x):
    return 1.0 / (1.0 + jnp.exp(-x))


def _rms(x, g):
    ms = jnp.mean(x * x, axis=-1, keepdims=True)
    return x * lax.rsqrt(ms + EPS) * g


def _inproj_kernel(x_ref, g_ref, w_ref, wba_ref, out_ref, ba_ref, h_scr):
    @pl.when(pl.program_id(1) == 0)
    def _():
        h = _rms(x_ref[...], g_ref[...]).astype(BF16)
        h_scr[...] = h
        ba_ref[...] = jnp.dot(h, wba_ref[...], preferred_element_type=F32)

    out_ref[...] = jnp.dot(h_scr[...], w_ref[...], preferred_element_type=F32)


def _inproj(x, g, w_main, w_ba, tm, tn):
    t, d = x.shape
    n = w_main.shape[1]
    return pl.pallas_call(
        _inproj_kernel,
        grid=(t // tm, n // tn),
        in_specs=[
            pl.BlockSpec((tm, d), lambda i, j: (i, 0)),
            pl.BlockSpec((1, d), lambda i, j: (0, 0)),
            pl.BlockSpec((d, tn), lambda i, j: (0, j)),
            pl.BlockSpec((d, LANE), lambda i, j: (0, 0)),
        ],
        out_specs=[
            pl.BlockSpec((tm, tn), lambda i, j: (i, j)),
            pl.BlockSpec((tm, LANE), lambda i, j: (i, 0)),
        ],
        out_shape=[jax.ShapeDtypeStruct((t, n), F32), jax.ShapeDtypeStruct((t, LANE), F32)],
        scratch_shapes=[pltpu.VMEM((tm, d), BF16)],
        compiler_params=_cparams("parallel", "arbitrary"),
        name="inproj",
    )(x, g, w_main, w_ba)


def _s5_operators(a_re, a_im, b_re, b_im, c_re, c_im, log_dt):
    g, n = a_re.shape
    nt = g // S5_TILE_GROUPS
    ns = S5_TILE_STATE
    wide = S5_ROW * LANE
    row = lambda x: x.reshape(1, g * n)
    bt = lambda x: x.transpose(2, 0, 1).reshape(S5_GROUP, g * n)
    ct = lambda x: x.transpose(1, 0, 2).reshape(S5_GROUP, g * n)
    vec = pl.BlockSpec((1, ns), lambda t: (0, t))
    mat = pl.BlockSpec((S5_GROUP, ns), lambda t: (0, t))
    return pl.pallas_call(
        _s5_ops_kernel,
        grid=(nt,),
        in_specs=[vec, vec, vec, mat, mat, mat, mat],
        out_specs=[
            pl.BlockSpec((None, wide, 2 * ns), lambda t: (t, 0, 0)),
            pl.BlockSpec((None, wide, 2 * ns), lambda t: (t, 0, 0)),
            pl.BlockSpec((None, wide, wide), lambda t: (t, 0, 0)),
            pl.BlockSpec((None, 1, 2 * ns), lambda t: (t, 0, 0)),
        ],
        out_shape=[
            jax.ShapeDtypeStruct((nt, wide, 2 * ns), BF16),
            jax.ShapeDtypeStruct((nt, wide, 2 * ns), BF16),
            jax.ShapeDtypeStruct((nt, wide, wide), BF16),
            jax.ShapeDtypeStruct((nt, 1, 2 * ns), F32),
        ],
        compiler_params=_cparams("parallel"),
        name="s5_ops",
    )(row(a_re), row(a_im), row(jnp.repeat(log_dt, n)), bt(b_re), bt(b_im), ct(c_re), ct(c_im))


def _mm_nt_split(a, b):
    ah = a.astype(BF16)
    al = (a - ah.astype(F32)).astype(BF16)
    bh = b.astype(BF16)
    bl = (b - bh.astype(F32)).astype(BF16)
    dims = (((1,), (1,)), ((), ()))
    return (lax.dot_general(ah, bh, dims, preferred_element_type=F32)
            + lax.dot_general(ah, bl, dims, preferred_element_type=F32)
            + lax.dot_general(al, bh, dims, preferred_element_type=F32))


def _s5_ops_kernel(ar_ref, ai_ref, ldt_ref, btr_ref, bti_ref, ctr_ref, cti_ref, we_ref, wyt_ref, wk_ref, lam_ref):
    ns = S5_TILE_STATE
    ar = ar_ref[...]
    ai = ai_ref[...]
    dt = jnp.exp(ldt_ref[...])
    kk = lax.broadcasted_iota(jnp.int32, (2 * SUBLANE, ns), 0).astype(F32)
    mag = jnp.exp(ar * dt * kk)
    lr = mag * jnp.cos(ai * dt * kk)
    li = mag * jnp.sin(ai * dt * kk)
    nr = lr[1:2] - 1.0
    ni = li[1:2]
    den = ar * ar + ai * ai
    cr = (nr * ar + ni * ai) / den
    ci = (ni * ar - nr * ai) / den
    btr = btr_ref[...]
    bti = bti_ref[...]
    bbr = cr * btr - ci * bti
    bbi = cr * bti + ci * btr
    ctr = ctr_ref[...]
    cti = cti_ref[...]
    same_group = (lax.broadcasted_iota(jnp.int32, (LANE, ns), 0) // S5_GROUP
                  == lax.broadcasted_iota(jnp.int32, (LANE, ns), 1) // S5_STATE)

    def blockdiag(x):
        return jnp.where(same_group, jnp.concatenate([x] * S5_TILE_GROUPS, axis=0), 0.0)

    for j in range(S5_ROW):
        k = S5_ROW - 1 - j
        er = lr[k:k + 1] * bbr - li[k:k + 1] * bbi
        ei = lr[k:k + 1] * bbi + li[k:k + 1] * bbr
        we_ref[j * LANE:(j + 1) * LANE, :] = jnp.concatenate([blockdiag(er), blockdiag(ei)], axis=1).astype(BF16)
    bq = jnp.concatenate([blockdiag(bbr), blockdiag(bbi)], axis=1)
    kblocks = []
    for k in range(S5_ROW + 1):
        mr = ctr * lr[k:k + 1] - cti * li[k:k + 1]
        mi = ctr * li[k:k + 1] + cti * lr[k:k + 1]
        wy_k = jnp.concatenate([blockdiag(mr), -blockdiag(mi)], axis=1)
        if k >= 1:
            wyt_ref[(k - 1) * LANE:k * LANE, :] = wy_k.astype(BF16)
        if k < S5_ROW:
            kblocks.append(_mm_nt_split(bq, wy_k).astype(BF16))
    zero = jnp.zeros((LANE, LANE), BF16)
    for i in range(S5_ROW):
        for j in range(S5_ROW):
            wk_ref[i * LANE:(i + 1) * LANE, j * LANE:(j + 1) * LANE] = kblocks[j - i] if j >= i else zero
    lam_ref[...] = jnp.concatenate([lr[S5_ROW:S5_ROW + 1], li[S5_ROW:S5_ROW + 1]], axis=1)


def _s5_kernel(*refs, scan, nseq, rps):
    nu = S5_ROW
    rows = nseq * rps
    u_ref, we_ref, wy_ref, wk_ref, d_ref, lam_ref = refs[:6]
    pos = 6
    if not scan:
        h0r_ref, h0i_ref = refs[pos:pos + 2]
        pos += 2
    y_ref, hre_ref, him_ref = refs[pos:pos + 3]
    scratch = refs[pos + 3:]

    ns = S5_TILE_STATE
    us = [u_ref[pl.ds(j, rows, stride=nu), :] for j in range(nu)]
    u = jnp.concatenate(us, axis=1)
    ub = u.astype(BF16)
    e = jnp.dot(ub, we_ref[...], preferred_element_type=F32)
    lam = lam_ref[...]
    lr = lam[:, :ns]
    li = lam[:, ns:]
    if scan:
        e_scr, h_scr = scratch
        nslab = 2 * ns // LANE
        for k in range(nslab):
            e_scr[k] = e[:, k * LANE:(k + 1) * LANE]

        def body(c, carry):
            hr, hi = carry
            hcat = jnp.concatenate([hr, hi], axis=1)
            for k in range(nslab):
                h_scr[k, pl.ds(c, nseq, stride=rps), :] = hcat[:, k * LANE:(k + 1) * LANE]
            ec = jnp.concatenate([e_scr[k, pl.ds(c, nseq, stride=rps), :] for k in range(nslab)], axis=1)
            return (lr * hr - li * hi + ec[:, :ns], lr * hi + li * hr + ec[:, ns:])

        zero = jnp.zeros((nseq, ns), F32)
        hr, hi = lax.fori_loop(0, rps, body, (zero, zero))
        hre_ref[...] = hr
        him_ref[...] = hi
        hin = jnp.concatenate([h_scr[k] for k in range(nslab)], axis=1)
    else:
        h0r = h0r_ref[...]
        h0i = h0i_ref[...]
        hre_ref[...] = lr * h0r - li * h0i + e[:, :ns]
        him_ref[...] = lr * h0i + li * h0r + e[:, ns:]
        hin = jnp.concatenate([h0r, h0i], axis=1)
    y = (lax.dot_general(hin.astype(BF16), wy_ref[...], (((1,), (1,)), ((), ())), preferred_element_type=F32)
         + jnp.dot(ub, wk_ref[...], preferred_element_type=F32))
    d = d_ref[...]
    for j in range(nu):
        y_ref[pl.ds(j, rows, stride=nu), :] = y[:, j * LANE:(j + 1) * LANE] + d * us[j]


def _s5_apply(proj, nseq, seqlen, ops, d, h0=None):
    we_t, wy_t, wk_t, lam = ops
    nt = we_t.shape[0]
    width = nt * LANE
    t = nseq * seqlen
    scan = h0 is None
    rps = seqlen // S5_ROW
    assert scan or rps == 1
    ns2 = 2 * S5_TILE_STATE
    tile3 = lambda shape: pl.BlockSpec((None,) + shape, lambda tt: (tt, 0, 0))
    state = pl.BlockSpec((nseq, S5_TILE_STATE), lambda tt: (0, tt))
    in_specs = [
        pl.BlockSpec((t, LANE), lambda tt: (0, tt)),
        tile3((S5_ROW * LANE, ns2)),
        tile3((ns2, S5_ROW * LANE)),
        tile3((S5_ROW * LANE, S5_ROW * LANE)),
        tile3((1, LANE)),
        tile3((1, ns2)),
    ]
    args = [proj, we_t, wy_t, wk_t, d.reshape(nt, 1, LANE), lam]
    scratch = []
    if scan:
        scratch = [pltpu.VMEM((ns2 // LANE, nseq * rps, LANE), F32)] * 2
    else:
        in_specs += [state, state]
        args += [h0[0].reshape(nseq, nt * S5_TILE_STATE), h0[1].reshape(nseq, nt * S5_TILE_STATE)]
    return pl.pallas_call(
        functools.partial(_s5_kernel, scan=scan, nseq=nseq, rps=rps),
        grid=(nt,),
        in_specs=in_specs,
        out_specs=[pl.BlockSpec((t, LANE), lambda tt: (0, tt)), state, state],
        out_shape=[jax.ShapeDtypeStruct((t, width), F32)]
        + [jax.ShapeDtypeStruct((nseq, nt * S5_TILE_STATE), F32)] * 2,
        scratch_shapes=scratch,
        compiler_params=_cparams("parallel"),
        name="s5_scan" if scan else "s5_step",
    )(*args)


def _delta_kernel(qkv_ref, z_ref, ba_ref, cs_ref, s0_ref, convw_ref, gpar_ref, onw_ref,
                  y_ref, cnew_ref, snew_ref, ext_scr, csx_scr, s_scr, *, nseg, lt, chained, nheads):
    rb = nseg * lt
    nsb = 1 if chained else nseg
    hd = DN_HEAD_DIM
    width = nheads * hd
    c = pl.program_id(1)
    nc = pl.num_programs(1)
    tail = CONV_WIDTH - 1

    @pl.when(c == 0)
    def _init():
        s_scr[...] = s0_ref[...]
        if nsb == 1:
            ext_scr[0:SUBLANE, :] = cs_ref[0]
        else:
            ext_scr[0:SUBLANE, :] = jnp.zeros((SUBLANE, 3 * width), F32)

    x = qkv_ref[...]
    ext_scr[SUBLANE:SUBLANE + rb, :] = x
    cw = convw_ref[...]
    acc = x * cw[tail:tail + 1, :]
    if nsb > 1:
        csx_scr[0:rb, :] = cs_ref[...].reshape(rb, 3 * width)
        csx_scr[rb:rb + SUBLANE, :] = jnp.zeros((SUBLANE, 3 * width), F32)
        tl = lax.broadcasted_iota(jnp.int32, (rb, 1), 0) % lt
    for k in range(1, CONV_WIDTH):
        xk = ext_scr[SUBLANE - k:SUBLANE - k + rb, :]
        if nsb > 1:
            xk = jnp.where(tl < k, csx_scr[SUBLANE - k:SUBLANE - k + rb, :], xk)
        acc = acc + xk * cw[tail - k:tail - k + 1, :]
    xs = acc * _sigmoid(acc)

    @pl.when(c == nc - 1)
    def _conv_out():
        for s in range(nsb):
            r0 = SUBLANE + (s + 1) * (rb // nsb) - tail
            cnew_ref[s] = ext_scr[r0:r0 + tail, :]

    if nsb == 1:
        ext_scr[0:SUBLANE, :] = ext_scr[rb:rb + SUBLANE, :]

    ba = ba_ref[...]
    gpar = gpar_ref[...]
    beta_all = _sigmoid(ba)
    xg = ba + gpar[1:2, :]
    g_all = gpar[0:1, :] * (jnp.maximum(xg, 0.0) + jnp.log(1.0 + jnp.exp(-jnp.abs(xg))))

    ri = lax.broadcasted_iota(jnp.int32, (rb, rb), 0)
    ci = lax.broadcasted_iota(jnp.int32, (rb, rb), 1)
    same = (ri // lt) == (ci // lt)
    incl = (ri >= ci) & same
    strict = (ri > ci) & same
    tri = jnp.where(incl, 1.0, 0.0).astype(BF16)
    ghi, gmid, glo = _split3(g_all)
    gc_col = (jnp.dot(tri, ghi, preferred_element_type=F32)
              + jnp.dot(tri, gmid, preferred_element_type=F32)
              + jnp.dot(tri, glo, preferred_element_type=F32))
    sel = jnp.where(lax.broadcasted_iota(jnp.int32, (2 * SUBLANE, LANE), 1)
                    == lax.broadcasted_iota(jnp.int32, (2 * SUBLANE, LANE), 0) + nheads, 1.0, 0.0).astype(BF16)
    chi, cmid, clo = _split3(gc_col)
    nt_dims = (((1,), (1,)), ((), ()))
    gc_row = (lax.dot_general(sel, chi, nt_dims, preferred_element_type=F32)
              + lax.dot_general(sel, cmid, nt_dims, preferred_element_type=F32)
              + lax.dot_general(sel, clo, nt_dims, preferred_element_type=F32))
    lastsel = jnp.where(same & ((ci % lt) == lt - 1), 1.0, 0.0).astype(BF16)
    glast_col = (jnp.dot(lastsel, chi, preferred_element_type=F32)
                 + jnp.dot(lastsel, cmid, preferred_element_type=F32)
                 + jnp.dot(lastsel, clo, preferred_element_type=F32))

    eye = jnp.where(ri == ci, 1.0, 0.0)
    onw = onw_ref[...]
    n_sq = max(int(math.log2(lt)) - 1, 0)
    heads = range(nheads)
    rowseq = lax.broadcasted_iota(jnp.int32, (rb, 1), 0) // lt
    q, k, v, beta, gcc, glc, decay = [], [], [], [], [], [], []
    for h in heads:
        qh = xs[:, h * hd:(h + 1) * hd]
        kh = xs[:, width + h * hd:width + (h + 1) * hd]
        q.append(qh * lax.rsqrt(jnp.sum(qh * qh, axis=-1, keepdims=True) + EPS) * (hd ** -0.5))
        k.append(kh * lax.rsqrt(jnp.sum(kh * kh, axis=-1, keepdims=True) + EPS))
        v.append(xs[:, 2 * width + h * hd:2 * width + (h + 1) * hd])
        beta.append(beta_all[:, h:h + 1])
        gcc.append(gc_col[:, nheads + h:nheads + h + 1])
        glc.append(glast_col[:, nheads + h:nheads + h + 1])
        gcr = gc_row[h:h + 1, :]
        decay.append(jnp.where(incl, jnp.exp(jnp.where(incl, gcc[h] - gcr, 0.0)), 0.0))
    qk_kk = [_mm_nt(jnp.concatenate([q[h], k[h]], axis=0), k[h]) for h in heads]
    qk = [qk_kk[h][:rb] * decay[h] for h in heads]
    a = [jnp.where(strict, beta[h] * qk_kk[h][rb:] * decay[h], 0.0) for h in heads]
    tm = [eye - a[h] for h in heads]
    if n_sq > 0:
        bpow = [_mm(a[h], a[h]) for h in heads]
    for r in range(n_sq):
        if r == n_sq - 1:
            tm = [tm[h] + _mm(tm[h], bpow[h]) for h in heads]
        else:
            nxt = [_mm(jnp.concatenate([tm[h], bpow[h]], axis=0), bpow[h]) for h in heads]
            bpow = [nxt[h][rb:] for h in heads]
            tm = [tm[h] + nxt[h][:rb] for h in heads]
    egc = [jnp.exp(gcc[h]) for h in heads]
    uw = [_mm(tm[h], jnp.concatenate([v[h] * beta[h], k[h] * (beta[h] * egc[h])], axis=1)) for h in heads]
    u = [uw[h][:, :hd] for h in heads]
    w = [uw[h][:, hd:] for h in heads]
    qe = [q[h] * egc[h] for h in heads]
    ks = [k[h] * jnp.exp(glc[h] - gcc[h]) for h in heads]
    if chained:
        st = [s_scr[0, h] for h in heads]
        o_parts = [[] for _ in heads]
        for sg in range(nseg):
            rows = slice(sg * lt, (sg + 1) * lt)
            vnew = [u[h][rows] - _mm(w[h][rows], st[h]) for h in heads]
            above = [jnp.zeros((sg * lt, hd), F32)] if sg > 0 else []
            below = [jnp.zeros(((nseg - 1 - sg) * lt, hd), F32)] if sg < nseg - 1 else []
            vpad = [jnp.concatenate(above + [vnew[h]] + below, axis=0) if nseg > 1 else vnew[h] for h in heads]
            for h in heads:
                o_parts[h].append(_mm(qe[h][rows], st[h]) + _mm(qk[h][rows], vpad[h]))
            st = [st[h] * jnp.exp(glc[h][sg * lt:sg * lt + 1, :]) + _mm_tn(ks[h][rows], vnew[h]) for h in heads]
        for h in heads:
            s_scr[0, h] = st[h]
        o = [jnp.concatenate(o_parts[h], axis=0) if nseg > 1 else o_parts[h][0] for h in heads]
    else:
        ws, qs = [], []
        for h in heads:
            parts = [_mm(jnp.concatenate([w[h][s * lt:(s + 1) * lt], qe[h][s * lt:(s + 1) * lt]], axis=0),
                         s_scr[s, h]) for s in range(nsb)]
            ws.append(jnp.concatenate([p[:lt] for p in parts], axis=0))
            qs.append(jnp.concatenate([p[lt:] for p in parts], axis=0))
        vnew = [u[h] - ws[h] for h in heads]
        o = [qs[h] + _mm(qk[h], vnew[h]) for h in heads]
        for h in heads:
            for s in range(nsb):
                ksm = jnp.where(rowseq == s, ks[h], 0.0)
                s_scr[s, h] = s_scr[s, h] * jnp.exp(glc[h][s * lt:s * lt + 1, :]) + _mm_tn(ksm, vnew[h])
    for h in heads:
        oh = o[h] * lax.rsqrt(jnp.mean(o[h] * o[h], axis=-1, keepdims=True) + EPS) * onw
        zh = z_ref[:, h * hd:(h + 1) * hd]
        y_ref[:, h * hd:(h + 1) * hd] = oh * (zh * _sigmoid(zh))

    @pl.when(c == nc - 1)
    def _state_out():
        snew_ref[...] = s_scr[...]


def _delta_apply(proj, ba, qkv_col, z_col, nseq, seqlen, cstate, s0, conv_w, a_log, dt_bias, onorm_w):
    nheads = s0.shape[1]
    width = nheads * DN_HEAD_DIM
    tail = CONV_WIDTH - 1
    chained = seqlen >= DN_CHUNK
    if chained:
        lt, nseg, nsb = DN_CHUNK, DN_CHUNKS_PER_STEP, 1
        rb = nseg * lt
        nc = seqlen // rb
        grid = (nseq, nc)
        row = lambda b, c: b * nc + c
    else:
        assert seqlen == SUBLANE
        lt, nseg = seqlen, DN_CHUNK // seqlen
        nsb = nseg
        rb = nseg * lt
        assert nseq % nsb == 0
        nc = 1
        grid = (nseq // nsb, 1)
        row = lambda b, c: b
    cs8 = jnp.pad(cstate, ((0, 0), (SUBLANE - tail, 0), (0, 0)))
    gpar = jnp.zeros((SUBLANE, LANE), F32)
    gpar = gpar.at[0, nheads:2 * nheads].set(-jnp.exp(a_log))
    gpar = gpar.at[1, nheads:2 * nheads].set(dt_bias)
    t = nseq * seqlen
    return pl.pallas_call(
        functools.partial(_delta_kernel, nseg=nseg, lt=lt, chained=chained, nheads=nheads),
        grid=grid,
        in_specs=[
            pl.BlockSpec((rb, 3 * width), lambda b, c: (row(b, c), qkv_col // (3 * width))),
            pl.BlockSpec((rb, width), lambda b, c: (row(b, c), z_col // width)),
            pl.BlockSpec((rb, LANE), lambda b, c: (row(b, c), 0)),
            pl.BlockSpec((nsb, SUBLANE, 3 * width), lambda b, c: (b, 0, 0)),
            pl.BlockSpec((nsb, nheads, DN_HEAD_DIM, DN_HEAD_DIM), lambda b, c: (b, 0, 0, 0)),
            pl.BlockSpec((CONV_WIDTH, 3 * width), lambda b, c: (0, 0)),
            pl.BlockSpec((SUBLANE, LANE), lambda b, c: (0, 0)),
            pl.BlockSpec((1, DN_HEAD_DIM), lambda b, c: (0, 0)),
        ],
        out_specs=[
            pl.BlockSpec((rb, width), lambda b, c: (row(b, c), 0)),
            pl.BlockSpec((nsb, tail, 3 * width), lambda b, c: (b, 0, 0)),
            pl.BlockSpec((nsb, nheads, DN_HEAD_DIM, DN_HEAD_DIM), lambda b, c: (b, 0, 0, 0)),
        ],
        out_shape=[
            jax.ShapeDtypeStruct((t, width), F32),
            jax.ShapeDtypeStruct((nseq, tail, 3 * width), F32),
            jax.ShapeDtypeStruct(s0.shape, F32),
        ],
        scratch_shapes=[
            pltpu.VMEM((rb + 2 * SUBLANE, 3 * width), F32),
            pltpu.VMEM((rb + SUBLANE, 3 * width), F32),
            pltpu.VMEM((nsb, nheads, DN_HEAD_DIM, DN_HEAD_DIM), F32),
        ],
        compiler_params=_cparams("parallel", "arbitrary"),
        name="delta_chunk" if chained else "delta_step",
    )(proj, proj, ba, cs8, s0, conv_w, gpar, onorm_w.reshape(1, DN_HEAD_DIM))


def _merge_kernel(y5_ref, ydn_ref, ga_ref, gb_ref, x_ref, wglu_ref, bglu_ref, wa_ref, wb_ref, wout_ref, gffn_ref,
                  x1_ref, h_ref):
    y = y5_ref[...]
    y = 0.5 * y * (1.0 + jnp.tanh(math.sqrt(2.0 / math.pi) * (y + 0.044715 * (y * y * y))))
    lin = jnp.dot(y.astype(BF16), wglu_ref[...], preferred_element_type=F32) + bglu_ref[...]
    glu = (y * _sigmoid(lin)).astype(BF16)
    a = jnp.dot(glu, wa_ref[...], preferred_element_type=F32)
    b = jnp.dot(ydn_ref[...].astype(BF16), wb_ref[...], preferred_element_type=F32)
    mix = (_sigmoid(ga_ref[...]) * a + _sigmoid(gb_ref[...]) * b).astype(BF16)
    x1 = x_ref[...] + jnp.dot(mix, wout_ref[...], preferred_element_type=F32)
    x1_ref[...] = x1
    h_ref[...] = _rms(x1, gffn_ref[...]).astype(BF16)


def _merge(y5, ydn, proj, x, ga_col, gb_col, w_glu, b_glu, w_a, w_b, w_out, g_ffn, tm):
    t, w5 = y5.shape
    d = x.shape[1]
    row = lambda cols: pl.BlockSpec((tm, cols), lambda i: (i, 0))
    const = lambda shape: pl.BlockSpec(shape, lambda i: (0, 0), pipeline_mode=pl.Buffered(1))
    return pl.pallas_call(
        _merge_kernel,
        grid=(t // tm,),
        in_specs=[
            row(w5), row(w5),
            pl.BlockSpec((tm, d), lambda i: (i, ga_col // d)),
            pl.BlockSpec((tm, d), lambda i: (i, gb_col // d)),
            row(d),
            const((w5, w5)), const((1, w5)), const((w5, d)), const((w5, d)), const((d, d)), const((1, d)),
        ],
        out_specs=[row(d), row(d)],
        out_shape=[jax.ShapeDtypeStruct((t, d), F32), jax.ShapeDtypeStruct((t, d), BF16)],
        compiler_params=_cparams("parallel"),
        name="merge",
    )(y5, ydn, proj, proj, x, w_glu, b_glu, w_a, w_b, w_out, g_ffn)


def _ffn_kernel(h_ref, xn_ref, wg_ref, wu_ref, wd_ref, out_ref, act_scr, *, nk, tk):
    s = pl.program_id(1)

    @pl.when(s < nk)
    def _():
        h = h_ref[...]
        gate = jnp.dot(h, wg_ref[...], preferred_element_type=F32)
        up = jnp.dot(h, wu_ref[...], preferred_element_type=F32)
        act_scr[s] = (gate * _sigmoid(gate) * up).astype(BF16)

    @pl.when(s >= nk)
    def _():
        acc = xn_ref[...]
        for kk in range(nk):
            acc = acc + jnp.dot(act_scr[kk], wd_ref[kk * tk:(kk + 1) * tk, :], preferred_element_type=F32)
        out_ref[...] = acc


def _ffn(x, h, w_gate, w_up, w_down, tm, tk, tn):
    t, d = x.shape
    dff = w_gate.shape[1]
    nk = dff // tk
    up_blk = lambda i, s: (0, jnp.minimum(s, nk - 1))
    down_blk = lambda i, s: (0, jnp.maximum(s - nk, 0))
    out_blk = lambda i, s: (i, jnp.maximum(s - nk, 0))
    return pl.pallas_call(
        functools.partial(_ffn_kernel, nk=nk, tk=tk),
        grid=(t // tm, nk + d // tn),
        in_specs=[
            pl.BlockSpec((tm, d), lambda i, s: (i, 0)),
            pl.BlockSpec((tm, tn), out_blk),
            pl.BlockSpec((d, tk), up_blk),
            pl.BlockSpec((d, tk), up_blk),
            pl.BlockSpec((dff, tn), down_blk),
        ],
        out_specs=pl.BlockSpec((tm, tn), out_blk),
        out_shape=jax.ShapeDtypeStruct((t, d), F32),
        scratch_shapes=[pltpu.VMEM((nk, tm, tk), BF16)],
        compiler_params=_cparams("parallel", "arbitrary"),
        name="ffn",
    )(h, x, w_gate, w_up, w_down)


def _ple_kernel(x_ref, p_ref, gple_ref, gfin_ref, wple_ref, wpg_ref, out_ref, h_scr, *, nj, tn, final):
    j = pl.program_id(1)

    @pl.when(j == 0)
    def _():
        h_scr[...] = _rms(x_ref[...], gple_ref[...]).astype(BF16)

    gate = _sigmoid(jnp.dot(h_scr[...], wpg_ref[...], preferred_element_type=F32))
    emb = jnp.dot(p_ref[...].astype(BF16), wple_ref[...], preferred_element_type=F32)
    upd = emb * gate
    for jj in range(nj):
        @pl.when(j == jj)
        def _(jj=jj):
            out_ref[:, jj * tn:(jj + 1) * tn] = x_ref[:, jj * tn:(jj + 1) * tn] + upd

    if final:
        @pl.when(j == nj - 1)
        def _():
            out_ref[...] = _rms(out_ref[...], gfin_ref[...])


def _ple_final(x, p, g_ple, g_final, w_ple, w_ple_gate, tm, tn, final):
    t, d = x.shape
    pd = p.shape[1]
    nj = d // tn
    return pl.pallas_call(
        functools.partial(_ple_kernel, nj=nj, tn=tn, final=final),
        grid=(t // tm, nj),
        in_specs=[
            pl.BlockSpec((tm, d), lambda i, j: (i, 0)),
            pl.BlockSpec((tm, pd), lambda i, j: (i, 0)),
            pl.BlockSpec((1, d), lambda i, j: (0, 0)),
            pl.BlockSpec((1, d), lambda i, j: (0, 0)),
            pl.BlockSpec((pd, tn), lambda i, j: (0, j)),
            pl.BlockSpec((d, tn), lambda i, j: (0, j)),
        ],
        out_specs=pl.BlockSpec((tm, d), lambda i, j: (i, 0)),
        out_shape=jax.ShapeDtypeStruct((t, d), F32),
        scratch_shapes=[pltpu.VMEM((tm, d), BF16)],
        compiler_params=_cparams("parallel", "arbitrary"),
        name="ple_final",
    )(x, p, g_ple, g_final, w_ple, w_ple_gate)


COL_U, COL_Z, COL_GA, COL_GB, COL_QKV = 0, 1024, 2048, 4096, 6144


def _prep_w_in(w_in, d_model, nheads):
    s5w = d_model // 2
    dnw = d_model // 2
    off_u = s5w
    off_qkv = off_u + 3 * dnw
    off_z = off_qkv + dnw
    off_a = off_z + 2 * nheads
    off_ga = off_a + d_model
    w_main = jnp.concatenate([w_in[:, :off_u], w_in[:, off_qkv:off_z], w_in[:, off_a:off_ga],
                              w_in[:, off_ga:], w_in[:, off_u:off_qkv]], axis=1).astype(BF16)
    w_ba = jnp.pad(w_in[:, off_z:off_a], ((0, 0), (0, LANE - 2 * nheads))).astype(BF16)
    return w_main, w_ba


def _layer(x3, p3, cstate, s0, h0, lw, tm, final):
    nseq, seqlen, d = x3.shape
    t = nseq * seqlen
    x = x3.reshape(t, d)
    proj, ba = _inproj(x, lw['g_mix'], lw['w_main'], lw['w_ba'], tm, 1536)
    y5, hre, him = _s5_apply(proj, nseq, seqlen, lw['s5_ops'], lw['s5_d'], h0)
    ydn, cnew, snew = _delta_apply(proj, ba, COL_QKV, COL_Z, nseq, seqlen, cstate, s0,
                                   lw['conv_w'], lw['a_log'], lw['dt_bias'], lw['onorm_w'])
    x1, h2 = _merge(y5, ydn, proj, x, COL_GA, COL_GB, lw['w_glu'], lw['b_glu'], lw['w_a'], lw['w_b'],
                    lw['w_out'], lw['g_ffn'], 256)
    x2 = _ffn(x1, h2, lw['w_gate'], lw['w_up'], lw['w_down'], tm, 512, 512)
    y = _ple_final(x2, p3.reshape(t, -1), lw['g_ple'], lw['g_final'], lw['w_ple'], lw['w_ple_gate'],
                   tm, 512, final)
    ng = hre.shape[-1] // S5_STATE
    return (y.reshape(nseq, seqlen, d), cnew, snew,
            hre.reshape(nseq, ng, S5_STATE), him.reshape(nseq, ng, S5_STATE))


def kernel(x_prompt, x_sample, state_conv, state_delta, state_s5_re, state_s5_im, p_prompt, p_sample, g_mix, w_in, conv_w, a_log, dt_bias, onorm_w, s5_a_re, s5_a_im, s5_b_re, s5_b_im, s5_c_re, s5_c_im, s5_d, s5_log_dt, w_glu, b_glu, w_a, w_b, w_out, g_ffn, w_gate, w_up, w_down, g_ple, w_ple, w_ple_gate, g_final):
    depth = w_in.shape[0]
    d_model = x_prompt.shape[-1]
    nheads = state_delta.shape[2]
    nb_p = x_prompt.shape[0]
    f32z = functools.partial(jnp.zeros, dtype=F32)
    yp, ys = x_prompt, x_sample
    outs_p, outs_s = [], []
    for i in range(depth):
        w_main, w_ba = _prep_w_in(w_in[i], d_model, nheads)
        lw = dict(
            g_mix=g_mix[i][None], w_main=w_main, w_ba=w_ba,
            conv_w=conv_w[i], a_log=a_log[i], dt_bias=dt_bias[i], onorm_w=onorm_w[i],
            s5_ops=_s5_operators(s5_a_re[i], s5_a_im[i], s5_b_re[i], s5_b_im[i], s5_c_re[i], s5_c_im[i],
                                 s5_log_dt[i]),
            s5_d=s5_d[i],
            w_glu=w_glu[i].astype(BF16), b_glu=b_glu[i][None], w_a=w_a[i].astype(BF16), w_b=w_b[i].astype(BF16),
            w_out=w_out[i].astype(BF16), g_ffn=g_ffn[i][None],
            w_gate=w_gate[i].astype(BF16), w_up=w_up[i].astype(BF16), w_down=w_down[i].astype(BF16),
            g_ple=g_ple[i][None], w_ple=w_ple[i].astype(BF16), w_ple_gate=w_ple_gate[i].astype(BF16),
            g_final=g_final[None],
        )
        final = i == depth - 1
        yp, c1, d1, r1, m1 = _layer(yp, p_prompt[i], f32z((nb_p,) + state_conv.shape[2:]),
                                    f32z((nb_p,) + state_delta.shape[2:]), None, lw, 1024, final)
        ys, c2, d2, r2, m2 = _layer(ys, p_sample[i], state_conv[i], state_delta[i],
                                    (state_s5_re[i], state_s5_im[i]), lw, 1024, final)
        outs_p.append((c1, d1, r1, m1))
        outs_s.append((c2, d2, r2, m2))
    stack = lambda outs, k: jnp.stack([o[k] for o in outs])
    return (yp, ys,
            stack(outs_p, 0), stack(outs_p, 1), stack(outs_p, 2), stack(outs_p, 3),
            stack(outs_s, 0), stack(outs_s, 1), stack(outs_s, 2), stack(outs_s, 3))
```

```python
import functools
import math

import jax
import jax.numpy as jnp
from jax import lax
from jax.experimental import pallas as pl
from jax.experimental.pallas import tpu as pltpu

F32 = jnp.float32
BF16 = jnp.bfloat16

EPS = 1e-6
LANE = 128
SUBLANE = 8
VMEM_LIMIT_BYTES = 56 * 1024 * 1024

S5_GROUP = 16
S5_STATE = 64
S5_ROW = SUBLANE
S5_TILE_GROUPS = LANE // S5_GROUP
S5_TILE_STATE = S5_TILE_GROUPS * S5_STATE
DN_HEAD_DIM = 128
CONV_WIDTH = 4
DN_CHUNK = 64
DN_CHUNKS_PER_STEP = 2


def _cparams(*sem):
    return pltpu.CompilerParams(dimension_semantics=sem, vmem_limit_bytes=VMEM_LIMIT_BYTES)


def _mm(a, b):
    return jnp.dot(a.astype(BF16), b.astype(BF16), preferred_element_type=F32)


def _mm_nt(a, b):
    return lax.dot_general(a.astype(BF16), b.astype(BF16), (((1,), (1,)), ((), ())),
                           preferred_element_type=F32)


def _mm_tn(a, b):
    return lax.dot_general(a.astype(BF16), b.astype(BF16), (((0,), (0,)), ((), ())),
                           preferred_element_type=F32)


def _split3(x):
    hi = x.astype(BF16)
    r1 = x - hi.astype(F32)
    mid = r1.astype(BF16)
    lo = (r1 - mid.astype(F32)).astype(BF16)
    return hi, mid, lo


def _sigmoid(x):
    return 1.0 / (1.0 + jnp.exp(-x))


def _rms(x, g):
    ms = jnp.mean(x * x, axis=-1, keepdims=True)
    return x * lax.rsqrt(ms + EPS) * g


def _inproj_kernel(x_ref, g_ref, w_ref, wba_ref, out_ref, ba_ref, h_scr):
    @pl.when(pl.program_id(1) == 0)
    def _():
        h = _rms(x_ref[...], g_ref[...]).astype(BF16)
        h_scr[...] = h
        ba_ref[...] = jnp.dot(h, wba_ref[...], preferred_element_type=F32)

    out_ref[...] = jnp.dot(h_scr[...], w_ref[...], preferred_element_type=F32)


def _inproj(x, g, w_main, w_ba, tm, tn):
    t, d = x.shape
    n = w_main.shape[1]
    return pl.pallas_call(
        _inproj_kernel,
        grid=(t // tm, n // tn),
        in_specs=[
            pl.BlockSpec((tm, d), lambda i, j: (i, 0)),
            pl.BlockSpec((1, d), lambda i, j: (0, 0)),
            pl.BlockSpec((d, tn), lambda i, j: (0, j)),
            pl.BlockSpec((d, LANE), lambda i, j: (0, 0)),
        ],
        out_specs=[
            pl.BlockSpec((tm, tn), lambda i, j: (i, j)),
            pl.BlockSpec((tm, LANE), lambda i, j: (i, 0)),
        ],
        out_shape=[jax.ShapeDtypeStruct((t, n), F32), jax.ShapeDtypeStruct((t, LANE), F32)],
        scratch_shapes=[pltpu.VMEM((tm, d), BF16)],
        compiler_params=_cparams("parallel", "arbitrary"),
        name="inproj",
    )(x, g, w_main, w_ba)


def _s5_operators(a_re, a_im, b_re, b_im, c_re, c_im, log_dt):
    g, n = a_re.shape
    nt = g // S5_TILE_GROUPS
    ns = S5_TILE_STATE
    wide = S5_ROW * LANE
    row = lambda x: x.reshape(1, g * n)
    bt = lambda x: x.transpose(2, 0, 1).reshape(S5_GROUP, g * n)
    ct = lambda x: x.transpose(1, 0, 2).reshape(S5_GROUP, g * n)
    vec = pl.BlockSpec((1, ns), lambda t: (0, t))
    mat = pl.BlockSpec((S5_GROUP, ns), lambda t: (0, t))
    return pl.pallas_call(
        _s5_ops_kernel,
        grid=(nt,),
        in_specs=[vec, vec, vec, mat, mat, mat, mat],
        out_specs=[
            pl.BlockSpec((None, wide, 2 * ns), lambda t: (t, 0, 0)),
            pl.BlockSpec((None, wide, 2 * ns), lambda t: (t, 0, 0)),
            pl.BlockSpec((None, wide, wide), lambda t: (t, 0, 0)),
            pl.BlockSpec((None, 1, 2 * ns), lambda t: (t, 0, 0)),
        ],
        out_shape=[
            jax.ShapeDtypeStruct((nt, wide, 2 * ns), BF16),
            jax.ShapeDtypeStruct((nt, wide, 2 * ns), BF16),
            jax.ShapeDtypeStruct((nt, wide, wide), BF16),
            jax.ShapeDtypeStruct((nt, 1, 2 * ns), F32),
        ],
        compiler_params=_cparams("parallel"),
        name="s5_ops",
    )(row(a_re), row(a_im), row(jnp.repeat(log_dt, n)), bt(b_re), bt(b_im), ct(c_re), ct(c_im))


def _mm_nt_split(a, b):
    ah = a.astype(BF16)
    al = (a - ah.astype(F32)).astype(BF16)
    bh = b.astype(BF16)
    bl = (b - bh.astype(F32)).astype(BF16)
    dims = (((1,), (1,)), ((), ()))
    return (lax.dot_general(ah, bh, dims, preferred_element_type=F32)
            + lax.dot_general(ah, bl, dims, preferred_element_type=F32)
            + lax.dot_general(al, bh, dims, preferred_element_type=F32))


def _s5_ops_kernel(ar_ref, ai_ref, ldt_ref, btr_ref, bti_ref, ctr_ref, cti_ref, we_ref, wyt_ref, wk_ref, lam_ref):
    ns = S5_TILE_STATE
    ar = ar_ref[...]
    ai = ai_ref[...]
    dt = jnp.exp(ldt_ref[...])
    kk = lax.broadcasted_iota(jnp.int32, (2 * SUBLANE, ns), 0).astype(F32)
    mag = jnp.exp(ar * dt * kk)
    lr = mag * jnp.cos(ai * dt * kk)
    li = mag * jnp.sin(ai * dt * kk)
    nr = lr[1:2] - 1.0
    ni = li[1:2]
    den = ar * ar + ai * ai
    cr = (nr * ar + ni * ai) / den
    ci = (ni * ar - nr * ai) / den
    btr = btr_ref[...]
    bti = bti_ref[...]
    bbr = cr * btr - ci * bti
    bbi = cr * bti + ci * btr
    ctr = ctr_ref[...]
    cti = cti_ref[...]
    same_group = (lax.broadcasted_iota(jnp.int32, (LANE, ns), 0) // S5_GROUP
                  == lax.broadcasted_iota(jnp.int32, (LANE, ns), 1) // S5_STATE)

    def blockdiag(x):
        return jnp.where(same_group, jnp.concatenate([x] * S5_TILE_GROUPS, axis=0), 0.0)

    for j in range(S5_ROW):
        k = S5_ROW - 1 - j
        er = lr[k:k + 1] * bbr - li[k:k + 1] * bbi
        ei = lr[k:k + 1] * bbi + li[k:k + 1] * bbr
        we_ref[j * LANE:(j + 1) * LANE, :] = jnp.concatenate([blockdiag(er), blockdiag(ei)], axis=1).astype(BF16)
    bq = jnp.concatenate([blockdiag(bbr), blockdiag(bbi)], axis=1)
    kblocks = []
    for k in range(S5_ROW + 1):
        mr = ctr * lr[k:k + 1] - cti * li[k:k + 1]
        mi = ctr * li[k:k + 1] + cti * lr[k:k + 1]
        wy_k = jnp.concatenate([blockdiag(mr), -blockdiag(mi)], axis=1)
        if k >= 1:
            wyt_ref[(k - 1) * LANE:k * LANE, :] = wy_k.astype(BF16)
        if k < S5_ROW:
            kblocks.append(_mm_nt_split(bq, wy_k).astype(BF16))
    zero = jnp.zeros((LANE, LANE), BF16)
    for i in range(S5_ROW):
        for j in range(S5_ROW):
            wk_ref[i * LANE:(i + 1) * LANE, j * LANE:(j + 1) * LANE] = kblocks[j - i] if j >= i else zero
    lam_ref[...] = jnp.concatenate([lr[S5_ROW:S5_ROW + 1], li[S5_ROW:S5_ROW + 1]], axis=1)


def _s5_kernel(*refs, scan, nseq, rps):
    nu = S5_ROW
    rows = nseq * rps
    u_ref, we_ref, wy_ref, wk_ref, d_ref, lam_ref = refs[:6]
    pos = 6
    if not scan:
        h0r_ref, h0i_ref = refs[pos:pos + 2]
        pos += 2
    y_ref, hre_ref, him_ref = refs[pos:pos + 3]
    scratch = refs[pos + 3:]

    ns = S5_TILE_STATE
    us = [u_ref[pl.ds(j, rows, stride=nu), :] for j in range(nu)]
    u = jnp.concatenate(us, axis=1)
    ub = u.astype(BF16)
    e = jnp.dot(ub, we_ref[...], preferred_element_type=F32)
    lam = lam_ref[...]
    lr = lam[:, :ns]
    li = lam[:, ns:]
    if scan:
        e_scr, h_scr = scratch
        nslab = 2 * ns // LANE
        for k in range(nslab):
            e_scr[k] = e[:, k * LANE:(k + 1) * LANE]

        def body(c, carry):
            hr, hi = carry
            hcat = jnp.concatenate([hr, hi], axis=1)
            for k in range(nslab):
                h_scr[k, pl.ds(c, nseq, stride=rps), :] = hcat[:, k * LANE:(k + 1) * LANE]
            ec = jnp.concatenate([e_scr[k, pl.ds(c, nseq, stride=rps), :] for k in range(nslab)], axis=1)
            return (lr * hr - li * hi + ec[:, :ns], lr * hi + li * hr + ec[:, ns:])

        zero = jnp.zeros((nseq, ns), F32)
        hr, hi = lax.fori_loop(0, rps, body, (zero, zero))
        hre_ref[...] = hr
        him_ref[...] = hi
        hin = jnp.concatenate([h_scr[k] for k in range(nslab)], axis=1)
    else:
        h0r = h0r_ref[...]
        h0i = h0i_ref[...]
        hre_ref[...] = lr * h0r - li * h0i + e[:, :ns]
        him_ref[...] = lr * h0i + li * h0r + e[:, ns:]
        hin = jnp.concatenate([h0r, h0i], axis=1)
    y = (lax.dot_general(hin.astype(BF16), wy_ref[...], (((1,), (1,)), ((), ())), preferred_element_type=F32)
         + jnp.dot(ub, wk_ref[...], preferred_element_type=F32))
    d = d_ref[...]
    for j in range(nu):
        y_ref[pl.ds(j, rows, stride=nu), :] = y[:, j * LANE:(j + 1) * LANE] + d * us[j]


def _s5_apply(proj, nseq, seqlen, ops, d, h0=None):
    we_t, wy_t, wk_t, lam = ops
    nt = we_t.shape[0]
    width = nt * LANE
    t = nseq * seqlen
    scan = h0 is None
    rps = seqlen // S5_ROW
    assert scan or rps == 1
    ns2 = 2 * S5_TILE_STATE
    tile3 = lambda shape: pl.BlockSpec((None,) + shape, lambda tt: (tt, 0, 0))
    state = pl.BlockSpec((nseq, S5_TILE_STATE), lambda tt: (0, tt))
    in_specs = [
        pl.BlockSpec((t, LANE), lambda tt: (0, tt)),
        tile3((S5_ROW * LANE, ns2)),
        tile3((ns2, S5_ROW * LANE)),
        tile3((S5_ROW * LANE, S5_ROW * LANE)),
        tile3((1, LANE)),
        tile3((1, ns2)),
    ]
    args = [proj, we_t, wy_t, wk_t, d.reshape(nt, 1, LANE), lam]
    scratch = []
    if scan:
        scratch = [pltpu.VMEM((ns2 // LANE, nseq * rps, LANE), F32)] * 2
    else:
        in_specs += [state, state]
        args += [h0[0].reshape(nseq, nt * S5_TILE_STATE), h0[1].reshape(nseq, nt * S5_TILE_STATE)]
    return pl.pallas_call(
        functools.partial(_s5_kernel, scan=scan, nseq=nseq, rps=rps),
        grid=(nt,),
        in_specs=in_specs,
        out_specs=[pl.BlockSpec((t, LANE), lambda tt: (0, tt)), state, state],
        out_shape=[jax.ShapeDtypeStruct((t, width), F32)]
        + [jax.ShapeDtypeStruct((nseq, nt * S5_TILE_STATE), F32)] * 2,
        scratch_shapes=scratch,
        compiler_params=_cparams("parallel"),
        name="s5_scan" if scan else "s5_step",
    )(*args)


def _delta_kernel(qkv_ref, qkvn_ref, z_ref, ba_ref, cs_ref, s0_ref, convw_ref, gpar_ref, onw_ref,
                  y_ref, cnew_ref, snew_ref, ext_scr, csx_scr, s_scr, xs_scr, *, nseg, lt, chained, nheads):
    rb = nseg * lt
    nsb = 1 if chained else nseg
    hd = DN_HEAD_DIM
    width = nheads * hd
    c = pl.program_id(1)
    nc = pl.num_programs(1)
    tail = CONV_WIDTH - 1
    cw = convw_ref[...]

    def prepare_head(x_ref, dst_slot, h):
        if nsb > 1:
            tl = lax.broadcasted_iota(jnp.int32, (rb, 1), 0) % lt
        for part in range(3):
            cs = slice(part * width + h * hd, part * width + (h + 1) * hd)
            x = x_ref[:, cs]
            ext_scr[SUBLANE:SUBLANE + rb, cs] = x
            acc = x * cw[tail:tail + 1, cs]
            for k in range(1, CONV_WIDTH):
                xk = ext_scr[SUBLANE - k:SUBLANE - k + rb, cs]
                if nsb > 1:
                    xk = jnp.where(tl < k, csx_scr[SUBLANE - k:SUBLANE - k + rb, cs], xk)
                acc = acc + xk * cw[tail - k:tail - k + 1, cs]
            if nsb == 1:
                ext_scr[0:SUBLANE, cs] = ext_scr[rb:rb + SUBLANE, cs]
            xh = acc * _sigmoid(acc)
            if part < 2:
                scale = hd ** -0.5 if part == 0 else 1.0
                xh = xh * (lax.rsqrt(jnp.sum(xh * xh, axis=-1, keepdims=True) + EPS) * scale)
            xs_scr[dst_slot, :, cs] = xh

    @pl.when(c == 0)
    def _init():
        s_scr[...] = s0_ref[...]
        if chained:
            ext_scr[0:SUBLANE, :] = cs_ref[0]
            for h in range(nheads):
                prepare_head(qkv_ref, 0, h)
        else:
            ext_scr[0:SUBLANE, :] = jnp.zeros((SUBLANE, 3 * width), F32)

    if chained:
        slot = c % 2
        pending = list(range(nheads))

        def tick():
            if pending:
                prepare_head(qkvn_ref, 1 - slot, pending.pop(0))

        @pl.when(c == nc - 1)
        def _conv_out():
            cnew_ref[0] = qkvn_ref[rb - tail:rb, :]
    else:
        csx_scr[0:rb, :] = cs_ref[...].reshape(rb, 3 * width)
        csx_scr[rb:rb + SUBLANE, :] = jnp.zeros((SUBLANE, 3 * width), F32)
        slot = 0
        for h in range(nheads):
            prepare_head(qkv_ref, 0, h)
        for s in range(nsb):
            r0 = SUBLANE + (s + 1) * lt - tail
            cnew_ref[s] = ext_scr[r0:r0 + tail, :]

        def tick():
            pass
    xs = xs_scr[slot]

    ba = ba_ref[...]
    gpar = gpar_ref[...]
    beta_all = _sigmoid(ba)
    xg = ba + gpar[1:2, :]
    g_all = gpar[0:1, :] * (jnp.maximum(xg, 0.0) + jnp.log(1.0 + jnp.exp(-jnp.abs(xg))))

    ri = lax.broadcasted_iota(jnp.int32, (rb, rb), 0)
    ci = lax.broadcasted_iota(jnp.int32, (rb, rb), 1)
    same = (ri // lt) == (ci // lt)
    incl = (ri >= ci) & same
    strict = (ri > ci) & same
    tri = jnp.where(incl, 1.0, 0.0).astype(BF16)
    ghi, gmid, glo = _split3(g_all)
    gc_col = (jnp.dot(tri, ghi, preferred_element_type=F32)
              + jnp.dot(tri, gmid, preferred_element_type=F32)
              + jnp.dot(tri, glo, preferred_element_type=F32))
    sel = jnp.where(lax.broadcasted_iota(jnp.int32, (2 * SUBLANE, LANE), 1)
                    == lax.broadcasted_iota(jnp.int32, (2 * SUBLANE, LANE), 0) + nheads, 1.0, 0.0).astype(BF16)
    chi, cmid, clo = _split3(gc_col)
    nt_dims = (((1,), (1,)), ((), ()))
    gc_row = (lax.dot_general(sel, chi, nt_dims, preferred_element_type=F32)
              + lax.dot_general(sel, cmid, nt_dims, preferred_element_type=F32)
              + lax.dot_general(sel, clo, nt_dims, preferred_element_type=F32))
    lastsel = jnp.where(same & ((ci % lt) == lt - 1), 1.0, 0.0).astype(BF16)
    glast_col = (jnp.dot(lastsel, chi, preferred_element_type=F32)
                 + jnp.dot(lastsel, cmid, preferred_element_type=F32)
                 + jnp.dot(lastsel, clo, preferred_element_type=F32))

    eye = jnp.where(ri == ci, 1.0, 0.0)
    onw = onw_ref[...]
    n_sq = max(int(math.log2(lt)) - 1, 0)
    heads = range(nheads)
    rowseq = lax.broadcasted_iota(jnp.int32, (rb, 1), 0) // lt
    q, k, v, beta, gcc, glc, decay = [], [], [], [], [], [], []
    for h in heads:
        q.append(xs[:, h * hd:(h + 1) * hd])
        k.append(xs[:, width + h * hd:width + (h + 1) * hd])
        v.append(xs[:, 2 * width + h * hd:2 * width + (h + 1) * hd])
        beta.append(beta_all[:, h:h + 1])
        gcc.append(gc_col[:, nheads + h:nheads + h + 1])
        glc.append(glast_col[:, nheads + h:nheads + h + 1])
        gcr = gc_row[h:h + 1, :]
        decay.append(jnp.where(incl, jnp.exp(jnp.where(incl, gcc[h] - gcr, 0.0)), 0.0))
    qk_kk = [_mm_nt(jnp.concatenate([q[h], k[h]], axis=0), k[h]) for h in heads]
    tick()
    qk = [qk_kk[h][:rb] * decay[h] for h in heads]
    a = [jnp.where(strict, beta[h] * qk_kk[h][rb:] * decay[h], 0.0) for h in heads]
    tm = [eye - a[h] for h in heads]
    if n_sq > 0:
        bpow = [_mm(a[h], a[h]) for h in heads]
        tick()
    for r in range(n_sq):
        if r == n_sq - 1:
            tm = [tm[h] + _mm(tm[h], bpow[h]) for h in heads]
        else:
            nxt = [_mm(jnp.concatenate([tm[h], bpow[h]], axis=0), bpow[h]) for h in heads]
            bpow = [nxt[h][rb:] for h in heads]
            tm = [tm[h] + nxt[h][:rb] for h in heads]
        tick()
    egc = [jnp.exp(gcc[h]) for h in heads]
    uw = [_mm(tm[h], jnp.concatenate([v[h] * beta[h], k[h] * (beta[h] * egc[h])], axis=1)) for h in heads]
    for _ in heads:
        tick()
    u = [uw[h][:, :hd] for h in heads]
    w = [uw[h][:, hd:] for h in heads]
    qe = [q[h] * egc[h] for h in heads]
    ks = [k[h] * jnp.exp(glc[h] - gcc[h]) for h in heads]
    if chained:
        st = [s_scr[0, h] for h in heads]
        o_parts = [[] for _ in heads]
        for sg in range(nseg):
            rows = slice(sg * lt, (sg + 1) * lt)
            vnew = [u[h][rows] - _mm(w[h][rows], st[h]) for h in heads]
            above = [jnp.zeros((sg * lt, hd), F32)] if sg > 0 else []
            below = [jnp.zeros(((nseg - 1 - sg) * lt, hd), F32)] if sg < nseg - 1 else []
            vpad = [jnp.concatenate(above + [vnew[h]] + below, axis=0) if nseg > 1 else vnew[h] for h in heads]
            for h in heads:
                o_parts[h].append(_mm(qe[h][rows], st[h]) + _mm(qk[h][rows], vpad[h]))
            st = [st[h] * jnp.exp(glc[h][sg * lt:sg * lt + 1, :]) + _mm_tn(ks[h][rows], vnew[h]) for h in heads]
        for h in heads:
            s_scr[0, h] = st[h]
        o = [jnp.concatenate(o_parts[h], axis=0) if nseg > 1 else o_parts[h][0] for h in heads]
    else:
        ws, qs = [], []
        for h in heads:
            parts = [_mm(jnp.concatenate([w[h][s * lt:(s + 1) * lt], qe[h][s * lt:(s + 1) * lt]], axis=0),
                         s_scr[s, h]) for s in range(nsb)]
            ws.append(jnp.concatenate([p[:lt] for p in parts], axis=0))
            qs.append(jnp.concatenate([p[lt:] for p in parts], axis=0))
        vnew = [u[h] - ws[h] for h in heads]
        o = [qs[h] + _mm(qk[h], vnew[h]) for h in heads]
        for h in heads:
            for s in range(nsb):
                ksm = jnp.where(rowseq == s, ks[h], 0.0)
                s_scr[s, h] = s_scr[s, h] * jnp.exp(glc[h][s * lt:s * lt + 1, :]) + _mm_tn(ksm, vnew[h])
    for h in heads:
        oh = o[h] * lax.rsqrt(jnp.mean(o[h] * o[h], axis=-1, keepdims=True) + EPS) * onw
        zh = z_ref[:, h * hd:(h + 1) * hd]
        y_ref[:, h * hd:(h + 1) * hd] = oh * (zh * _sigmoid(zh))

    @pl.when(c == nc - 1)
    def _state_out():
        snew_ref[...] = s_scr[...]


def _delta_apply(proj, ba, qkv_col, z_col, nseq, seqlen, cstate, s0, conv_w, a_log, dt_bias, onorm_w):
    nheads = s0.shape[1]
    width = nheads * DN_HEAD_DIM
    tail = CONV_WIDTH - 1
    chained = seqlen >= DN_CHUNK
    if chained:
        lt, nseg, nsb = DN_CHUNK, DN_CHUNKS_PER_STEP, 1
        rb = nseg * lt
        nc = seqlen // rb
        grid = (nseq, nc)
        row = lambda b, c: b * nc + c
    else:
        assert seqlen == SUBLANE
        lt, nseg = seqlen, DN_CHUNK // seqlen
        nsb = nseg
        rb = nseg * lt
        assert nseq % nsb == 0
        nc = 1
        grid = (nseq // nsb, 1)
        row = lambda b, c: b
    cs8 = jnp.pad(cstate, ((0, 0), (SUBLANE - tail, 0), (0, 0)))
    gpar = jnp.zeros((SUBLANE, LANE), F32)
    gpar = gpar.at[0, nheads:2 * nheads].set(-jnp.exp(a_log))
    gpar = gpar.at[1, nheads:2 * nheads].set(dt_bias)
    t = nseq * seqlen
    return pl.pallas_call(
        functools.partial(_delta_kernel, nseg=nseg, lt=lt, chained=chained, nheads=nheads),
        grid=grid,
        in_specs=[
            pl.BlockSpec((rb, 3 * width), lambda b, c: (row(b, c), qkv_col // (3 * width))),
            pl.BlockSpec((rb, 3 * width), lambda b, c: (row(b, jnp.minimum(c + 1, nc - 1)), qkv_col // (3 * width))),
            pl.BlockSpec((rb, width), lambda b, c: (row(b, c), z_col // width)),
            pl.BlockSpec((rb, LANE), lambda b, c: (row(b, c), 0)),
            pl.BlockSpec((nsb, SUBLANE, 3 * width), lambda b, c: (b, 0, 0)),
            pl.BlockSpec((nsb, nheads, DN_HEAD_DIM, DN_HEAD_DIM), lambda b, c: (b, 0, 0, 0)),
            pl.BlockSpec((CONV_WIDTH, 3 * width), lambda b, c: (0, 0)),
            pl.BlockSpec((SUBLANE, LANE), lambda b, c: (0, 0)),
            pl.BlockSpec((1, DN_HEAD_DIM), lambda b, c: (0, 0)),
        ],
        out_specs=[
            pl.BlockSpec((rb, width), lambda b, c: (row(b, c), 0)),
            pl.BlockSpec((nsb, tail, 3 * width), lambda b, c: (b, 0, 0)),
            pl.BlockSpec((nsb, nheads, DN_HEAD_DIM, DN_HEAD_DIM), lambda b, c: (b, 0, 0, 0)),
        ],
        out_shape=[
            jax.ShapeDtypeStruct((t, width), F32),
            jax.ShapeDtypeStruct((nseq, tail, 3 * width), F32),
            jax.ShapeDtypeStruct(s0.shape, F32),
        ],
        scratch_shapes=[
            pltpu.VMEM((rb + 2 * SUBLANE, 3 * width), F32),
            pltpu.VMEM((rb + SUBLANE, 3 * width), F32),
            pltpu.VMEM((nsb, nheads, DN_HEAD_DIM, DN_HEAD_DIM), F32),
            pltpu.VMEM((2, rb, 3 * width), F32),
        ],
        compiler_params=_cparams("parallel", "arbitrary"),
        name="delta_chunk" if chained else "delta_step",
    )(proj, proj, proj, ba, cs8, s0, conv_w, gpar, onorm_w.reshape(1, DN_HEAD_DIM))


def _merge_kernel(y5_ref, ydn_ref, ga_ref, gb_ref, x_ref, wglu_ref, bglu_ref, wa_ref, wb_ref, wout_ref, gffn_ref,
                  x1_ref, h_ref):
    y = y5_ref[...]
    y = 0.5 * y * (1.0 + jnp.tanh(math.sqrt(2.0 / math.pi) * (y + 0.044715 * (y * y * y))))
    lin = jnp.dot(y.astype(BF16), wglu_ref[...], preferred_element_type=F32) + bglu_ref[...]
    glu = (y * _sigmoid(lin)).astype(BF16)
    a = jnp.dot(glu, wa_ref[...], preferred_element_type=F32)
    b = jnp.dot(ydn_ref[...].astype(BF16), wb_ref[...], preferred_element_type=F32)
    mix = (_sigmoid(ga_ref[...]) * a + _sigmoid(gb_ref[...]) * b).astype(BF16)
    x1 = x_ref[...] + jnp.dot(mix, wout_ref[...], preferred_element_type=F32)
    x1_ref[...] = x1
    h_ref[...] = _rms(x1, gffn_ref[...]).astype(BF16)


def _merge(y5, ydn, proj, x, ga_col, gb_col, w_glu, b_glu, w_a, w_b, w_out, g_ffn, tm):
    t, w5 = y5.shape
    d = x.shape[1]
    row = lambda cols: pl.BlockSpec((tm, cols), lambda i: (i, 0))
    const = lambda shape: pl.BlockSpec(shape, lambda i: (0, 0), pipeline_mode=pl.Buffered(1))
    return pl.pallas_call(
        _merge_kernel,
        grid=(t // tm,),
        in_specs=[
            row(w5), row(w5),
            pl.BlockSpec((tm, d), lambda i: (i, ga_col // d)),
            pl.BlockSpec((tm, d), lambda i: (i, gb_col // d)),
            row(d),
            const((w5, w5)), const((1, w5)), const((w5, d)), const((w5, d)), const((d, d)), const((1, d)),
        ],
        out_specs=[row(d), row(d)],
        out_shape=[jax.ShapeDtypeStruct((t, d), F32), jax.ShapeDtypeStruct((t, d), BF16)],
        compiler_params=_cparams("parallel"),
        name="merge",
    )(y5, ydn, proj, proj, x, w_glu, b_glu, w_a, w_b, w_out, g_ffn)


def _ffn_kernel(h_ref, xn_ref, wg_ref, wu_ref, wd_ref, out_ref, act_scr, *, nk, tk):
    s = pl.program_id(1)

    @pl.when(s < nk)
    def _():
        h = h_ref[...]
        gate = jnp.dot(h, wg_ref[...], preferred_element_type=F32)
        up = jnp.dot(h, wu_ref[...], preferred_element_type=F32)
        act_scr[s] = (gate * _sigmoid(gate) * up).astype(BF16)

    @pl.when(s >= nk)
    def _():
        acc = xn_ref[...]
        for kk in range(nk):
            acc = acc + jnp.dot(act_scr[kk], wd_ref[kk * tk:(kk + 1) * tk, :], preferred_element_type=F32)
        out_ref[...] = acc


def _ffn(x, h, w_gate, w_up, w_down, tm, tk, tn):
    t, d = x.shape
    dff = w_gate.shape[1]
    nk = dff // tk
    up_blk = lambda i, s: (0, jnp.minimum(s, nk - 1))
    down_blk = lambda i, s: (0, jnp.maximum(s - nk, 0))
    out_blk = lambda i, s: (i, jnp.maximum(s - nk, 0))
    return pl.pallas_call(
        functools.partial(_ffn_kernel, nk=nk, tk=tk),
        grid=(t // tm, nk + d // tn),
        in_specs=[
            pl.BlockSpec((tm, d), lambda i, s: (i, 0)),
            pl.BlockSpec((tm, tn), out_blk),
            pl.BlockSpec((d, tk), up_blk),
            pl.BlockSpec((d, tk), up_blk),
            pl.BlockSpec((dff, tn), down_blk),
        ],
        out_specs=pl.BlockSpec((tm, tn), out_blk),
        out_shape=jax.ShapeDtypeStruct((t, d), F32),
        scratch_shapes=[pltpu.VMEM((nk, tm, tk), BF16)],
        compiler_params=_cparams("parallel", "arbitrary"),
        name="ffn",
    )(h, x, w_gate, w_up, w_down)


def _ple_kernel(x_ref, p_ref, gple_ref, gfin_ref, wple_ref, wpg_ref, out_ref, h_scr, *, nj, tn, final):
    j = pl.program_id(1)

    @pl.when(j == 0)
    def _():
        h_scr[...] = _rms(x_ref[...], gple_ref[...]).astype(BF16)

    gate = _sigmoid(jnp.dot(h_scr[...], wpg_ref[...], preferred_element_type=F32))
    emb = jnp.dot(p_ref[...].astype(BF16), wple_ref[...], preferred_element_type=F32)
    upd = emb * gate
    for jj in range(nj):
        @pl.when(j == jj)
        def _(jj=jj):
            out_ref[:, jj * tn:(jj + 1) * tn] = x_ref[:, jj * tn:(jj + 1) * tn] + upd

    if final:
        @pl.when(j == nj - 1)
        def _():
            out_ref[...] = _rms(out_ref[...], gfin_ref[...])


def _ple_final(x, p, g_ple, g_final, w_ple, w_ple_gate, tm, tn, final):
    t, d = x.shape
    pd = p.shape[1]
    nj = d // tn
    return pl.pallas_call(
        functools.partial(_ple_kernel, nj=nj, tn=tn, final=final),
        grid=(t // tm, nj),
        in_specs=[
            pl.BlockSpec((tm, d), lambda i, j: (i, 0)),
            pl.BlockSpec((tm, pd), lambda i, j: (i, 0)),
            pl.BlockSpec((1, d), lambda i, j: (0, 0)),
            pl.BlockSpec((1, d), lambda i, j: (0, 0)),
            pl.BlockSpec((pd, tn), lambda i, j: (0, j)),
            pl.BlockSpec((d, tn), lambda i, j: (0, j)),
        ],
        out_specs=pl.BlockSpec((tm, d), lambda i, j: (i, 0)),
        out_shape=jax.ShapeDtypeStruct((t, d), F32),
        scratch_shapes=[pltpu.VMEM((tm, d), BF16)],
        compiler_params=_cparams("parallel", "arbitrary"),
        name="ple_final",
    )(x, p, g_ple, g_final, w_ple, w_ple_gate)


COL_U, COL_Z, COL_GA, COL_GB, COL_QKV = 0, 1024, 2048, 4096, 6144


def _prep_w_in(w_in, d_model, nheads):
    s5w = d_model // 2
    dnw = d_model // 2
    off_u = s5w
    off_qkv = off_u + 3 * dnw
    off_z = off_qkv + dnw
    off_a = off_z + 2 * nheads
    off_ga = off_a + d_model
    w_main = jnp.concatenate([w_in[:, :off_u], w_in[:, off_qkv:off_z], w_in[:, off_a:off_ga],
                              w_in[:, off_ga:], w_in[:, off_u:off_qkv]], axis=1).astype(BF16)
    w_ba = jnp.pad(w_in[:, off_z:off_a], ((0, 0), (0, LANE - 2 * nheads))).astype(BF16)
    return w_main, w_ba


def _layer(x3, p3, cstate, s0, h0, lw, tm, final):
    nseq, seqlen, d = x3.shape
    t = nseq * seqlen
    x = x3.reshape(t, d)
    proj, ba = _inproj(x, lw['g_mix'], lw['w_main'], lw['w_ba'], tm, 1536)
    y5, hre, him = _s5_apply(proj, nseq, seqlen, lw['s5_ops'], lw['s5_d'], h0)
    ydn, cnew, snew = _delta_apply(proj, ba, COL_QKV, COL_Z, nseq, seqlen, cstate, s0,
                                   lw['conv_w'], lw['a_log'], lw['dt_bias'], lw['onorm_w'])
    x1, h2 = _merge(y5, ydn, proj, x, COL_GA, COL_GB, lw['w_glu'], lw['b_glu'], lw['w_a'], lw['w_b'],
                    lw['w_out'], lw['g_ffn'], 256)
    x2 = _ffn(x1, h2, lw['w_gate'], lw['w_up'], lw['w_down'], tm, 512, 512)
    y = _ple_final(x2, p3.reshape(t, -1), lw['g_ple'], lw['g_final'], lw['w_ple'], lw['w_ple_gate'],
                   tm, 512, final)
    ng = hre.shape[-1] // S5_STATE
    return (y.reshape(nseq, seqlen, d), cnew, snew,
            hre.reshape(nseq, ng, S5_STATE), him.reshape(nseq, ng, S5_STATE))


def kernel(x_prompt, x_sample, state_conv, state_delta, state_s5_re, state_s5_im, p_prompt, p_sample, g_mix, w_in, conv_w, a_log, dt_bias, onorm_w, s5_a_re, s5_a_im, s5_b_re, s5_b_im, s5_c_re, s5_c_im, s5_d, s5_log_dt, w_glu, b_glu, w_a, w_b, w_out, g_ffn, w_gate, w_up, w_down, g_ple, w_ple, w_ple_gate, g_final):
    depth = w_in.shape[0]
    d_model = x_prompt.shape[-1]
    nheads = state_delta.shape[2]
    nb_p = x_prompt.shape[0]
    f32z = functools.partial(jnp.zeros, dtype=F32)
    yp, ys = x_prompt, x_sample
    outs_p, outs_s = [], []
    for i in range(depth):
        w_main, w_ba = _prep_w_in(w_in[i], d_model, nheads)
        lw = dict(
            g_mix=g_mix[i][None], w_main=w_main, w_ba=w_ba,
            conv_w=conv_w[i], a_log=a_log[i], dt_bias=dt_bias[i], onorm_w=onorm_w[i],
            s5_ops=_s5_operators(s5_a_re[i], s5_a_im[i], s5_b_re[i], s5_b_im[i], s5_c_re[i], s5_c_im[i],
                                 s5_log_dt[i]),
            s5_d=s5_d[i],
            w_glu=w_glu[i].astype(BF16), b_glu=b_glu[i][None], w_a=w_a[i].astype(BF16), w_b=w_b[i].astype(BF16),
            w_out=w_out[i].astype(BF16), g_ffn=g_ffn[i][None],
            w_gate=w_gate[i].astype(BF16), w_up=w_up[i].astype(BF16), w_down=w_down[i].astype(BF16),
            g_ple=g_ple[i][None], w_ple=w_ple[i].astype(BF16), w_ple_gate=w_ple_gate[i].astype(BF16),
            g_final=g_final[None],
        )
        final = i == depth - 1
        yp, c1, d1, r1, m1 = _layer(yp, p_prompt[i], f32z((nb_p,) + state_conv.shape[2:]),
                                    f32z((nb_p,) + state_delta.shape[2:]), None, lw, 1024, final)
        ys, c2, d2, r2, m2 = _layer(ys, p_sample[i], state_conv[i], state_delta[i],
                                    (state_s5_re[i], state_s5_im[i]), lw, 1024, final)
        outs_p.append((c1, d1, r1, m1))
        outs_s.append((c2, d2, r2, m2))
    stack = lambda outs, k: jnp.stack([o[k] for o in outs])
    return (yp, ys,
            stack(outs_p, 0), stack(outs_p, 1), stack(outs_p, 2), stack(outs_p, 3),
            stack(outs_s, 0), stack(outs_s, 1), stack(outs_s, 2), stack(outs_s, 3))
```

```python
import functools
import math

import jax
import jax.numpy as jnp
from jax import lax
from jax.experimental import pallas as pl
from jax.experimental.pallas import tpu as pltpu

F32 = jnp.float32
BF16 = jnp.bfloat16

EPS = 1e-6
LANE = 128
SUBLANE = 8
VMEM_LIMIT_BYTES = 56 * 1024 * 1024

S5_GROUP = 16
S5_STATE = 64
S5_ROW = SUBLANE
S5_TILE_GROUPS = LANE // S5_GROUP
S5_TILE_STATE = S5_TILE_GROUPS * S5_STATE
DN_HEAD_DIM = 128
CONV_WIDTH = 4
DN_CHUNK = 64
DN_CHUNKS_PER_STEP = 2


def _cparams(*sem):
    return pltpu.CompilerParams(dimension_semantics=sem, vmem_limit_bytes=VMEM_LIMIT_BYTES)


def _mm(a, b):
    return jnp.dot(a.astype(BF16), b.astype(BF16), preferred_element_type=F32)


def _mm_nt(a, b):
    return lax.dot_general(a.astype(BF16), b.astype(BF16), (((1,), (1,)), ((), ())),
                           preferred_element_type=F32)


def _mm_tn(a, b):
    return lax.dot_general(a.astype(BF16), b.astype(BF16), (((0,), (0,)), ((), ())),
                           preferred_element_type=F32)


def _split3(x):
    hi = x.astype(BF16)
    r1 = x - hi.astype(F32)
    mid = r1.astype(BF16)
    lo = (r1 - mid.astype(F32)).astype(BF16)
    return hi, mid, lo


def _sigmoid(x):
    return 1.0 / (1.0 + jnp.exp(-x))


def _rms(x, g):
    ms = jnp.mean(x * x, axis=-1, keepdims=True)
    return x * lax.rsqrt(ms + EPS) * g


def _inproj_kernel(x_ref, g_ref, w_ref, wba_ref, out_ref, ba_ref, h_scr):
    @pl.when(pl.program_id(1) == 0)
    def _():
        h = _rms(x_ref[...], g_ref[...]).astype(BF16)
        h_scr[...] = h
        ba_ref[...] = jnp.dot(h, wba_ref[...], preferred_element_type=F32)

    out_ref[...] = jnp.dot(h_scr[...], w_ref[...], preferred_element_type=F32)


def _inproj(x, g, w_main, w_ba, tm, tn):
    t, d = x.shape
    n = w_main.shape[1]
    return pl.pallas_call(
        _inproj_kernel,
        grid=(t // tm, n // tn),
        in_specs=[
            pl.BlockSpec((tm, d), lambda i, j: (i, 0)),
            pl.BlockSpec((1, d), lambda i, j: (0, 0)),
            pl.BlockSpec((d, tn), lambda i, j: (0, j)),
            pl.BlockSpec((d, LANE), lambda i, j: (0, 0)),
        ],
        out_specs=[
            pl.BlockSpec((tm, tn), lambda i, j: (i, j)),
            pl.BlockSpec((tm, LANE), lambda i, j: (i, 0)),
        ],
        out_shape=[jax.ShapeDtypeStruct((t, n), F32), jax.ShapeDtypeStruct((t, LANE), F32)],
        scratch_shapes=[pltpu.VMEM((tm, d), BF16)],
        compiler_params=_cparams("parallel", "arbitrary"),
        name="inproj",
    )(x, g, w_main, w_ba)


def _s5_operators(a_re, a_im, b_re, b_im, c_re, c_im, log_dt):
    g, n = a_re.shape
    nt = g // S5_TILE_GROUPS
    ns = S5_TILE_STATE
    wide = S5_ROW * LANE
    row = lambda x: x.reshape(1, g * n)
    bt = lambda x: x.transpose(2, 0, 1).reshape(S5_GROUP, g * n)
    ct = lambda x: x.transpose(1, 0, 2).reshape(S5_GROUP, g * n)
    vec = pl.BlockSpec((1, ns), lambda t: (0, t))
    mat = pl.BlockSpec((S5_GROUP, ns), lambda t: (0, t))
    return pl.pallas_call(
        _s5_ops_kernel,
        grid=(nt,),
        in_specs=[vec, vec, vec, mat, mat, mat, mat],
        out_specs=[
            pl.BlockSpec((None, wide, 2 * ns), lambda t: (t, 0, 0)),
            pl.BlockSpec((None, wide, 2 * ns), lambda t: (t, 0, 0)),
            pl.BlockSpec((None, wide, wide), lambda t: (t, 0, 0)),
            pl.BlockSpec((None, 1, 2 * ns), lambda t: (t, 0, 0)),
        ],
        out_shape=[
            jax.ShapeDtypeStruct((nt, wide, 2 * ns), BF16),
            jax.ShapeDtypeStruct((nt, wide, 2 * ns), BF16),
            jax.ShapeDtypeStruct((nt, wide, wide), BF16),
            jax.ShapeDtypeStruct((nt, 1, 2 * ns), F32),
        ],
        compiler_params=_cparams("parallel"),
        name="s5_ops",
    )(row(a_re), row(a_im), row(jnp.repeat(log_dt, n)), bt(b_re), bt(b_im), ct(c_re), ct(c_im))


def _mm_nt_split(a, b):
    ah = a.astype(BF16)
    al = (a - ah.astype(F32)).astype(BF16)
    bh = b.astype(BF16)
    bl = (b - bh.astype(F32)).astype(BF16)
    dims = (((1,), (1,)), ((), ()))
    return (lax.dot_general(ah, bh, dims, preferred_element_type=F32)
            + lax.dot_general(ah, bl, dims, preferred_element_type=F32)
            + lax.dot_general(al, bh, dims, preferred_element_type=F32))


def _s5_ops_kernel(ar_ref, ai_ref, ldt_ref, btr_ref, bti_ref, ctr_ref, cti_ref, we_ref, wyt_ref, wk_ref, lam_ref):
    ns = S5_TILE_STATE
    ar = ar_ref[...]
    ai = ai_ref[...]
    dt = jnp.exp(ldt_ref[...])
    kk = lax.broadcasted_iota(jnp.int32, (2 * SUBLANE, ns), 0).astype(F32)
    mag = jnp.exp(ar * dt * kk)
    lr = mag * jnp.cos(ai * dt * kk)
    li = mag * jnp.sin(ai * dt * kk)
    nr = lr[1:2] - 1.0
    ni = li[1:2]
    den = ar * ar + ai * ai
    cr = (nr * ar + ni * ai) / den
    ci = (ni * ar - nr * ai) / den
    btr = btr_ref[...]
    bti = bti_ref[...]
    bbr = cr * btr - ci * bti
    bbi = cr * bti + ci * btr
    ctr = ctr_ref[...]
    cti = cti_ref[...]
    same_group = (lax.broadcasted_iota(jnp.int32, (LANE, ns), 0) // S5_GROUP
                  == lax.broadcasted_iota(jnp.int32, (LANE, ns), 1) // S5_STATE)

    def blockdiag(x):
        return jnp.where(same_group, jnp.concatenate([x] * S5_TILE_GROUPS, axis=0), 0.0)

    for j in range(S5_ROW):
        k = S5_ROW - 1 - j
        er = lr[k:k + 1] * bbr - li[k:k + 1] * bbi
        ei = lr[k:k + 1] * bbi + li[k:k + 1] * bbr
        we_ref[j * LANE:(j + 1) * LANE, :] = jnp.concatenate([blockdiag(er), blockdiag(ei)], axis=1).astype(BF16)
    bq = jnp.concatenate([blockdiag(bbr), blockdiag(bbi)], axis=1)
    kblocks = []
    for k in range(S5_ROW + 1):
        mr = ctr * lr[k:k + 1] - cti * li[k:k + 1]
        mi = ctr * li[k:k + 1] + cti * lr[k:k + 1]
        wy_k = jnp.concatenate([blockdiag(mr), -blockdiag(mi)], axis=1)
        if k >= 1:
            wyt_ref[(k - 1) * LANE:k * LANE, :] = wy_k.astype(BF16)
        if k < S5_ROW:
            kblocks.append(_mm_nt_split(bq, wy_k).astype(BF16))
    zero = jnp.zeros((LANE, LANE), BF16)
    for i in range(S5_ROW):
        for j in range(S5_ROW):
            wk_ref[i * LANE:(i + 1) * LANE, j * LANE:(j + 1) * LANE] = kblocks[j - i] if j >= i else zero
    lam_ref[...] = jnp.concatenate([lr[S5_ROW:S5_ROW + 1], li[S5_ROW:S5_ROW + 1]], axis=1)


def _s5_kernel(*refs, scan, nseq, rps):
    nu = S5_ROW
    rows = nseq * rps
    u_ref, we_ref, wy_ref, wk_ref, d_ref, lam_ref = refs[:6]
    pos = 6
    if not scan:
        h0r_ref, h0i_ref = refs[pos:pos + 2]
        pos += 2
    y_ref, hre_ref, him_ref = refs[pos:pos + 3]
    scratch = refs[pos + 3:]

    ns = S5_TILE_STATE
    us = [u_ref[pl.ds(j, rows, stride=nu), :] for j in range(nu)]
    u = jnp.concatenate(us, axis=1)
    ub = u.astype(BF16)
    e = jnp.dot(ub, we_ref[...], preferred_element_type=F32)
    lam = lam_ref[...]
    lr = lam[:, :ns]
    li = lam[:, ns:]
    if scan:
        e_scr, h_scr = scratch
        nslab = 2 * ns // LANE
        for k in range(nslab):
            for b in range(nseq):
                e_scr[k, pl.ds(b, rps, stride=nseq), :] = e[b * rps:(b + 1) * rps, k * LANE:(k + 1) * LANE]

        def body(c, carry):
            hr, hi = carry
            hcat = jnp.concatenate([hr, hi], axis=1)
            for k in range(nslab):
                h_scr[k, pl.ds(c * nseq, nseq), :] = hcat[:, k * LANE:(k + 1) * LANE]
            ec = jnp.concatenate([e_scr[k, pl.ds(c * nseq, nseq), :] for k in range(nslab)], axis=1)
            return (lr * hr - li * hi + ec[:, :ns], lr * hi + li * hr + ec[:, ns:])

        zero = jnp.zeros((nseq, ns), F32)
        hr, hi = lax.fori_loop(0, rps, body, (zero, zero), unroll=4)
        hre_ref[...] = hr
        him_ref[...] = hi
        hin = jnp.concatenate(
            [jnp.concatenate([h_scr[k, pl.ds(b, rps, stride=nseq), :] for b in range(nseq)], axis=0)
             for k in range(nslab)], axis=1)
    else:
        h0r = h0r_ref[...]
        h0i = h0i_ref[...]
        hre_ref[...] = lr * h0r - li * h0i + e[:, :ns]
        him_ref[...] = lr * h0i + li * h0r + e[:, ns:]
        hin = jnp.concatenate([h0r, h0i], axis=1)
    y = (lax.dot_general(hin.astype(BF16), wy_ref[...], (((1,), (1,)), ((), ())), preferred_element_type=F32)
         + jnp.dot(ub, wk_ref[...], preferred_element_type=F32))
    d = d_ref[...]
    for j in range(nu):
        y_ref[pl.ds(j, rows, stride=nu), :] = y[:, j * LANE:(j + 1) * LANE] + d * us[j]


def _s5_apply(proj, nseq, seqlen, ops, d, h0=None):
    we_t, wy_t, wk_t, lam = ops
    nt = we_t.shape[0]
    width = nt * LANE
    t = nseq * seqlen
    scan = h0 is None
    rps = seqlen // S5_ROW
    assert scan or rps == 1
    ns2 = 2 * S5_TILE_STATE
    tile3 = lambda shape: pl.BlockSpec((None,) + shape, lambda tt: (tt, 0, 0))
    state = pl.BlockSpec((nseq, S5_TILE_STATE), lambda tt: (0, tt))
    in_specs = [
        pl.BlockSpec((t, LANE), lambda tt: (0, tt)),
        tile3((S5_ROW * LANE, ns2)),
        tile3((ns2, S5_ROW * LANE)),
        tile3((S5_ROW * LANE, S5_ROW * LANE)),
        tile3((1, LANE)),
        tile3((1, ns2)),
    ]
    args = [proj, we_t, wy_t, wk_t, d.reshape(nt, 1, LANE), lam]
    scratch = []
    if scan:
        scratch = [pltpu.VMEM((ns2 // LANE, nseq * rps, LANE), F32)] * 2
    else:
        in_specs += [state, state]
        args += [h0[0].reshape(nseq, nt * S5_TILE_STATE), h0[1].reshape(nseq, nt * S5_TILE_STATE)]
    return pl.pallas_call(
        functools.partial(_s5_kernel, scan=scan, nseq=nseq, rps=rps),
        grid=(nt,),
        in_specs=in_specs,
        out_specs=[pl.BlockSpec((t, LANE), lambda tt: (0, tt)), state, state],
        out_shape=[jax.ShapeDtypeStruct((t, width), F32)]
        + [jax.ShapeDtypeStruct((nseq, nt * S5_TILE_STATE), F32)] * 2,
        scratch_shapes=scratch,
        compiler_params=_cparams("parallel"),
        name="s5_scan" if scan else "s5_step",
    )(*args)


def _delta_kernel(qkv_ref, qkvn_ref, z_ref, ba_ref, cs_ref, s0_ref, convw_ref, gpar_ref, onw_ref,
                  y_ref, cnew_ref, snew_ref, ext_scr, csx_scr, s_scr, xs_scr, *, nseg, lt, chained, nheads):
    rb = nseg * lt
    nsb = 1 if chained else nseg
    hd = DN_HEAD_DIM
    width = nheads * hd
    c = pl.program_id(1)
    nc = pl.num_programs(1)
    tail = CONV_WIDTH - 1
    cw = convw_ref[...]

    def prepare_head(x_ref, dst_slot, h):
        if nsb > 1:
            tl = lax.broadcasted_iota(jnp.int32, (rb, 1), 0) % lt
        for part in range(3):
            cs = slice(part * width + h * hd, part * width + (h + 1) * hd)
            x = x_ref[:, cs]
            ext_scr[SUBLANE:SUBLANE + rb, cs] = x
            acc = x * cw[tail:tail + 1, cs]
            for k in range(1, CONV_WIDTH):
                xk = ext_scr[SUBLANE - k:SUBLANE - k + rb, cs]
                if nsb > 1:
                    xk = jnp.where(tl < k, csx_scr[SUBLANE - k:SUBLANE - k + rb, cs], xk)
                acc = acc + xk * cw[tail - k:tail - k + 1, cs]
            if nsb == 1:
                ext_scr[0:SUBLANE, cs] = ext_scr[rb:rb + SUBLANE, cs]
            xh = acc * _sigmoid(acc)
            if part < 2:
                scale = hd ** -0.5 if part == 0 else 1.0
                xh = xh * (lax.rsqrt(jnp.sum(xh * xh, axis=-1, keepdims=True) + EPS) * scale)
            xs_scr[dst_slot, :, cs] = xh

    @pl.when(c == 0)
    def _init():
        s_scr[...] = s0_ref[...]
        if chained:
            ext_scr[0:SUBLANE, :] = cs_ref[0]
            for h in range(nheads):
                prepare_head(qkv_ref, 0, h)
        else:
            ext_scr[0:SUBLANE, :] = jnp.zeros((SUBLANE, 3 * width), F32)

    if chained:
        slot = c % 2
        pending = list(range(nheads))

        def tick():
            if pending:
                prepare_head(qkvn_ref, 1 - slot, pending.pop(0))

        @pl.when(c == nc - 1)
        def _conv_out():
            cnew_ref[0] = qkvn_ref[rb - tail:rb, :]
    else:
        csx_scr[0:rb, :] = cs_ref[...].reshape(rb, 3 * width)
        csx_scr[rb:rb + SUBLANE, :] = jnp.zeros((SUBLANE, 3 * width), F32)
        slot = 0
        for h in range(nheads):
            prepare_head(qkv_ref, 0, h)
        for s in range(nsb):
            r0 = SUBLANE + (s + 1) * lt - tail
            cnew_ref[s] = ext_scr[r0:r0 + tail, :]

        def tick():
            pass
    xs = xs_scr[slot]

    ba = ba_ref[...]
    gpar = gpar_ref[...]
    beta_all = _sigmoid(ba)
    xg = ba + gpar[1:2, :]
    g_all = gpar[0:1, :] * (jnp.maximum(xg, 0.0) + jnp.log(1.0 + jnp.exp(-jnp.abs(xg))))

    ri = lax.broadcasted_iota(jnp.int32, (rb, rb), 0)
    ci = lax.broadcasted_iota(jnp.int32, (rb, rb), 1)
    same = (ri // lt) == (ci // lt)
    incl = (ri >= ci) & same
    strict = (ri > ci) & same
    tri = jnp.where(incl, 1.0, 0.0).astype(BF16)
    ghi, gmid, glo = _split3(g_all)
    gc_col = (jnp.dot(tri, ghi, preferred_element_type=F32)
              + jnp.dot(tri, gmid, preferred_element_type=F32)
              + jnp.dot(tri, glo, preferred_element_type=F32))
    sel = jnp.where(lax.broadcasted_iota(jnp.int32, (2 * SUBLANE, LANE), 1)
                    == lax.broadcasted_iota(jnp.int32, (2 * SUBLANE, LANE), 0) + nheads, 1.0, 0.0).astype(BF16)
    chi, cmid, clo = _split3(gc_col)
    nt_dims = (((1,), (1,)), ((), ()))
    gc_row = (lax.dot_general(sel, chi, nt_dims, preferred_element_type=F32)
              + lax.dot_general(sel, cmid, nt_dims, preferred_element_type=F32)
              + lax.dot_general(sel, clo, nt_dims, preferred_element_type=F32))
    lastsel = jnp.where(same & ((ci % lt) == lt - 1), 1.0, 0.0).astype(BF16)
    glast_col = (jnp.dot(lastsel, chi, preferred_element_type=F32)
                 + jnp.dot(lastsel, cmid, preferred_element_type=F32)
                 + jnp.dot(lastsel, clo, preferred_element_type=F32))

    eye = jnp.where(ri == ci, 1.0, 0.0)
    onw = onw_ref[...]
    n_sq = max(int(math.log2(lt)) - 1, 0)
    heads = range(nheads)
    rowseq = lax.broadcasted_iota(jnp.int32, (rb, 1), 0) // lt
    q, k, v, beta, gcc, glc, decay = [], [], [], [], [], [], []
    for h in heads:
        q.append(xs[:, h * hd:(h + 1) * hd])
        k.append(xs[:, width + h * hd:width + (h + 1) * hd])
        v.append(xs[:, 2 * width + h * hd:2 * width + (h + 1) * hd])
        beta.append(beta_all[:, h:h + 1])
        gcc.append(gc_col[:, nheads + h:nheads + h + 1])
        glc.append(glast_col[:, nheads + h:nheads + h + 1])
        gcr = gc_row[h:h + 1, :]
        decay.append(jnp.where(incl, jnp.exp(jnp.where(incl, gcc[h] - gcr, 0.0)), 0.0))
    qk_kk = [_mm_nt(jnp.concatenate([q[h], k[h]], axis=0), k[h]) for h in heads]
    tick()
    qk = [qk_kk[h][:rb] * decay[h] for h in heads]
    a = [jnp.where(strict, beta[h] * qk_kk[h][rb:] * decay[h], 0.0) for h in heads]
    tm = [eye - a[h] for h in heads]
    if n_sq > 0:
        bpow = [_mm(a[h], a[h]) for h in heads]
        tick()
    for r in range(n_sq):
        if r == n_sq - 1:
            tm = [tm[h] + _mm(tm[h], bpow[h]) for h in heads]
        else:
            nxt = [_mm(jnp.concatenate([tm[h], bpow[h]], axis=0), bpow[h]) for h in heads]
            bpow = [nxt[h][rb:] for h in heads]
            tm = [tm[h] + nxt[h][:rb] for h in heads]
        tick()
    egc = [jnp.exp(gcc[h]) for h in heads]
    uw = [_mm(tm[h], jnp.concatenate([v[h] * beta[h], k[h] * (beta[h] * egc[h])], axis=1)) for h in heads]
    for _ in heads:
        tick()
    u = [uw[h][:, :hd] for h in heads]
    w = [uw[h][:, hd:] for h in heads]
    qe = [q[h] * egc[h] for h in heads]
    ks = [k[h] * jnp.exp(glc[h] - gcc[h]) for h in heads]
    if chained:
        st = [s_scr[0, h] for h in heads]
        o_parts = [[] for _ in heads]
        for sg in range(nseg):
            rows = slice(sg * lt, (sg + 1) * lt)
            vnew = [u[h][rows] - _mm(w[h][rows], st[h]) for h in heads]
            above = [jnp.zeros((sg * lt, hd), F32)] if sg > 0 else []
            below = [jnp.zeros(((nseg - 1 - sg) * lt, hd), F32)] if sg < nseg - 1 else []
            vpad = [jnp.concatenate(above + [vnew[h]] + below, axis=0) if nseg > 1 else vnew[h] for h in heads]
            for h in heads:
                o_parts[h].append(_mm(qe[h][rows], st[h]) + _mm(qk[h][rows], vpad[h]))
            st = [st[h] * jnp.exp(glc[h][sg * lt:sg * lt + 1, :]) + _mm_tn(ks[h][rows], vnew[h]) for h in heads]
        for h in heads:
            s_scr[0, h] = st[h]
        o = [jnp.concatenate(o_parts[h], axis=0) if nseg > 1 else o_parts[h][0] for h in heads]
    else:
        ws, qs = [], []
        for h in heads:
            parts = [_mm(jnp.concatenate([w[h][s * lt:(s + 1) * lt], qe[h][s * lt:(s + 1) * lt]], axis=0),
                         s_scr[s, h]) for s in range(nsb)]
            ws.append(jnp.concatenate([p[:lt] for p in parts], axis=0))
            qs.append(jnp.concatenate([p[lt:] for p in parts], axis=0))
        vnew = [u[h] - ws[h] for h in heads]
        o = [qs[h] + _mm(qk[h], vnew[h]) for h in heads]
        for h in heads:
            for s in range(nsb):
                ksm = jnp.where(rowseq == s, ks[h], 0.0)
                s_scr[s, h] = s_scr[s, h] * jnp.exp(glc[h][s * lt:s * lt + 1, :]) + _mm_tn(ksm, vnew[h])
    for h in heads:
        oh = o[h] * lax.rsqrt(jnp.mean(o[h] * o[h], axis=-1, keepdims=True) + EPS) * onw
        zh = z_ref[:, h * hd:(h + 1) * hd]
        y_ref[:, h * hd:(h + 1) * hd] = oh * (zh * _sigmoid(zh))

    @pl.when(c == nc - 1)
    def _state_out():
        snew_ref[...] = s_scr[...]


def _delta_apply(proj, ba, qkv_col, z_col, nseq, seqlen, cstate, s0, conv_w, a_log, dt_bias, onorm_w):
    nheads = s0.shape[1]
    width = nheads * DN_HEAD_DIM
    tail = CONV_WIDTH - 1
    chained = seqlen >= DN_CHUNK
    if chained:
        lt, nseg, nsb = DN_CHUNK, DN_CHUNKS_PER_STEP, 1
        rb = nseg * lt
        nc = seqlen // rb
        grid = (nseq, nc)
        row = lambda b, c: b * nc + c
    else:
        assert seqlen == SUBLANE
        lt, nseg = seqlen, DN_CHUNK // seqlen
        nsb = nseg
        rb = nseg * lt
        assert nseq % nsb == 0
        nc = 1
        grid = (nseq // nsb, 1)
        row = lambda b, c: b
    cs8 = jnp.pad(cstate, ((0, 0), (SUBLANE - tail, 0), (0, 0)))
    gpar = jnp.zeros((SUBLANE, LANE), F32)
    gpar = gpar.at[0, nheads:2 * nheads].set(-jnp.exp(a_log))
    gpar = gpar.at[1, nheads:2 * nheads].set(dt_bias)
    t = nseq * seqlen
    return pl.pallas_call(
        functools.partial(_delta_kernel, nseg=nseg, lt=lt, chained=chained, nheads=nheads),
        grid=grid,
        in_specs=[
            pl.BlockSpec((rb, 3 * width), lambda b, c: (row(b, c), qkv_col // (3 * width))),
            pl.BlockSpec((rb, 3 * width), lambda b, c: (row(b, jnp.minimum(c + 1, nc - 1)), qkv_col // (3 * width))),
            pl.BlockSpec((rb, width), lambda b, c: (row(b, c), z_col // width)),
            pl.BlockSpec((rb, LANE), lambda b, c: (row(b, c), 0)),
            pl.BlockSpec((nsb, SUBLANE, 3 * width), lambda b, c: (b, 0, 0)),
            pl.BlockSpec((nsb, nheads, DN_HEAD_DIM, DN_HEAD_DIM), lambda b, c: (b, 0, 0, 0)),
            pl.BlockSpec((CONV_WIDTH, 3 * width), lambda b, c: (0, 0)),
            pl.BlockSpec((SUBLANE, LANE), lambda b, c: (0, 0)),
            pl.BlockSpec((1, DN_HEAD_DIM), lambda b, c: (0, 0)),
        ],
        out_specs=[
            pl.BlockSpec((rb, width), lambda b, c: (row(b, c), 0)),
            pl.BlockSpec((nsb, tail, 3 * width), lambda b, c: (b, 0, 0)),
            pl.BlockSpec((nsb, nheads, DN_HEAD_DIM, DN_HEAD_DIM), lambda b, c: (b, 0, 0, 0)),
        ],
        out_shape=[
            jax.ShapeDtypeStruct((t, width), F32),
            jax.ShapeDtypeStruct((nseq, tail, 3 * width), F32),
            jax.ShapeDtypeStruct(s0.shape, F32),
        ],
        scratch_shapes=[
            pltpu.VMEM((rb + 2 * SUBLANE, 3 * width), F32),
            pltpu.VMEM((rb + SUBLANE, 3 * width), F32),
            pltpu.VMEM((nsb, nheads, DN_HEAD_DIM, DN_HEAD_DIM), F32),
            pltpu.VMEM((2, rb, 3 * width), F32),
        ],
        compiler_params=_cparams("parallel", "arbitrary"),
        name="delta_chunk" if chained else "delta_step",
    )(proj, proj, proj, ba, cs8, s0, conv_w, gpar, onorm_w.reshape(1, DN_HEAD_DIM))


def _merge_kernel(y5_ref, ydn_ref, ga_ref, gb_ref, x_ref, wglu_ref, bglu_ref, wa_ref, wb_ref, wout_ref, gffn_ref,
                  x1_ref, h_ref):
    y = y5_ref[...]
    y = 0.5 * y * (1.0 + jnp.tanh(math.sqrt(2.0 / math.pi) * (y + 0.044715 * (y * y * y))))
    lin = jnp.dot(y.astype(BF16), wglu_ref[...], preferred_element_type=F32) + bglu_ref[...]
    glu = (y * _sigmoid(lin)).astype(BF16)
    a = jnp.dot(glu, wa_ref[...], preferred_element_type=F32)
    b = jnp.dot(ydn_ref[...].astype(BF16), wb_ref[...], preferred_element_type=F32)
    mix = (_sigmoid(ga_ref[...]) * a + _sigmoid(gb_ref[...]) * b).astype(BF16)
    x1 = x_ref[...] + jnp.dot(mix, wout_ref[...], preferred_element_type=F32)
    x1_ref[...] = x1
    h_ref[...] = _rms(x1, gffn_ref[...]).astype(BF16)


def _merge(y5, ydn, proj, x, ga_col, gb_col, w_glu, b_glu, w_a, w_b, w_out, g_ffn, tm):
    t, w5 = y5.shape
    d = x.shape[1]
    row = lambda cols: pl.BlockSpec((tm, cols), lambda i: (i, 0))
    const = lambda shape: pl.BlockSpec(shape, lambda i: (0, 0), pipeline_mode=pl.Buffered(1))
    return pl.pallas_call(
        _merge_kernel,
        grid=(t // tm,),
        in_specs=[
            row(w5), row(w5),
            pl.BlockSpec((tm, d), lambda i: (i, ga_col // d)),
            pl.BlockSpec((tm, d), lambda i: (i, gb_col // d)),
            row(d),
            const((w5, w5)), const((1, w5)), const((w5, d)), const((w5, d)), const((d, d)), const((1, d)),
        ],
        out_specs=[row(d), row(d)],
        out_shape=[jax.ShapeDtypeStruct((t, d), F32), jax.ShapeDtypeStruct((t, d), BF16)],
        compiler_params=_cparams("parallel"),
        name="merge",
    )(y5, ydn, proj, proj, x, w_glu, b_glu, w_a, w_b, w_out, g_ffn)


def _ffn_kernel(h_ref, xn_ref, wg_ref, wu_ref, wd_ref, out_ref, act_scr, *, nk, tk):
    s = pl.program_id(1)

    @pl.when(s < nk)
    def _():
        h = h_ref[...]
        gate = jnp.dot(h, wg_ref[...], preferred_element_type=F32)
        up = jnp.dot(h, wu_ref[...], preferred_element_type=F32)
        act_scr[s] = (gate * _sigmoid(gate) * up).astype(BF16)

    @pl.when(s >= nk)
    def _():
        acc = xn_ref[...]
        for kk in range(nk):
            acc = acc + jnp.dot(act_scr[kk], wd_ref[kk * tk:(kk + 1) * tk, :], preferred_element_type=F32)
        out_ref[...] = acc


def _ffn(x, h, w_gate, w_up, w_down, tm, tk, tn):
    t, d = x.shape
    dff = w_gate.shape[1]
    nk = dff // tk
    up_blk = lambda i, s: (0, jnp.minimum(s, nk - 1))
    down_blk = lambda i, s: (0, jnp.maximum(s - nk, 0))
    out_blk = lambda i, s: (i, jnp.maximum(s - nk, 0))
    return pl.pallas_call(
        functools.partial(_ffn_kernel, nk=nk, tk=tk),
        grid=(t // tm, nk + d // tn),
        in_specs=[
            pl.BlockSpec((tm, d), lambda i, s: (i, 0)),
            pl.BlockSpec((tm, tn), out_blk),
            pl.BlockSpec((d, tk), up_blk),
            pl.BlockSpec((d, tk), up_blk),
            pl.BlockSpec((dff, tn), down_blk),
        ],
        out_specs=pl.BlockSpec((tm, tn), out_blk),
        out_shape=jax.ShapeDtypeStruct((t, d), F32),
        scratch_shapes=[pltpu.VMEM((nk, tm, tk), BF16)],
        compiler_params=_cparams("parallel", "arbitrary"),
        name="ffn",
    )(h, x, w_gate, w_up, w_down)


def _ple_kernel(x_ref, p_ref, gple_ref, gfin_ref, wple_ref, wpg_ref, out_ref, h_scr, *, nj, tn, final):
    j = pl.program_id(1)

    @pl.when(j == 0)
    def _():
        h_scr[...] = _rms(x_ref[...], gple_ref[...]).astype(BF16)

    gate = _sigmoid(jnp.dot(h_scr[...], wpg_ref[...], preferred_element_type=F32))
    emb = jnp.dot(p_ref[...].astype(BF16), wple_ref[...], preferred_element_type=F32)
    upd = emb * gate
    for jj in range(nj):
        @pl.when(j == jj)
        def _(jj=jj):
            out_ref[:, jj * tn:(jj + 1) * tn] = x_ref[:, jj * tn:(jj + 1) * tn] + upd

    if final:
        @pl.when(j == nj - 1)
        def _():
            out_ref[...] = _rms(out_ref[...], gfin_ref[...])


def _ple_final(x, p, g_ple, g_final, w_ple, w_ple_gate, tm, tn, final):
    t, d = x.shape
    pd = p.shape[1]
    nj = d // tn
    return pl.pallas_call(
        functools.partial(_ple_kernel, nj=nj, tn=tn, final=final),
        grid=(t // tm, nj),
        in_specs=[
            pl.BlockSpec((tm, d), lambda i, j: (i, 0)),
            pl.BlockSpec((tm, pd), lambda i, j: (i, 0)),
            pl.BlockSpec((1, d), lambda i, j: (0, 0)),
            pl.BlockSpec((1, d), lambda i, j: (0, 0)),
            pl.BlockSpec((pd, tn), lambda i, j: (0, j)),
            pl.BlockSpec((d, tn), lambda i, j: (0, j)),
        ],
        out_specs=pl.BlockSpec((tm, d), lambda i, j: (i, 0)),
        out_shape=jax.ShapeDtypeStruct((t, d), F32),
        scratch_shapes=[pltpu.VMEM((tm, d), BF16)],
        compiler_params=_cparams("parallel", "arbitrary"),
        name="ple_final",
    )(x, p, g_ple, g_final, w_ple, w_ple_gate)


COL_U, COL_Z, COL_GA, COL_GB, COL_QKV = 0, 1024, 2048, 4096, 6144


def _prep_w_in(w_in, d_model, nheads):
    s5w = d_model // 2
    dnw = d_model // 2
    off_u = s5w
    off_qkv = off_u + 3 * dnw
    off_z = off_qkv + dnw
    off_a = off_z + 2 * nheads
    w_ba = jnp.pad(w_in[:, off_z:off_a], ((0, 0), (0, LANE - 2 * nheads))).astype(BF16)
    tb = W_IN_BLOCK
    assert off_u % tb == 0 and off_qkv % tb == 0 and off_z % tb == 0 and d_model % tb == 0
    shift = off_a - off_z
    src = ([0] + [off_qkv // tb + i for i in range(dnw // tb)] + [off_z // tb + i for i in range(2 * d_model // tb)]
           + [off_u // tb + i for i in range(3 * dnw // tb)])
    n_al = 1 + dnw // tb
    n_sh = 2 * d_model // tb
    src_tab = jnp.asarray(src, jnp.int32)
    hi_tab = jnp.asarray([src[min(max(j, n_al), n_al + n_sh - 1)] + 1 for j in range(len(src))], jnp.int32)
    d_in = w_in.shape[0]
    w_main = pl.pallas_call(
        functools.partial(_w_in_kernel, n_al=n_al, n_sh=n_sh, shift=shift),
        grid_spec=pltpu.PrefetchScalarGridSpec(
            num_scalar_prefetch=2,
            grid=(len(src),),
            in_specs=[pl.BlockSpec((d_in, tb), lambda j, lo, hi: (0, lo[j])),
                      pl.BlockSpec((d_in, tb), lambda j, lo, hi: (0, hi[j]))],
            out_specs=pl.BlockSpec((d_in, tb), lambda j, lo, hi: (0, j)),
        ),
        out_shape=jax.ShapeDtypeStruct((d_in, len(src) * tb), BF16),
        compiler_params=_cparams("arbitrary"),
        name="w_in_cast",
    )(src_tab, hi_tab, w_in, w_in)
    return w_main, w_ba


W_IN_BLOCK = 1024


def _w_in_kernel(lo_tab, hi_tab, lo_ref, hi_ref, out_ref, *, n_al, n_sh, shift):
    del lo_tab, hi_tab
    j = pl.program_id(0)
    tb = out_ref.shape[1]
    stitched = (j >= n_al) & (j < n_al + n_sh)

    @pl.when(stitched)
    def _():
        cat = jnp.concatenate([lo_ref[...], hi_ref[...]], axis=1)
        out_ref[...] = pltpu.roll(cat, 2 * tb - shift, axis=1)[:, :tb].astype(BF16)

    @pl.when(jnp.logical_not(stitched))
    def _():
        out_ref[...] = lo_ref[...].astype(BF16)


def _layer(x3, p3, cstate, s0, h0, lw, tm, final):
    nseq, seqlen, d = x3.shape
    t = nseq * seqlen
    x = x3.reshape(t, d)
    proj, ba = _inproj(x, lw['g_mix'], lw['w_main'], lw['w_ba'], tm, 1536)
    y5, hre, him = _s5_apply(proj, nseq, seqlen, lw['s5_ops'], lw['s5_d'], h0)
    ydn, cnew, snew = _delta_apply(proj, ba, COL_QKV, COL_Z, nseq, seqlen, cstate, s0,
                                   lw['conv_w'], lw['a_log'], lw['dt_bias'], lw['onorm_w'])
    x1, h2 = _merge(y5, ydn, proj, x, COL_GA, COL_GB, lw['w_glu'], lw['b_glu'], lw['w_a'], lw['w_b'],
                    lw['w_out'], lw['g_ffn'], 256)
    x2 = _ffn(x1, h2, lw['w_gate'], lw['w_up'], lw['w_down'], tm, 512, 512)
    y = _ple_final(x2, p3.reshape(t, -1), lw['g_ple'], lw['g_final'], lw['w_ple'], lw['w_ple_gate'],
                   tm, 512, final)
    ng = hre.shape[-1] // S5_STATE
    return (y.reshape(nseq, seqlen, d), cnew, snew,
            hre.reshape(nseq, ng, S5_STATE), him.reshape(nseq, ng, S5_STATE))


def kernel(x_prompt, x_sample, state_conv, state_delta, state_s5_re, state_s5_im, p_prompt, p_sample, g_mix, w_in, conv_w, a_log, dt_bias, onorm_w, s5_a_re, s5_a_im, s5_b_re, s5_b_im, s5_c_re, s5_c_im, s5_d, s5_log_dt, w_glu, b_glu, w_a, w_b, w_out, g_ffn, w_gate, w_up, w_down, g_ple, w_ple, w_ple_gate, g_final):
    depth = w_in.shape[0]
    d_model = x_prompt.shape[-1]
    nheads = state_delta.shape[2]
    nb_p = x_prompt.shape[0]
    f32z = functools.partial(jnp.zeros, dtype=F32)
    yp, ys = x_prompt, x_sample
    outs_p, outs_s = [], []
    for i in range(depth):
        w_main, w_ba = _prep_w_in(w_in[i], d_model, nheads)
        lw = dict(
            g_mix=g_mix[i][None], w_main=w_main, w_ba=w_ba,
            conv_w=conv_w[i], a_log=a_log[i], dt_bias=dt_bias[i], onorm_w=onorm_w[i],
            s5_ops=_s5_operators(s5_a_re[i], s5_a_im[i], s5_b_re[i], s5_b_im[i], s5_c_re[i], s5_c_im[i],
                                 s5_log_dt[i]),
            s5_d=s5_d[i],
            w_glu=w_glu[i].astype(BF16), b_glu=b_glu[i][None], w_a=w_a[i].astype(BF16), w_b=w_b[i].astype(BF16),
            w_out=w_out[i].astype(BF16), g_ffn=g_ffn[i][None],
            w_gate=w_gate[i].astype(BF16), w_up=w_up[i].astype(BF16), w_down=w_down[i].astype(BF16),
            g_ple=g_ple[i][None], w_ple=w_ple[i].astype(BF16), w_ple_gate=w_ple_gate[i].astype(BF16),
            g_final=g_final[None],
        )
        final = i == depth - 1
        yp, c1, d1, r1, m1 = _layer(yp, p_prompt[i], f32z((nb_p,) + state_conv.shape[2:]),
                                    f32z((nb_p,) + state_delta.shape[2:]), None, lw, 1024, final)
        ys, c2, d2, r2, m2 = _layer(ys, p_sample[i], state_conv[i], state_delta[i],
                                    (state_s5_re[i], state_s5_im[i]), lw, 1024, final)
        outs_p.append((c1, d1, r1, m1))
        outs_s.append((c2, d2, r2, m2))
    stack = lambda outs, k: jnp.stack([o[k] for o in outs])
    return (yp, ys,
            stack(outs_p, 0), stack(outs_p, 1), stack(outs_p, 2), stack(outs_p, 3),
            stack(outs_s, 0), stack(outs_s, 1), stack(outs_s, 2), stack(outs_s, 3))
```

```python
import functools
import math

import jax
import jax.numpy as jnp
from jax import lax
from jax.experimental import pallas as pl
from jax.experimental.pallas import tpu as pltpu

F32 = jnp.float32
BF16 = jnp.bfloat16

EPS = 1e-6
LANE = 128
SUBLANE = 8
VMEM_LIMIT_BYTES = 56 * 1024 * 1024

S5_GROUP = 16
S5_STATE = 64
S5_ROW = SUBLANE
S5_TILE_GROUPS = LANE // S5_GROUP
S5_TILE_STATE = S5_TILE_GROUPS * S5_STATE
DN_HEAD_DIM = 128
CONV_WIDTH = 4
DN_CHUNK = 64
DN_CHUNKS_PER_STEP = 2


def _cparams(*sem):
    return pltpu.CompilerParams(dimension_semantics=sem, vmem_limit_bytes=VMEM_LIMIT_BYTES)


def _mm(a, b):
    return jnp.dot(a.astype(BF16), b.astype(BF16), preferred_element_type=F32)


def _mm_nt(a, b):
    return lax.dot_general(a.astype(BF16), b.astype(BF16), (((1,), (1,)), ((), ())),
                           preferred_element_type=F32)


def _mm_tn(a, b):
    return lax.dot_general(a.astype(BF16), b.astype(BF16), (((0,), (0,)), ((), ())),
                           preferred_element_type=F32)


def _split3(x):
    hi = x.astype(BF16)
    r1 = x - hi.astype(F32)
    mid = r1.astype(BF16)
    lo = (r1 - mid.astype(F32)).astype(BF16)
    return hi, mid, lo


def _sigmoid(x):
    return 1.0 / (1.0 + jnp.exp(-x))


def _rms(x, g):
    ms = jnp.mean(x * x, axis=-1, keepdims=True)
    return x * lax.rsqrt(ms + EPS) * g


def _inproj_kernel(x_ref, g_ref, wt_ref, wbat_ref, out_ref, ba_ref, h_scr):
    nt_dims = (((1,), (1,)), ((), ()))

    @pl.when(pl.program_id(1) == 0)
    def _():
        h = _rms(x_ref[...], g_ref[...]).astype(BF16)
        h_scr[...] = h
        ba_ref[...] = lax.dot_general(h, wbat_ref[...], nt_dims, preferred_element_type=F32)

    out_ref[...] = lax.dot_general(h_scr[...], wt_ref[...], nt_dims, preferred_element_type=F32)


def _inproj(x, g, wt_main, wt_ba, tm, tn):
    t, d = x.shape
    n = wt_main.shape[0]
    return pl.pallas_call(
        _inproj_kernel,
        grid=(t // tm, n // tn),
        in_specs=[
            pl.BlockSpec((tm, d), lambda i, j: (i, 0)),
            pl.BlockSpec((1, d), lambda i, j: (0, 0)),
            pl.BlockSpec((tn, d), lambda i, j: (j, 0)),
            pl.BlockSpec((LANE, d), lambda i, j: (0, 0)),
        ],
        out_specs=[
            pl.BlockSpec((tm, tn), lambda i, j: (i, j)),
            pl.BlockSpec((tm, LANE), lambda i, j: (i, 0)),
        ],
        out_shape=[jax.ShapeDtypeStruct((t, n), F32), jax.ShapeDtypeStruct((t, LANE), F32)],
        scratch_shapes=[pltpu.VMEM((tm, d), BF16)],
        compiler_params=_cparams("parallel", "arbitrary"),
        name="inproj",
    )(x, g, wt_main, wt_ba)


def _s5_operators(a_re, a_im, b_re, b_im, c_re, c_im, log_dt):
    g, n = a_re.shape
    nt = g // S5_TILE_GROUPS
    ns = S5_TILE_STATE
    wide = S5_ROW * LANE
    row = lambda x: x.reshape(1, g * n)
    bt = lambda x: x.transpose(2, 0, 1).reshape(S5_GROUP, g * n)
    ct = lambda x: x.transpose(1, 0, 2).reshape(S5_GROUP, g * n)
    vec = pl.BlockSpec((1, ns), lambda t: (0, t))
    mat = pl.BlockSpec((S5_GROUP, ns), lambda t: (0, t))
    return pl.pallas_call(
        _s5_ops_kernel,
        grid=(nt,),
        in_specs=[vec, vec, vec, mat, mat, mat, mat],
        out_specs=[
            pl.BlockSpec((None, wide, 2 * ns), lambda t: (t, 0, 0)),
            pl.BlockSpec((None, wide, 2 * ns), lambda t: (t, 0, 0)),
            pl.BlockSpec((None, wide, wide), lambda t: (t, 0, 0)),
            pl.BlockSpec((None, 1, 2 * ns), lambda t: (t, 0, 0)),
        ],
        out_shape=[
            jax.ShapeDtypeStruct((nt, wide, 2 * ns), BF16),
            jax.ShapeDtypeStruct((nt, wide, 2 * ns), BF16),
            jax.ShapeDtypeStruct((nt, wide, wide), BF16),
            jax.ShapeDtypeStruct((nt, 1, 2 * ns), F32),
        ],
        compiler_params=_cparams("parallel"),
        name="s5_ops",
    )(row(a_re), row(a_im), row(jnp.repeat(log_dt, n)), bt(b_re), bt(b_im), ct(c_re), ct(c_im))


def _mm_nt_split(a, b):
    ah = a.astype(BF16)
    al = (a - ah.astype(F32)).astype(BF16)
    bh = b.astype(BF16)
    bl = (b - bh.astype(F32)).astype(BF16)
    dims = (((1,), (1,)), ((), ()))
    return (lax.dot_general(ah, bh, dims, preferred_element_type=F32)
            + lax.dot_general(ah, bl, dims, preferred_element_type=F32)
            + lax.dot_general(al, bh, dims, preferred_element_type=F32))


def _s5_ops_kernel(ar_ref, ai_ref, ldt_ref, btr_ref, bti_ref, ctr_ref, cti_ref, we_ref, wyt_ref, wk_ref, lam_ref):
    ns = S5_TILE_STATE
    ar = ar_ref[...]
    ai = ai_ref[...]
    dt = jnp.exp(ldt_ref[...])
    kk = lax.broadcasted_iota(jnp.int32, (2 * SUBLANE, ns), 0).astype(F32)
    mag = jnp.exp(ar * dt * kk)
    lr = mag * jnp.cos(ai * dt * kk)
    li = mag * jnp.sin(ai * dt * kk)
    nr = lr[1:2] - 1.0
    ni = li[1:2]
    den = ar * ar + ai * ai
    cr = (nr * ar + ni * ai) / den
    ci = (ni * ar - nr * ai) / den
    btr = btr_ref[...]
    bti = bti_ref[...]
    bbr = cr * btr - ci * bti
    bbi = cr * bti + ci * btr
    ctr = ctr_ref[...]
    cti = cti_ref[...]
    same_group = (lax.broadcasted_iota(jnp.int32, (LANE, ns), 0) // S5_GROUP
                  == lax.broadcasted_iota(jnp.int32, (LANE, ns), 1) // S5_STATE)

    def blockdiag(x):
        return jnp.where(same_group, jnp.concatenate([x] * S5_TILE_GROUPS, axis=0), 0.0)

    for j in range(S5_ROW):
        k = S5_ROW - 1 - j
        er = lr[k:k + 1] * bbr - li[k:k + 1] * bbi
        ei = lr[k:k + 1] * bbi + li[k:k + 1] * bbr
        we_ref[j * LANE:(j + 1) * LANE, :] = jnp.concatenate([blockdiag(er), blockdiag(ei)], axis=1).astype(BF16)
    bq = jnp.concatenate([blockdiag(bbr), blockdiag(bbi)], axis=1)
    kblocks = []
    for k in range(S5_ROW + 1):
        mr = ctr * lr[k:k + 1] - cti * li[k:k + 1]
        mi = ctr * li[k:k + 1] + cti * lr[k:k + 1]
        wy_k = jnp.concatenate([blockdiag(mr), -blockdiag(mi)], axis=1)
        if k >= 1:
            wyt_ref[(k - 1) * LANE:k * LANE, :] = wy_k.astype(BF16)
        if k < S5_ROW:
            kblocks.append(_mm_nt_split(bq, wy_k).astype(BF16))
    zero = jnp.zeros((LANE, LANE), BF16)
    for i in range(S5_ROW):
        for j in range(S5_ROW):
            wk_ref[i * LANE:(i + 1) * LANE, j * LANE:(j + 1) * LANE] = kblocks[j - i] if j >= i else zero
    lam_ref[...] = jnp.concatenate([lr[S5_ROW:S5_ROW + 1], li[S5_ROW:S5_ROW + 1]], axis=1)


def _s5_kernel(*refs, scan, nseq, rps):
    nu = S5_ROW
    rows = nseq * rps
    u_ref, we_ref, wy_ref, wk_ref, d_ref, lam_ref = refs[:6]
    pos = 6
    if not scan:
        h0r_ref, h0i_ref = refs[pos:pos + 2]
        pos += 2
    y_ref, hre_ref, him_ref = refs[pos:pos + 3]
    scratch = refs[pos + 3:]

    ns = S5_TILE_STATE
    us = [u_ref[pl.ds(j, rows, stride=nu), :] for j in range(nu)]
    u = jnp.concatenate(us, axis=1)
    ub = u.astype(BF16)
    e = jnp.dot(ub, we_ref[...], preferred_element_type=F32)
    lam = lam_ref[...]
    lr = lam[:, :ns]
    li = lam[:, ns:]
    if scan:
        e_scr, h_scr = scratch
        nslab = 2 * ns // LANE
        for k in range(nslab):
            for b in range(nseq):
                e_scr[k, pl.ds(b, rps, stride=nseq), :] = e[b * rps:(b + 1) * rps, k * LANE:(k + 1) * LANE]

        def body(c, carry):
            hr, hi = carry
            hcat = jnp.concatenate([hr, hi], axis=1)
            for k in range(nslab):
                h_scr[k, pl.ds(c * nseq, nseq), :] = hcat[:, k * LANE:(k + 1) * LANE]
            ec = jnp.concatenate([e_scr[k, pl.ds(c * nseq, nseq), :] for k in range(nslab)], axis=1)
            return (lr * hr - li * hi + ec[:, :ns], lr * hi + li * hr + ec[:, ns:])

        zero = jnp.zeros((nseq, ns), F32)
        hr, hi = lax.fori_loop(0, rps, body, (zero, zero), unroll=4)
        hre_ref[...] = hr
        him_ref[...] = hi
        hin = jnp.concatenate(
            [jnp.concatenate([h_scr[k, pl.ds(b, rps, stride=nseq), :] for b in range(nseq)], axis=0)
             for k in range(nslab)], axis=1)
    else:
        h0r = h0r_ref[...]
        h0i = h0i_ref[...]
        hre_ref[...] = lr * h0r - li * h0i + e[:, :ns]
        him_ref[...] = lr * h0i + li * h0r + e[:, ns:]
        hin = jnp.concatenate([h0r, h0i], axis=1)
    y = (lax.dot_general(hin.astype(BF16), wy_ref[...], (((1,), (1,)), ((), ())), preferred_element_type=F32)
         + jnp.dot(ub, wk_ref[...], preferred_element_type=F32))
    d = d_ref[...]
    for j in range(nu):
        y_ref[pl.ds(j, rows, stride=nu), :] = y[:, j * LANE:(j + 1) * LANE] + d * us[j]


def _s5_apply(proj, nseq, seqlen, ops, d, h0=None):
    we_t, wy_t, wk_t, lam = ops
    nt = we_t.shape[0]
    width = nt * LANE
    t = nseq * seqlen
    scan = h0 is None
    rps = seqlen // S5_ROW
    assert scan or rps == 1
    ns2 = 2 * S5_TILE_STATE
    tile3 = lambda shape: pl.BlockSpec((None,) + shape, lambda tt: (tt, 0, 0))
    state = pl.BlockSpec((nseq, S5_TILE_STATE), lambda tt: (0, tt))
    in_specs = [
        pl.BlockSpec((t, LANE), lambda tt: (0, tt)),
        tile3((S5_ROW * LANE, ns2)),
        tile3((ns2, S5_ROW * LANE)),
        tile3((S5_ROW * LANE, S5_ROW * LANE)),
        tile3((1, LANE)),
        tile3((1, ns2)),
    ]
    args = [proj, we_t, wy_t, wk_t, d.reshape(nt, 1, LANE), lam]
    scratch = []
    if scan:
        scratch = [pltpu.VMEM((ns2 // LANE, nseq * rps, LANE), F32)] * 2
    else:
        in_specs += [state, state]
        args += [h0[0].reshape(nseq, nt * S5_TILE_STATE), h0[1].reshape(nseq, nt * S5_TILE_STATE)]
    return pl.pallas_call(
        functools.partial(_s5_kernel, scan=scan, nseq=nseq, rps=rps),
        grid=(nt,),
        in_specs=in_specs,
        out_specs=[pl.BlockSpec((t, LANE), lambda tt: (0, tt)), state, state],
        out_shape=[jax.ShapeDtypeStruct((t, width), F32)]
        + [jax.ShapeDtypeStruct((nseq, nt * S5_TILE_STATE), F32)] * 2,
        scratch_shapes=scratch,
        compiler_params=_cparams("parallel"),
        name="s5_scan" if scan else "s5_step",
    )(*args)


def _delta_kernel(qkv_ref, qkvn_ref, z_ref, ba_ref, cs_ref, s0_ref, convw_ref, gpar_ref, onw_ref,
                  y_ref, cnew_ref, snew_ref, ext_scr, csx_scr, s_scr, xs_scr, *, nseg, lt, chained, nheads):
    rb = nseg * lt
    nsb = 1 if chained else nseg
    hd = DN_HEAD_DIM
    width = nheads * hd
    c = pl.program_id(1)
    nc = pl.num_programs(1)
    tail = CONV_WIDTH - 1
    cw = convw_ref[...]

    def prepare_head(x_ref, dst_slot, h):
        if nsb > 1:
            tl = lax.broadcasted_iota(jnp.int32, (rb, 1), 0) % lt
        for part in range(3):
            cs = slice(part * width + h * hd, part * width + (h + 1) * hd)
            x = x_ref[:, cs]
            ext_scr[SUBLANE:SUBLANE + rb, cs] = x
            acc = x * cw[tail:tail + 1, cs]
            for k in range(1, CONV_WIDTH):
                xk = ext_scr[SUBLANE - k:SUBLANE - k + rb, cs]
                if nsb > 1:
                    xk = jnp.where(tl < k, csx_scr[SUBLANE - k:SUBLANE - k + rb, cs], xk)
                acc = acc + xk * cw[tail - k:tail - k + 1, cs]
            if nsb == 1:
                ext_scr[0:SUBLANE, cs] = ext_scr[rb:rb + SUBLANE, cs]
            xh = acc * _sigmoid(acc)
            if part < 2:
                scale = hd ** -0.5 if part == 0 else 1.0
                xh = xh * (lax.rsqrt(jnp.sum(xh * xh, axis=-1, keepdims=True) + EPS) * scale)
            xs_scr[dst_slot, :, cs] = xh

    @pl.when(c == 0)
    def _init():
        s_scr[...] = s0_ref[...]
        if chained:
            ext_scr[0:SUBLANE, :] = cs_ref[0]
            for h in range(nheads):
                prepare_head(qkv_ref, 0, h)
        else:
            ext_scr[0:SUBLANE, :] = jnp.zeros((SUBLANE, 3 * width), F32)

    if chained:
        slot = c % 2
        pending = list(range(nheads))

        def tick():
            if pending:
                prepare_head(qkvn_ref, 1 - slot, pending.pop(0))

        @pl.when(c == nc - 1)
        def _conv_out():
            cnew_ref[0] = qkvn_ref[rb - tail:rb, :]
    else:
        csx_scr[0:rb, :] = cs_ref[...].reshape(rb, 3 * width)
        csx_scr[rb:rb + SUBLANE, :] = jnp.zeros((SUBLANE, 3 * width), F32)
        slot = 0
        for h in range(nheads):
            prepare_head(qkv_ref, 0, h)
        for s in range(nsb):
            r0 = SUBLANE + (s + 1) * lt - tail
            cnew_ref[s] = ext_scr[r0:r0 + tail, :]

        def tick():
            pass
    xs = xs_scr[slot]

    ba = ba_ref[...]
    gpar = gpar_ref[...]
    beta_all = _sigmoid(ba)
    xg = ba + gpar[1:2, :]
    g_all = gpar[0:1, :] * (jnp.maximum(xg, 0.0) + jnp.log(1.0 + jnp.exp(-jnp.abs(xg))))

    ri = lax.broadcasted_iota(jnp.int32, (rb, rb), 0)
    ci = lax.broadcasted_iota(jnp.int32, (rb, rb), 1)
    same = (ri // lt) == (ci // lt)
    incl = (ri >= ci) & same
    strict = (ri > ci) & same
    tri = jnp.where(incl, 1.0, 0.0).astype(BF16)
    ghi, gmid, glo = _split3(g_all)
    gc_col = (jnp.dot(tri, ghi, preferred_element_type=F32)
              + jnp.dot(tri, gmid, preferred_element_type=F32)
              + jnp.dot(tri, glo, preferred_element_type=F32))
    sel = jnp.where(lax.broadcasted_iota(jnp.int32, (2 * SUBLANE, LANE), 1)
                    == lax.broadcasted_iota(jnp.int32, (2 * SUBLANE, LANE), 0) + nheads, 1.0, 0.0).astype(BF16)
    chi, cmid, clo = _split3(gc_col)
    nt_dims = (((1,), (1,)), ((), ()))
    gc_row = (lax.dot_general(sel, chi, nt_dims, preferred_element_type=F32)
              + lax.dot_general(sel, cmid, nt_dims, preferred_element_type=F32)
              + lax.dot_general(sel, clo, nt_dims, preferred_element_type=F32))
    lastsel = jnp.where(same & ((ci % lt) == lt - 1), 1.0, 0.0).astype(BF16)
    glast_col = (jnp.dot(lastsel, chi, preferred_element_type=F32)
                 + jnp.dot(lastsel, cmid, preferred_element_type=F32)
                 + jnp.dot(lastsel, clo, preferred_element_type=F32))

    eye = jnp.where(ri == ci, 1.0, 0.0)
    onw = onw_ref[...]
    n_sq = max(int(math.log2(lt)) - 1, 0)
    heads = range(nheads)
    rowseq = lax.broadcasted_iota(jnp.int32, (rb, 1), 0) // lt
    q, k, v, beta, gcc, glc, decay = [], [], [], [], [], [], []
    for h in heads:
        q.append(xs[:, h * hd:(h + 1) * hd])
        k.append(xs[:, width + h * hd:width + (h + 1) * hd])
        v.append(xs[:, 2 * width + h * hd:2 * width + (h + 1) * hd])
        beta.append(beta_all[:, h:h + 1])
        gcc.append(gc_col[:, nheads + h:nheads + h + 1])
        glc.append(glast_col[:, nheads + h:nheads + h + 1])
        gcr = gc_row[h:h + 1, :]
        decay.append(jnp.where(incl, jnp.exp(jnp.where(incl, gcc[h] - gcr, 0.0)), 0.0))
    qk_kk = [_mm_nt(jnp.concatenate([q[h], k[h]], axis=0), k[h]) for h in heads]
    tick()
    qk = [qk_kk[h][:rb] * decay[h] for h in heads]
    a = [jnp.where(strict, beta[h] * qk_kk[h][rb:] * decay[h], 0.0) for h in heads]
    tm = [eye - a[h] for h in heads]
    if n_sq > 0:
        bpow = [_mm(a[h], a[h]) for h in heads]
        tick()
    for r in range(n_sq):
        if r == n_sq - 1:
            tm = [tm[h] + _mm(tm[h], bpow[h]) for h in heads]
        else:
            nxt = [_mm(jnp.concatenate([tm[h], bpow[h]], axis=0), bpow[h]) for h in heads]
            bpow = [nxt[h][rb:] for h in heads]
            tm = [tm[h] + nxt[h][:rb] for h in heads]
        tick()
    egc = [jnp.exp(gcc[h]) for h in heads]
    uw = [_mm(tm[h], jnp.concatenate([v[h] * beta[h], k[h] * (beta[h] * egc[h])], axis=1)) for h in heads]
    for _ in heads:
        tick()
    u = [uw[h][:, :hd] for h in heads]
    w = [uw[h][:, hd:] for h in heads]
    qe = [q[h] * egc[h] for h in heads]
    ks = [k[h] * jnp.exp(glc[h] - gcc[h]) for h in heads]
    if chained:
        st = [s_scr[0, h] for h in heads]
        o_parts = [[] for _ in heads]
        for sg in range(nseg):
            rows = slice(sg * lt, (sg + 1) * lt)
            vnew = [u[h][rows] - _mm(w[h][rows], st[h]) for h in heads]
            above = [jnp.zeros((sg * lt, hd), F32)] if sg > 0 else []
            below = [jnp.zeros(((nseg - 1 - sg) * lt, hd), F32)] if sg < nseg - 1 else []
            vpad = [jnp.concatenate(above + [vnew[h]] + below, axis=0) if nseg > 1 else vnew[h] for h in heads]
            for h in heads:
                o_parts[h].append(_mm(qe[h][rows], st[h]) + _mm(qk[h][rows], vpad[h]))
            st = [st[h] * jnp.exp(glc[h][sg * lt:sg * lt + 1, :]) + _mm_tn(ks[h][rows], vnew[h]) for h in heads]
        for h in heads:
            s_scr[0, h] = st[h]
        o = [jnp.concatenate(o_parts[h], axis=0) if nseg > 1 else o_parts[h][0] for h in heads]
    else:
        ws, qs = [], []
        for h in heads:
            parts = [_mm(jnp.concatenate([w[h][s * lt:(s + 1) * lt], qe[h][s * lt:(s + 1) * lt]], axis=0),
                         s_scr[s, h]) for s in range(nsb)]
            ws.append(jnp.concatenate([p[:lt] for p in parts], axis=0))
            qs.append(jnp.concatenate([p[lt:] for p in parts], axis=0))
        vnew = [u[h] - ws[h] for h in heads]
        o = [qs[h] + _mm(qk[h], vnew[h]) for h in heads]
        for h in heads:
            for s in range(nsb):
                ksm = jnp.where(rowseq == s, ks[h], 0.0)
                s_scr[s, h] = s_scr[s, h] * jnp.exp(glc[h][s * lt:s * lt + 1, :]) + _mm_tn(ksm, vnew[h])
    for h in heads:
        oh = o[h] * lax.rsqrt(jnp.mean(o[h] * o[h], axis=-1, keepdims=True) + EPS) * onw
        zh = z_ref[:, h * hd:(h + 1) * hd]
        y_ref[:, h * hd:(h + 1) * hd] = oh * (zh * _sigmoid(zh))

    @pl.when(c == nc - 1)
    def _state_out():
        snew_ref[...] = s_scr[...]


def _delta_apply(proj, ba, qkv_col, z_col, nseq, seqlen, cstate, s0, conv_w, a_log, dt_bias, onorm_w):
    nheads = s0.shape[1]
    width = nheads * DN_HEAD_DIM
    tail = CONV_WIDTH - 1
    chained = seqlen >= DN_CHUNK
    if chained:
        lt, nseg, nsb = DN_CHUNK, DN_CHUNKS_PER_STEP, 1
        rb = nseg * lt
        nc = seqlen // rb
        grid = (nseq, nc)
        row = lambda b, c: b * nc + c
    else:
        assert seqlen == SUBLANE
        lt, nseg = seqlen, DN_CHUNK // seqlen
        nsb = nseg
        rb = nseg * lt
        assert nseq % nsb == 0
        nc = 1
        grid = (nseq // nsb, 1)
        row = lambda b, c: b
    cs8 = jnp.pad(cstate, ((0, 0), (SUBLANE - tail, 0), (0, 0)))
    gpar = jnp.zeros((SUBLANE, LANE), F32)
    gpar = gpar.at[0, nheads:2 * nheads].set(-jnp.exp(a_log))
    gpar = gpar.at[1, nheads:2 * nheads].set(dt_bias)
    t = nseq * seqlen
    return pl.pallas_call(
        functools.partial(_delta_kernel, nseg=nseg, lt=lt, chained=chained, nheads=nheads),
        grid=grid,
        in_specs=[
            pl.BlockSpec((rb, 3 * width), lambda b, c: (row(b, c), qkv_col // (3 * width))),
            pl.BlockSpec((rb, 3 * width), lambda b, c: (row(b, jnp.minimum(c + 1, nc - 1)), qkv_col // (3 * width))),
            pl.BlockSpec((rb, width), lambda b, c: (row(b, c), z_col // width)),
            pl.BlockSpec((rb, LANE), lambda b, c: (row(b, c), 0)),
            pl.BlockSpec((nsb, SUBLANE, 3 * width), lambda b, c: (b, 0, 0)),
            pl.BlockSpec((nsb, nheads, DN_HEAD_DIM, DN_HEAD_DIM), lambda b, c: (b, 0, 0, 0)),
            pl.BlockSpec((CONV_WIDTH, 3 * width), lambda b, c: (0, 0)),
            pl.BlockSpec((SUBLANE, LANE), lambda b, c: (0, 0)),
            pl.BlockSpec((1, DN_HEAD_DIM), lambda b, c: (0, 0)),
        ],
        out_specs=[
            pl.BlockSpec((rb, width), lambda b, c: (row(b, c), 0)),
            pl.BlockSpec((nsb, tail, 3 * width), lambda b, c: (b, 0, 0)),
            pl.BlockSpec((nsb, nheads, DN_HEAD_DIM, DN_HEAD_DIM), lambda b, c: (b, 0, 0, 0)),
        ],
        out_shape=[
            jax.ShapeDtypeStruct((t, width), F32),
            jax.ShapeDtypeStruct((nseq, tail, 3 * width), F32),
            jax.ShapeDtypeStruct(s0.shape, F32),
        ],
        scratch_shapes=[
            pltpu.VMEM((rb + 2 * SUBLANE, 3 * width), F32),
            pltpu.VMEM((rb + SUBLANE, 3 * width), F32),
            pltpu.VMEM((nsb, nheads, DN_HEAD_DIM, DN_HEAD_DIM), F32),
            pltpu.VMEM((2, rb, 3 * width), F32),
        ],
        compiler_params=_cparams("parallel", "arbitrary"),
        name="delta_chunk" if chained else "delta_step",
    )(proj, proj, proj, ba, cs8, s0, conv_w, gpar, onorm_w.reshape(1, DN_HEAD_DIM))


def _merge_kernel(y5_ref, ydn_ref, ga_ref, gb_ref, x_ref, wglu_ref, bglu_ref, wa_ref, wb_ref, wout_ref, gffn_ref,
                  x1_ref, h_ref):
    y = y5_ref[...]
    y = 0.5 * y * (1.0 + jnp.tanh(math.sqrt(2.0 / math.pi) * (y + 0.044715 * (y * y * y))))
    lin = jnp.dot(y.astype(BF16), wglu_ref[...], preferred_element_type=F32) + bglu_ref[...]
    glu = (y * _sigmoid(lin)).astype(BF16)
    a = jnp.dot(glu, wa_ref[...], preferred_element_type=F32)
    b = jnp.dot(ydn_ref[...].astype(BF16), wb_ref[...], preferred_element_type=F32)
    mix = (_sigmoid(ga_ref[...]) * a + _sigmoid(gb_ref[...]) * b).astype(BF16)
    x1 = x_ref[...] + jnp.dot(mix, wout_ref[...], preferred_element_type=F32)
    x1_ref[...] = x1
    h_ref[...] = _rms(x1, gffn_ref[...]).astype(BF16)


def _merge(y5, ydn, proj, x, ga_col, gb_col, w_glu, b_glu, w_a, w_b, w_out, g_ffn, tm):
    t, w5 = y5.shape
    d = x.shape[1]
    row = lambda cols: pl.BlockSpec((tm, cols), lambda i: (i, 0))
    const = lambda shape: pl.BlockSpec(shape, lambda i: (0, 0), pipeline_mode=pl.Buffered(1))
    return pl.pallas_call(
        _merge_kernel,
        grid=(t // tm,),
        in_specs=[
            row(w5), row(w5),
            pl.BlockSpec((tm, d), lambda i: (i, ga_col // d)),
            pl.BlockSpec((tm, d), lambda i: (i, gb_col // d)),
            row(d),
            const((w5, w5)), const((1, w5)), const((w5, d)), const((w5, d)), const((d, d)), const((1, d)),
        ],
        out_specs=[row(d), row(d)],
        out_shape=[jax.ShapeDtypeStruct((t, d), F32), jax.ShapeDtypeStruct((t, d), BF16)],
        compiler_params=_cparams("parallel"),
        name="merge",
    )(y5, ydn, proj, proj, x, w_glu, b_glu, w_a, w_b, w_out, g_ffn)


def _ffn_kernel(h_ref, xn_ref, wg_ref, wu_ref, wd_ref, out_ref, act_scr, *, nk, tk):
    s = pl.program_id(1)

    @pl.when(s < nk)
    def _():
        h = h_ref[...]
        gate = jnp.dot(h, wg_ref[...], preferred_element_type=F32)
        up = jnp.dot(h, wu_ref[...], preferred_element_type=F32)
        act_scr[s] = (gate * _sigmoid(gate) * up).astype(BF16)

    @pl.when(s >= nk)
    def _():
        acc = xn_ref[...]
        for kk in range(nk):
            acc = acc + jnp.dot(act_scr[kk], wd_ref[kk * tk:(kk + 1) * tk, :], preferred_element_type=F32)
        out_ref[...] = acc


def _ffn(x, h, w_gate, w_up, w_down, tm, tk, tn):
    t, d = x.shape
    dff = w_gate.shape[1]
    nk = dff // tk
    up_blk = lambda i, s: (0, jnp.minimum(s, nk - 1))
    down_blk = lambda i, s: (0, jnp.maximum(s - nk, 0))
    out_blk = lambda i, s: (i, jnp.maximum(s - nk, 0))
    return pl.pallas_call(
        functools.partial(_ffn_kernel, nk=nk, tk=tk),
        grid=(t // tm, nk + d // tn),
        in_specs=[
            pl.BlockSpec((tm, d), lambda i, s: (i, 0)),
            pl.BlockSpec((tm, tn), out_blk),
            pl.BlockSpec((d, tk), up_blk),
            pl.BlockSpec((d, tk), up_blk),
            pl.BlockSpec((dff, tn), down_blk),
        ],
        out_specs=pl.BlockSpec((tm, tn), out_blk),
        out_shape=jax.ShapeDtypeStruct((t, d), F32),
        scratch_shapes=[pltpu.VMEM((nk, tm, tk), BF16)],
        compiler_params=_cparams("parallel", "arbitrary"),
        name="ffn",
    )(h, x, w_gate, w_up, w_down)


def _ple_kernel(x_ref, p_ref, gple_ref, gfin_ref, wple_ref, wpg_ref, out_ref, h_scr, *, nj, tn, final):
    j = pl.program_id(1)

    @pl.when(j == 0)
    def _():
        h_scr[...] = _rms(x_ref[...], gple_ref[...]).astype(BF16)

    gate = _sigmoid(jnp.dot(h_scr[...], wpg_ref[...], preferred_element_type=F32))
    emb = jnp.dot(p_ref[...].astype(BF16), wple_ref[...], preferred_element_type=F32)
    upd = emb * gate
    for jj in range(nj):
        @pl.when(j == jj)
        def _(jj=jj):
            out_ref[:, jj * tn:(jj + 1) * tn] = x_ref[:, jj * tn:(jj + 1) * tn] + upd

    if final:
        @pl.when(j == nj - 1)
        def _():
            out_ref[...] = _rms(out_ref[...], gfin_ref[...])


def _ple_final(x, p, g_ple, g_final, w_ple, w_ple_gate, tm, tn, final):
    t, d = x.shape
    pd = p.shape[1]
    nj = d // tn
    return pl.pallas_call(
        functools.partial(_ple_kernel, nj=nj, tn=tn, final=final),
        grid=(t // tm, nj),
        in_specs=[
            pl.BlockSpec((tm, d), lambda i, j: (i, 0)),
            pl.BlockSpec((tm, pd), lambda i, j: (i, 0)),
            pl.BlockSpec((1, d), lambda i, j: (0, 0)),
            pl.BlockSpec((1, d), lambda i, j: (0, 0)),
            pl.BlockSpec((pd, tn), lambda i, j: (0, j)),
            pl.BlockSpec((d, tn), lambda i, j: (0, j)),
        ],
        out_specs=pl.BlockSpec((tm, d), lambda i, j: (i, 0)),
        out_shape=jax.ShapeDtypeStruct((t, d), F32),
        scratch_shapes=[pltpu.VMEM((tm, d), BF16)],
        compiler_params=_cparams("parallel", "arbitrary"),
        name="ple_final",
    )(x, p, g_ple, g_final, w_ple, w_ple_gate)


COL_U, COL_Z, COL_GA, COL_GB, COL_QKV = 0, 1024, 2048, 4096, 6144


def _prep_w_in(w_in, layer, d_model, nheads):
    w_t = jnp.swapaxes(w_in, 1, 2)
    s5w = d_model // 2
    dnw = d_model // 2
    off_u = s5w
    off_qkv = off_u + 3 * dnw
    off_z = off_qkv + dnw
    off_a = off_z + 2 * nheads
    tb = W_IN_BLOCK
    assert off_u % tb == 0 and off_qkv % tb == 0 and off_z % tb == 0 and d_model % tb == 0
    shift = off_a - off_z
    assert shift % SUBLANE == 0 and shift <= LANE
    src = ([0] + [off_qkv // tb + i for i in range(dnw // tb)] + [off_z // tb + i for i in range(2 * d_model // tb)]
           + [off_u // tb + i for i in range(3 * dnw // tb)])
    n_al = 1 + dnw // tb
    n_sh = 2 * d_model // tb
    src_tab = jnp.asarray(src, jnp.int32)
    hi_tab = jnp.asarray([src[min(max(j, n_al), n_al + n_sh - 1)] + 1 for j in range(len(src))], jnp.int32)
    d_in = w_in.shape[1]
    return pl.pallas_call(
        functools.partial(_w_in_kernel, n_al=n_al, n_sh=n_sh, shift=shift),
        grid_spec=pltpu.PrefetchScalarGridSpec(
            num_scalar_prefetch=2,
            grid=(len(src),),
            in_specs=[pl.BlockSpec((None, tb, d_in), lambda j, lo, hi: (layer, lo[j], 0)),
                      pl.BlockSpec((None, tb, d_in), lambda j, lo, hi: (layer, hi[j], 0))],
            out_specs=[pl.BlockSpec((tb, d_in), lambda j, lo, hi: (j, 0)),
                       pl.BlockSpec((LANE, d_in), lambda j, lo, hi: (0, 0))],
        ),
        out_shape=[jax.ShapeDtypeStruct((len(src) * tb, d_in), BF16), jax.ShapeDtypeStruct((LANE, d_in), BF16)],
        compiler_params=_cparams("arbitrary"),
        name="w_in_cast",
    )(src_tab, hi_tab, w_t, w_t)


W_IN_BLOCK = 1024


def _w_in_kernel(lo_tab, hi_tab, lo_ref, hi_ref, out_ref, ba_ref, *, n_al, n_sh, shift):
    del lo_tab, hi_tab
    j = pl.program_id(0)
    tb, d_in = out_ref.shape
    stitched = (j >= n_al) & (j < n_al + n_sh)

    @pl.when(j == n_al)
    def _():
        ba_ref[...] = jnp.concatenate([lo_ref[0:shift, :], jnp.zeros((LANE - shift, d_in), F32)],
                                      axis=0).astype(BF16)

    @pl.when(stitched)
    def _():
        out_ref[...] = jnp.concatenate([lo_ref[shift:, :], hi_ref[0:shift, :]], axis=0).astype(BF16)

    @pl.when(jnp.logical_not(stitched))
    def _():
        out_ref[...] = lo_ref[...].astype(BF16)


def _layer(x3, p3, cstate, s0, h0, lw, tm, final):
    nseq, seqlen, d = x3.shape
    t = nseq * seqlen
    x = x3.reshape(t, d)
    proj, ba = _inproj(x, lw['g_mix'], lw['w_main'], lw['w_ba'], tm, 1536)
    y5, hre, him = _s5_apply(proj, nseq, seqlen, lw['s5_ops'], lw['s5_d'], h0)
    ydn, cnew, snew = _delta_apply(proj, ba, COL_QKV, COL_Z, nseq, seqlen, cstate, s0,
                                   lw['conv_w'], lw['a_log'], lw['dt_bias'], lw['onorm_w'])
    x1, h2 = _merge(y5, ydn, proj, x, COL_GA, COL_GB, lw['w_glu'], lw['b_glu'], lw['w_a'], lw['w_b'],
                    lw['w_out'], lw['g_ffn'], 256)
    x2 = _ffn(x1, h2, lw['w_gate'], lw['w_up'], lw['w_down'], tm, 512, 512)
    y = _ple_final(x2, p3.reshape(t, -1), lw['g_ple'], lw['g_final'], lw['w_ple'], lw['w_ple_gate'],
                   tm, 512, final)
    ng = hre.shape[-1] // S5_STATE
    return (y.reshape(nseq, seqlen, d), cnew, snew,
            hre.reshape(nseq, ng, S5_STATE), him.reshape(nseq, ng, S5_STATE))


def kernel(x_prompt, x_sample, state_conv, state_delta, state_s5_re, state_s5_im, p_prompt, p_sample, g_mix, w_in, conv_w, a_log, dt_bias, onorm_w, s5_a_re, s5_a_im, s5_b_re, s5_b_im, s5_c_re, s5_c_im, s5_d, s5_log_dt, w_glu, b_glu, w_a, w_b, w_out, g_ffn, w_gate, w_up, w_down, g_ple, w_ple, w_ple_gate, g_final):
    depth = w_in.shape[0]
    d_model = x_prompt.shape[-1]
    nheads = state_delta.shape[2]
    nb_p = x_prompt.shape[0]
    f32z = functools.partial(jnp.zeros, dtype=F32)
    yp, ys = x_prompt, x_sample
    outs_p, outs_s = [], []
    for i in range(depth):
        w_main, w_ba = _prep_w_in(w_in, i, d_model, nheads)
        lw = dict(
            g_mix=g_mix[i][None], w_main=w_main, w_ba=w_ba,
            conv_w=conv_w[i], a_log=a_log[i], dt_bias=dt_bias[i], onorm_w=onorm_w[i],
            s5_ops=_s5_operators(s5_a_re[i], s5_a_im[i], s5_b_re[i], s5_b_im[i], s5_c_re[i], s5_c_im[i],
                                 s5_log_dt[i]),
            s5_d=s5_d[i],
            w_glu=w_glu[i].astype(BF16), b_glu=b_glu[i][None], w_a=w_a[i].astype(BF16), w_b=w_b[i].astype(BF16),
            w_out=w_out[i].astype(BF16), g_ffn=g_ffn[i][None],
            w_gate=w_gate[i].astype(BF16), w_up=w_up[i].astype(BF16), w_down=w_down[i].astype(BF16),
            g_ple=g_ple[i][None], w_ple=w_ple[i].astype(BF16), w_ple_gate=w_ple_gate[i].astype(BF16),
            g_final=g_final[None],
        )
        final = i == depth - 1
        yp, c1, d1, r1, m1 = _layer(yp, p_prompt[i], f32z((nb_p,) + state_conv.shape[2:]),
                                    f32z((nb_p,) + state_delta.shape[2:]), None, lw, 1024, final)
        ys, c2, d2, r2, m2 = _layer(ys, p_sample[i], state_conv[i], state_delta[i],
                                    (state_s5_re[i], state_s5_im[i]), lw, 1024, final)
        outs_p.append((c1, d1, r1, m1))
        outs_s.append((c2, d2, r2, m2))
    stack = lambda outs, k: jnp.stack([o[k] for o in outs])
    return (yp, ys,
            stack(outs_p, 0), stack(outs_p, 1), stack(outs_p, 2), stack(outs_p, 3),
            stack(outs_s, 0), stack(outs_s, 1), stack(outs_s, 2), stack(outs_s, 3))
```

```python
import functools
import math

import jax
import jax.numpy as jnp
from jax import lax
from jax.experimental import pallas as pl
from jax.experimental.pallas import tpu as pltpu

F32 = jnp.float32
BF16 = jnp.bfloat16

EPS = 1e-6
LANE = 128
SUBLANE = 8
VMEM_LIMIT_BYTES = 56 * 1024 * 1024

S5_GROUP = 16
S5_STATE = 64
S5_ROW = SUBLANE
S5_TILE_GROUPS = LANE // S5_GROUP
S5_TILE_STATE = S5_TILE_GROUPS * S5_STATE
DN_HEAD_DIM = 128
CONV_WIDTH = 4
DN_CHUNK = 64
DN_CHUNKS_PER_STEP = 2


def _cparams(*sem):
    return pltpu.CompilerParams(dimension_semantics=sem, vmem_limit_bytes=VMEM_LIMIT_BYTES)


def _mm(a, b):
    return jnp.dot(a.astype(BF16), b.astype(BF16), preferred_element_type=F32)


def _mm_nt(a, b):
    return lax.dot_general(a.astype(BF16), b.astype(BF16), (((1,), (1,)), ((), ())),
                           preferred_element_type=F32)


def _mm_tn(a, b):
    return lax.dot_general(a.astype(BF16), b.astype(BF16), (((0,), (0,)), ((), ())),
                           preferred_element_type=F32)


def _split3(x):
    hi = x.astype(BF16)
    r1 = x - hi.astype(F32)
    mid = r1.astype(BF16)
    lo = (r1 - mid.astype(F32)).astype(BF16)
    return hi, mid, lo


def _sigmoid(x):
    return 1.0 / (1.0 + jnp.exp(-x))


def _rms(x, g):
    ms = jnp.mean(x * x, axis=-1, keepdims=True)
    return x * lax.rsqrt(ms + EPS) * g


def _inproj_kernel(x_ref, g_ref, wt_ref, wbat_ref, out_ref, ba_ref, h_scr):
    nt_dims = (((1,), (1,)), ((), ()))

    @pl.when(pl.program_id(1) == 0)
    def _():
        h = _rms(x_ref[...], g_ref[...]).astype(BF16)
        h_scr[...] = h
        ba_ref[...] = lax.dot_general(h, wbat_ref[...], nt_dims, preferred_element_type=F32)

    out_ref[...] = lax.dot_general(h_scr[...], wt_ref[...], nt_dims, preferred_element_type=F32)


def _inproj(x, g, wt_main, wt_ba, tm, tn):
    t, d = x.shape
    n = wt_main.shape[0]
    return pl.pallas_call(
        _inproj_kernel,
        grid=(t // tm, n // tn),
        in_specs=[
            pl.BlockSpec((tm, d), lambda i, j: (i, 0)),
            pl.BlockSpec((1, d), lambda i, j: (0, 0)),
            pl.BlockSpec((tn, d), lambda i, j: (j, 0)),
            pl.BlockSpec((LANE, d), lambda i, j: (0, 0)),
        ],
        out_specs=[
            pl.BlockSpec((tm, tn), lambda i, j: (i, j)),
            pl.BlockSpec((tm, LANE), lambda i, j: (i, 0)),
        ],
        out_shape=[jax.ShapeDtypeStruct((t, n), F32), jax.ShapeDtypeStruct((t, LANE), F32)],
        scratch_shapes=[pltpu.VMEM((tm, d), BF16)],
        compiler_params=_cparams("parallel", "arbitrary"),
        name="inproj",
    )(x, g, wt_main, wt_ba)


def _s5_operators(a_re, a_im, b_re, b_im, c_re, c_im, log_dt):
    g, n = a_re.shape
    nt = g // S5_TILE_GROUPS
    ns = S5_TILE_STATE
    wide = S5_ROW * LANE
    row = lambda x: x.reshape(1, g * n)
    bt = lambda x: x.transpose(2, 0, 1).reshape(S5_GROUP, g * n)
    ct = lambda x: x.transpose(1, 0, 2).reshape(S5_GROUP, g * n)
    vec = pl.BlockSpec((1, ns), lambda t: (0, t))
    mat = pl.BlockSpec((S5_GROUP, ns), lambda t: (0, t))
    return pl.pallas_call(
        _s5_ops_kernel,
        grid=(nt,),
        in_specs=[vec, vec, vec, mat, mat, mat, mat],
        out_specs=[
            pl.BlockSpec((None, wide, 2 * ns), lambda t: (t, 0, 0)),
            pl.BlockSpec((None, wide, 2 * ns), lambda t: (t, 0, 0)),
            pl.BlockSpec((None, wide, wide), lambda t: (t, 0, 0)),
            pl.BlockSpec((None, 1, 2 * ns), lambda t: (t, 0, 0)),
        ],
        out_shape=[
            jax.ShapeDtypeStruct((nt, wide, 2 * ns), BF16),
            jax.ShapeDtypeStruct((nt, wide, 2 * ns), BF16),
            jax.ShapeDtypeStruct((nt, wide, wide), BF16),
            jax.ShapeDtypeStruct((nt, 1, 2 * ns), F32),
        ],
        compiler_params=_cparams("parallel"),
        name="s5_ops",
    )(row(a_re), row(a_im), row(jnp.repeat(log_dt, n)), bt(b_re), bt(b_im), ct(c_re), ct(c_im))


def _mm_nt_split(a, b):
    ah = a.astype(BF16)
    al = (a - ah.astype(F32)).astype(BF16)
    bh = b.astype(BF16)
    bl = (b - bh.astype(F32)).astype(BF16)
    dims = (((1,), (1,)), ((), ()))
    return (lax.dot_general(ah, bh, dims, preferred_element_type=F32)
            + lax.dot_general(ah, bl, dims, preferred_element_type=F32)
            + lax.dot_general(al, bh, dims, preferred_element_type=F32))


def _s5_ops_kernel(ar_ref, ai_ref, ldt_ref, btr_ref, bti_ref, ctr_ref, cti_ref, we_ref, wyt_ref, wk_ref, lam_ref):
    ns = S5_TILE_STATE
    ar = ar_ref[...]
    ai = ai_ref[...]
    dt = jnp.exp(ldt_ref[...])
    kk = lax.broadcasted_iota(jnp.int32, (2 * SUBLANE, ns), 0).astype(F32)
    mag = jnp.exp(ar * dt * kk)
    lr = mag * jnp.cos(ai * dt * kk)
    li = mag * jnp.sin(ai * dt * kk)
    nr = lr[1:2] - 1.0
    ni = li[1:2]
    den = ar * ar + ai * ai
    cr = (nr * ar + ni * ai) / den
    ci = (ni * ar - nr * ai) / den
    btr = btr_ref[...]
    bti = bti_ref[...]
    bbr = cr * btr - ci * bti
    bbi = cr * bti + ci * btr
    ctr = ctr_ref[...]
    cti = cti_ref[...]
    same_group = (lax.broadcasted_iota(jnp.int32, (LANE, ns), 0) // S5_GROUP
                  == lax.broadcasted_iota(jnp.int32, (LANE, ns), 1) // S5_STATE)

    def blockdiag(x):
        return jnp.where(same_group, jnp.concatenate([x] * S5_TILE_GROUPS, axis=0), 0.0)

    for j in range(S5_ROW):
        k = S5_ROW - 1 - j
        er = lr[k:k + 1] * bbr - li[k:k + 1] * bbi
        ei = lr[k:k + 1] * bbi + li[k:k + 1] * bbr
        we_ref[j * LANE:(j + 1) * LANE, :] = jnp.concatenate([blockdiag(er), blockdiag(ei)], axis=1).astype(BF16)
    bq = jnp.concatenate([blockdiag(bbr), blockdiag(bbi)], axis=1)
    kblocks = []
    for k in range(S5_ROW + 1):
        mr = ctr * lr[k:k + 1] - cti * li[k:k + 1]
        mi = ctr * li[k:k + 1] + cti * lr[k:k + 1]
        wy_k = jnp.concatenate([blockdiag(mr), -blockdiag(mi)], axis=1)
        if k >= 1:
            wyt_ref[(k - 1) * LANE:k * LANE, :] = wy_k.astype(BF16)
        if k < S5_ROW:
            kblocks.append(_mm_nt_split(bq, wy_k).astype(BF16))
    zero = jnp.zeros((LANE, LANE), BF16)
    for i in range(S5_ROW):
        for j in range(S5_ROW):
            wk_ref[i * LANE:(i + 1) * LANE, j * LANE:(j + 1) * LANE] = kblocks[j - i] if j >= i else zero
    lam_ref[...] = jnp.concatenate([lr[S5_ROW:S5_ROW + 1], li[S5_ROW:S5_ROW + 1]], axis=1)


def _s5_kernel(*refs, scan, nseq, rps):
    nu = S5_ROW
    rows = nseq * rps
    u_ref, we_ref, wy_ref, wk_ref, d_ref, lam_ref = refs[:6]
    pos = 6
    if not scan:
        h0r_ref, h0i_ref = refs[pos:pos + 2]
        pos += 2
    y_ref, hre_ref, him_ref = refs[pos:pos + 3]
    scratch = refs[pos + 3:]

    ns = S5_TILE_STATE
    us = [u_ref[pl.ds(j, rows, stride=nu), :] for j in range(nu)]
    u = jnp.concatenate(us, axis=1)
    ub = u.astype(BF16)
    e = jnp.dot(ub, we_ref[...], preferred_element_type=F32)
    lam = lam_ref[...]
    lr = lam[:, :ns]
    li = lam[:, ns:]
    if scan:
        e_scr, h_scr = scratch
        nslab = 2 * ns // LANE
        for k in range(nslab):
            for b in range(nseq):
                e_scr[k, pl.ds(b, rps, stride=nseq), :] = e[b * rps:(b + 1) * rps, k * LANE:(k + 1) * LANE]

        def body(c, carry):
            hr, hi = carry
            hcat = jnp.concatenate([hr, hi], axis=1)
            for k in range(nslab):
                h_scr[k, pl.ds(c * nseq, nseq), :] = hcat[:, k * LANE:(k + 1) * LANE]
            ec = jnp.concatenate([e_scr[k, pl.ds(c * nseq, nseq), :] for k in range(nslab)], axis=1)
            return (lr * hr - li * hi + ec[:, :ns], lr * hi + li * hr + ec[:, ns:])

        zero = jnp.zeros((nseq, ns), F32)
        hr, hi = lax.fori_loop(0, rps, body, (zero, zero), unroll=4)
        hre_ref[...] = hr
        him_ref[...] = hi
        hin = jnp.concatenate(
            [jnp.concatenate([h_scr[k, pl.ds(b, rps, stride=nseq), :] for b in range(nseq)], axis=0)
             for k in range(nslab)], axis=1)
    else:
        h0r = h0r_ref[...]
        h0i = h0i_ref[...]
        hre_ref[...] = lr * h0r - li * h0i + e[:, :ns]
        him_ref[...] = lr * h0i + li * h0r + e[:, ns:]
        hin = jnp.concatenate([h0r, h0i], axis=1)
    y = (lax.dot_general(hin.astype(BF16), wy_ref[...], (((1,), (1,)), ((), ())), preferred_element_type=F32)
         + jnp.dot(ub, wk_ref[...], preferred_element_type=F32))
    d = d_ref[...]
    for j in range(nu):
        y_ref[pl.ds(j, rows, stride=nu), :] = y[:, j * LANE:(j + 1) * LANE] + d * us[j]


def _s5_apply(proj, nseq, seqlen, ops, d, h0=None):
    we_t, wy_t, wk_t, lam = ops
    nt = we_t.shape[0]
    width = nt * LANE
    t = nseq * seqlen
    scan = h0 is None
    rps = seqlen // S5_ROW
    assert scan or rps == 1
    ns2 = 2 * S5_TILE_STATE
    tile3 = lambda shape: pl.BlockSpec((None,) + shape, lambda tt: (tt, 0, 0))
    state = pl.BlockSpec((nseq, S5_TILE_STATE), lambda tt: (0, tt))
    in_specs = [
        pl.BlockSpec((t, LANE), lambda tt: (0, tt)),
        tile3((S5_ROW * LANE, ns2)),
        tile3((ns2, S5_ROW * LANE)),
        tile3((S5_ROW * LANE, S5_ROW * LANE)),
        tile3((1, LANE)),
        tile3((1, ns2)),
    ]
    args = [proj, we_t, wy_t, wk_t, d.reshape(nt, 1, LANE), lam]
    scratch = []
    if scan:
        scratch = [pltpu.VMEM((ns2 // LANE, nseq * rps, LANE), F32)] * 2
    else:
        in_specs += [state, state]
        args += [h0[0].reshape(nseq, nt * S5_TILE_STATE), h0[1].reshape(nseq, nt * S5_TILE_STATE)]
    return pl.pallas_call(
        functools.partial(_s5_kernel, scan=scan, nseq=nseq, rps=rps),
        grid=(nt,),
        in_specs=in_specs,
        out_specs=[pl.BlockSpec((t, LANE), lambda tt: (0, tt)), state, state],
        out_shape=[jax.ShapeDtypeStruct((t, width), F32)]
        + [jax.ShapeDtypeStruct((nseq, nt * S5_TILE_STATE), F32)] * 2,
        scratch_shapes=scratch,
        compiler_params=_cparams("parallel"),
        name="s5_scan" if scan else "s5_step",
    )(*args)


def _delta_kernel(qkv_ref, qkvn_ref, z_ref, ba_ref, cs_ref, s0_ref, convw_ref, gpar_ref, onw_ref,
                  y_ref, cnew_ref, snew_ref, ext_scr, csx_scr, s_scr, xs_scr, *, nseg, lt, chained, nheads):
    rb = nseg * lt
    nsb = 1 if chained else nseg
    hd = DN_HEAD_DIM
    width = nheads * hd
    c = pl.program_id(1)
    nc = pl.num_programs(1)
    tail = CONV_WIDTH - 1
    nslab = 3 * width // LANE
    cw = convw_ref[...]

    def prepare_head(x_ref, dst_slot, h):
        if nsb > 1:
            tl = lax.broadcasted_iota(jnp.int32, (rb, 1), 0) % lt
        for part in range(3):
            cs = slice(part * width + h * hd, part * width + (h + 1) * hd)
            sl = part * nheads + h
            x = x_ref[:, cs]
            ext_scr[sl, SUBLANE:SUBLANE + rb, :] = x
            acc = x * cw[tail:tail + 1, cs]
            for k in range(1, CONV_WIDTH):
                xk = ext_scr[sl, SUBLANE - k:SUBLANE - k + rb, :]
                if nsb > 1:
                    xk = jnp.where(tl < k, csx_scr[sl, SUBLANE - k:SUBLANE - k + rb, :], xk)
                acc = acc + xk * cw[tail - k:tail - k + 1, cs]
            if nsb == 1:
                ext_scr[sl, 0:SUBLANE, :] = ext_scr[sl, rb:rb + SUBLANE, :]
            xh = acc * _sigmoid(acc)
            if part < 2:
                scale = hd ** -0.5 if part == 0 else 1.0
                xh = xh * (lax.rsqrt(jnp.sum(xh * xh, axis=-1, keepdims=True) + EPS) * scale)
            xs_scr[dst_slot, :, cs] = xh

    @pl.when(c == 0)
    def _init():
        s_scr[...] = s0_ref[...]
        if chained:
            for sl in range(nslab):
                ext_scr[sl, 0:SUBLANE, :] = cs_ref[0, :, sl * LANE:(sl + 1) * LANE]
            for h in range(nheads):
                prepare_head(qkv_ref, 0, h)
        else:
            ext_scr[:, 0:SUBLANE, :] = jnp.zeros((nslab, SUBLANE, LANE), F32)

    if chained:
        slot = c % 2
        pending = list(range(nheads))

        def tick():
            if pending:
                prepare_head(qkvn_ref, 1 - slot, pending.pop(0))

        @pl.when(c == nc - 1)
        def _conv_out():
            cnew_ref[0] = qkvn_ref[rb - tail:rb, :]
    else:
        cs_all = cs_ref[...].reshape(rb, 3 * width)
        for sl in range(nslab):
            csx_scr[sl, 0:rb, :] = cs_all[:, sl * LANE:(sl + 1) * LANE]
        csx_scr[:, rb:rb + SUBLANE, :] = jnp.zeros((nslab, SUBLANE, LANE), F32)
        slot = 0
        for h in range(nheads):
            prepare_head(qkv_ref, 0, h)
        for s in range(nsb):
            r0 = SUBLANE + (s + 1) * lt - tail
            cnew_ref[s] = jnp.concatenate([ext_scr[sl, r0:r0 + tail, :] for sl in range(nslab)], axis=1)

        def tick():
            pass
    xs = xs_scr[slot]

    ba = ba_ref[...]
    gpar = gpar_ref[...]
    beta_all = _sigmoid(ba)
    xg = ba + gpar[1:2, :]
    g_all = gpar[0:1, :] * (jnp.maximum(xg, 0.0) + jnp.log(1.0 + jnp.exp(-jnp.abs(xg))))

    ri = lax.broadcasted_iota(jnp.int32, (rb, rb), 0)
    ci = lax.broadcasted_iota(jnp.int32, (rb, rb), 1)
    same = (ri // lt) == (ci // lt)
    incl = (ri >= ci) & same
    strict = (ri > ci) & same
    tri = jnp.where(incl, 1.0, 0.0).astype(BF16)
    ghi, gmid, glo = _split3(g_all)
    gc_col = (jnp.dot(tri, ghi, preferred_element_type=F32)
              + jnp.dot(tri, gmid, preferred_element_type=F32)
              + jnp.dot(tri, glo, preferred_element_type=F32))
    sel = jnp.where(lax.broadcasted_iota(jnp.int32, (2 * SUBLANE, LANE), 1)
                    == lax.broadcasted_iota(jnp.int32, (2 * SUBLANE, LANE), 0) + nheads, 1.0, 0.0).astype(BF16)
    chi, cmid, clo = _split3(gc_col)
    nt_dims = (((1,), (1,)), ((), ()))
    gc_row = (lax.dot_general(sel, chi, nt_dims, preferred_element_type=F32)
              + lax.dot_general(sel, cmid, nt_dims, preferred_element_type=F32)
              + lax.dot_general(sel, clo, nt_dims, preferred_element_type=F32))
    lastsel = jnp.where(same & ((ci % lt) == lt - 1), 1.0, 0.0).astype(BF16)
    glast_col = (jnp.dot(lastsel, chi, preferred_element_type=F32)
                 + jnp.dot(lastsel, cmid, preferred_element_type=F32)
                 + jnp.dot(lastsel, clo, preferred_element_type=F32))

    eye = jnp.where(ri == ci, 1.0, 0.0)
    onw = onw_ref[...]
    n_sq = max(int(math.log2(lt)) - 1, 0)
    heads = range(nheads)
    rowseq = lax.broadcasted_iota(jnp.int32, (rb, 1), 0) // lt
    q, k, v, beta, gcc, glc, decay = [], [], [], [], [], [], []
    for h in heads:
        q.append(xs[:, h * hd:(h + 1) * hd])
        k.append(xs[:, width + h * hd:width + (h + 1) * hd])
        v.append(xs[:, 2 * width + h * hd:2 * width + (h + 1) * hd])
        beta.append(beta_all[:, h:h + 1])
        gcc.append(gc_col[:, nheads + h:nheads + h + 1])
        glc.append(glast_col[:, nheads + h:nheads + h + 1])
        gcr = gc_row[h:h + 1, :]
        decay.append(jnp.where(incl, jnp.exp(jnp.where(incl, gcc[h] - gcr, 0.0)), 0.0))
    qk_kk = [_mm_nt(jnp.concatenate([q[h], k[h]], axis=0), k[h]) for h in heads]
    tick()
    qk = [qk_kk[h][:rb] * decay[h] for h in heads]
    a = [jnp.where(strict, beta[h] * qk_kk[h][rb:] * decay[h], 0.0) for h in heads]
    tm = [eye - a[h] for h in heads]
    if n_sq > 0:
        bpow = [_mm(a[h], a[h]) for h in heads]
        tick()
    for r in range(n_sq):
        if r == n_sq - 1:
            tm = [tm[h] + _mm(tm[h], bpow[h]) for h in heads]
        else:
            nxt = [_mm(jnp.concatenate([tm[h], bpow[h]], axis=0), bpow[h]) for h in heads]
            bpow = [nxt[h][rb:] for h in heads]
            tm = [tm[h] + nxt[h][:rb] for h in heads]
        tick()
    egc = [jnp.exp(gcc[h]) for h in heads]
    uw = [_mm(tm[h], jnp.concatenate([v[h] * beta[h], k[h] * (beta[h] * egc[h])], axis=1)) for h in heads]
    for _ in heads:
        tick()
    u = [uw[h][:, :hd] for h in heads]
    w = [uw[h][:, hd:] for h in heads]
    qe = [q[h] * egc[h] for h in heads]
    ks = [k[h] * jnp.exp(glc[h] - gcc[h]) for h in heads]
    if chained:
        st = [s_scr[0, h] for h in heads]
        o_parts = [[] for _ in heads]
        for sg in range(nseg):
            rows = slice(sg * lt, (sg + 1) * lt)
            vnew = [u[h][rows] - _mm(w[h][rows], st[h]) for h in heads]
            above = [jnp.zeros((sg * lt, hd), F32)] if sg > 0 else []
            below = [jnp.zeros(((nseg - 1 - sg) * lt, hd), F32)] if sg < nseg - 1 else []
            vpad = [jnp.concatenate(above + [vnew[h]] + below, axis=0) if nseg > 1 else vnew[h] for h in heads]
            for h in heads:
                o_parts[h].append(_mm(qe[h][rows], st[h]) + _mm(qk[h][rows], vpad[h]))
            st = [st[h] * jnp.exp(glc[h][sg * lt:sg * lt + 1, :]) + _mm_tn(ks[h][rows], vnew[h]) for h in heads]
        for h in heads:
            s_scr[0, h] = st[h]
        o = [jnp.concatenate(o_parts[h], axis=0) if nseg > 1 else o_parts[h][0] for h in heads]
    else:
        ws, qs = [], []
        for h in heads:
            parts = [_mm(jnp.concatenate([w[h][s * lt:(s + 1) * lt], qe[h][s * lt:(s + 1) * lt]], axis=0),
                         s_scr[s, h]) for s in range(nsb)]
            ws.append(jnp.concatenate([p[:lt] for p in parts], axis=0))
            qs.append(jnp.concatenate([p[lt:] for p in parts], axis=0))
        vnew = [u[h] - ws[h] for h in heads]
        o = [qs[h] + _mm(qk[h], vnew[h]) for h in heads]
        for h in heads:
            for s in range(nsb):
                ksm = jnp.where(rowseq == s, ks[h], 0.0)
                s_scr[s, h] = s_scr[s, h] * jnp.exp(glc[h][s * lt:s * lt + 1, :]) + _mm_tn(ksm, vnew[h])
    for h in heads:
        oh = o[h] * lax.rsqrt(jnp.mean(o[h] * o[h], axis=-1, keepdims=True) + EPS) * onw
        zh = z_ref[:, h * hd:(h + 1) * hd]
        y_ref[:, h * hd:(h + 1) * hd] = oh * (zh * _sigmoid(zh))

    @pl.when(c == nc - 1)
    def _state_out():
        snew_ref[...] = s_scr[...]


def _delta_apply(proj, ba, qkv_col, z_col, nseq, seqlen, cstate, s0, conv_w, a_log, dt_bias, onorm_w):
    nheads = s0.shape[1]
    width = nheads * DN_HEAD_DIM
    tail = CONV_WIDTH - 1
    chained = seqlen >= DN_CHUNK
    if chained:
        lt, nseg, nsb = DN_CHUNK, DN_CHUNKS_PER_STEP, 1
        rb = nseg * lt
        nc = seqlen // rb
        grid = (nseq, nc)
        row = lambda b, c: b * nc + c
    else:
        assert seqlen == SUBLANE
        lt, nseg = seqlen, DN_CHUNK // seqlen
        nsb = nseg
        rb = nseg * lt
        assert nseq % nsb == 0
        nc = 1
        grid = (nseq // nsb, 1)
        row = lambda b, c: b
    cs8 = jnp.pad(cstate, ((0, 0), (SUBLANE - tail, 0), (0, 0)))
    gpar = jnp.zeros((SUBLANE, LANE), F32)
    gpar = gpar.at[0, nheads:2 * nheads].set(-jnp.exp(a_log))
    gpar = gpar.at[1, nheads:2 * nheads].set(dt_bias)
    t = nseq * seqlen
    return pl.pallas_call(
        functools.partial(_delta_kernel, nseg=nseg, lt=lt, chained=chained, nheads=nheads),
        grid=grid,
        in_specs=[
            pl.BlockSpec((rb, 3 * width), lambda b, c: (row(b, c), qkv_col // (3 * width))),
            pl.BlockSpec((rb, 3 * width), lambda b, c: (row(b, jnp.minimum(c + 1, nc - 1)), qkv_col // (3 * width))),
            pl.BlockSpec((rb, width), lambda b, c: (row(b, c), z_col // width)),
            pl.BlockSpec((rb, LANE), lambda b, c: (row(b, c), 0)),
            pl.BlockSpec((nsb, SUBLANE, 3 * width), lambda b, c: (b, 0, 0)),
            pl.BlockSpec((nsb, nheads, DN_HEAD_DIM, DN_HEAD_DIM), lambda b, c: (b, 0, 0, 0)),
            pl.BlockSpec((CONV_WIDTH, 3 * width), lambda b, c: (0, 0)),
            pl.BlockSpec((SUBLANE, LANE), lambda b, c: (0, 0)),
            pl.BlockSpec((1, DN_HEAD_DIM), lambda b, c: (0, 0)),
        ],
        out_specs=[
            pl.BlockSpec((rb, width), lambda b, c: (row(b, c), 0)),
            pl.BlockSpec((nsb, tail, 3 * width), lambda b, c: (b, 0, 0)),
            pl.BlockSpec((nsb, nheads, DN_HEAD_DIM, DN_HEAD_DIM), lambda b, c: (b, 0, 0, 0)),
        ],
        out_shape=[
            jax.ShapeDtypeStruct((t, width), F32),
            jax.ShapeDtypeStruct((nseq, tail, 3 * width), F32),
            jax.ShapeDtypeStruct(s0.shape, F32),
        ],
        scratch_shapes=[
            pltpu.VMEM((3 * width // LANE, rb + 2 * SUBLANE, LANE), F32),
            pltpu.VMEM((3 * width // LANE, rb + SUBLANE, LANE), F32),
            pltpu.VMEM((nsb, nheads, DN_HEAD_DIM, DN_HEAD_DIM), F32),
            pltpu.VMEM((2, rb, 3 * width), F32),
        ],
        compiler_params=_cparams("parallel", "arbitrary"),
        name="delta_chunk" if chained else "delta_step",
    )(proj, proj, proj, ba, cs8, s0, conv_w, gpar, onorm_w.reshape(1, DN_HEAD_DIM))


def _merge_kernel(y5_ref, ydn_ref, ga_ref, gb_ref, x_ref, wglu_ref, bglu_ref, wa_ref, wb_ref, wout_ref, gffn_ref,
                  x1_ref, h_ref):
    y = y5_ref[...]
    y = 0.5 * y * (1.0 + jnp.tanh(math.sqrt(2.0 / math.pi) * (y + 0.044715 * (y * y * y))))
    lin = jnp.dot(y.astype(BF16), wglu_ref[...], preferred_element_type=F32) + bglu_ref[...]
    glu = (y * _sigmoid(lin)).astype(BF16)
    a = jnp.dot(glu, wa_ref[...], preferred_element_type=F32)
    b = jnp.dot(ydn_ref[...].astype(BF16), wb_ref[...], preferred_element_type=F32)
    mix = (_sigmoid(ga_ref[...]) * a + _sigmoid(gb_ref[...]) * b).astype(BF16)
    x1 = x_ref[...] + jnp.dot(mix, wout_ref[...], preferred_element_type=F32)
    x1_ref[...] = x1
    h_ref[...] = _rms(x1, gffn_ref[...]).astype(BF16)


def _merge(y5, ydn, proj, x, ga_col, gb_col, w_glu, b_glu, w_a, w_b, w_out, g_ffn, tm):
    t, w5 = y5.shape
    d = x.shape[1]
    row = lambda cols: pl.BlockSpec((tm, cols), lambda i: (i, 0))
    const = lambda shape: pl.BlockSpec(shape, lambda i: (0, 0), pipeline_mode=pl.Buffered(1))
    return pl.pallas_call(
        _merge_kernel,
        grid=(t // tm,),
        in_specs=[
            row(w5), row(w5),
            pl.BlockSpec((tm, d), lambda i: (i, ga_col // d)),
            pl.BlockSpec((tm, d), lambda i: (i, gb_col // d)),
            row(d),
            const((w5, w5)), const((1, w5)), const((w5, d)), const((w5, d)), const((d, d)), const((1, d)),
        ],
        out_specs=[row(d), row(d)],
        out_shape=[jax.ShapeDtypeStruct((t, d), F32), jax.ShapeDtypeStruct((t, d), BF16)],
        compiler_params=_cparams("parallel"),
        name="merge",
    )(y5, ydn, proj, proj, x, w_glu, b_glu, w_a, w_b, w_out, g_ffn)


def _ffn_kernel(h_ref, xn_ref, wg_ref, wu_ref, wd_ref, out_ref, act_scr, *, nk, tk):
    s = pl.program_id(1)

    @pl.when(s < nk)
    def _():
        h = h_ref[...]
        gate = jnp.dot(h, wg_ref[...], preferred_element_type=F32)
        up = jnp.dot(h, wu_ref[...], preferred_element_type=F32)
        act_scr[s] = (gate * _sigmoid(gate) * up).astype(BF16)

    @pl.when(s >= nk)
    def _():
        acc = xn_ref[...]
        for kk in range(nk):
            acc = acc + jnp.dot(act_scr[kk], wd_ref[kk * tk:(kk + 1) * tk, :], preferred_element_type=F32)
        out_ref[...] = acc


def _ffn(x, h, w_gate, w_up, w_down, tm, tk, tn):
    t, d = x.shape
    dff = w_gate.shape[1]
    nk = dff // tk
    up_blk = lambda i, s: (0, jnp.minimum(s, nk - 1))
    down_blk = lambda i, s: (0, jnp.maximum(s - nk, 0))
    out_blk = lambda i, s: (i, jnp.maximum(s - nk, 0))
    return pl.pallas_call(
        functools.partial(_ffn_kernel, nk=nk, tk=tk),
        grid=(t // tm, nk + d // tn),
        in_specs=[
            pl.BlockSpec((tm, d), lambda i, s: (i, 0)),
            pl.BlockSpec((tm, tn), out_blk),
            pl.BlockSpec((d, tk), up_blk),
            pl.BlockSpec((d, tk), up_blk),
            pl.BlockSpec((dff, tn), down_blk),
        ],
        out_specs=pl.BlockSpec((tm, tn), out_blk),
        out_shape=jax.ShapeDtypeStruct((t, d), F32),
        scratch_shapes=[pltpu.VMEM((nk, tm, tk), BF16)],
        compiler_params=_cparams("parallel", "arbitrary"),
        name="ffn",
    )(h, x, w_gate, w_up, w_down)


def _ple_kernel(x_ref, p_ref, gple_ref, gfin_ref, wple_ref, wpg_ref, out_ref, h_scr, *, nj, tn, final):
    j = pl.program_id(1)

    @pl.when(j == 0)
    def _():
        h_scr[...] = _rms(x_ref[...], gple_ref[...]).astype(BF16)

    gate = _sigmoid(jnp.dot(h_scr[...], wpg_ref[...], preferred_element_type=F32))
    emb = jnp.dot(p_ref[...].astype(BF16), wple_ref[...], preferred_element_type=F32)
    upd = emb * gate
    for jj in range(nj):
        @pl.when(j == jj)
        def _(jj=jj):
            out_ref[:, jj * tn:(jj + 1) * tn] = x_ref[:, jj * tn:(jj + 1) * tn] + upd

    if final:
        @pl.when(j == nj - 1)
        def _():
            out_ref[...] = _rms(out_ref[...], gfin_ref[...])


def _ple_final(x, p, g_ple, g_final, w_ple, w_ple_gate, tm, tn, final):
    t, d = x.shape
    pd = p.shape[1]
    nj = d // tn
    return pl.pallas_call(
        functools.partial(_ple_kernel, nj=nj, tn=tn, final=final),
        grid=(t // tm, nj),
        in_specs=[
            pl.BlockSpec((tm, d), lambda i, j: (i, 0)),
            pl.BlockSpec((tm, pd), lambda i, j: (i, 0)),
            pl.BlockSpec((1, d), lambda i, j: (0, 0)),
            pl.BlockSpec((1, d), lambda i, j: (0, 0)),
            pl.BlockSpec((pd, tn), lambda i, j: (0, j)),
            pl.BlockSpec((d, tn), lambda i, j: (0, j)),
        ],
        out_specs=pl.BlockSpec((tm, d), lambda i, j: (i, 0)),
        out_shape=jax.ShapeDtypeStruct((t, d), F32),
        scratch_shapes=[pltpu.VMEM((tm, d), BF16)],
        compiler_params=_cparams("parallel", "arbitrary"),
        name="ple_final",
    )(x, p, g_ple, g_final, w_ple, w_ple_gate)


COL_U, COL_Z, COL_GA, COL_GB, COL_QKV = 0, 1024, 2048, 4096, 6144


def _prep_w_in(w_in, layer, d_model, nheads):
    w_t = jnp.swapaxes(w_in, 1, 2)
    s5w = d_model // 2
    dnw = d_model // 2
    off_u = s5w
    off_qkv = off_u + 3 * dnw
    off_z = off_qkv + dnw
    off_a = off_z + 2 * nheads
    tb = W_IN_BLOCK
    assert off_u % tb == 0 and off_qkv % tb == 0 and off_z % tb == 0 and d_model % tb == 0
    shift = off_a - off_z
    assert shift % SUBLANE == 0 and shift <= LANE
    src = ([0] + [off_qkv // tb + i for i in range(dnw // tb)] + [off_z // tb + i for i in range(2 * d_model // tb)]
           + [off_u // tb + i for i in range(3 * dnw // tb)])
    n_al = 1 + dnw // tb
    n_sh = 2 * d_model // tb
    src_tab = jnp.asarray(src, jnp.int32)
    hi_tab = jnp.asarray([(src[min(max(j, n_al), n_al + n_sh - 1)] + 1) * (tb // LANE) for j in range(len(src))],
                         jnp.int32)
    d_in = w_in.shape[1]
    return pl.pallas_call(
        functools.partial(_w_in_kernel, n_al=n_al, n_sh=n_sh, shift=shift),
        grid_spec=pltpu.PrefetchScalarGridSpec(
            num_scalar_prefetch=2,
            grid=(len(src),),
            in_specs=[pl.BlockSpec((None, tb, d_in), lambda j, lo, hi: (layer, lo[j], 0)),
                      pl.BlockSpec((None, LANE, d_in), lambda j, lo, hi: (layer, hi[j], 0))],
            out_specs=[pl.BlockSpec((tb, d_in), lambda j, lo, hi: (j, 0)),
                       pl.BlockSpec((LANE, d_in), lambda j, lo, hi: (0, 0))],
        ),
        out_shape=[jax.ShapeDtypeStruct((len(src) * tb, d_in), BF16), jax.ShapeDtypeStruct((LANE, d_in), BF16)],
        compiler_params=_cparams("arbitrary"),
        name="w_in_cast",
    )(src_tab, hi_tab, w_t, w_t)


W_IN_BLOCK = 1024


def _w_in_kernel(lo_tab, hi_tab, lo_ref, hi_ref, out_ref, ba_ref, *, n_al, n_sh, shift):
    del lo_tab, hi_tab
    j = pl.program_id(0)
    tb, d_in = out_ref.shape
    stitched = (j >= n_al) & (j < n_al + n_sh)

    @pl.when(j == n_al)
    def _():
        ba_ref[...] = jnp.concatenate([lo_ref[0:shift, :], jnp.zeros((LANE - shift, d_in), F32)],
                                      axis=0).astype(BF16)

    @pl.when(stitched)
    def _():
        out_ref[...] = jnp.concatenate([lo_ref[shift:, :], hi_ref[0:shift, :]], axis=0).astype(BF16)

    @pl.when(jnp.logical_not(stitched))
    def _():
        out_ref[...] = lo_ref[...].astype(BF16)


def _layer(x3, p3, cstate, s0, h0, lw, tm, final):
    nseq, seqlen, d = x3.shape
    t = nseq * seqlen
    x = x3.reshape(t, d)
    proj, ba = _inproj(x, lw['g_mix'], lw['w_main'], lw['w_ba'], tm, 1536)
    y5, hre, him = _s5_apply(proj, nseq, seqlen, lw['s5_ops'], lw['s5_d'], h0)
    ydn, cnew, snew = _delta_apply(proj, ba, COL_QKV, COL_Z, nseq, seqlen, cstate, s0,
                                   lw['conv_w'], lw['a_log'], lw['dt_bias'], lw['onorm_w'])
    x1, h2 = _merge(y5, ydn, proj, x, COL_GA, COL_GB, lw['w_glu'], lw['b_glu'], lw['w_a'], lw['w_b'],
                    lw['w_out'], lw['g_ffn'], 256)
    x2 = _ffn(x1, h2, lw['w_gate'], lw['w_up'], lw['w_down'], tm, 512, 512)
    y = _ple_final(x2, p3.reshape(t, -1), lw['g_ple'], lw['g_final'], lw['w_ple'], lw['w_ple_gate'],
                   tm, 512, final)
    ng = hre.shape[-1] // S5_STATE
    return (y.reshape(nseq, seqlen, d), cnew, snew,
            hre.reshape(nseq, ng, S5_STATE), him.reshape(nseq, ng, S5_STATE))


def kernel(x_prompt, x_sample, state_conv, state_delta, state_s5_re, state_s5_im, p_prompt, p_sample, g_mix, w_in, conv_w, a_log, dt_bias, onorm_w, s5_a_re, s5_a_im, s5_b_re, s5_b_im, s5_c_re, s5_c_im, s5_d, s5_log_dt, w_glu, b_glu, w_a, w_b, w_out, g_ffn, w_gate, w_up, w_down, g_ple, w_ple, w_ple_gate, g_final):
    depth = w_in.shape[0]
    d_model = x_prompt.shape[-1]
    nheads = state_delta.shape[2]
    nb_p = x_prompt.shape[0]
    f32z = functools.partial(jnp.zeros, dtype=F32)
    yp, ys = x_prompt, x_sample
    outs_p, outs_s = [], []
    for i in range(depth):
        w_main, w_ba = _prep_w_in(w_in, i, d_model, nheads)
        lw = dict(
            g_mix=g_mix[i][None], w_main=w_main, w_ba=w_ba,
            conv_w=conv_w[i], a_log=a_log[i], dt_bias=dt_bias[i], onorm_w=onorm_w[i],
            s5_ops=_s5_operators(s5_a_re[i], s5_a_im[i], s5_b_re[i], s5_b_im[i], s5_c_re[i], s5_c_im[i],
                                 s5_log_dt[i]),
            s5_d=s5_d[i],
            w_glu=w_glu[i].astype(BF16), b_glu=b_glu[i][None], w_a=w_a[i].astype(BF16), w_b=w_b[i].astype(BF16),
            w_out=w_out[i].astype(BF16), g_ffn=g_ffn[i][None],
            w_gate=w_gate[i].astype(BF16), w_up=w_up[i].astype(BF16), w_down=w_down[i].astype(BF16),
            g_ple=g_ple[i][None], w_ple=w_ple[i].astype(BF16), w_ple_gate=w_ple_gate[i].astype(BF16),
            g_final=g_final[None],
        )
        final = i == depth - 1
        yp, c1, d1, r1, m1 = _layer(yp, p_prompt[i], f32z((nb_p,) + state_conv.shape[2:]),
                                    f32z((nb_p,) + state_delta.shape[2:]), None, lw, 1024, final)
        ys, c2, d2, r2, m2 = _layer(ys, p_sample[i], state_conv[i], state_delta[i],
                                    (state_s5_re[i], state_s5_im[i]), lw, 1024, final)
        outs_p.append((c1, d1, r1, m1))
        outs_s.append((c2, d2, r2, m2))
    stack = lambda outs, k: jnp.stack([o[k] for o in outs])
    return (yp, ys,
            stack(outs_p, 0), stack(outs_p, 1), stack(outs_p, 2), stack(outs_p, 3),
            stack(outs_s, 0), stack(outs_s, 1), stack(outs_s, 2), stack(outs_s, 3))
```

```python
import functools
import math

import jax
import jax.numpy as jnp
from jax import lax
from jax.experimental import pallas as pl
from jax.experimental.pallas import tpu as pltpu

F32 = jnp.float32
BF16 = jnp.bfloat16

EPS = 1e-6
LANE = 128
SUBLANE = 8
VMEM_LIMIT_BYTES = 56 * 1024 * 1024

S5_GROUP = 16
S5_STATE = 64
S5_ROW = SUBLANE
S5_TILE_GROUPS = LANE // S5_GROUP
S5_TILE_STATE = S5_TILE_GROUPS * S5_STATE
DN_HEAD_DIM = 128
CONV_WIDTH = 4
DN_CHUNK = 64
DN_CHUNKS_PER_STEP = 2
DN_STREAMS = 2


def _cparams(*sem):
    return pltpu.CompilerParams(dimension_semantics=sem, vmem_limit_bytes=VMEM_LIMIT_BYTES)


def _mm(a, b):
    return jnp.dot(a.astype(BF16), b.astype(BF16), preferred_element_type=F32)


def _mm_nt(a, b):
    return lax.dot_general(a.astype(BF16), b.astype(BF16), (((1,), (1,)), ((), ())),
                           preferred_element_type=F32)


def _mm_tn(a, b):
    return lax.dot_general(a.astype(BF16), b.astype(BF16), (((0,), (0,)), ((), ())),
                           preferred_element_type=F32)


def _split3(x):
    hi = x.astype(BF16)
    r1 = x - hi.astype(F32)
    mid = r1.astype(BF16)
    lo = (r1 - mid.astype(F32)).astype(BF16)
    return hi, mid, lo


def _sigmoid(x):
    return 1.0 / (1.0 + jnp.exp(-x))


def _rms(x, g):
    ms = jnp.mean(x * x, axis=-1, keepdims=True)
    return x * lax.rsqrt(ms + EPS) * g


def _inproj_kernel(x_ref, g_ref, wt_ref, wbat_ref, out_ref, ba_ref, h_scr):
    nt_dims = (((1,), (1,)), ((), ()))

    @pl.when(pl.program_id(1) == 0)
    def _():
        h = _rms(x_ref[...], g_ref[...]).astype(BF16)
        h_scr[...] = h
        ba_ref[...] = lax.dot_general(h, wbat_ref[...], nt_dims, preferred_element_type=F32)

    out_ref[...] = lax.dot_general(h_scr[...], wt_ref[...], nt_dims, preferred_element_type=F32)


def _inproj(x, g, wt_main, wt_ba, tm, tn):
    t, d = x.shape
    n = wt_main.shape[0]
    return pl.pallas_call(
        _inproj_kernel,
        grid=(t // tm, n // tn),
        in_specs=[
            pl.BlockSpec((tm, d), lambda i, j: (i, 0)),
            pl.BlockSpec((1, d), lambda i, j: (0, 0)),
            pl.BlockSpec((tn, d), lambda i, j: (j, 0)),
            pl.BlockSpec((LANE, d), lambda i, j: (0, 0)),
        ],
        out_specs=[
            pl.BlockSpec((tm, tn), lambda i, j: (i, j)),
            pl.BlockSpec((tm, LANE), lambda i, j: (i, 0)),
        ],
        out_shape=[jax.ShapeDtypeStruct((t, n), F32), jax.ShapeDtypeStruct((t, LANE), F32)],
        scratch_shapes=[pltpu.VMEM((tm, d), BF16)],
        compiler_params=_cparams("parallel", "arbitrary"),
        name="inproj",
    )(x, g, wt_main, wt_ba)


def _s5_operators(a_re, a_im, b_re, b_im, c_re, c_im, log_dt):
    g, n = a_re.shape
    nt = g // S5_TILE_GROUPS
    ns = S5_TILE_STATE
    wide = S5_ROW * LANE
    row = lambda x: x.reshape(1, g * n)
    bt = lambda x: x.transpose(2, 0, 1).reshape(S5_GROUP, g * n)
    ct = lambda x: x.transpose(1, 0, 2).reshape(S5_GROUP, g * n)
    vec = pl.BlockSpec((1, ns), lambda t: (0, t))
    mat = pl.BlockSpec((S5_GROUP, ns), lambda t: (0, t))
    return pl.pallas_call(
        _s5_ops_kernel,
        grid=(nt,),
        in_specs=[vec, vec, vec, mat, mat, mat, mat],
        out_specs=[
            pl.BlockSpec((None, wide, 2 * ns), lambda t: (t, 0, 0)),
            pl.BlockSpec((None, wide, 2 * ns), lambda t: (t, 0, 0)),
            pl.BlockSpec((None, wide, wide), lambda t: (t, 0, 0)),
            pl.BlockSpec((None, 1, 2 * ns), lambda t: (t, 0, 0)),
        ],
        out_shape=[
            jax.ShapeDtypeStruct((nt, wide, 2 * ns), BF16),
            jax.ShapeDtypeStruct((nt, wide, 2 * ns), BF16),
            jax.ShapeDtypeStruct((nt, wide, wide), BF16),
            jax.ShapeDtypeStruct((nt, 1, 2 * ns), F32),
        ],
        compiler_params=_cparams("parallel"),
        name="s5_ops",
    )(row(a_re), row(a_im), row(jnp.repeat(log_dt, n)), bt(b_re), bt(b_im), ct(c_re), ct(c_im))


def _mm_nt_split(a, b):
    ah = a.astype(BF16)
    al = (a - ah.astype(F32)).astype(BF16)
    bh = b.astype(BF16)
    bl = (b - bh.astype(F32)).astype(BF16)
    dims = (((1,), (1,)), ((), ()))
    return (lax.dot_general(ah, bh, dims, preferred_element_type=F32)
            + lax.dot_general(ah, bl, dims, preferred_element_type=F32)
            + lax.dot_general(al, bh, dims, preferred_element_type=F32))


def _s5_ops_kernel(ar_ref, ai_ref, ldt_ref, btr_ref, bti_ref, ctr_ref, cti_ref, we_ref, wyt_ref, wk_ref, lam_ref):
    ns = S5_TILE_STATE
    ar = ar_ref[...]
    ai = ai_ref[...]
    dt = jnp.exp(ldt_ref[...])
    kk = lax.broadcasted_iota(jnp.int32, (2 * SUBLANE, ns), 0).astype(F32)
    mag = jnp.exp(ar * dt * kk)
    lr = mag * jnp.cos(ai * dt * kk)
    li = mag * jnp.sin(ai * dt * kk)
    nr = lr[1:2] - 1.0
    ni = li[1:2]
    den = ar * ar + ai * ai
    cr = (nr * ar + ni * ai) / den
    ci = (ni * ar - nr * ai) / den
    btr = btr_ref[...]
    bti = bti_ref[...]
    bbr = cr * btr - ci * bti
    bbi = cr * bti + ci * btr
    ctr = ctr_ref[...]
    cti = cti_ref[...]
    same_group = (lax.broadcasted_iota(jnp.int32, (LANE, ns), 0) // S5_GROUP
                  == lax.broadcasted_iota(jnp.int32, (LANE, ns), 1) // S5_STATE)

    def blockdiag(x):
        return jnp.where(same_group, jnp.concatenate([x] * S5_TILE_GROUPS, axis=0), 0.0)

    for j in range(S5_ROW):
        k = S5_ROW - 1 - j
        er = lr[k:k + 1] * bbr - li[k:k + 1] * bbi
        ei = lr[k:k + 1] * bbi + li[k:k + 1] * bbr
        we_ref[j * LANE:(j + 1) * LANE, :] = jnp.concatenate([blockdiag(er), blockdiag(ei)], axis=1).astype(BF16)
    bq = jnp.concatenate([blockdiag(bbr), blockdiag(bbi)], axis=1)
    kblocks = []
    for k in range(S5_ROW + 1):
        mr = ctr * lr[k:k + 1] - cti * li[k:k + 1]
        mi = ctr * li[k:k + 1] + cti * lr[k:k + 1]
        wy_k = jnp.concatenate([blockdiag(mr), -blockdiag(mi)], axis=1)
        if k >= 1:
            wyt_ref[(k - 1) * LANE:k * LANE, :] = wy_k.astype(BF16)
        if k < S5_ROW:
            kblocks.append(_mm_nt_split(bq, wy_k).astype(BF16))
    zero = jnp.zeros((LANE, LANE), BF16)
    for i in range(S5_ROW):
        for j in range(S5_ROW):
            wk_ref[i * LANE:(i + 1) * LANE, j * LANE:(j + 1) * LANE] = kblocks[j - i] if j >= i else zero
    lam_ref[...] = jnp.concatenate([lr[S5_ROW:S5_ROW + 1], li[S5_ROW:S5_ROW + 1]], axis=1)


def _s5_kernel(*refs, scan, nseq, rps):
    nu = S5_ROW
    rows = nseq * rps
    u_ref, we_ref, wy_ref, wk_ref, d_ref, lam_ref = refs[:6]
    pos = 6
    if not scan:
        h0r_ref, h0i_ref = refs[pos:pos + 2]
        pos += 2
    y_ref, hre_ref, him_ref = refs[pos:pos + 3]
    scratch = refs[pos + 3:]

    ns = S5_TILE_STATE
    us = [u_ref[pl.ds(j, rows, stride=nu), :] for j in range(nu)]
    u = jnp.concatenate(us, axis=1)
    ub = u.astype(BF16)
    e = jnp.dot(ub, we_ref[...], preferred_element_type=F32)
    lam = lam_ref[...]
    lr = lam[:, :ns]
    li = lam[:, ns:]
    if scan:
        e_scr, h_scr = scratch
        nslab = 2 * ns // LANE
        for k in range(nslab):
            for b in range(nseq):
                e_scr[k, pl.ds(b, rps, stride=nseq), :] = e[b * rps:(b + 1) * rps, k * LANE:(k + 1) * LANE]

        def body(c, carry):
            hr, hi = carry
            hcat = jnp.concatenate([hr, hi], axis=1)
            for k in range(nslab):
                h_scr[k, pl.ds(c * nseq, nseq), :] = hcat[:, k * LANE:(k + 1) * LANE]
            ec = jnp.concatenate([e_scr[k, pl.ds(c * nseq, nseq), :] for k in range(nslab)], axis=1)
            return (lr * hr - li * hi + ec[:, :ns], lr * hi + li * hr + ec[:, ns:])

        zero = jnp.zeros((nseq, ns), F32)
        hr, hi = lax.fori_loop(0, rps, body, (zero, zero), unroll=4)
        hre_ref[...] = hr
        him_ref[...] = hi
        hin = jnp.concatenate(
            [jnp.concatenate([h_scr[k, pl.ds(b, rps, stride=nseq), :] for b in range(nseq)], axis=0)
             for k in range(nslab)], axis=1)
    else:
        h0r = h0r_ref[...]
        h0i = h0i_ref[...]
        hre_ref[...] = lr * h0r - li * h0i + e[:, :ns]
        him_ref[...] = lr * h0i + li * h0r + e[:, ns:]
        hin = jnp.concatenate([h0r, h0i], axis=1)
    y = (lax.dot_general(hin.astype(BF16), wy_ref[...], (((1,), (1,)), ((), ())), preferred_element_type=F32)
         + jnp.dot(ub, wk_ref[...], preferred_element_type=F32))
    d = d_ref[...]
    for j in range(nu):
        y_ref[pl.ds(j, rows, stride=nu), :] = y[:, j * LANE:(j + 1) * LANE] + d * us[j]


def _s5_apply(proj, nseq, seqlen, ops, d, h0=None):
    we_t, wy_t, wk_t, lam = ops
    nt = we_t.shape[0]
    width = nt * LANE
    t = nseq * seqlen
    scan = h0 is None
    rps = seqlen // S5_ROW
    assert scan or rps == 1
    ns2 = 2 * S5_TILE_STATE
    tile3 = lambda shape: pl.BlockSpec((None,) + shape, lambda tt: (tt, 0, 0))
    state = pl.BlockSpec((nseq, S5_TILE_STATE), lambda tt: (0, tt))
    in_specs = [
        pl.BlockSpec((t, LANE), lambda tt: (0, tt)),
        tile3((S5_ROW * LANE, ns2)),
        tile3((ns2, S5_ROW * LANE)),
        tile3((S5_ROW * LANE, S5_ROW * LANE)),
        tile3((1, LANE)),
        tile3((1, ns2)),
    ]
    args = [proj, we_t, wy_t, wk_t, d.reshape(nt, 1, LANE), lam]
    scratch = []
    if scan:
        scratch = [pltpu.VMEM((ns2 // LANE, nseq * rps, LANE), F32)] * 2
    else:
        in_specs += [state, state]
        args += [h0[0].reshape(nseq, nt * S5_TILE_STATE), h0[1].reshape(nseq, nt * S5_TILE_STATE)]
    return pl.pallas_call(
        functools.partial(_s5_kernel, scan=scan, nseq=nseq, rps=rps),
        grid=(nt,),
        in_specs=in_specs,
        out_specs=[pl.BlockSpec((t, LANE), lambda tt: (0, tt)), state, state],
        out_shape=[jax.ShapeDtypeStruct((t, width), F32)]
        + [jax.ShapeDtypeStruct((nseq, nt * S5_TILE_STATE), F32)] * 2,
        scratch_shapes=scratch,
        compiler_params=_cparams("parallel"),
        name="s5_scan" if scan else "s5_step",
    )(*args)


def _delta_kernel(qkv_ref, qkvn_ref, z_ref, ba_ref, cs_ref, s0_ref, convw_ref, gpar_ref, onw_ref,
                  y_ref, cnew_ref, snew_ref, ext_scr, csx_scr, s_scr, xs_scr, *, nstr, nseg, lt, chained, nheads):
    rb = nseg * lt
    nsb = 1 if chained else nseg
    hd = DN_HEAD_DIM
    width = nheads * hd
    c = pl.program_id(1)
    nc = pl.num_programs(1)
    tail = CONV_WIDTH - 1
    nslab = 3 * width // LANE
    cw = convw_ref[...]
    streams = range(nstr)
    heads = range(nheads)
    units = [(s, h) for s in streams for h in heads]

    def prepare_head(x_ref, dst_slot, s, h):
        if nsb > 1:
            tl = lax.broadcasted_iota(jnp.int32, (rb, 1), 0) % lt
        for part in range(3):
            cs = slice(part * width + h * hd, part * width + (h + 1) * hd)
            sl = s * nslab + part * nheads + h
            x = x_ref[s, :, cs]
            ext_scr[sl, SUBLANE:SUBLANE + rb, :] = x
            acc = x * cw[tail:tail + 1, cs]
            for k in range(1, CONV_WIDTH):
                xk = ext_scr[sl, SUBLANE - k:SUBLANE - k + rb, :]
                if nsb > 1:
                    xk = jnp.where(tl < k, csx_scr[sl, SUBLANE - k:SUBLANE - k + rb, :], xk)
                acc = acc + xk * cw[tail - k:tail - k + 1, cs]
            if nsb == 1:
                ext_scr[sl, 0:SUBLANE, :] = ext_scr[sl, rb:rb + SUBLANE, :]
            xh = acc * _sigmoid(acc)
            if part < 2:
                scale = hd ** -0.5 if part == 0 else 1.0
                xh = xh * (lax.rsqrt(jnp.sum(xh * xh, axis=-1, keepdims=True) + EPS) * scale)
            xs_scr[dst_slot, s, :, cs] = xh

    @pl.when(c == 0)
    def _init():
        s_scr[...] = s0_ref[...]
        if chained:
            for s in streams:
                for sl in range(nslab):
                    ext_scr[s * nslab + sl, 0:SUBLANE, :] = cs_ref[s, :, sl * LANE:(sl + 1) * LANE]
            for s, h in units:
                prepare_head(qkv_ref, 0, s, h)
        else:
            ext_scr[:, 0:SUBLANE, :] = jnp.zeros((nslab, SUBLANE, LANE), F32)

    if chained:
        slot = c % 2
        pending = list(units)

        def tick(n=1):
            for _ in range(n):
                if pending:
                    prepare_head(qkvn_ref, 1 - slot, *pending.pop(0))

        @pl.when(c == nc - 1)
        def _conv_out():
            for s in streams:
                cnew_ref[s] = qkvn_ref[s, rb - tail:rb, :]
    else:
        cs_all = cs_ref[...].reshape(rb, 3 * width)
        for sl in range(nslab):
            csx_scr[sl, 0:rb, :] = cs_all[:, sl * LANE:(sl + 1) * LANE]
        csx_scr[:, rb:rb + SUBLANE, :] = jnp.zeros((nslab, SUBLANE, LANE), F32)
        slot = 0
        for s, h in units:
            prepare_head(qkv_ref, 0, s, h)
        for sq in range(nsb):
            r0 = SUBLANE + (sq + 1) * lt - tail
            cnew_ref[sq] = jnp.concatenate([ext_scr[sl, r0:r0 + tail, :] for sl in range(nslab)], axis=1)

        def tick(n=1):
            pass

    ri = lax.broadcasted_iota(jnp.int32, (rb, rb), 0)
    ci = lax.broadcasted_iota(jnp.int32, (rb, rb), 1)
    same = (ri // lt) == (ci // lt)
    incl = (ri >= ci) & same
    strict = (ri > ci) & same
    tri = jnp.where(incl, 1.0, 0.0).astype(BF16)
    sel = jnp.where(lax.broadcasted_iota(jnp.int32, (2 * SUBLANE, LANE), 1)
                    == lax.broadcasted_iota(jnp.int32, (2 * SUBLANE, LANE), 0) + nheads, 1.0, 0.0).astype(BF16)
    lastsel = jnp.where(same & ((ci % lt) == lt - 1), 1.0, 0.0).astype(BF16)
    nt_dims = (((1,), (1,)), ((), ()))
    gpar = gpar_ref[...]
    eye = jnp.where(ri == ci, 1.0, 0.0)
    onw = onw_ref[...]
    n_sq = max(int(math.log2(lt)) - 1, 0)
    rowseq = lax.broadcasted_iota(jnp.int32, (rb, 1), 0) // lt

    def exact3(mat, parts, dims=None):
        if dims is None:
            return sum(jnp.dot(mat, p, preferred_element_type=F32) for p in parts)
        return sum(lax.dot_general(mat, p, dims, preferred_element_type=F32) for p in parts)

    q, k, v, beta, gcc, glc, decay = {}, {}, {}, {}, {}, {}, {}
    for s in streams:
        ba = ba_ref[s]
        beta_all = _sigmoid(ba)
        xg = ba + gpar[1:2, :]
        g_all = gpar[0:1, :] * (jnp.maximum(xg, 0.0) + jnp.log(1.0 + jnp.exp(-jnp.abs(xg))))
        gc_col = exact3(tri, _split3(g_all))
        csplit = _split3(gc_col)
        gc_row = exact3(sel, csplit, nt_dims)
        glast_col = exact3(lastsel, csplit)
        xs = xs_scr[slot, s]
        for h in heads:
            u_ = (s, h)
            q[u_] = xs[:, h * hd:(h + 1) * hd]
            k[u_] = xs[:, width + h * hd:width + (h + 1) * hd]
            v[u_] = xs[:, 2 * width + h * hd:2 * width + (h + 1) * hd]
            beta[u_] = beta_all[:, h:h + 1]
            gcc[u_] = gc_col[:, nheads + h:nheads + h + 1]
            glc[u_] = glast_col[:, nheads + h:nheads + h + 1]
            gcr = gc_row[h:h + 1, :]
            decay[u_] = jnp.where(incl, jnp.exp(jnp.where(incl, gcc[u_] - gcr, 0.0)), 0.0)
    per_stage = max(len(units) // (n_sq + 3), 1)
    qk_kk = {u_: _mm_nt(jnp.concatenate([q[u_], k[u_]], axis=0), k[u_]) for u_ in units}
    tick(per_stage)
    qk = {u_: qk_kk[u_][:rb] * decay[u_] for u_ in units}
    a = {u_: jnp.where(strict, beta[u_] * qk_kk[u_][rb:] * decay[u_], 0.0) for u_ in units}
    tm = {u_: eye - a[u_] for u_ in units}
    if n_sq > 0:
        bpow = {u_: _mm(a[u_], a[u_]) for u_ in units}
        tick(per_stage)
    for r in range(n_sq):
        if r == n_sq - 1:
            tm = {u_: tm[u_] + _mm(tm[u_], bpow[u_]) for u_ in units}
        else:
            nxt = {u_: _mm(jnp.concatenate([tm[u_], bpow[u_]], axis=0), bpow[u_]) for u_ in units}
            bpow = {u_: nxt[u_][rb:] for u_ in units}
            tm = {u_: tm[u_] + nxt[u_][:rb] for u_ in units}
        tick(per_stage)
    egc = {u_: jnp.exp(gcc[u_]) for u_ in units}
    uw = {u_: _mm(tm[u_], jnp.concatenate([v[u_] * beta[u_], k[u_] * (beta[u_] * egc[u_])], axis=1)) for u_ in units}
    tick(len(units))
    u = {u_: uw[u_][:, :hd] for u_ in units}
    w = {u_: uw[u_][:, hd:] for u_ in units}
    qe = {u_: q[u_] * egc[u_] for u_ in units}
    ks = {u_: k[u_] * jnp.exp(glc[u_] - gcc[u_]) for u_ in units}
    if chained:
        st = {u_: s_scr[u_[0], u_[1]] for u_ in units}
        o_parts = {u_: [] for u_ in units}
        for sg in range(nseg):
            rows = slice(sg * lt, (sg + 1) * lt)
            vnew = {u_: u[u_][rows] - _mm(w[u_][rows], st[u_]) for u_ in units}
            above = [jnp.zeros((sg * lt, hd), F32)] if sg > 0 else []
            below = [jnp.zeros(((nseg - 1 - sg) * lt, hd), F32)] if sg < nseg - 1 else []
            vpad = {u_: jnp.concatenate(above + [vnew[u_]] + below, axis=0) if nseg > 1 else vnew[u_] for u_ in units}
            for u_ in units:
                o_parts[u_].append(_mm(qe[u_][rows], st[u_]) + _mm(qk[u_][rows], vpad[u_]))
            st = {u_: st[u_] * jnp.exp(glc[u_][sg * lt:sg * lt + 1, :]) + _mm_tn(ks[u_][rows], vnew[u_])
                  for u_ in units}
        for u_ in units:
            s_scr[u_[0], u_[1]] = st[u_]
        o = {u_: jnp.concatenate(o_parts[u_], axis=0) if nseg > 1 else o_parts[u_][0] for u_ in units}
    else:
        ws, qs = {}, {}
        for u_ in units:
            h = u_[1]
            parts = [_mm(jnp.concatenate([w[u_][sq * lt:(sq + 1) * lt], qe[u_][sq * lt:(sq + 1) * lt]], axis=0),
                         s_scr[sq, h]) for sq in range(nsb)]
            ws[u_] = jnp.concatenate([p[:lt] for p in parts], axis=0)
            qs[u_] = jnp.concatenate([p[lt:] for p in parts], axis=0)
        vnew = {u_: u[u_] - ws[u_] for u_ in units}
        o = {u_: qs[u_] + _mm(qk[u_], vnew[u_]) for u_ in units}
        for u_ in units:
            h = u_[1]
            for sq in range(nsb):
                ksm = jnp.where(rowseq == sq, ks[u_], 0.0)
                s_scr[sq, h] = s_scr[sq, h] * jnp.exp(glc[u_][sq * lt:sq * lt + 1, :]) + _mm_tn(ksm, vnew[u_])
    for s, h in units:
        ou = o[(s, h)]
        oh = ou * lax.rsqrt(jnp.mean(ou * ou, axis=-1, keepdims=True) + EPS) * onw
        zh = z_ref[s, :, h * hd:(h + 1) * hd]
        y_ref[s, :, h * hd:(h + 1) * hd] = oh * (zh * _sigmoid(zh))

    @pl.when(c == nc - 1)
    def _state_out():
        snew_ref[...] = s_scr[...]


def _delta_apply(proj, ba, qkv_col, z_col, nseq, seqlen, cstate, s0, conv_w, a_log, dt_bias, onorm_w):
    nheads = s0.shape[1]
    width = nheads * DN_HEAD_DIM
    tail = CONV_WIDTH - 1
    t, ncols = proj.shape
    chained = seqlen >= DN_CHUNK
    if chained:
        lt, nseg = DN_CHUNK, DN_CHUNKS_PER_STEP
        nstr = DN_STREAMS if nseq % DN_STREAMS == 0 else 1
        nlead = nstr
        rb = nseg * lt
        nc = seqlen // rb
        grid = (nseq // nstr, nc)
        tok3 = lambda x: x.reshape(nseq, seqlen, x.shape[-1])
        tok_idx = lambda b, c: (b, c)
    else:
        assert seqlen == SUBLANE
        lt, nseg, nstr = seqlen, DN_CHUNK // seqlen, 1
        nlead = nseg
        rb = nseg * lt
        assert nseq % nseg == 0
        nc = 1
        grid = (nseq // nseg, 1)
        tok3 = lambda x: x.reshape(1, t, x.shape[-1])
        tok_idx = lambda b, c: (0, b)
    cs8 = jnp.pad(cstate, ((0, 0), (SUBLANE - tail, 0), (0, 0)))
    gpar = jnp.zeros((SUBLANE, LANE), F32)
    gpar = gpar.at[0, nheads:2 * nheads].set(-jnp.exp(a_log))
    gpar = gpar.at[1, nheads:2 * nheads].set(dt_bias)
    proj3 = tok3(proj)
    tok = lambda cols, colblk: pl.BlockSpec((nstr, rb, cols), lambda b, c: tok_idx(b, c) + (colblk,))
    tok_next = pl.BlockSpec((nstr, rb, 3 * width),
                            lambda b, c: tok_idx(b, jnp.minimum(c + 1, nc - 1)) + (qkv_col // (3 * width),))
    lead = lambda shape: pl.BlockSpec((nlead,) + shape, lambda b, c: (b,) + (0,) * len(shape))
    const = lambda shape: pl.BlockSpec(shape, lambda b, c: (0,) * len(shape))
    nslab = 3 * width // LANE
    y, cnew, snew = pl.pallas_call(
        functools.partial(_delta_kernel, nstr=nstr, nseg=nseg, lt=lt, chained=chained, nheads=nheads),
        grid=grid,
        in_specs=[
            tok(3 * width, qkv_col // (3 * width)),
            tok_next,
            tok(width, z_col // width),
            tok(LANE, 0),
            lead((SUBLANE, 3 * width)),
            lead((nheads, DN_HEAD_DIM, DN_HEAD_DIM)),
            const((CONV_WIDTH, 3 * width)),
            const((SUBLANE, LANE)),
            const((1, DN_HEAD_DIM)),
        ],
        out_specs=[
            tok(width, 0),
            lead((tail, 3 * width)),
            lead((nheads, DN_HEAD_DIM, DN_HEAD_DIM)),
        ],
        out_shape=[
            jax.ShapeDtypeStruct(proj3.shape[:2] + (width,), F32),
            jax.ShapeDtypeStruct((nseq, tail, 3 * width), F32),
            jax.ShapeDtypeStruct(s0.shape, F32),
        ],
        scratch_shapes=[
            pltpu.VMEM((nstr * nslab, rb + 2 * SUBLANE, LANE), F32),
            pltpu.VMEM((nslab, rb + SUBLANE, LANE), F32),
            pltpu.VMEM((nlead, nheads, DN_HEAD_DIM, DN_HEAD_DIM), F32),
            pltpu.VMEM((2, nstr, rb, 3 * width), F32),
        ],
        compiler_params=_cparams("parallel", "arbitrary"),
        name="delta_chunk" if chained else "delta_step",
    )(proj3, proj3, proj3, tok3(ba), cs8, s0, conv_w, gpar, onorm_w.reshape(1, DN_HEAD_DIM))
    return y.reshape(t, width), cnew, snew


def _merge_kernel(y5_ref, ydn_ref, ga_ref, gb_ref, x_ref, wglu_ref, bglu_ref, wa_ref, wb_ref, wout_ref, gffn_ref,
                  x1_ref, h_ref):
    y = y5_ref[...]
    y = 0.5 * y * (1.0 + jnp.tanh(math.sqrt(2.0 / math.pi) * (y + 0.044715 * (y * y * y))))
    lin = jnp.dot(y.astype(BF16), wglu_ref[...], preferred_element_type=F32) + bglu_ref[...]
    glu = (y * _sigmoid(lin)).astype(BF16)
    a = jnp.dot(glu, wa_ref[...], preferred_element_type=F32)
    b = jnp.dot(ydn_ref[...].astype(BF16), wb_ref[...], preferred_element_type=F32)
    mix = (_sigmoid(ga_ref[...]) * a + _sigmoid(gb_ref[...]) * b).astype(BF16)
    x1 = x_ref[...] + jnp.dot(mix, wout_ref[...], preferred_element_type=F32)
    x1_ref[...] = x1
    h_ref[...] = _rms(x1, gffn_ref[...]).astype(BF16)


def _merge(y5, ydn, proj, x, ga_col, gb_col, w_glu, b_glu, w_a, w_b, w_out, g_ffn, tm):
    t, w5 = y5.shape
    d = x.shape[1]
    row = lambda cols: pl.BlockSpec((tm, cols), lambda i: (i, 0))
    const = lambda shape: pl.BlockSpec(shape, lambda i: (0, 0), pipeline_mode=pl.Buffered(1))
    return pl.pallas_call(
        _merge_kernel,
        grid=(t // tm,),
        in_specs=[
            row(w5), row(w5),
            pl.BlockSpec((tm, d), lambda i: (i, ga_col // d)),
            pl.BlockSpec((tm, d), lambda i: (i, gb_col // d)),
            row(d),
            const((w5, w5)), const((1, w5)), const((w5, d)), const((w5, d)), const((d, d)), const((1, d)),
        ],
        out_specs=[row(d), row(d)],
        out_shape=[jax.ShapeDtypeStruct((t, d), F32), jax.ShapeDtypeStruct((t, d), BF16)],
        compiler_params=_cparams("parallel"),
        name="merge",
    )(y5, ydn, proj, proj, x, w_glu, b_glu, w_a, w_b, w_out, g_ffn)


def _ffn_kernel(h_ref, xn_ref, wg_ref, wu_ref, wd_ref, out_ref, act_scr, *, nk, tk):
    s = pl.program_id(1)

    @pl.when(s < nk)
    def _():
        h = h_ref[...]
        gate = jnp.dot(h, wg_ref[...], preferred_element_type=F32)
        up = jnp.dot(h, wu_ref[...], preferred_element_type=F32)
        act_scr[s] = (gate * _sigmoid(gate) * up).astype(BF16)

    @pl.when(s >= nk)
    def _():
        acc = xn_ref[...]
        for kk in range(nk):
            acc = acc + jnp.dot(act_scr[kk], wd_ref[kk * tk:(kk + 1) * tk, :], preferred_element_type=F32)
        out_ref[...] = acc


def _ffn(x, h, w_gate, w_up, w_down, tm, tk, tn):
    t, d = x.shape
    dff = w_gate.shape[1]
    nk = dff // tk
    up_blk = lambda i, s: (0, jnp.minimum(s, nk - 1))
    down_blk = lambda i, s: (0, jnp.maximum(s - nk, 0))
    out_blk = lambda i, s: (i, jnp.maximum(s - nk, 0))
    return pl.pallas_call(
        functools.partial(_ffn_kernel, nk=nk, tk=tk),
        grid=(t // tm, nk + d // tn),
        in_specs=[
            pl.BlockSpec((tm, d), lambda i, s: (i, 0)),
            pl.BlockSpec((tm, tn), out_blk),
            pl.BlockSpec((d, tk), up_blk),
            pl.BlockSpec((d, tk), up_blk),
            pl.BlockSpec((dff, tn), down_blk),
        ],
        out_specs=pl.BlockSpec((tm, tn), out_blk),
        out_shape=jax.ShapeDtypeStruct((t, d), F32),
        scratch_shapes=[pltpu.VMEM((nk, tm, tk), BF16)],
        compiler_params=_cparams("parallel", "arbitrary"),
        name="ffn",
    )(h, x, w_gate, w_up, w_down)


def _ple_kernel(x_ref, p_ref, gple_ref, gfin_ref, wple_ref, wpg_ref, out_ref, h_scr, *, nj, tn, final):
    j = pl.program_id(1)

    @pl.when(j == 0)
    def _():
        h_scr[...] = _rms(x_ref[...], gple_ref[...]).astype(BF16)

    gate = _sigmoid(jnp.dot(h_scr[...], wpg_ref[...], preferred_element_type=F32))
    emb = jnp.dot(p_ref[...].astype(BF16), wple_ref[...], preferred_element_type=F32)
    upd = emb * gate
    for jj in range(nj):
        @pl.when(j == jj)
        def _(jj=jj):
            out_ref[:, jj * tn:(jj + 1) * tn] = x_ref[:, jj * tn:(jj + 1) * tn] + upd

    if final:
        @pl.when(j == nj - 1)
        def _():
            out_ref[...] = _rms(out_ref[...], gfin_ref[...])


def _ple_final(x, p, g_ple, g_final, w_ple, w_ple_gate, tm, tn, final):
    t, d = x.shape
    pd = p.shape[1]
    nj = d // tn
    return pl.pallas_call(
        functools.partial(_ple_kernel, nj=nj, tn=tn, final=final),
        grid=(t // tm, nj),
        in_specs=[
            pl.BlockSpec((tm, d), lambda i, j: (i, 0)),
            pl.BlockSpec((tm, pd), lambda i, j: (i, 0)),
            pl.BlockSpec((1, d), lambda i, j: (0, 0)),
            pl.BlockSpec((1, d), lambda i, j: (0, 0)),
            pl.BlockSpec((pd, tn), lambda i, j: (0, j)),
            pl.BlockSpec((d, tn), lambda i, j: (0, j)),
        ],
        out_specs=pl.BlockSpec((tm, d), lambda i, j: (i, 0)),
        out_shape=jax.ShapeDtypeStruct((t, d), F32),
        scratch_shapes=[pltpu.VMEM((tm, d), BF16)],
        compiler_params=_cparams("parallel", "arbitrary"),
        name="ple_final",
    )(x, p, g_ple, g_final, w_ple, w_ple_gate)


COL_U, COL_Z, COL_GA, COL_GB, COL_QKV = 0, 1024, 2048, 4096, 6144


def _prep_w_in(w_in, layer, d_model, nheads):
    w_t = jnp.swapaxes(w_in, 1, 2)
    s5w = d_model // 2
    dnw = d_model // 2
    off_u = s5w
    off_qkv = off_u + 3 * dnw
    off_z = off_qkv + dnw
    off_a = off_z + 2 * nheads
    tb = W_IN_BLOCK
    assert off_u % tb == 0 and off_qkv % tb == 0 and off_z % tb == 0 and d_model % tb == 0
    shift = off_a - off_z
    assert shift % SUBLANE == 0 and shift <= LANE
    src = ([0] + [off_qkv // tb + i for i in range(dnw // tb)] + [off_z // tb + i for i in range(2 * d_model // tb)]
           + [off_u // tb + i for i in range(3 * dnw // tb)])
    n_al = 1 + dnw // tb
    n_sh = 2 * d_model // tb
    src_tab = jnp.asarray(src, jnp.int32)
    hi_tab = jnp.asarray([(src[min(max(j, n_al), n_al + n_sh - 1)] + 1) * (tb // LANE) for j in range(len(src))],
                         jnp.int32)
    d_in = w_in.shape[1]
    return pl.pallas_call(
        functools.partial(_w_in_kernel, n_al=n_al, n_sh=n_sh, shift=shift),
        grid_spec=pltpu.PrefetchScalarGridSpec(
            num_scalar_prefetch=2,
            grid=(len(src),),
            in_specs=[pl.BlockSpec((None, tb, d_in), lambda j, lo, hi: (layer, lo[j], 0)),
                      pl.BlockSpec((None, LANE, d_in), lambda j, lo, hi: (layer, hi[j], 0))],
            out_specs=[pl.BlockSpec((tb, d_in), lambda j, lo, hi: (j, 0)),
                       pl.BlockSpec((LANE, d_in), lambda j, lo, hi: (0, 0))],
        ),
        out_shape=[jax.ShapeDtypeStruct((len(src) * tb, d_in), BF16), jax.ShapeDtypeStruct((LANE, d_in), BF16)],
        compiler_params=_cparams("arbitrary"),
        name="w_in_cast",
    )(src_tab, hi_tab, w_t, w_t)


W_IN_BLOCK = 1024


def _w_in_kernel(lo_tab, hi_tab, lo_ref, hi_ref, out_ref, ba_ref, *, n_al, n_sh, shift):
    del lo_tab, hi_tab
    j = pl.program_id(0)
    tb, d_in = out_ref.shape
    stitched = (j >= n_al) & (j < n_al + n_sh)

    @pl.when(j == n_al)
    def _():
        ba_ref[...] = jnp.concatenate([lo_ref[0:shift, :], jnp.zeros((LANE - shift, d_in), F32)],
                                      axis=0).astype(BF16)

    @pl.when(stitched)
    def _():
        out_ref[...] = jnp.concatenate([lo_ref[shift:, :], hi_ref[0:shift, :]], axis=0).astype(BF16)

    @pl.when(jnp.logical_not(stitched))
    def _():
        out_ref[...] = lo_ref[...].astype(BF16)


def _layer(x3, p3, cstate, s0, h0, lw, tm, final):
    nseq, seqlen, d = x3.shape
    t = nseq * seqlen
    x = x3.reshape(t, d)
    proj, ba = _inproj(x, lw['g_mix'], lw['w_main'], lw['w_ba'], tm, 1536)
    y5, hre, him = _s5_apply(proj, nseq, seqlen, lw['s5_ops'], lw['s5_d'], h0)
    ydn, cnew, snew = _delta_apply(proj, ba, COL_QKV, COL_Z, nseq, seqlen, cstate, s0,
                                   lw['conv_w'], lw['a_log'], lw['dt_bias'], lw['onorm_w'])
    x1, h2 = _merge(y5, ydn, proj, x, COL_GA, COL_GB, lw['w_glu'], lw['b_glu'], lw['w_a'], lw['w_b'],
                    lw['w_out'], lw['g_ffn'], 256)
    x2 = _ffn(x1, h2, lw['w_gate'], lw['w_up'], lw['w_down'], tm, 512, 512)
    y = _ple_final(x2, p3.reshape(t, -1), lw['g_ple'], lw['g_final'], lw['w_ple'], lw['w_ple_gate'],
                   tm, 1024, final)
    ng = hre.shape[-1] // S5_STATE
    return (y.reshape(nseq, seqlen, d), cnew, snew,
            hre.reshape(nseq, ng, S5_STATE), him.reshape(nseq, ng, S5_STATE))


def kernel(x_prompt, x_sample, state_conv, state_delta, state_s5_re, state_s5_im, p_prompt, p_sample, g_mix, w_in, conv_w, a_log, dt_bias, onorm_w, s5_a_re, s5_a_im, s5_b_re, s5_b_im, s5_c_re, s5_c_im, s5_d, s5_log_dt, w_glu, b_glu, w_a, w_b, w_out, g_ffn, w_gate, w_up, w_down, g_ple, w_ple, w_ple_gate, g_final):
    depth = w_in.shape[0]
    d_model = x_prompt.shape[-1]
    nheads = state_delta.shape[2]
    nb_p = x_prompt.shape[0]
    f32z = functools.partial(jnp.zeros, dtype=F32)
    yp, ys = x_prompt, x_sample
    outs_p, outs_s = [], []
    for i in range(depth):
        w_main, w_ba = _prep_w_in(w_in, i, d_model, nheads)
        lw = dict(
            g_mix=g_mix[i][None], w_main=w_main, w_ba=w_ba,
            conv_w=conv_w[i], a_log=a_log[i], dt_bias=dt_bias[i], onorm_w=onorm_w[i],
            s5_ops=_s5_operators(s5_a_re[i], s5_a_im[i], s5_b_re[i], s5_b_im[i], s5_c_re[i], s5_c_im[i],
                                 s5_log_dt[i]),
            s5_d=s5_d[i],
            w_glu=w_glu[i].astype(BF16), b_glu=b_glu[i][None], w_a=w_a[i].astype(BF16), w_b=w_b[i].astype(BF16),
            w_out=w_out[i].astype(BF16), g_ffn=g_ffn[i][None],
            w_gate=w_gate[i].astype(BF16), w_up=w_up[i].astype(BF16), w_down=w_down[i].astype(BF16),
            g_ple=g_ple[i][None], w_ple=w_ple[i].astype(BF16), w_ple_gate=w_ple_gate[i].astype(BF16),
            g_final=g_final[None],
        )
        final = i == depth - 1
        yp, c1, d1, r1, m1 = _layer(yp, p_prompt[i], f32z((nb_p,) + state_conv.shape[2:]),
                                    f32z((nb_p,) + state_delta.shape[2:]), None, lw, 1024, final)
        ys, c2, d2, r2, m2 = _layer(ys, p_sample[i], state_conv[i], state_delta[i],
                                    (state_s5_re[i], state_s5_im[i]), lw, 1024, final)
        outs_p.append((c1, d1, r1, m1))
        outs_s.append((c2, d2, r2, m2))
    stack = lambda outs, k: jnp.stack([o[k] for o in outs])
    return (yp, ys,
            stack(outs_p, 0), stack(outs_p, 1), stack(outs_p, 2), stack(outs_p, 3),
            stack(outs_s, 0), stack(outs_s, 1), stack(outs_s, 2), stack(outs_s, 3))
```

```python
import functools
import math

import jax
import jax.numpy as jnp
from jax import lax
from jax.experimental import pallas as pl
from jax.experimental.pallas import tpu as pltpu

F32 = jnp.float32
BF16 = jnp.bfloat16

EPS = 1e-6
LANE = 128
SUBLANE = 8
VMEM_LIMIT_BYTES = 56 * 1024 * 1024

S5_GROUP = 16
S5_STATE = 64
S5_ROW = SUBLANE
S5_TILE_GROUPS = LANE // S5_GROUP
S5_TILE_STATE = S5_TILE_GROUPS * S5_STATE
DN_HEAD_DIM = 128
CONV_WIDTH = 4
DN_CHUNK = 64
DN_CHUNKS_PER_STEP = 2
DN_STEP_ROWS = 128
DN_STREAMS = 2


def _cparams(*sem):
    return pltpu.CompilerParams(dimension_semantics=sem, vmem_limit_bytes=VMEM_LIMIT_BYTES)


def _mm(a, b):
    return jnp.dot(a.astype(BF16), b.astype(BF16), preferred_element_type=F32)


def _mm_nt(a, b):
    return lax.dot_general(a.astype(BF16), b.astype(BF16), (((1,), (1,)), ((), ())),
                           preferred_element_type=F32)


def _mm_tn(a, b):
    return lax.dot_general(a.astype(BF16), b.astype(BF16), (((0,), (0,)), ((), ())),
                           preferred_element_type=F32)


def _split3(x):
    hi = x.astype(BF16)
    r1 = x - hi.astype(F32)
    mid = r1.astype(BF16)
    lo = (r1 - mid.astype(F32)).astype(BF16)
    return hi, mid, lo


def _sigmoid(x):
    return 1.0 / (1.0 + jnp.exp(-x))


def _rms(x, g):
    ms = jnp.mean(x * x, axis=-1, keepdims=True)
    return x * lax.rsqrt(ms + EPS) * g


def _inproj_kernel(x_ref, g_ref, wt_ref, wbat_ref, out_ref, ba_ref, h_scr):
    nt_dims = (((1,), (1,)), ((), ()))

    @pl.when(pl.program_id(1) == 0)
    def _():
        h = _rms(x_ref[...], g_ref[...]).astype(BF16)
        h_scr[...] = h
        ba_ref[...] = lax.dot_general(h, wbat_ref[...], nt_dims, preferred_element_type=F32)

    out_ref[...] = lax.dot_general(h_scr[...], wt_ref[...], nt_dims, preferred_element_type=F32)


def _inproj(x, g, wt_main, wt_ba, tm, tn):
    t, d = x.shape
    n = wt_main.shape[0]
    return pl.pallas_call(
        _inproj_kernel,
        grid=(t // tm, n // tn),
        in_specs=[
            pl.BlockSpec((tm, d), lambda i, j: (i, 0)),
            pl.BlockSpec((1, d), lambda i, j: (0, 0)),
            pl.BlockSpec((tn, d), lambda i, j: (j, 0)),
            pl.BlockSpec((LANE, d), lambda i, j: (0, 0)),
        ],
        out_specs=[
            pl.BlockSpec((tm, tn), lambda i, j: (i, j)),
            pl.BlockSpec((tm, LANE), lambda i, j: (i, 0)),
        ],
        out_shape=[jax.ShapeDtypeStruct((t, n), F32), jax.ShapeDtypeStruct((t, LANE), F32)],
        scratch_shapes=[pltpu.VMEM((tm, d), BF16)],
        compiler_params=_cparams("parallel", "arbitrary"),
        name="inproj",
    )(x, g, wt_main, wt_ba)


def _s5_operators(a_re, a_im, b_re, b_im, c_re, c_im, log_dt):
    g, n = a_re.shape
    nt = g // S5_TILE_GROUPS
    ns = S5_TILE_STATE
    wide = S5_ROW * LANE
    row = lambda x: x.reshape(1, g * n)
    bt = lambda x: x.transpose(2, 0, 1).reshape(S5_GROUP, g * n)
    ct = lambda x: x.transpose(1, 0, 2).reshape(S5_GROUP, g * n)
    vec = pl.BlockSpec((1, ns), lambda t: (0, t))
    mat = pl.BlockSpec((S5_GROUP, ns), lambda t: (0, t))
    return pl.pallas_call(
        _s5_ops_kernel,
        grid=(nt,),
        in_specs=[vec, vec, vec, mat, mat, mat, mat],
        out_specs=[
            pl.BlockSpec((None, wide, 2 * ns), lambda t: (t, 0, 0)),
            pl.BlockSpec((None, wide, 2 * ns), lambda t: (t, 0, 0)),
            pl.BlockSpec((None, wide, wide), lambda t: (t, 0, 0)),
            pl.BlockSpec((None, 1, 2 * ns), lambda t: (t, 0, 0)),
        ],
        out_shape=[
            jax.ShapeDtypeStruct((nt, wide, 2 * ns), BF16),
            jax.ShapeDtypeStruct((nt, wide, 2 * ns), BF16),
            jax.ShapeDtypeStruct((nt, wide, wide), BF16),
            jax.ShapeDtypeStruct((nt, 1, 2 * ns), F32),
        ],
        compiler_params=_cparams("parallel"),
        name="s5_ops",
    )(row(a_re), row(a_im), row(jnp.repeat(log_dt, n)), bt(b_re), bt(b_im), ct(c_re), ct(c_im))


def _mm_nt_split(a, b):
    ah = a.astype(BF16)
    al = (a - ah.astype(F32)).astype(BF16)
    bh = b.astype(BF16)
    bl = (b - bh.astype(F32)).astype(BF16)
    dims = (((1,), (1,)), ((), ()))
    return (lax.dot_general(ah, bh, dims, preferred_element_type=F32)
            + lax.dot_general(ah, bl, dims, preferred_element_type=F32)
            + lax.dot_general(al, bh, dims, preferred_element_type=F32))


def _s5_ops_kernel(ar_ref, ai_ref, ldt_ref, btr_ref, bti_ref, ctr_ref, cti_ref, we_ref, wyt_ref, wk_ref, lam_ref):
    ns = S5_TILE_STATE
    ar = ar_ref[...]
    ai = ai_ref[...]
    dt = jnp.exp(ldt_ref[...])
    kk = lax.broadcasted_iota(jnp.int32, (2 * SUBLANE, ns), 0).astype(F32)
    mag = jnp.exp(ar * dt * kk)
    lr = mag * jnp.cos(ai * dt * kk)
    li = mag * jnp.sin(ai * dt * kk)
    nr = lr[1:2] - 1.0
    ni = li[1:2]
    den = ar * ar + ai * ai
    cr = (nr * ar + ni * ai) / den
    ci = (ni * ar - nr * ai) / den
    btr = btr_ref[...]
    bti = bti_ref[...]
    bbr = cr * btr - ci * bti
    bbi = cr * bti + ci * btr
    ctr = ctr_ref[...]
    cti = cti_ref[...]
    same_group = (lax.broadcasted_iota(jnp.int32, (LANE, ns), 0) // S5_GROUP
                  == lax.broadcasted_iota(jnp.int32, (LANE, ns), 1) // S5_STATE)

    def blockdiag(x):
        return jnp.where(same_group, jnp.concatenate([x] * S5_TILE_GROUPS, axis=0), 0.0)

    for j in range(S5_ROW):
        k = S5_ROW - 1 - j
        er = lr[k:k + 1] * bbr - li[k:k + 1] * bbi
        ei = lr[k:k + 1] * bbi + li[k:k + 1] * bbr
        we_ref[j * LANE:(j + 1) * LANE, :] = jnp.concatenate([blockdiag(er), blockdiag(ei)], axis=1).astype(BF16)
    bq = jnp.concatenate([blockdiag(bbr), blockdiag(bbi)], axis=1)
    kblocks = []
    for k in range(S5_ROW + 1):
        mr = ctr * lr[k:k + 1] - cti * li[k:k + 1]
        mi = ctr * li[k:k + 1] + cti * lr[k:k + 1]
        wy_k = jnp.concatenate([blockdiag(mr), -blockdiag(mi)], axis=1)
        if k >= 1:
            wyt_ref[(k - 1) * LANE:k * LANE, :] = wy_k.astype(BF16)
        if k < S5_ROW:
            kblocks.append(_mm_nt_split(bq, wy_k).astype(BF16))
    zero = jnp.zeros((LANE, LANE), BF16)
    for i in range(S5_ROW):
        for j in range(S5_ROW):
            wk_ref[i * LANE:(i + 1) * LANE, j * LANE:(j + 1) * LANE] = kblocks[j - i] if j >= i else zero
    lam_ref[...] = jnp.concatenate([lr[S5_ROW:S5_ROW + 1], li[S5_ROW:S5_ROW + 1]], axis=1)


def _s5_kernel(*refs, scan, nseq, rps):
    nu = S5_ROW
    rows = nseq * rps
    u_ref, we_ref, wy_ref, wk_ref, d_ref, lam_ref = refs[:6]
    pos = 6
    if not scan:
        h0r_ref, h0i_ref = refs[pos:pos + 2]
        pos += 2
    y_ref, hre_ref, him_ref = refs[pos:pos + 3]
    scratch = refs[pos + 3:]

    ns = S5_TILE_STATE
    us = [u_ref[pl.ds(j, rows, stride=nu), :] for j in range(nu)]
    u = jnp.concatenate(us, axis=1)
    ub = u.astype(BF16)
    e = jnp.dot(ub, we_ref[...], preferred_element_type=F32)
    lam = lam_ref[...]
    lr = lam[:, :ns]
    li = lam[:, ns:]
    if scan:
        e_scr, h_scr = scratch
        nslab = 2 * ns // LANE
        for k in range(nslab):
            for b in range(nseq):
                e_scr[k, pl.ds(b, rps, stride=nseq), :] = e[b * rps:(b + 1) * rps, k * LANE:(k + 1) * LANE]

        def body(c, carry):
            hr, hi = carry
            hcat = jnp.concatenate([hr, hi], axis=1)
            for k in range(nslab):
                h_scr[k, pl.ds(c * nseq, nseq), :] = hcat[:, k * LANE:(k + 1) * LANE]
            ec = jnp.concatenate([e_scr[k, pl.ds(c * nseq, nseq), :] for k in range(nslab)], axis=1)
            return (lr * hr - li * hi + ec[:, :ns], lr * hi + li * hr + ec[:, ns:])

        zero = jnp.zeros((nseq, ns), F32)
        hr, hi = lax.fori_loop(0, rps, body, (zero, zero), unroll=4)
        y_local = jnp.dot(ub, wk_ref[...], preferred_element_type=F32)
        hre_ref[...] = hr
        him_ref[...] = hi
        hin = jnp.concatenate(
            [jnp.concatenate([h_scr[k, pl.ds(b, rps, stride=nseq), :] for b in range(nseq)], axis=0)
             for k in range(nslab)], axis=1)
    else:
        h0r = h0r_ref[...]
        h0i = h0i_ref[...]
        hre_ref[...] = lr * h0r - li * h0i + e[:, :ns]
        him_ref[...] = lr * h0i + li * h0r + e[:, ns:]
        hin = jnp.concatenate([h0r, h0i], axis=1)
        y_local = jnp.dot(ub, wk_ref[...], preferred_element_type=F32)
    y = y_local + lax.dot_general(hin.astype(BF16), wy_ref[...], (((1,), (1,)), ((), ())),
                                  preferred_element_type=F32)
    d = d_ref[...]
    for j in range(nu):
        y_ref[pl.ds(j, rows, stride=nu), :] = y[:, j * LANE:(j + 1) * LANE] + d * us[j]


def _s5_apply(proj, nseq, seqlen, ops, d, h0=None):
    we_t, wy_t, wk_t, lam = ops
    nt = we_t.shape[0]
    width = nt * LANE
    t = nseq * seqlen
    scan = h0 is None
    rps = seqlen // S5_ROW
    assert scan or rps == 1
    ns2 = 2 * S5_TILE_STATE
    tile3 = lambda shape: pl.BlockSpec((None,) + shape, lambda tt: (tt, 0, 0))
    state = pl.BlockSpec((nseq, S5_TILE_STATE), lambda tt: (0, tt))
    in_specs = [
        pl.BlockSpec((t, LANE), lambda tt: (0, tt)),
        tile3((S5_ROW * LANE, ns2)),
        tile3((ns2, S5_ROW * LANE)),
        tile3((S5_ROW * LANE, S5_ROW * LANE)),
        tile3((1, LANE)),
        tile3((1, ns2)),
    ]
    args = [proj, we_t, wy_t, wk_t, d.reshape(nt, 1, LANE), lam]
    scratch = []
    if scan:
        scratch = [pltpu.VMEM((ns2 // LANE, nseq * rps, LANE), F32)] * 2
    else:
        in_specs += [state, state]
        args += [h0[0].reshape(nseq, nt * S5_TILE_STATE), h0[1].reshape(nseq, nt * S5_TILE_STATE)]
    return pl.pallas_call(
        functools.partial(_s5_kernel, scan=scan, nseq=nseq, rps=rps),
        grid=(nt,),
        in_specs=in_specs,
        out_specs=[pl.BlockSpec((t, LANE), lambda tt: (0, tt)), state, state],
        out_shape=[jax.ShapeDtypeStruct((t, width), F32)]
        + [jax.ShapeDtypeStruct((nseq, nt * S5_TILE_STATE), F32)] * 2,
        scratch_shapes=scratch,
        compiler_params=_cparams("parallel"),
        name="s5_scan" if scan else "s5_step",
    )(*args)


def _delta_kernel(qkv_ref, qkvn_ref, z_ref, ba_ref, cs_ref, s0_ref, convw_ref, gpar_ref, onw_ref,
                  y_ref, cnew_ref, snew_ref, ext_scr, csx_scr, s_scr, xs_scr, *, nstr, nseg, lt, chained, nheads):
    rb = nseg * lt
    nsb = 1 if chained else nseg
    hd = DN_HEAD_DIM
    width = nheads * hd
    c = pl.program_id(1)
    nc = pl.num_programs(1)
    tail = CONV_WIDTH - 1
    nslab = 3 * width // LANE
    cw = convw_ref[...]
    streams = range(nstr)
    heads = range(nheads)
    units = [(s, h) for s in streams for h in heads]

    def prepare_head(x_ref, dst_slot, s, h):
        if nsb > 1:
            tl = lax.broadcasted_iota(jnp.int32, (rb, 1), 0) % lt
        for part in range(3):
            cs = slice(part * width + h * hd, part * width + (h + 1) * hd)
            sl = s * nslab + part * nheads + h
            x = x_ref[s, :, cs]
            ext_scr[sl, SUBLANE:SUBLANE + rb, :] = x
            acc = x * cw[tail:tail + 1, cs]
            for k in range(1, CONV_WIDTH):
                xk = ext_scr[sl, SUBLANE - k:SUBLANE - k + rb, :]
                if nsb > 1:
                    xk = jnp.where(tl < k, csx_scr[sl, SUBLANE - k:SUBLANE - k + rb, :], xk)
                acc = acc + xk * cw[tail - k:tail - k + 1, cs]
            if nsb == 1:
                ext_scr[sl, 0:SUBLANE, :] = ext_scr[sl, rb:rb + SUBLANE, :]
            xh = acc * _sigmoid(acc)
            if part < 2:
                scale = hd ** -0.5 if part == 0 else 1.0
                xh = xh * (lax.rsqrt(jnp.sum(xh * xh, axis=-1, keepdims=True) + EPS) * scale)
            xs_scr[dst_slot, s, :, cs] = xh

    @pl.when(c == 0)
    def _init():
        s_scr[...] = s0_ref[...]
        if chained:
            for s in streams:
                for sl in range(nslab):
                    ext_scr[s * nslab + sl, 0:SUBLANE, :] = cs_ref[s, :, sl * LANE:(sl + 1) * LANE]
            for s, h in units:
                prepare_head(qkv_ref, 0, s, h)
        else:
            ext_scr[:, 0:SUBLANE, :] = jnp.zeros((nslab, SUBLANE, LANE), F32)

    if chained:
        slot = c % 2
        pending = list(units)

        def tick(n=1):
            for _ in range(n):
                if pending:
                    prepare_head(qkvn_ref, 1 - slot, *pending.pop(0))

        @pl.when(c == nc - 1)
        def _conv_out():
            for s in streams:
                cnew_ref[s] = qkvn_ref[s, rb - tail:rb, :]
    else:
        cs_all = cs_ref[...].reshape(rb, 3 * width)
        for sl in range(nslab):
            csx_scr[sl, 0:rb, :] = cs_all[:, sl * LANE:(sl + 1) * LANE]
        csx_scr[:, rb:rb + SUBLANE, :] = jnp.zeros((nslab, SUBLANE, LANE), F32)
        slot = 0
        for s, h in units:
            prepare_head(qkv_ref, 0, s, h)
        for sq in range(nsb):
            r0 = SUBLANE + (sq + 1) * lt - tail
            cnew_ref[sq] = jnp.concatenate([ext_scr[sl, r0:r0 + tail, :] for sl in range(nslab)], axis=1)

        def tick(n=1):
            pass

    ri = lax.broadcasted_iota(jnp.int32, (rb, rb), 0)
    ci = lax.broadcasted_iota(jnp.int32, (rb, rb), 1)
    same = (ri // lt) == (ci // lt)
    incl = (ri >= ci) & same
    strict = (ri > ci) & same
    tri = jnp.where(incl, 1.0, 0.0).astype(BF16)
    sel = jnp.where(lax.broadcasted_iota(jnp.int32, (2 * SUBLANE, LANE), 1)
                    == lax.broadcasted_iota(jnp.int32, (2 * SUBLANE, LANE), 0) + nheads, 1.0, 0.0).astype(BF16)
    lastsel = jnp.where(same & ((ci % lt) == lt - 1), 1.0, 0.0).astype(BF16)
    nt_dims = (((1,), (1,)), ((), ()))
    gpar = gpar_ref[...]
    eye = jnp.where(ri == ci, 1.0, 0.0)
    onw = onw_ref[...]
    n_sq = max(int(math.log2(lt)) - 1, 0)
    rowseq = lax.broadcasted_iota(jnp.int32, (rb, 1), 0) // lt

    def exact3(mat, parts, dims=None):
        if dims is None:
            return sum(jnp.dot(mat, p, preferred_element_type=F32) for p in parts)
        return sum(lax.dot_general(mat, p, dims, preferred_element_type=F32) for p in parts)

    q, k, v, beta, gcc, glc, decay = {}, {}, {}, {}, {}, {}, {}
    for s in streams:
        ba = ba_ref[s]
        beta_all = _sigmoid(ba)
        xg = ba + gpar[1:2, :]
        g_all = gpar[0:1, :] * (jnp.maximum(xg, 0.0) + jnp.log(1.0 + jnp.exp(-jnp.abs(xg))))
        gc_col = exact3(tri, _split3(g_all))
        csplit = _split3(gc_col)
        gc_row = exact3(sel, csplit, nt_dims)
        glast_col = exact3(lastsel, csplit)
        xs = xs_scr[slot, s]
        for h in heads:
            u_ = (s, h)
            q[u_] = xs[:, h * hd:(h + 1) * hd]
            k[u_] = xs[:, width + h * hd:width + (h + 1) * hd]
            v[u_] = xs[:, 2 * width + h * hd:2 * width + (h + 1) * hd]
            beta[u_] = beta_all[:, h:h + 1]
            gcc[u_] = gc_col[:, nheads + h:nheads + h + 1]
            glc[u_] = glast_col[:, nheads + h:nheads + h + 1]
            gcr = gc_row[h:h + 1, :]
            decay[u_] = jnp.where(incl, jnp.exp(jnp.where(incl, gcc[u_] - gcr, 0.0)), 0.0)
    per_stage = max(len(units) // (n_sq + 3), 1)
    qk_kk = {u_: _mm_nt(jnp.concatenate([q[u_], k[u_]], axis=0), k[u_]) for u_ in units}
    tick(per_stage)
    qk = {u_: qk_kk[u_][:rb] * decay[u_] for u_ in units}
    a = {u_: jnp.where(strict, beta[u_] * qk_kk[u_][rb:] * decay[u_], 0.0) for u_ in units}
    tm = {u_: eye - a[u_] for u_ in units}
    if n_sq > 0:
        bpow = {u_: _mm(a[u_], a[u_]) for u_ in units}
        tick(per_stage)
    for r in range(n_sq):
        if r == n_sq - 1:
            tm = {u_: tm[u_] + _mm(tm[u_], bpow[u_]) for u_ in units}
        else:
            nxt = {u_: _mm(jnp.concatenate([tm[u_], bpow[u_]], axis=0), bpow[u_]) for u_ in units}
            bpow = {u_: nxt[u_][rb:] for u_ in units}
            tm = {u_: tm[u_] + nxt[u_][:rb] for u_ in units}
        tick(per_stage)
    egc = {u_: jnp.exp(gcc[u_]) for u_ in units}
    uw = {u_: _mm(tm[u_], jnp.concatenate([v[u_] * beta[u_], k[u_] * (beta[u_] * egc[u_])], axis=1)) for u_ in units}
    tick(len(units))
    u = {u_: uw[u_][:, :hd] for u_ in units}
    w = {u_: uw[u_][:, hd:] for u_ in units}
    qe = {u_: q[u_] * egc[u_] for u_ in units}
    ks = {u_: k[u_] * jnp.exp(glc[u_] - gcc[u_]) for u_ in units}
    if chained:
        st = {u_: s_scr[u_[0], u_[1]] for u_ in units}
        o_parts = {u_: [] for u_ in units}
        for sg in range(nseg):
            rows = slice(sg * lt, (sg + 1) * lt)
            vnew = {u_: u[u_][rows] - _mm(w[u_][rows], st[u_]) for u_ in units}
            above = [jnp.zeros((sg * lt, hd), F32)] if sg > 0 else []
            below = [jnp.zeros(((nseg - 1 - sg) * lt, hd), F32)] if sg < nseg - 1 else []
            vpad = {u_: jnp.concatenate(above + [vnew[u_]] + below, axis=0) if nseg > 1 else vnew[u_] for u_ in units}
            for u_ in units:
                o_parts[u_].append(_mm(qe[u_][rows], st[u_]) + _mm(qk[u_][rows], vpad[u_]))
            st = {u_: st[u_] * jnp.exp(glc[u_][sg * lt:sg * lt + 1, :]) + _mm_tn(ks[u_][rows], vnew[u_])
                  for u_ in units}
        for u_ in units:
            s_scr[u_[0], u_[1]] = st[u_]
        o = {u_: jnp.concatenate(o_parts[u_], axis=0) if nseg > 1 else o_parts[u_][0] for u_ in units}
    else:
        ws, qs = {}, {}
        for u_ in units:
            h = u_[1]
            parts = [_mm(jnp.concatenate([w[u_][sq * lt:(sq + 1) * lt], qe[u_][sq * lt:(sq + 1) * lt]], axis=0),
                         s_scr[sq, h]) for sq in range(nsb)]
            ws[u_] = jnp.concatenate([p[:lt] for p in parts], axis=0)
            qs[u_] = jnp.concatenate([p[lt:] for p in parts], axis=0)
        vnew = {u_: u[u_] - ws[u_] for u_ in units}
        o = {u_: qs[u_] + _mm(qk[u_], vnew[u_]) for u_ in units}
        for u_ in units:
            h = u_[1]
            for sq in range(nsb):
                ksm = jnp.where(rowseq == sq, ks[u_], 0.0)
                s_scr[sq, h] = s_scr[sq, h] * jnp.exp(glc[u_][sq * lt:sq * lt + 1, :]) + _mm_tn(ksm, vnew[u_])
    for s, h in units:
        ou = o[(s, h)]
        oh = ou * lax.rsqrt(jnp.mean(ou * ou, axis=-1, keepdims=True) + EPS) * onw
        zh = z_ref[s, :, h * hd:(h + 1) * hd]
        y_ref[s, :, h * hd:(h + 1) * hd] = oh * (zh * _sigmoid(zh))

    @pl.when(c == nc - 1)
    def _state_out():
        snew_ref[...] = s_scr[...]


def _delta_apply(proj, ba, qkv_col, z_col, nseq, seqlen, cstate, s0, conv_w, a_log, dt_bias, onorm_w):
    nheads = s0.shape[1]
    width = nheads * DN_HEAD_DIM
    tail = CONV_WIDTH - 1
    t, ncols = proj.shape
    chained = seqlen >= DN_CHUNK
    if chained:
        lt, nseg = DN_CHUNK, DN_CHUNKS_PER_STEP
        nstr = DN_STREAMS if nseq % DN_STREAMS == 0 else 1
        nlead = nstr
        rb = nseg * lt
        nc = seqlen // rb
        grid = (nseq // nstr, nc)
        tok3 = lambda x: x.reshape(nseq, seqlen, x.shape[-1])
        tok_idx = lambda b, c: (b, c)
    else:
        assert seqlen == SUBLANE
        lt, nseg, nstr = seqlen, DN_STEP_ROWS // seqlen, 1
        nlead = nseg
        rb = nseg * lt
        assert nseq % nseg == 0
        nc = 1
        grid = (nseq // nseg, 1)
        tok3 = lambda x: x.reshape(1, t, x.shape[-1])
        tok_idx = lambda b, c: (0, b)
    cs8 = jnp.pad(cstate, ((0, 0), (SUBLANE - tail, 0), (0, 0)))
    gpar = jnp.zeros((SUBLANE, LANE), F32)
    gpar = gpar.at[0, nheads:2 * nheads].set(-jnp.exp(a_log))
    gpar = gpar.at[1, nheads:2 * nheads].set(dt_bias)
    proj3 = tok3(proj)
    tok = lambda cols, colblk: pl.BlockSpec((nstr, rb, cols), lambda b, c: tok_idx(b, c) + (colblk,))
    tok_next = pl.BlockSpec((nstr, rb, 3 * width),
                            lambda b, c: tok_idx(b, jnp.minimum(c + 1, nc - 1)) + (qkv_col // (3 * width),))
    lead = lambda shape: pl.BlockSpec((nlead,) + shape, lambda b, c: (b,) + (0,) * len(shape))
    const = lambda shape: pl.BlockSpec(shape, lambda b, c: (0,) * len(shape))
    nslab = 3 * width // LANE
    y, cnew, snew = pl.pallas_call(
        functools.partial(_delta_kernel, nstr=nstr, nseg=nseg, lt=lt, chained=chained, nheads=nheads),
        grid=grid,
        in_specs=[
            tok(3 * width, qkv_col // (3 * width)),
            tok_next,
            tok(width, z_col // width),
            tok(LANE, 0),
            lead((SUBLANE, 3 * width)),
            lead((nheads, DN_HEAD_DIM, DN_HEAD_DIM)),
            const((CONV_WIDTH, 3 * width)),
            const((SUBLANE, LANE)),
            const((1, DN_HEAD_DIM)),
        ],
        out_specs=[
            tok(width, 0),
            lead((tail, 3 * width)),
            lead((nheads, DN_HEAD_DIM, DN_HEAD_DIM)),
        ],
        out_shape=[
            jax.ShapeDtypeStruct(proj3.shape[:2] + (width,), F32),
            jax.ShapeDtypeStruct((nseq, tail, 3 * width), F32),
            jax.ShapeDtypeStruct(s0.shape, F32),
        ],
        scratch_shapes=[
            pltpu.VMEM((nstr * nslab, rb + 2 * SUBLANE, LANE), F32),
            pltpu.VMEM((nslab, rb + SUBLANE, LANE), F32),
            pltpu.VMEM((nlead, nheads, DN_HEAD_DIM, DN_HEAD_DIM), F32),
            pltpu.VMEM((2, nstr, rb, 3 * width), F32),
        ],
        compiler_params=_cparams("parallel", "arbitrary"),
        name="delta_chunk" if chained else "delta_step",
    )(proj3, proj3, proj3, tok3(ba), cs8, s0, conv_w, gpar, onorm_w.reshape(1, DN_HEAD_DIM))
    return y.reshape(t, width), cnew, snew


def _merge_kernel(y5_ref, ydn_ref, ga_ref, gb_ref, x_ref, wglu_ref, bglu_ref, wa_ref, wb_ref, wout_ref, gffn_ref,
                  x1_ref, h_ref):
    y = y5_ref[...]
    y = 0.5 * y * (1.0 + jnp.tanh(math.sqrt(2.0 / math.pi) * (y + 0.044715 * (y * y * y))))
    lin = jnp.dot(y.astype(BF16), wglu_ref[...], preferred_element_type=F32) + bglu_ref[...]
    glu = (y * _sigmoid(lin)).astype(BF16)
    a = jnp.dot(glu, wa_ref[...], preferred_element_type=F32)
    b = jnp.dot(ydn_ref[...].astype(BF16), wb_ref[...], preferred_element_type=F32)
    mix = (_sigmoid(ga_ref[...]) * a + _sigmoid(gb_ref[...]) * b).astype(BF16)
    x1 = x_ref[...] + jnp.dot(mix, wout_ref[...], preferred_element_type=F32)
    x1_ref[...] = x1
    h_ref[...] = _rms(x1, gffn_ref[...]).astype(BF16)


def _merge(y5, ydn, proj, x, ga_col, gb_col, w_glu, b_glu, w_a, w_b, w_out, g_ffn, tm):
    t, w5 = y5.shape
    d = x.shape[1]
    row = lambda cols: pl.BlockSpec((tm, cols), lambda i: (i, 0))
    const = lambda shape: pl.BlockSpec(shape, lambda i: (0, 0), pipeline_mode=pl.Buffered(1))
    return pl.pallas_call(
        _merge_kernel,
        grid=(t // tm,),
        in_specs=[
            row(w5), row(w5),
            pl.BlockSpec((tm, d), lambda i: (i, ga_col // d)),
            pl.BlockSpec((tm, d), lambda i: (i, gb_col // d)),
            row(d),
            const((w5, w5)), const((1, w5)), const((w5, d)), const((w5, d)), const((d, d)), const((1, d)),
        ],
        out_specs=[row(d), row(d)],
        out_shape=[jax.ShapeDtypeStruct((t, d), F32), jax.ShapeDtypeStruct((t, d), BF16)],
        compiler_params=_cparams("parallel"),
        name="merge",
    )(y5, ydn, proj, proj, x, w_glu, b_glu, w_a, w_b, w_out, g_ffn)


def _ffn_kernel(h_ref, xn_ref, wg_ref, wu_ref, wd_ref, out_ref, act_scr, *, nk, tk):
    s = pl.program_id(1)

    @pl.when(s < nk)
    def _():
        h = h_ref[...]
        gate = jnp.dot(h, wg_ref[...], preferred_element_type=F32)
        up = jnp.dot(h, wu_ref[...], preferred_element_type=F32)
        act_scr[s] = (gate * _sigmoid(gate) * up).astype(BF16)

    @pl.when(s >= nk)
    def _():
        acc = xn_ref[...]
        for kk in range(nk):
            acc = acc + jnp.dot(act_scr[kk], wd_ref[kk * tk:(kk + 1) * tk, :], preferred_element_type=F32)
        out_ref[...] = acc


def _ffn(x, h, w_gate, w_up, w_down, tm, tk, tn):
    t, d = x.shape
    dff = w_gate.shape[1]
    nk = dff // tk
    up_blk = lambda i, s: (0, jnp.minimum(s, nk - 1))
    down_blk = lambda i, s: (0, jnp.maximum(s - nk, 0))
    out_blk = lambda i, s: (i, jnp.maximum(s - nk, 0))
    return pl.pallas_call(
        functools.partial(_ffn_kernel, nk=nk, tk=tk),
        grid=(t // tm, nk + d // tn),
        in_specs=[
            pl.BlockSpec((tm, d), lambda i, s: (i, 0)),
            pl.BlockSpec((tm, tn), out_blk),
            pl.BlockSpec((d, tk), up_blk),
            pl.BlockSpec((d, tk), up_blk),
            pl.BlockSpec((dff, tn), down_blk),
        ],
        out_specs=pl.BlockSpec((tm, tn), out_blk),
        out_shape=jax.ShapeDtypeStruct((t, d), F32),
        scratch_shapes=[pltpu.VMEM((nk, tm, tk), BF16)],
        compiler_params=_cparams("parallel", "arbitrary"),
        name="ffn",
    )(h, x, w_gate, w_up, w_down)


def _ple_kernel(x_ref, p_ref, gple_ref, gfin_ref, wple_ref, wpg_ref, out_ref, h_scr, *, nj, tn, final):
    j = pl.program_id(1)

    @pl.when(j == 0)
    def _():
        h_scr[...] = _rms(x_ref[...], gple_ref[...]).astype(BF16)

    gate = _sigmoid(jnp.dot(h_scr[...], wpg_ref[...], preferred_element_type=F32))
    emb = jnp.dot(p_ref[...].astype(BF16), wple_ref[...], preferred_element_type=F32)
    upd = emb * gate
    for jj in range(nj):
        @pl.when(j == jj)
        def _(jj=jj):
            out_ref[:, jj * tn:(jj + 1) * tn] = x_ref[:, jj * tn:(jj + 1) * tn] + upd

    if final:
        @pl.when(j == nj - 1)
        def _():
            out_ref[...] = _rms(out_ref[...], gfin_ref[...])


def _ple_final(x, p, g_ple, g_final, w_ple, w_ple_gate, tm, tn, final):
    t, d = x.shape
    pd = p.shape[1]
    nj = d // tn
    return pl.pallas_call(
        functools.partial(_ple_kernel, nj=nj, tn=tn, final=final),
        grid=(t // tm, nj),
        in_specs=[
            pl.BlockSpec((tm, d), lambda i, j: (i, 0)),
            pl.BlockSpec((tm, pd), lambda i, j: (i, 0)),
            pl.BlockSpec((1, d), lambda i, j: (0, 0)),
            pl.BlockSpec((1, d), lambda i, j: (0, 0)),
            pl.BlockSpec((pd, tn), lambda i, j: (0, j)),
            pl.BlockSpec((d, tn), lambda i, j: (0, j)),
        ],
        out_specs=pl.BlockSpec((tm, d), lambda i, j: (i, 0)),
        out_shape=jax.ShapeDtypeStruct((t, d), F32),
        scratch_shapes=[pltpu.VMEM((tm, d), BF16)],
        compiler_params=_cparams("parallel", "arbitrary"),
        name="ple_final",
    )(x, p, g_ple, g_final, w_ple, w_ple_gate)


COL_U, COL_Z, COL_GA, COL_GB, COL_QKV = 0, 1024, 2048, 4096, 6144


def _prep_w_in(w_in, layer, d_model, nheads):
    w_t = jnp.swapaxes(w_in, 1, 2)
    s5w = d_model // 2
    dnw = d_model // 2
    off_u = s5w
    off_qkv = off_u + 3 * dnw
    off_z = off_qkv + dnw
    off_a = off_z + 2 * nheads
    tb = W_IN_BLOCK
    assert off_u % tb == 0 and off_qkv % tb == 0 and off_z % tb == 0 and d_model % tb == 0
    shift = off_a - off_z
    assert shift % SUBLANE == 0 and shift <= LANE
    src = ([0] + [off_qkv // tb + i for i in range(dnw // tb)] + [off_z // tb + i for i in range(2 * d_model // tb)]
           + [off_u // tb + i for i in range(3 * dnw // tb)])
    n_al = 1 + dnw // tb
    n_sh = 2 * d_model // tb
    src_tab = jnp.asarray(src, jnp.int32)
    hi_tab = jnp.asarray([(src[min(max(j, n_al), n_al + n_sh - 1)] + 1) * (tb // LANE) for j in range(len(src))],
                         jnp.int32)
    d_in = w_in.shape[1]
    return pl.pallas_call(
        functools.partial(_w_in_kernel, n_al=n_al, n_sh=n_sh, shift=shift),
        grid_spec=pltpu.PrefetchScalarGridSpec(
            num_scalar_prefetch=2,
            grid=(len(src),),
            in_specs=[pl.BlockSpec((None, tb, d_in), lambda j, lo, hi: (layer, lo[j], 0)),
                      pl.BlockSpec((None, LANE, d_in), lambda j, lo, hi: (layer, hi[j], 0))],
            out_specs=[pl.BlockSpec((tb, d_in), lambda j, lo, hi: (j, 0)),
                       pl.BlockSpec((LANE, d_in), lambda j, lo, hi: (0, 0))],
        ),
        out_shape=[jax.ShapeDtypeStruct((len(src) * tb, d_in), BF16), jax.ShapeDtypeStruct((LANE, d_in), BF16)],
        compiler_params=_cparams("arbitrary"),
        name="w_in_cast",
    )(src_tab, hi_tab, w_t, w_t)


W_IN_BLOCK = 1024


def _w_in_kernel(lo_tab, hi_tab, lo_ref, hi_ref, out_ref, ba_ref, *, n_al, n_sh, shift):
    del lo_tab, hi_tab
    j = pl.program_id(0)
    tb, d_in = out_ref.shape
    stitched = (j >= n_al) & (j < n_al + n_sh)

    @pl.when(j == n_al)
    def _():
        ba_ref[...] = jnp.concatenate([lo_ref[0:shift, :], jnp.zeros((LANE - shift, d_in), F32)],
                                      axis=0).astype(BF16)

    @pl.when(stitched)
    def _():
        out_ref[...] = jnp.concatenate([lo_ref[shift:, :], hi_ref[0:shift, :]], axis=0).astype(BF16)

    @pl.when(jnp.logical_not(stitched))
    def _():
        out_ref[...] = lo_ref[...].astype(BF16)


def _layer(x3, p3, cstate, s0, h0, lw, tm, final):
    nseq, seqlen, d = x3.shape
    t = nseq * seqlen
    x = x3.reshape(t, d)
    proj, ba = _inproj(x, lw['g_mix'], lw['w_main'], lw['w_ba'], tm, 1536)
    y5, hre, him = _s5_apply(proj, nseq, seqlen, lw['s5_ops'], lw['s5_d'], h0)
    ydn, cnew, snew = _delta_apply(proj, ba, COL_QKV, COL_Z, nseq, seqlen, cstate, s0,
                                   lw['conv_w'], lw['a_log'], lw['dt_bias'], lw['onorm_w'])
    x1, h2 = _merge(y5, ydn, proj, x, COL_GA, COL_GB, lw['w_glu'], lw['b_glu'], lw['w_a'], lw['w_b'],
                    lw['w_out'], lw['g_ffn'], 256)
    x2 = _ffn(x1, h2, lw['w_gate'], lw['w_up'], lw['w_down'], tm, 512, 512)
    y = _ple_final(x2, p3.reshape(t, -1), lw['g_ple'], lw['g_final'], lw['w_ple'], lw['w_ple_gate'],
                   tm, 1024, final)
    ng = hre.shape[-1] // S5_STATE
    return (y.reshape(nseq, seqlen, d), cnew, snew,
            hre.reshape(nseq, ng, S5_STATE), him.reshape(nseq, ng, S5_STATE))


def kernel(x_prompt, x_sample, state_conv, state_delta, state_s5_re, state_s5_im, p_prompt, p_sample, g_mix, w_in, conv_w, a_log, dt_bias, onorm_w, s5_a_re, s5_a_im, s5_b_re, s5_b_im, s5_c_re, s5_c_im, s5_d, s5_log_dt, w_glu, b_glu, w_a, w_b, w_out, g_ffn, w_gate, w_up, w_down, g_ple, w_ple, w_ple_gate, g_final):
    depth = w_in.shape[0]
    d_model = x_prompt.shape[-1]
    nheads = state_delta.shape[2]
    nb_p = x_prompt.shape[0]
    f32z = functools.partial(jnp.zeros, dtype=F32)
    yp, ys = x_prompt, x_sample
    outs_p, outs_s = [], []
    for i in range(depth):
        w_main, w_ba = _prep_w_in(w_in, i, d_model, nheads)
        lw = dict(
            g_mix=g_mix[i][None], w_main=w_main, w_ba=w_ba,
            conv_w=conv_w[i], a_log=a_log[i], dt_bias=dt_bias[i], onorm_w=onorm_w[i],
            s5_ops=_s5_operators(s5_a_re[i], s5_a_im[i], s5_b_re[i], s5_b_im[i], s5_c_re[i], s5_c_im[i],
                                 s5_log_dt[i]),
            s5_d=s5_d[i],
            w_glu=w_glu[i].astype(BF16), b_glu=b_glu[i][None], w_a=w_a[i].astype(BF16), w_b=w_b[i].astype(BF16),
            w_out=w_out[i].astype(BF16), g_ffn=g_ffn[i][None],
            w_gate=w_gate[i].astype(BF16), w_up=w_up[i].astype(BF16), w_down=w_down[i].astype(BF16),
            g_ple=g_ple[i][None], w_ple=w_ple[i].astype(BF16), w_ple_gate=w_ple_gate[i].astype(BF16),
            g_final=g_final[None],
        )
        final = i == depth - 1
        yp, c1, d1, r1, m1 = _layer(yp, p_prompt[i], f32z((nb_p,) + state_conv.shape[2:]),
                                    f32z((nb_p,) + state_delta.shape[2:]), None, lw, 1024, final)
        ys, c2, d2, r2, m2 = _layer(ys, p_sample[i], state_conv[i], state_delta[i],
                                    (state_s5_re[i], state_s5_im[i]), lw, 1024, final)
        outs_p.append((c1, d1, r1, m1))
        outs_s.append((c2, d2, r2, m2))
    stack = lambda outs, k: jnp.stack([o[k] for o in outs])
    return (yp, ys,
            stack(outs_p, 0), stack(outs_p, 1), stack(outs_p, 2), stack(outs_p, 3),
            stack(outs_s, 0), stack(outs_s, 1), stack(outs_s, 2), stack(outs_s, 3))
```

```python
import functools
import math

import jax
import jax.numpy as jnp
from jax import lax
from jax.experimental import pallas as pl
from jax.experimental.pallas import tpu as pltpu

F32 = jnp.float32
BF16 = jnp.bfloat16

EPS = 1e-6
LANE = 128
SUBLANE = 8
VMEM_LIMIT_BYTES = 56 * 1024 * 1024

S5_GROUP = 16
S5_STATE = 64
S5_ROW = SUBLANE
S5_TILE_GROUPS = LANE // S5_GROUP
S5_TILE_STATE = S5_TILE_GROUPS * S5_STATE
DN_HEAD_DIM = 128
CONV_WIDTH = 4
DN_CHUNK = 64
DN_CHUNKS_PER_STEP = 2
DN_STEP_ROWS = 128
DN_STREAMS = 2


def _cparams(*sem):
    return pltpu.CompilerParams(dimension_semantics=sem, vmem_limit_bytes=VMEM_LIMIT_BYTES)


def _mm(a, b):
    return jnp.dot(a.astype(BF16), b.astype(BF16), preferred_element_type=F32)


def _mm_nt(a, b):
    return lax.dot_general(a.astype(BF16), b.astype(BF16), (((1,), (1,)), ((), ())),
                           preferred_element_type=F32)


def _mm_tn(a, b):
    return lax.dot_general(a.astype(BF16), b.astype(BF16), (((0,), (0,)), ((), ())),
                           preferred_element_type=F32)


def _split3(x):
    hi = x.astype(BF16)
    r1 = x - hi.astype(F32)
    mid = r1.astype(BF16)
    lo = (r1 - mid.astype(F32)).astype(BF16)
    return hi, mid, lo


def _sigmoid(x):
    return 1.0 / (1.0 + jnp.exp(-x))


def _rms(x, g):
    ms = jnp.mean(x * x, axis=-1, keepdims=True)
    return x * lax.rsqrt(ms + EPS) * g


def _inproj_kernel(x_ref, g_ref, wt_ref, wbat_ref, out_ref, ba_ref, h_scr):
    nt_dims = (((1,), (1,)), ((), ()))

    @pl.when(pl.program_id(1) == 0)
    def _():
        h = _rms(x_ref[...], g_ref[...]).astype(BF16)
        h_scr[...] = h
        ba_ref[...] = lax.dot_general(h, wbat_ref[...], nt_dims, preferred_element_type=F32)

    out_ref[...] = lax.dot_general(h_scr[...], wt_ref[...], nt_dims, preferred_element_type=F32)


def _inproj(x, g, wt_main, wt_ba, tm, tn):
    t, d = x.shape
    n = wt_main.shape[0]
    return pl.pallas_call(
        _inproj_kernel,
        grid=(t // tm, n // tn),
        in_specs=[
            pl.BlockSpec((tm, d), lambda i, j: (i, 0)),
            pl.BlockSpec((1, d), lambda i, j: (0, 0)),
            pl.BlockSpec((tn, d), lambda i, j: (j, 0)),
            pl.BlockSpec((LANE, d), lambda i, j: (0, 0)),
        ],
        out_specs=[
            pl.BlockSpec((tm, tn), lambda i, j: (i, j)),
            pl.BlockSpec((tm, LANE), lambda i, j: (i, 0)),
        ],
        out_shape=[jax.ShapeDtypeStruct((t, n), F32), jax.ShapeDtypeStruct((t, LANE), F32)],
        scratch_shapes=[pltpu.VMEM((tm, d), BF16)],
        compiler_params=_cparams("parallel", "arbitrary"),
        name="inproj",
    )(x, g, wt_main, wt_ba)


def _s5_operators(a_re, a_im, b_re, b_im, c_re, c_im, log_dt):
    g, n = a_re.shape
    nt = g // S5_TILE_GROUPS
    ns = S5_TILE_STATE
    wide = S5_ROW * LANE
    row = lambda x: x.reshape(1, g * n)
    bt = lambda x: x.transpose(2, 0, 1).reshape(S5_GROUP, g * n)
    ct = lambda x: x.transpose(1, 0, 2).reshape(S5_GROUP, g * n)
    vec = pl.BlockSpec((1, ns), lambda t: (0, t))
    mat = pl.BlockSpec((S5_GROUP, ns), lambda t: (0, t))
    return pl.pallas_call(
        _s5_ops_kernel,
        grid=(nt,),
        in_specs=[vec, vec, vec, mat, mat, mat, mat],
        out_specs=[
            pl.BlockSpec((None, wide, 2 * ns), lambda t: (t, 0, 0)),
            pl.BlockSpec((None, wide, 2 * ns), lambda t: (t, 0, 0)),
            pl.BlockSpec((None, wide, wide), lambda t: (t, 0, 0)),
            pl.BlockSpec((None, 1, 2 * ns), lambda t: (t, 0, 0)),
        ],
        out_shape=[
            jax.ShapeDtypeStruct((nt, wide, 2 * ns), BF16),
            jax.ShapeDtypeStruct((nt, wide, 2 * ns), BF16),
            jax.ShapeDtypeStruct((nt, wide, wide), BF16),
            jax.ShapeDtypeStruct((nt, 1, 2 * ns), F32),
        ],
        compiler_params=_cparams("parallel"),
        name="s5_ops",
    )(row(a_re), row(a_im), row(jnp.repeat(log_dt, n)), bt(b_re), bt(b_im), ct(c_re), ct(c_im))


def _mm_nt_split(a, b):
    ah = a.astype(BF16)
    al = (a - ah.astype(F32)).astype(BF16)
    bh = b.astype(BF16)
    bl = (b - bh.astype(F32)).astype(BF16)
    dims = (((1,), (1,)), ((), ()))
    return (lax.dot_general(ah, bh, dims, preferred_element_type=F32)
            + lax.dot_general(ah, bl, dims, preferred_element_type=F32)
            + lax.dot_general(al, bh, dims, preferred_element_type=F32))


def _s5_ops_kernel(ar_ref, ai_ref, ldt_ref, btr_ref, bti_ref, ctr_ref, cti_ref, we_ref, wyt_ref, wk_ref, lam_ref):
    ns = S5_TILE_STATE
    ar = ar_ref[...]
    ai = ai_ref[...]
    dt = jnp.exp(ldt_ref[...])
    kk = lax.broadcasted_iota(jnp.int32, (2 * SUBLANE, ns), 0).astype(F32)
    mag = jnp.exp(ar * dt * kk)
    lr = mag * jnp.cos(ai * dt * kk)
    li = mag * jnp.sin(ai * dt * kk)
    nr = lr[1:2] - 1.0
    ni = li[1:2]
    den = ar * ar + ai * ai
    cr = (nr * ar + ni * ai) / den
    ci = (ni * ar - nr * ai) / den
    btr = btr_ref[...]
    bti = bti_ref[...]
    bbr = cr * btr - ci * bti
    bbi = cr * bti + ci * btr
    ctr = ctr_ref[...]
    cti = cti_ref[...]
    same_group = (lax.broadcasted_iota(jnp.int32, (LANE, ns), 0) // S5_GROUP
                  == lax.broadcasted_iota(jnp.int32, (LANE, ns), 1) // S5_STATE)

    def blockdiag(x):
        return jnp.where(same_group, jnp.concatenate([x] * S5_TILE_GROUPS, axis=0), 0.0)

    for j in range(S5_ROW):
        k = S5_ROW - 1 - j
        er = lr[k:k + 1] * bbr - li[k:k + 1] * bbi
        ei = lr[k:k + 1] * bbi + li[k:k + 1] * bbr
        we_ref[j * LANE:(j + 1) * LANE, :] = jnp.concatenate([blockdiag(er), blockdiag(ei)], axis=1).astype(BF16)
    bq = jnp.concatenate([blockdiag(bbr), blockdiag(bbi)], axis=1)
    kblocks = []
    for k in range(S5_ROW + 1):
        mr = ctr * lr[k:k + 1] - cti * li[k:k + 1]
        mi = ctr * li[k:k + 1] + cti * lr[k:k + 1]
        wy_k = jnp.concatenate([blockdiag(mr), -blockdiag(mi)], axis=1)
        if k >= 1:
            wyt_ref[(k - 1) * LANE:k * LANE, :] = wy_k.astype(BF16)
        if k < S5_ROW:
            kblocks.append(_mm_nt_split(bq, wy_k).astype(BF16))
    zero = jnp.zeros((LANE, LANE), BF16)
    for i in range(S5_ROW):
        for j in range(S5_ROW):
            wk_ref[i * LANE:(i + 1) * LANE, j * LANE:(j + 1) * LANE] = kblocks[j - i] if j >= i else zero
    lam_ref[...] = jnp.concatenate([lr[S5_ROW:S5_ROW + 1], li[S5_ROW:S5_ROW + 1]], axis=1)


def _s5_kernel(*refs, scan, nseq, rps):
    nu = S5_ROW
    rows = nseq * rps
    u_ref, we_ref, wy_ref, wk_ref, d_ref, lam_ref = refs[:6]
    pos = 6
    if not scan:
        h0r_ref, h0i_ref = refs[pos:pos + 2]
        pos += 2
    y_ref, hre_ref, him_ref = refs[pos:pos + 3]
    scratch = refs[pos + 3:]

    ns = S5_TILE_STATE
    us = [u_ref[pl.ds(j, rows, stride=nu), :] for j in range(nu)]
    u = jnp.concatenate(us, axis=1)
    ub = u.astype(BF16)
    e = jnp.dot(ub, we_ref[...], preferred_element_type=F32)
    lam = lam_ref[...]
    lr = lam[:, :ns]
    li = lam[:, ns:]
    if scan:
        e_scr, h_scr = scratch
        nslab = 2 * ns // LANE
        for k in range(nslab):
            for b in range(nseq):
                e_scr[k, pl.ds(b, rps, stride=nseq), :] = e[b * rps:(b + 1) * rps, k * LANE:(k + 1) * LANE]

        def body(c, carry):
            hr, hi = carry
            hcat = jnp.concatenate([hr, hi], axis=1)
            for k in range(nslab):
                h_scr[k, pl.ds(c * nseq, nseq), :] = hcat[:, k * LANE:(k + 1) * LANE]
            ec = jnp.concatenate([e_scr[k, pl.ds(c * nseq, nseq), :] for k in range(nslab)], axis=1)
            return (lr * hr - li * hi + ec[:, :ns], lr * hi + li * hr + ec[:, ns:])

        zero = jnp.zeros((nseq, ns), F32)
        hr, hi = lax.fori_loop(0, rps, body, (zero, zero), unroll=4)
        y_local = jnp.dot(ub, wk_ref[...], preferred_element_type=F32)
        hre_ref[...] = hr
        him_ref[...] = hi
        hin = jnp.concatenate(
            [jnp.concatenate([h_scr[k, pl.ds(b, rps, stride=nseq), :] for b in range(nseq)], axis=0)
             for k in range(nslab)], axis=1)
    else:
        h0r = h0r_ref[...]
        h0i = h0i_ref[...]
        hre_ref[...] = lr * h0r - li * h0i + e[:, :ns]
        him_ref[...] = lr * h0i + li * h0r + e[:, ns:]
        hin = jnp.concatenate([h0r, h0i], axis=1)
        y_local = jnp.dot(ub, wk_ref[...], preferred_element_type=F32)
    y = y_local + lax.dot_general(hin.astype(BF16), wy_ref[...], (((1,), (1,)), ((), ())),
                                  preferred_element_type=F32)
    d = d_ref[...]
    for j in range(nu):
        y_ref[pl.ds(j, rows, stride=nu), :] = y[:, j * LANE:(j + 1) * LANE] + d * us[j]


def _s5_apply(proj, nseq, seqlen, ops, d, h0=None):
    we_t, wy_t, wk_t, lam = ops
    nt = we_t.shape[0]
    width = nt * LANE
    t = nseq * seqlen
    scan = h0 is None
    rps = seqlen // S5_ROW
    assert scan or rps == 1
    ns2 = 2 * S5_TILE_STATE
    tile3 = lambda shape: pl.BlockSpec((None,) + shape, lambda tt: (tt, 0, 0))
    state = pl.BlockSpec((nseq, S5_TILE_STATE), lambda tt: (0, tt))
    in_specs = [
        pl.BlockSpec((t, LANE), lambda tt: (0, tt)),
        tile3((S5_ROW * LANE, ns2)),
        tile3((ns2, S5_ROW * LANE)),
        tile3((S5_ROW * LANE, S5_ROW * LANE)),
        tile3((1, LANE)),
        tile3((1, ns2)),
    ]
    args = [proj, we_t, wy_t, wk_t, d.reshape(nt, 1, LANE), lam]
    scratch = []
    if scan:
        scratch = [pltpu.VMEM((ns2 // LANE, nseq * rps, LANE), F32)] * 2
    else:
        in_specs += [state, state]
        args += [h0[0].reshape(nseq, nt * S5_TILE_STATE), h0[1].reshape(nseq, nt * S5_TILE_STATE)]
    return pl.pallas_call(
        functools.partial(_s5_kernel, scan=scan, nseq=nseq, rps=rps),
        grid=(nt,),
        in_specs=in_specs,
        out_specs=[pl.BlockSpec((t, LANE), lambda tt: (0, tt)), state, state],
        out_shape=[jax.ShapeDtypeStruct((t, width), F32)]
        + [jax.ShapeDtypeStruct((nseq, nt * S5_TILE_STATE), F32)] * 2,
        scratch_shapes=scratch,
        compiler_params=_cparams("parallel"),
        name="s5_scan" if scan else "s5_step",
    )(*args)


def _delta_kernel(qkv_ref, qkvn_ref, z_ref, ba_ref, cs_ref, s0_ref, convw_ref, gpar_ref, onw_ref,
                  y_ref, cnew_ref, snew_ref, ext_scr, csx_scr, s_scr, xs_scr, *, nstr, nseg, lt, chained, nheads):
    rb = nseg * lt
    nsb = 1 if chained else nseg
    hd = DN_HEAD_DIM
    width = nheads * hd
    c = pl.program_id(1)
    nc = pl.num_programs(1)
    tail = CONV_WIDTH - 1
    nslab = 3 * width // LANE
    cw = convw_ref[...]
    streams = range(nstr)
    heads = range(nheads)
    units = [(s, h) for s in streams for h in heads]

    def prepare_head(x_ref, dst_slot, s, h):
        if nsb > 1:
            tl = lax.broadcasted_iota(jnp.int32, (rb, 1), 0) % lt
        for part in range(3):
            cs = slice(part * width + h * hd, part * width + (h + 1) * hd)
            sl = s * nslab + part * nheads + h
            x = x_ref[s, :, cs]
            ext_scr[sl, SUBLANE:SUBLANE + rb, :] = x
            acc = x * cw[tail:tail + 1, cs]
            for k in range(1, CONV_WIDTH):
                xk = ext_scr[sl, SUBLANE - k:SUBLANE - k + rb, :]
                if nsb > 1:
                    xk = jnp.where(tl < k, csx_scr[sl, SUBLANE - k:SUBLANE - k + rb, :], xk)
                acc = acc + xk * cw[tail - k:tail - k + 1, cs]
            if nsb == 1:
                ext_scr[sl, 0:SUBLANE, :] = ext_scr[sl, rb:rb + SUBLANE, :]
            xh = acc * _sigmoid(acc)
            if part < 2:
                scale = hd ** -0.5 if part == 0 else 1.0
                xh = xh * (lax.rsqrt(jnp.sum(xh * xh, axis=-1, keepdims=True) + EPS) * scale)
            xs_scr[dst_slot, s, :, cs] = xh

    @pl.when(c == 0)
    def _init():
        s_scr[...] = s0_ref[...]
        if chained:
            for s in streams:
                for sl in range(nslab):
                    ext_scr[s * nslab + sl, 0:SUBLANE, :] = cs_ref[s, :, sl * LANE:(sl + 1) * LANE]
            for s, h in units:
                prepare_head(qkv_ref, 0, s, h)
        else:
            ext_scr[:, 0:SUBLANE, :] = jnp.zeros((nslab, SUBLANE, LANE), F32)

    if chained:
        slot = c % 2
        pending = list(units)

        def tick(n=1):
            for _ in range(n):
                if pending:
                    prepare_head(qkvn_ref, 1 - slot, *pending.pop(0))

        @pl.when(c == nc - 1)
        def _conv_out():
            for s in streams:
                cnew_ref[s] = qkvn_ref[s, rb - tail:rb, :]
    else:
        cs_all = cs_ref[...].reshape(rb, 3 * width)
        for sl in range(nslab):
            csx_scr[sl, 0:rb, :] = cs_all[:, sl * LANE:(sl + 1) * LANE]
        csx_scr[:, rb:rb + SUBLANE, :] = jnp.zeros((nslab, SUBLANE, LANE), F32)
        slot = 0
        for s, h in units:
            prepare_head(qkv_ref, 0, s, h)
        for sq in range(nsb):
            r0 = SUBLANE + (sq + 1) * lt - tail
            cnew_ref[sq] = jnp.concatenate([ext_scr[sl, r0:r0 + tail, :] for sl in range(nslab)], axis=1)

        def tick(n=1):
            pass

    ri = lax.broadcasted_iota(jnp.int32, (rb, rb), 0)
    ci = lax.broadcasted_iota(jnp.int32, (rb, rb), 1)
    same = (ri // lt) == (ci // lt)
    incl = (ri >= ci) & same
    strict = (ri > ci) & same
    tri = jnp.where(incl, 1.0, 0.0).astype(BF16)
    sel = jnp.where(lax.broadcasted_iota(jnp.int32, (2 * SUBLANE, LANE), 1)
                    == lax.broadcasted_iota(jnp.int32, (2 * SUBLANE, LANE), 0) + nheads, 1.0, 0.0).astype(BF16)
    lastsel = jnp.where(same & ((ci % lt) == lt - 1), 1.0, 0.0).astype(BF16)
    nt_dims = (((1,), (1,)), ((), ()))
    gpar = gpar_ref[...]
    eye = jnp.where(ri == ci, 1.0, 0.0)
    onw = onw_ref[...]
    n_sq = max(int(math.log2(lt)) - 1, 0)
    rowseq = lax.broadcasted_iota(jnp.int32, (rb, 1), 0) // lt

    def exact3(mat, parts, dims=None):
        if dims is None:
            return sum(jnp.dot(mat, p, preferred_element_type=F32) for p in parts)
        return sum(lax.dot_general(mat, p, dims, preferred_element_type=F32) for p in parts)

    q, k, v, beta, gcc, glc, decay = {}, {}, {}, {}, {}, {}, {}
    for s in streams:
        ba = ba_ref[s]
        beta_all = _sigmoid(ba)
        xg = ba + gpar[1:2, :]
        g_all = gpar[0:1, :] * (jnp.maximum(xg, 0.0) + jnp.log(1.0 + jnp.exp(-jnp.abs(xg))))
        gc_col = exact3(tri, _split3(g_all))
        csplit = _split3(gc_col)
        gc_row = exact3(sel, csplit, nt_dims)
        glast_col = exact3(lastsel, csplit)
        xs = xs_scr[slot, s]
        for h in heads:
            u_ = (s, h)
            q[u_] = xs[:, h * hd:(h + 1) * hd]
            k[u_] = xs[:, width + h * hd:width + (h + 1) * hd]
            v[u_] = xs[:, 2 * width + h * hd:2 * width + (h + 1) * hd]
            beta[u_] = beta_all[:, h:h + 1]
            gcc[u_] = gc_col[:, nheads + h:nheads + h + 1]
            glc[u_] = glast_col[:, nheads + h:nheads + h + 1]
            gcr = gc_row[h:h + 1, :]
            decay[u_] = jnp.where(incl, jnp.exp(jnp.where(incl, gcc[u_] - gcr, 0.0)), 0.0)
    per_stage = max(len(units) // (n_sq + 3), 1)
    qk_kk = {u_: _mm_nt(jnp.concatenate([q[u_], k[u_]], axis=0), k[u_]) for u_ in units}
    tick(per_stage)
    qk = {u_: qk_kk[u_][:rb] * decay[u_] for u_ in units}
    a = {u_: jnp.where(strict, beta[u_] * qk_kk[u_][rb:] * decay[u_], 0.0) for u_ in units}
    tm = {u_: eye - a[u_] for u_ in units}
    if n_sq > 0:
        bpow = {u_: _mm(a[u_], a[u_]) for u_ in units}
        tick(per_stage)
    for r in range(n_sq):
        if r == n_sq - 1:
            tm = {u_: tm[u_] + _mm(tm[u_], bpow[u_]) for u_ in units}
        else:
            nxt = {u_: _mm(jnp.concatenate([tm[u_], bpow[u_]], axis=0), bpow[u_]) for u_ in units}
            bpow = {u_: nxt[u_][rb:] for u_ in units}
            tm = {u_: tm[u_] + nxt[u_][:rb] for u_ in units}
        tick(per_stage)
    egc = {u_: jnp.exp(gcc[u_]) for u_ in units}
    uw = {u_: _mm(tm[u_], jnp.concatenate([v[u_] * beta[u_], k[u_] * (beta[u_] * egc[u_])], axis=1)) for u_ in units}
    tick(len(units))
    u = {u_: uw[u_][:, :hd] for u_ in units}
    w = {u_: uw[u_][:, hd:] for u_ in units}
    qe = {u_: q[u_] * egc[u_] for u_ in units}
    ks = {u_: k[u_] * jnp.exp(glc[u_] - gcc[u_]) for u_ in units}
    if chained:
        st = {u_: s_scr[u_[0], u_[1]] for u_ in units}
        o_parts = {u_: [] for u_ in units}
        for sg in range(nseg):
            rows = slice(sg * lt, (sg + 1) * lt)
            vnew = {u_: u[u_][rows] - _mm(w[u_][rows], st[u_]) for u_ in units}
            above = [jnp.zeros((sg * lt, hd), F32)] if sg > 0 else []
            below = [jnp.zeros(((nseg - 1 - sg) * lt, hd), F32)] if sg < nseg - 1 else []
            vpad = {u_: jnp.concatenate(above + [vnew[u_]] + below, axis=0) if nseg > 1 else vnew[u_] for u_ in units}
            for u_ in units:
                o_parts[u_].append(_mm(qe[u_][rows], st[u_]) + _mm(qk[u_][rows], vpad[u_]))
            st = {u_: st[u_] * jnp.exp(glc[u_][sg * lt:sg * lt + 1, :]) + _mm_tn(ks[u_][rows], vnew[u_])
                  for u_ in units}
        for u_ in units:
            s_scr[u_[0], u_[1]] = st[u_]
        o = {u_: jnp.concatenate(o_parts[u_], axis=0) if nseg > 1 else o_parts[u_][0] for u_ in units}
    else:
        ws, qs = {}, {}
        for u_ in units:
            h = u_[1]
            parts = [_mm(jnp.concatenate([w[u_][sq * lt:(sq + 1) * lt], qe[u_][sq * lt:(sq + 1) * lt]], axis=0),
                         s_scr[sq, h]) for sq in range(nsb)]
            ws[u_] = jnp.concatenate([p[:lt] for p in parts], axis=0)
            qs[u_] = jnp.concatenate([p[lt:] for p in parts], axis=0)
        vnew = {u_: u[u_] - ws[u_] for u_ in units}
        o = {u_: qs[u_] + _mm(qk[u_], vnew[u_]) for u_ in units}
        for u_ in units:
            h = u_[1]
            for sq in range(nsb):
                ksm = jnp.where(rowseq == sq, ks[u_], 0.0)
                s_scr[sq, h] = s_scr[sq, h] * jnp.exp(glc[u_][sq * lt:sq * lt + 1, :]) + _mm_tn(ksm, vnew[u_])
    for s, h in units:
        ou = o[(s, h)]
        oh = ou * lax.rsqrt(jnp.mean(ou * ou, axis=-1, keepdims=True) + EPS) * onw
        zh = z_ref[s, :, h * hd:(h + 1) * hd]
        y_ref[s, :, h * hd:(h + 1) * hd] = oh * (zh * _sigmoid(zh))

    @pl.when(c == nc - 1)
    def _state_out():
        snew_ref[...] = s_scr[...]


def _delta_apply(proj, ba, qkv_col, z_col, nseq, seqlen, cstate, s0, conv_w, a_log, dt_bias, onorm_w):
    nheads = s0.shape[1]
    width = nheads * DN_HEAD_DIM
    tail = CONV_WIDTH - 1
    t, ncols = proj.shape
    chained = seqlen >= DN_CHUNK
    if chained:
        lt, nseg = DN_CHUNK, DN_CHUNKS_PER_STEP
        nstr = DN_STREAMS if nseq % DN_STREAMS == 0 else 1
        nlead = nstr
        rb = nseg * lt
        nc = seqlen // rb
        grid = (nseq // nstr, nc)
        tok3 = lambda x: x.reshape(nseq, seqlen, x.shape[-1])
        tok_idx = lambda b, c: (b, c)
    else:
        assert seqlen == SUBLANE
        lt, nseg, nstr = seqlen, DN_STEP_ROWS // seqlen, 1
        nlead = nseg
        rb = nseg * lt
        assert nseq % nseg == 0
        nc = 1
        grid = (nseq // nseg, 1)
        tok3 = lambda x: x.reshape(1, t, x.shape[-1])
        tok_idx = lambda b, c: (0, b)
    cs8 = jnp.pad(cstate, ((0, 0), (SUBLANE - tail, 0), (0, 0)))
    gpar = jnp.zeros((SUBLANE, LANE), F32)
    gpar = gpar.at[0, nheads:2 * nheads].set(-jnp.exp(a_log))
    gpar = gpar.at[1, nheads:2 * nheads].set(dt_bias)
    proj3 = tok3(proj)
    tok = lambda cols, colblk: pl.BlockSpec((nstr, rb, cols), lambda b, c: tok_idx(b, c) + (colblk,))
    tok_next = pl.BlockSpec((nstr, rb, 3 * width),
                            lambda b, c: tok_idx(b, jnp.minimum(c + 1, nc - 1)) + (qkv_col // (3 * width),))
    lead = lambda shape: pl.BlockSpec((nlead,) + shape, lambda b, c: (b,) + (0,) * len(shape))
    const = lambda shape: pl.BlockSpec(shape, lambda b, c: (0,) * len(shape))
    nslab = 3 * width // LANE
    y, cnew, snew = pl.pallas_call(
        functools.partial(_delta_kernel, nstr=nstr, nseg=nseg, lt=lt, chained=chained, nheads=nheads),
        grid=grid,
        in_specs=[
            tok(3 * width, qkv_col // (3 * width)),
            tok_next,
            tok(width, z_col // width),
            tok(LANE, 0),
            lead((SUBLANE, 3 * width)),
            lead((nheads, DN_HEAD_DIM, DN_HEAD_DIM)),
            const((CONV_WIDTH, 3 * width)),
            const((SUBLANE, LANE)),
            const((1, DN_HEAD_DIM)),
        ],
        out_specs=[
            tok(width, 0),
            lead((tail, 3 * width)),
            lead((nheads, DN_HEAD_DIM, DN_HEAD_DIM)),
        ],
        out_shape=[
            jax.ShapeDtypeStruct(proj3.shape[:2] + (width,), F32),
            jax.ShapeDtypeStruct((nseq, tail, 3 * width), F32),
            jax.ShapeDtypeStruct(s0.shape, F32),
        ],
        scratch_shapes=[
            pltpu.VMEM((nstr * nslab, rb + 2 * SUBLANE, LANE), F32),
            pltpu.VMEM((nslab, rb + SUBLANE, LANE), F32),
            pltpu.VMEM((nlead, nheads, DN_HEAD_DIM, DN_HEAD_DIM), F32),
            pltpu.VMEM((2, nstr, rb, 3 * width), F32),
        ],
        compiler_params=_cparams("parallel", "arbitrary"),
        name="delta_chunk" if chained else "delta_step",
    )(proj3, proj3, proj3, tok3(ba), cs8, s0, conv_w, gpar, onorm_w.reshape(1, DN_HEAD_DIM))
    return y.reshape(t, width), cnew, snew


def _merge_kernel(y5_ref, ydn_ref, ga_ref, gb_ref, x_ref, wglu_ref, bglu_ref, wa_ref, wb_ref, wout_ref, gffn_ref,
                  x1_ref, h_ref):
    y = y5_ref[...]
    y = 0.5 * y * (1.0 + jnp.tanh(math.sqrt(2.0 / math.pi) * (y + 0.044715 * (y * y * y))))
    lin = jnp.dot(y.astype(BF16), wglu_ref[...], preferred_element_type=F32) + bglu_ref[...]
    glu = (y * _sigmoid(lin)).astype(BF16)
    a = jnp.dot(glu, wa_ref[...], preferred_element_type=F32)
    b = jnp.dot(ydn_ref[...].astype(BF16), wb_ref[...], preferred_element_type=F32)
    mix = (_sigmoid(ga_ref[...]) * a + _sigmoid(gb_ref[...]) * b).astype(BF16)
    x1 = x_ref[...] + jnp.dot(mix, wout_ref[...], preferred_element_type=F32)
    x1_ref[...] = x1
    h_ref[...] = _rms(x1, gffn_ref[...]).astype(BF16)


def _merge(y5, ydn, proj, x, ga_col, gb_col, w_glu, b_glu, w_a, w_b, w_out, g_ffn, tm):
    t, w5 = y5.shape
    d = x.shape[1]
    row = lambda cols: pl.BlockSpec((tm, cols), lambda i: (i, 0))
    const = lambda shape: pl.BlockSpec(shape, lambda i: (0, 0), pipeline_mode=pl.Buffered(1))
    return pl.pallas_call(
        _merge_kernel,
        grid=(t // tm,),
        in_specs=[
            row(w5), row(w5),
            pl.BlockSpec((tm, d), lambda i: (i, ga_col // d)),
            pl.BlockSpec((tm, d), lambda i: (i, gb_col // d)),
            row(d),
            const((w5, w5)), const((1, w5)), const((w5, d)), const((w5, d)), const((d, d)), const((1, d)),
        ],
        out_specs=[row(d), row(d)],
        out_shape=[jax.ShapeDtypeStruct((t, d), F32), jax.ShapeDtypeStruct((t, d), BF16)],
        compiler_params=_cparams("parallel"),
        name="merge",
    )(y5, ydn, proj, proj, x, w_glu, b_glu, w_a, w_b, w_out, g_ffn)


def _ffn_kernel(h_ref, xn_ref, wg_ref, wu_ref, wd_ref, out_ref, act_scr, *, nk, tk):
    s = pl.program_id(1)

    @pl.when(s < nk)
    def _():
        h = h_ref[...]
        gate = jnp.dot(h, wg_ref[...], preferred_element_type=F32)
        up = jnp.dot(h, wu_ref[...], preferred_element_type=F32)
        act_scr[s] = (gate * _sigmoid(gate) * up).astype(BF16)

    @pl.when(s >= nk)
    def _():
        acc = xn_ref[...]
        for kk in range(nk):
            acc = acc + jnp.dot(act_scr[kk], wd_ref[kk * tk:(kk + 1) * tk, :], preferred_element_type=F32)
        out_ref[...] = acc


def _ffn(x, h, w_gate, w_up, w_down, tm, tk, tn):
    t, d = x.shape
    dff = w_gate.shape[1]
    nk = dff // tk
    up_blk = lambda i, s: (0, jnp.minimum(s, nk - 1))
    down_blk = lambda i, s: (0, jnp.maximum(s - nk, 0))
    out_blk = lambda i, s: (i, jnp.maximum(s - nk, 0))
    return pl.pallas_call(
        functools.partial(_ffn_kernel, nk=nk, tk=tk),
        grid=(t // tm, nk + d // tn),
        in_specs=[
            pl.BlockSpec((tm, d), lambda i, s: (i, 0)),
            pl.BlockSpec((tm, tn), out_blk),
            pl.BlockSpec((d, tk), up_blk),
            pl.BlockSpec((d, tk), up_blk),
            pl.BlockSpec((dff, tn), down_blk),
        ],
        out_specs=pl.BlockSpec((tm, tn), out_blk),
        out_shape=jax.ShapeDtypeStruct((t, d), F32),
        scratch_shapes=[pltpu.VMEM((nk, tm, tk), BF16)],
        compiler_params=_cparams("parallel", "arbitrary"),
        name="ffn",
    )(h, x, w_gate, w_up, w_down)


def _ple_kernel(x_ref, p_ref, gple_ref, gfin_ref, wple_ref, wpg_ref, out_ref, h_scr, *, nj, tn, final):
    j = pl.program_id(1)

    @pl.when(j == 0)
    def _():
        h_scr[...] = _rms(x_ref[...], gple_ref[...]).astype(BF16)

    gate = _sigmoid(jnp.dot(h_scr[...], wpg_ref[...], preferred_element_type=F32))
    emb = jnp.dot(p_ref[...].astype(BF16), wple_ref[...], preferred_element_type=F32)
    upd = emb * gate
    for jj in range(nj):
        @pl.when(j == jj)
        def _(jj=jj):
            out_ref[:, jj * tn:(jj + 1) * tn] = x_ref[:, jj * tn:(jj + 1) * tn] + upd

    if final:
        @pl.when(j == nj - 1)
        def _():
            out_ref[...] = _rms(out_ref[...], gfin_ref[...])


def _ple_final(x, p, g_ple, g_final, w_ple, w_ple_gate, tm, tn, final):
    t, d = x.shape
    pd = p.shape[1]
    nj = d // tn
    wmode = dict(pipeline_mode=pl.Buffered(1)) if nj == 1 else {}
    return pl.pallas_call(
        functools.partial(_ple_kernel, nj=nj, tn=tn, final=final),
        grid=(t // tm, nj),
        in_specs=[
            pl.BlockSpec((tm, d), lambda i, j: (i, 0)),
            pl.BlockSpec((tm, pd), lambda i, j: (i, 0)),
            pl.BlockSpec((1, d), lambda i, j: (0, 0)),
            pl.BlockSpec((1, d), lambda i, j: (0, 0)),
            pl.BlockSpec((pd, tn), lambda i, j: (0, j), **wmode),
            pl.BlockSpec((d, tn), lambda i, j: (0, j), **wmode),
        ],
        out_specs=pl.BlockSpec((tm, d), lambda i, j: (i, 0)),
        out_shape=jax.ShapeDtypeStruct((t, d), F32),
        scratch_shapes=[pltpu.VMEM((tm, d), BF16)],
        compiler_params=_cparams("parallel", "arbitrary"),
        name="ple_final",
    )(x, p, g_ple, g_final, w_ple, w_ple_gate)


COL_Z, COL_GA, COL_GB, COL_QKV = 1024, 2048, 4096, 6144


def _prep_w_in(w_in, layer, d_model, nheads):
    w_t = jnp.swapaxes(w_in, 1, 2)
    s5w = d_model // 2
    dnw = d_model // 2
    off_u = s5w
    off_qkv = off_u + 3 * dnw
    off_z = off_qkv + dnw
    off_a = off_z + 2 * nheads
    tb = W_IN_BLOCK
    assert off_u % tb == 0 and off_qkv % tb == 0 and off_z % tb == 0 and d_model % tb == 0
    shift = off_a - off_z
    assert shift % SUBLANE == 0 and shift <= LANE
    src = ([0] + [off_qkv // tb + i for i in range(dnw // tb)] + [off_z // tb + i for i in range(2 * d_model // tb)]
           + [off_u // tb + i for i in range(3 * dnw // tb)])
    n_al = 1 + dnw // tb
    n_sh = 2 * d_model // tb
    src_tab = jnp.asarray(src, jnp.int32)
    hi_tab = jnp.asarray([(src[min(max(j, n_al), n_al + n_sh - 1)] + 1) * (tb // LANE) for j in range(len(src))],
                         jnp.int32)
    d_in = w_in.shape[1]
    return pl.pallas_call(
        functools.partial(_w_in_kernel, n_al=n_al, n_sh=n_sh, shift=shift),
        grid_spec=pltpu.PrefetchScalarGridSpec(
            num_scalar_prefetch=2,
            grid=(len(src),),
            in_specs=[pl.BlockSpec((None, tb, d_in), lambda j, lo, hi: (layer, lo[j], 0)),
                      pl.BlockSpec((None, LANE, d_in), lambda j, lo, hi: (layer, hi[j], 0))],
            out_specs=[pl.BlockSpec((tb, d_in), lambda j, lo, hi: (j, 0)),
                       pl.BlockSpec((LANE, d_in), lambda j, lo, hi: (0, 0))],
        ),
        out_shape=[jax.ShapeDtypeStruct((len(src) * tb, d_in), BF16), jax.ShapeDtypeStruct((LANE, d_in), BF16)],
        compiler_params=_cparams("arbitrary"),
        name="w_in_cast",
    )(src_tab, hi_tab, w_t, w_t)


W_IN_BLOCK = 1024


def _w_in_kernel(lo_tab, hi_tab, lo_ref, hi_ref, out_ref, ba_ref, *, n_al, n_sh, shift):
    del lo_tab, hi_tab
    j = pl.program_id(0)
    tb, d_in = out_ref.shape
    stitched = (j >= n_al) & (j < n_al + n_sh)

    @pl.when(j == n_al)
    def _():
        ba_ref[...] = jnp.concatenate([lo_ref[0:shift, :], jnp.zeros((LANE - shift, d_in), F32)],
                                      axis=0).astype(BF16)

    @pl.when(stitched)
    def _():
        out_ref[...] = jnp.concatenate([lo_ref[shift:, :], hi_ref[0:shift, :]], axis=0).astype(BF16)

    @pl.when(jnp.logical_not(stitched))
    def _():
        out_ref[...] = lo_ref[...].astype(BF16)


TOKEN_BLOCK = 1024
MERGE_ROWS = 256
INPROJ_COLS = 1536
FFN_HIDDEN_BLOCK = 512
FFN_OUT_BLOCK = 512
PLE_COLS = 2048


def _layer(x3, p3, cstate, s0, h0, lw, final):
    nseq, seqlen, d = x3.shape
    t = nseq * seqlen
    tm = min(TOKEN_BLOCK, t)
    x = x3.reshape(t, d)
    proj, ba = _inproj(x, lw['g_mix'], lw['w_main'], lw['w_ba'], tm, INPROJ_COLS)
    y5, hre, him = _s5_apply(proj, nseq, seqlen, lw['s5_ops'], lw['s5_d'], h0)
    ydn, cnew, snew = _delta_apply(proj, ba, COL_QKV, COL_Z, nseq, seqlen, cstate, s0,
                                   lw['conv_w'], lw['a_log'], lw['dt_bias'], lw['onorm_w'])
    x1, h2 = _merge(y5, ydn, proj, x, COL_GA, COL_GB, lw['w_glu'], lw['b_glu'], lw['w_a'], lw['w_b'],
                    lw['w_out'], lw['g_ffn'], min(MERGE_ROWS, tm))
    x2 = _ffn(x1, h2, lw['w_gate'], lw['w_up'], lw['w_down'], tm, FFN_HIDDEN_BLOCK, FFN_OUT_BLOCK)
    y = _ple_final(x2, p3.reshape(t, -1), lw['g_ple'], lw['g_final'], lw['w_ple'], lw['w_ple_gate'],
                   tm, PLE_COLS, final)
    ng = hre.shape[-1] // S5_STATE
    return (y.reshape(nseq, seqlen, d), cnew, snew,
            hre.reshape(nseq, ng, S5_STATE), him.reshape(nseq, ng, S5_STATE))


def kernel(x_prompt, x_sample, state_conv, state_delta, state_s5_re, state_s5_im, p_prompt, p_sample, g_mix, w_in, conv_w, a_log, dt_bias, onorm_w, s5_a_re, s5_a_im, s5_b_re, s5_b_im, s5_c_re, s5_c_im, s5_d, s5_log_dt, w_glu, b_glu, w_a, w_b, w_out, g_ffn, w_gate, w_up, w_down, g_ple, w_ple, w_ple_gate, g_final):
    depth = w_in.shape[0]
    d_model = x_prompt.shape[-1]
    nheads = state_delta.shape[2]
    nb_p = x_prompt.shape[0]
    f32z = functools.partial(jnp.zeros, dtype=F32)
    yp, ys = x_prompt, x_sample
    outs_p, outs_s = [], []
    for i in range(depth):
        w_main, w_ba = _prep_w_in(w_in, i, d_model, nheads)
        lw = dict(
            g_mix=g_mix[i][None], w_main=w_main, w_ba=w_ba,
            conv_w=conv_w[i], a_log=a_log[i], dt_bias=dt_bias[i], onorm_w=onorm_w[i],
            s5_ops=_s5_operators(s5_a_re[i], s5_a_im[i], s5_b_re[i], s5_b_im[i], s5_c_re[i], s5_c_im[i],
                                 s5_log_dt[i]),
            s5_d=s5_d[i],
            w_glu=w_glu[i].astype(BF16), b_glu=b_glu[i][None], w_a=w_a[i].astype(BF16), w_b=w_b[i].astype(BF16),
            w_out=w_out[i].astype(BF16), g_ffn=g_ffn[i][None],
            w_gate=w_gate[i].astype(BF16), w_up=w_up[i].astype(BF16), w_down=w_down[i].astype(BF16),
            g_ple=g_ple[i][None], w_ple=w_ple[i].astype(BF16), w_ple_gate=w_ple_gate[i].astype(BF16),
            g_final=g_final[None],
        )
        final = i == depth - 1
        yp, c1, d1, r1, m1 = _layer(yp, p_prompt[i], f32z((nb_p,) + state_conv.shape[2:]),
                                    f32z((nb_p,) + state_delta.shape[2:]), None, lw, final)
        ys, c2, d2, r2, m2 = _layer(ys, p_sample[i], state_conv[i], state_delta[i],
                                    (state_s5_re[i], state_s5_im[i]), lw, final)
        outs_p.append((c1, d1, r1, m1))
        outs_s.append((c2, d2, r2, m2))
    stack = lambda outs, k: jnp.stack([o[k] for o in outs])
    return (yp, ys,
            stack(outs_p, 0), stack(outs_p, 1), stack(outs_p, 2), stack(outs_p, 3),
            stack(outs_s, 0), stack(outs_s, 1), stack(outs_s, 2), stack(outs_s, 3))
```

```python
import functools
import math

import jax
import jax.numpy as jnp
from jax import lax
from jax.experimental import pallas as pl
from jax.experimental.pallas import tpu as pltpu

F32 = jnp.float32
BF16 = jnp.bfloat16

EPS = 1e-6
LANE = 128
SUBLANE = 8
VMEM_LIMIT_BYTES = 56 * 1024 * 1024

S5_GROUP = 16
S5_STATE = 64
S5_ROW = SUBLANE
S5_TILE_GROUPS = LANE // S5_GROUP
S5_TILE_STATE = S5_TILE_GROUPS * S5_STATE
DN_HEAD_DIM = 128
CONV_WIDTH = 4
DN_CHUNK = 64
DN_CHUNKS_PER_STEP = 2
DN_STEP_ROWS = 128
DN_STREAMS = 2


def _cparams(*sem):
    return pltpu.CompilerParams(dimension_semantics=sem, vmem_limit_bytes=VMEM_LIMIT_BYTES)


def _mm(a, b):
    return jnp.dot(a.astype(BF16), b.astype(BF16), preferred_element_type=F32)


def _mm_nt(a, b):
    return lax.dot_general(a.astype(BF16), b.astype(BF16), (((1,), (1,)), ((), ())),
                           preferred_element_type=F32)


def _mm_tn(a, b):
    return lax.dot_general(a.astype(BF16), b.astype(BF16), (((0,), (0,)), ((), ())),
                           preferred_element_type=F32)


def _split3(x):
    hi = x.astype(BF16)
    r1 = x - hi.astype(F32)
    mid = r1.astype(BF16)
    lo = (r1 - mid.astype(F32)).astype(BF16)
    return hi, mid, lo


def _sigmoid(x):
    return 1.0 / (1.0 + jnp.exp(-x))


def _rms(x, g):
    ms = jnp.mean(x * x, axis=-1, keepdims=True)
    return x * lax.rsqrt(ms + EPS) * g


def _inproj_kernel(x_ref, g_ref, wt_ref, wbat_ref, out_ref, ba_ref, h_scr):
    nt_dims = (((1,), (1,)), ((), ()))

    j = pl.program_id(1)

    @pl.when(j == 0)
    def _():
        h = _rms(x_ref[...], g_ref[...]).astype(BF16)
        h_scr[...] = h
        ba_ref[...] = lax.dot_general(h, wbat_ref[...], nt_dims, preferred_element_type=F32)
        out_ref[...] = lax.dot_general(h, wt_ref[...], nt_dims, preferred_element_type=F32)

    @pl.when(j > 0)
    def _():
        out_ref[...] = lax.dot_general(h_scr[...], wt_ref[...], nt_dims, preferred_element_type=F32)


def _inproj(x, g, wt_main, wt_ba, tm, tn):
    t, d = x.shape
    n = wt_main.shape[0]
    return pl.pallas_call(
        _inproj_kernel,
        grid=(t // tm, n // tn),
        in_specs=[
            pl.BlockSpec((tm, d), lambda i, j: (i, 0)),
            pl.BlockSpec((1, d), lambda i, j: (0, 0)),
            pl.BlockSpec((tn, d), lambda i, j: (j, 0)),
            pl.BlockSpec((LANE, d), lambda i, j: (0, 0)),
        ],
        out_specs=[
            pl.BlockSpec((tm, tn), lambda i, j: (i, j)),
            pl.BlockSpec((tm, LANE), lambda i, j: (i, 0)),
        ],
        out_shape=[jax.ShapeDtypeStruct((t, n), F32), jax.ShapeDtypeStruct((t, LANE), F32)],
        scratch_shapes=[pltpu.VMEM((tm, d), BF16)],
        compiler_params=_cparams("parallel", "arbitrary"),
        name="inproj",
    )(x, g, wt_main, wt_ba)


def _s5_operators(a_re, a_im, b_re, b_im, c_re, c_im, log_dt):
    g, n = a_re.shape
    nt = g // S5_TILE_GROUPS
    ns = S5_TILE_STATE
    wide = S5_ROW * LANE
    row = lambda x: x.reshape(1, g * n)
    bt = lambda x: x.transpose(2, 0, 1).reshape(S5_GROUP, g * n)
    ct = lambda x: x.transpose(1, 0, 2).reshape(S5_GROUP, g * n)
    vec = pl.BlockSpec((1, ns), lambda t: (0, t))
    mat = pl.BlockSpec((S5_GROUP, ns), lambda t: (0, t))
    return pl.pallas_call(
        _s5_ops_kernel,
        grid=(nt,),
        in_specs=[vec, vec, vec, mat, mat, mat, mat],
        out_specs=[
            pl.BlockSpec((None, wide, 2 * ns), lambda t: (t, 0, 0)),
            pl.BlockSpec((None, wide, 2 * ns), lambda t: (t, 0, 0)),
            pl.BlockSpec((None, wide, wide), lambda t: (t, 0, 0)),
            pl.BlockSpec((None, 1, 2 * ns), lambda t: (t, 0, 0)),
        ],
        out_shape=[
            jax.ShapeDtypeStruct((nt, wide, 2 * ns), BF16),
            jax.ShapeDtypeStruct((nt, wide, 2 * ns), BF16),
            jax.ShapeDtypeStruct((nt, wide, wide), BF16),
            jax.ShapeDtypeStruct((nt, 1, 2 * ns), F32),
        ],
        compiler_params=_cparams("parallel"),
        name="s5_ops",
    )(row(a_re), row(a_im), row(jnp.repeat(log_dt, n)), bt(b_re), bt(b_im), ct(c_re), ct(c_im))


def _mm_nt_split(a, b):
    ah = a.astype(BF16)
    al = (a - ah.astype(F32)).astype(BF16)
    bh = b.astype(BF16)
    bl = (b - bh.astype(F32)).astype(BF16)
    dims = (((1,), (1,)), ((), ()))
    return (lax.dot_general(ah, bh, dims, preferred_element_type=F32)
            + lax.dot_general(ah, bl, dims, preferred_element_type=F32)
            + lax.dot_general(al, bh, dims, preferred_element_type=F32))


def _s5_ops_kernel(ar_ref, ai_ref, ldt_ref, btr_ref, bti_ref, ctr_ref, cti_ref, we_ref, wyt_ref, wk_ref, lam_ref):
    ns = S5_TILE_STATE
    ar = ar_ref[...]
    ai = ai_ref[...]
    dt = jnp.exp(ldt_ref[...])
    kk = lax.broadcasted_iota(jnp.int32, (2 * SUBLANE, ns), 0).astype(F32)
    mag = jnp.exp(ar * dt * kk)
    lr = mag * jnp.cos(ai * dt * kk)
    li = mag * jnp.sin(ai * dt * kk)
    nr = lr[1:2] - 1.0
    ni = li[1:2]
    den = ar * ar + ai * ai
    cr = (nr * ar + ni * ai) / den
    ci = (ni * ar - nr * ai) / den
    btr = btr_ref[...]
    bti = bti_ref[...]
    bbr = cr * btr - ci * bti
    bbi = cr * bti + ci * btr
    ctr = ctr_ref[...]
    cti = cti_ref[...]
    same_group = (lax.broadcasted_iota(jnp.int32, (LANE, ns), 0) // S5_GROUP
                  == lax.broadcasted_iota(jnp.int32, (LANE, ns), 1) // S5_STATE)

    def blockdiag(x):
        return jnp.where(same_group, jnp.concatenate([x] * S5_TILE_GROUPS, axis=0), 0.0)

    for j in range(S5_ROW):
        k = S5_ROW - 1 - j
        er = lr[k:k + 1] * bbr - li[k:k + 1] * bbi
        ei = lr[k:k + 1] * bbi + li[k:k + 1] * bbr
        we_ref[j * LANE:(j + 1) * LANE, :] = jnp.concatenate([blockdiag(er), blockdiag(ei)], axis=1).astype(BF16)
    bq = jnp.concatenate([blockdiag(bbr), blockdiag(bbi)], axis=1)
    kblocks = []
    for k in range(S5_ROW + 1):
        mr = ctr * lr[k:k + 1] - cti * li[k:k + 1]
        mi = ctr * li[k:k + 1] + cti * lr[k:k + 1]
        wy_k = jnp.concatenate([blockdiag(mr), -blockdiag(mi)], axis=1)
        if k >= 1:
            wyt_ref[(k - 1) * LANE:k * LANE, :] = wy_k.astype(BF16)
        if k < S5_ROW:
            kblocks.append(_mm_nt_split(bq, wy_k).astype(BF16))
    zero = jnp.zeros((LANE, LANE), BF16)
    for i in range(S5_ROW):
        for j in range(S5_ROW):
            wk_ref[i * LANE:(i + 1) * LANE, j * LANE:(j + 1) * LANE] = kblocks[j - i] if j >= i else zero
    lam_ref[...] = jnp.concatenate([lr[S5_ROW:S5_ROW + 1], li[S5_ROW:S5_ROW + 1]], axis=1)


def _s5_kernel(*refs, scan, nseq, rps):
    nu = S5_ROW
    rows = nseq * rps
    u_ref, we_ref, wy_ref, wk_ref, d_ref, lam_ref = refs[:6]
    pos = 6
    if not scan:
        h0r_ref, h0i_ref = refs[pos:pos + 2]
        pos += 2
    y_ref, hre_ref, him_ref = refs[pos:pos + 3]
    scratch = refs[pos + 3:]

    ns = S5_TILE_STATE
    us = [u_ref[pl.ds(j, rows, stride=nu), :] for j in range(nu)]
    u = jnp.concatenate(us, axis=1)
    ub = u.astype(BF16)
    e = jnp.dot(ub, we_ref[...], preferred_element_type=F32)
    lam = lam_ref[...]
    lr = lam[:, :ns]
    li = lam[:, ns:]
    if scan:
        e_scr, h_scr = scratch
        nslab = 2 * ns // LANE
        for k in range(nslab):
            for b in range(nseq):
                e_scr[k, pl.ds(b, rps, stride=nseq), :] = e[b * rps:(b + 1) * rps, k * LANE:(k + 1) * LANE]

        def body(c, carry):
            hr, hi = carry
            hcat = jnp.concatenate([hr, hi], axis=1)
            for k in range(nslab):
                h_scr[k, pl.ds(c * nseq, nseq), :] = hcat[:, k * LANE:(k + 1) * LANE]
            ec = jnp.concatenate([e_scr[k, pl.ds(c * nseq, nseq), :] for k in range(nslab)], axis=1)
            return (lr * hr - li * hi + ec[:, :ns], lr * hi + li * hr + ec[:, ns:])

        zero = jnp.zeros((nseq, ns), F32)
        hr, hi = lax.fori_loop(0, rps, body, (zero, zero), unroll=4)
        hre_ref[...] = hr
        him_ref[...] = hi
        hin = jnp.concatenate(
            [jnp.concatenate([h_scr[k, pl.ds(b, rps, stride=nseq), :] for b in range(nseq)], axis=0)
             for k in range(nslab)], axis=1)
    else:
        h0r = h0r_ref[...]
        h0i = h0i_ref[...]
        hre_ref[...] = lr * h0r - li * h0i + e[:, :ns]
        him_ref[...] = lr * h0i + li * h0r + e[:, ns:]
        hin = jnp.concatenate([h0r, h0i], axis=1)
    half = nu * LANE // 2
    y_local = jnp.concatenate(
        [jnp.dot(ub[:, :half], wk_ref[:half, :half], preferred_element_type=F32),
         jnp.dot(ub, wk_ref[:, half:], preferred_element_type=F32)], axis=1)
    y = y_local + lax.dot_general(hin.astype(BF16), wy_ref[...], (((1,), (1,)), ((), ())),
                                  preferred_element_type=F32)
    d = d_ref[...]
    for j in range(nu):
        y_ref[pl.ds(j, rows, stride=nu), :] = y[:, j * LANE:(j + 1) * LANE] + d * us[j]


def _s5_apply(proj, nseq, seqlen, ops, d, h0=None):
    we_t, wy_t, wk_t, lam = ops
    nt = we_t.shape[0]
    width = nt * LANE
    t = nseq * seqlen
    scan = h0 is None
    rps = seqlen // S5_ROW
    assert seqlen % S5_ROW == 0 and (scan or rps == 1)
    ns2 = 2 * S5_TILE_STATE
    tile3 = lambda shape: pl.BlockSpec((None,) + shape, lambda tt: (tt, 0, 0))
    state = pl.BlockSpec((nseq, S5_TILE_STATE), lambda tt: (0, tt))
    in_specs = [
        pl.BlockSpec((t, LANE), lambda tt: (0, tt)),
        tile3((S5_ROW * LANE, ns2)),
        tile3((ns2, S5_ROW * LANE)),
        tile3((S5_ROW * LANE, S5_ROW * LANE)),
        tile3((1, LANE)),
        tile3((1, ns2)),
    ]
    args = [proj, we_t, wy_t, wk_t, d.reshape(nt, 1, LANE), lam]
    scratch = []
    if scan:
        scratch = [pltpu.VMEM((ns2 // LANE, nseq * rps, LANE), F32)] * 2
    else:
        in_specs += [state, state]
        args += [h0[0].reshape(nseq, nt * S5_TILE_STATE), h0[1].reshape(nseq, nt * S5_TILE_STATE)]
    return pl.pallas_call(
        functools.partial(_s5_kernel, scan=scan, nseq=nseq, rps=rps),
        grid=(nt,),
        in_specs=in_specs,
        out_specs=[pl.BlockSpec((t, LANE), lambda tt: (0, tt)), state, state],
        out_shape=[jax.ShapeDtypeStruct((t, width), F32)]
        + [jax.ShapeDtypeStruct((nseq, nt * S5_TILE_STATE), F32)] * 2,
        scratch_shapes=scratch,
        compiler_params=_cparams("parallel"),
        name="s5_scan" if scan else "s5_step",
    )(*args)


def _delta_kernel(qkv_ref, qkvn_ref, z_ref, ba_ref, cs_ref, s0_ref, convw_ref, gpar_ref, onw_ref,
                  y_ref, cnew_ref, snew_ref, ext_scr, csx_scr, s_scr, xs_scr, *, nstr, nseg, lt, chained, nheads):
    rb = nseg * lt
    nsb = 1 if chained else nseg
    hd = DN_HEAD_DIM
    width = nheads * hd
    c = pl.program_id(1)
    nc = pl.num_programs(1)
    tail = CONV_WIDTH - 1
    nslab = 3 * width // LANE
    cw = convw_ref[...]
    streams = range(nstr)
    heads = range(nheads)
    units = [(s, h) for s in streams for h in heads]

    def prepare_head(x_ref, dst_slot, s, h):
        if nsb > 1:
            tl = lax.broadcasted_iota(jnp.int32, (rb, 1), 0) % lt
        for part in range(3):
            cs = slice(part * width + h * hd, part * width + (h + 1) * hd)
            sl = s * nslab + part * nheads + h
            x = x_ref[s, :, cs]
            ext_scr[sl, SUBLANE:SUBLANE + rb, :] = x
            acc = x * cw[tail:tail + 1, cs]
            for k in range(1, CONV_WIDTH):
                xk = ext_scr[sl, SUBLANE - k:SUBLANE - k + rb, :]
                if nsb > 1:
                    xk = jnp.where(tl < k, csx_scr[sl, SUBLANE - k:SUBLANE - k + rb, :], xk)
                acc = acc + xk * cw[tail - k:tail - k + 1, cs]
            if nsb == 1:
                ext_scr[sl, 0:SUBLANE, :] = ext_scr[sl, rb:rb + SUBLANE, :]
            xh = acc * _sigmoid(acc)
            if part < 2:
                scale = hd ** -0.5 if part == 0 else 1.0
                xh = xh * (lax.rsqrt(jnp.sum(xh * xh, axis=-1, keepdims=True) + EPS) * scale)
            xs_scr[dst_slot, s, :, cs] = xh

    @pl.when(c == 0)
    def _init():
        s_scr[...] = s0_ref[...]
        if chained:
            for s in streams:
                for sl in range(nslab):
                    ext_scr[s * nslab + sl, 0:SUBLANE, :] = cs_ref[s, :, sl * LANE:(sl + 1) * LANE]
            for s, h in units:
                prepare_head(qkv_ref, 0, s, h)
        else:
            ext_scr[:, 0:SUBLANE, :] = jnp.zeros((nslab, SUBLANE, LANE), F32)

    if chained:
        slot = c % 2
        pending = list(units)

        def tick(n=1):
            for _ in range(n):
                if pending:
                    prepare_head(qkvn_ref, 1 - slot, *pending.pop(0))

        @pl.when(c == nc - 1)
        def _conv_out():
            for s in streams:
                cnew_ref[s] = qkvn_ref[s, rb - tail:rb, :]
    else:
        cs_all = cs_ref[...].reshape(rb, 3 * width)
        for sl in range(nslab):
            csx_scr[sl, 0:rb, :] = cs_all[:, sl * LANE:(sl + 1) * LANE]
        csx_scr[:, rb:rb + SUBLANE, :] = jnp.zeros((nslab, SUBLANE, LANE), F32)
        slot = 0
        for s, h in units:
            prepare_head(qkv_ref, 0, s, h)
        for sq in range(nsb):
            r0 = SUBLANE + (sq + 1) * lt - tail
            cnew_ref[sq] = jnp.concatenate([ext_scr[sl, r0:r0 + tail, :] for sl in range(nslab)], axis=1)

        def tick(n=1):
            pass

    ri = lax.broadcasted_iota(jnp.int32, (rb, rb), 0)
    ci = lax.broadcasted_iota(jnp.int32, (rb, rb), 1)
    same = (ri // lt) == (ci // lt)
    incl = (ri >= ci) & same
    strict = (ri > ci) & same
    tri = jnp.where(incl, 1.0, 0.0).astype(BF16)
    sel = jnp.where(lax.broadcasted_iota(jnp.int32, (2 * SUBLANE, LANE), 1)
                    == lax.broadcasted_iota(jnp.int32, (2 * SUBLANE, LANE), 0) + nheads, 1.0, 0.0).astype(BF16)
    lastsel = jnp.where(same & ((ci % lt) == lt - 1), 1.0, 0.0).astype(BF16)
    nt_dims = (((1,), (1,)), ((), ()))
    gpar = gpar_ref[...]
    eye = jnp.where(ri == ci, 1.0, 0.0)
    onw = onw_ref[...]
    n_sq = max(int(math.log2(lt)) - 1, 0)
    rowseq = lax.broadcasted_iota(jnp.int32, (rb, 1), 0) // lt

    def exact3(mat, parts, dims=None):
        if dims is None:
            return sum(jnp.dot(mat, p, preferred_element_type=F32) for p in parts)
        return sum(lax.dot_general(mat, p, dims, preferred_element_type=F32) for p in parts)

    q, k, v, beta, gcc, glc, decay = {}, {}, {}, {}, {}, {}, {}
    for s in streams:
        ba = ba_ref[s]
        beta_all = _sigmoid(ba)
        xg = ba + gpar[1:2, :]
        g_all = gpar[0:1, :] * (jnp.maximum(xg, 0.0) + jnp.log(1.0 + jnp.exp(-jnp.abs(xg))))
        gc_col = exact3(tri, _split3(g_all))
        csplit = _split3(gc_col)
        gc_row = exact3(sel, csplit, nt_dims)
        glast_col = exact3(lastsel, csplit)
        xs = xs_scr[slot, s]
        for h in heads:
            u_ = (s, h)
            q[u_] = xs[:, h * hd:(h + 1) * hd]
            k[u_] = xs[:, width + h * hd:width + (h + 1) * hd]
            v[u_] = xs[:, 2 * width + h * hd:2 * width + (h + 1) * hd]
            beta[u_] = beta_all[:, h:h + 1]
            gcc[u_] = gc_col[:, nheads + h:nheads + h + 1]
            glc[u_] = glast_col[:, nheads + h:nheads + h + 1]
            gcr = gc_row[h:h + 1, :]
            decay[u_] = jnp.where(incl, jnp.exp(jnp.where(incl, gcc[u_] - gcr, 0.0)), 0.0)
    per_stage = max(len(units) // (n_sq + 3), 1)
    qk_kk = {u_: _mm_nt(jnp.concatenate([q[u_], k[u_]], axis=0), k[u_]) for u_ in units}
    tick(per_stage)
    qk = {u_: qk_kk[u_][:rb] * decay[u_] for u_ in units}
    a = {u_: jnp.where(strict, beta[u_] * qk_kk[u_][rb:] * decay[u_], 0.0) for u_ in units}
    tm = {u_: eye - a[u_] for u_ in units}
    if n_sq > 0:
        bpow = {u_: _mm(a[u_], a[u_]) for u_ in units}
        tick(per_stage)
    for r in range(n_sq):
        if r == n_sq - 1:
            tm = {u_: tm[u_] + _mm(tm[u_], bpow[u_]) for u_ in units}
        else:
            nxt = {u_: _mm(jnp.concatenate([tm[u_], bpow[u_]], axis=0), bpow[u_]) for u_ in units}
            bpow = {u_: nxt[u_][rb:] for u_ in units}
            tm = {u_: tm[u_] + nxt[u_][:rb] for u_ in units}
        tick(per_stage)
    egc = {u_: jnp.exp(gcc[u_]) for u_ in units}
    uw = {u_: _mm(tm[u_], jnp.concatenate([v[u_] * beta[u_], k[u_] * (beta[u_] * egc[u_])], axis=1)) for u_ in units}
    tick(len(units))
    u = {u_: uw[u_][:, :hd] for u_ in units}
    w = {u_: uw[u_][:, hd:] for u_ in units}
    qe = {u_: q[u_] * egc[u_] for u_ in units}
    ks = {u_: k[u_] * jnp.exp(glc[u_] - gcc[u_]) for u_ in units}
    if chained:
        st = {u_: s_scr[u_[0], u_[1]] for u_ in units}
        o_parts = {u_: [] for u_ in units}
        for sg in range(nseg):
            rows = slice(sg * lt, (sg + 1) * lt)
            vnew = {u_: u[u_][rows] - _mm(w[u_][rows], st[u_]) for u_ in units}
            above = [jnp.zeros((sg * lt, hd), F32)] if sg > 0 else []
            below = [jnp.zeros(((nseg - 1 - sg) * lt, hd), F32)] if sg < nseg - 1 else []
            vpad = {u_: jnp.concatenate(above + [vnew[u_]] + below, axis=0) if nseg > 1 else vnew[u_] for u_ in units}
            for u_ in units:
                o_parts[u_].append(_mm(qe[u_][rows], st[u_]) + _mm(qk[u_][rows], vpad[u_]))
            st = {u_: st[u_] * jnp.exp(glc[u_][sg * lt:sg * lt + 1, :]) + _mm_tn(ks[u_][rows], vnew[u_])
                  for u_ in units}
        for u_ in units:
            s_scr[u_[0], u_[1]] = st[u_]
        o = {u_: jnp.concatenate(o_parts[u_], axis=0) if nseg > 1 else o_parts[u_][0] for u_ in units}
    else:
        ws, qs = {}, {}
        for u_ in units:
            h = u_[1]
            parts = [_mm(jnp.concatenate([w[u_][sq * lt:(sq + 1) * lt], qe[u_][sq * lt:(sq + 1) * lt]], axis=0),
                         s_scr[sq, h]) for sq in range(nsb)]
            ws[u_] = jnp.concatenate([p[:lt] for p in parts], axis=0)
            qs[u_] = jnp.concatenate([p[lt:] for p in parts], axis=0)
        vnew = {u_: u[u_] - ws[u_] for u_ in units}
        o = {u_: qs[u_] + _mm(qk[u_], vnew[u_]) for u_ in units}
        for u_ in units:
            h = u_[1]
            for sq in range(nsb):
                ksm = jnp.where(rowseq == sq, ks[u_], 0.0)
                s_scr[sq, h] = s_scr[sq, h] * jnp.exp(glc[u_][sq * lt:sq * lt + 1, :]) + _mm_tn(ksm, vnew[u_])
    for s, h in units:
        ou = o[(s, h)]
        oh = ou * lax.rsqrt(jnp.mean(ou * ou, axis=-1, keepdims=True) + EPS) * onw
        zh = z_ref[s, :, h * hd:(h + 1) * hd]
        y_ref[s, :, h * hd:(h + 1) * hd] = oh * (zh * _sigmoid(zh))

    @pl.when(c == nc - 1)
    def _state_out():
        snew_ref[...] = s_scr[...]


def _delta_apply(proj, ba, qkv_col, z_col, nseq, seqlen, cstate, s0, conv_w, a_log, dt_bias, onorm_w):
    nheads = s0.shape[1]
    width = nheads * DN_HEAD_DIM
    tail = CONV_WIDTH - 1
    t, ncols = proj.shape
    chained = seqlen >= DN_CHUNK
    if chained:
        lt, nseg = DN_CHUNK, DN_CHUNKS_PER_STEP
        nstr = DN_STREAMS if nseq % DN_STREAMS == 0 else 1
        nlead = nstr
        rb = nseg * lt
        assert seqlen % rb == 0
        nc = seqlen // rb
        grid = (nseq // nstr, nc)
        tok3 = lambda x: x.reshape(nseq, seqlen, x.shape[-1])
        tok_idx = lambda b, c: (b, c)
    else:
        assert seqlen == SUBLANE
        lt, nseg, nstr = seqlen, DN_STEP_ROWS // seqlen, 1
        nlead = nseg
        rb = nseg * lt
        assert nseq % nseg == 0
        nc = 1
        grid = (nseq // nseg, 1)
        tok3 = lambda x: x.reshape(1, t, x.shape[-1])
        tok_idx = lambda b, c: (0, b)
    cs8 = jnp.pad(cstate, ((0, 0), (SUBLANE - tail, 0), (0, 0)))
    gpar = jnp.zeros((SUBLANE, LANE), F32)
    gpar = gpar.at[0, nheads:2 * nheads].set(-jnp.exp(a_log))
    gpar = gpar.at[1, nheads:2 * nheads].set(dt_bias)
    proj3 = tok3(proj)
    tok = lambda cols, colblk: pl.BlockSpec((nstr, rb, cols), lambda b, c: tok_idx(b, c) + (colblk,))
    tok_next = pl.BlockSpec((nstr, rb, 3 * width),
                            lambda b, c: tok_idx(b, jnp.minimum(c + 1, nc - 1)) + (qkv_col // (3 * width),))
    lead = lambda shape: pl.BlockSpec((nlead,) + shape, lambda b, c: (b,) + (0,) * len(shape))
    const = lambda shape: pl.BlockSpec(shape, lambda b, c: (0,) * len(shape))
    nslab = 3 * width // LANE
    y, cnew, snew = pl.pallas_call(
        functools.partial(_delta_kernel, nstr=nstr, nseg=nseg, lt=lt, chained=chained, nheads=nheads),
        grid=grid,
        in_specs=[
            tok(3 * width, qkv_col // (3 * width)),
            tok_next,
            tok(width, z_col // width),
            tok(LANE, 0),
            lead((SUBLANE, 3 * width)),
            lead((nheads, DN_HEAD_DIM, DN_HEAD_DIM)),
            const((CONV_WIDTH, 3 * width)),
            const((SUBLANE, LANE)),
            const((1, DN_HEAD_DIM)),
        ],
        out_specs=[
            tok(width, 0),
            lead((tail, 3 * width)),
            lead((nheads, DN_HEAD_DIM, DN_HEAD_DIM)),
        ],
        out_shape=[
            jax.ShapeDtypeStruct(proj3.shape[:2] + (width,), F32),
            jax.ShapeDtypeStruct((nseq, tail, 3 * width), F32),
            jax.ShapeDtypeStruct(s0.shape, F32),
        ],
        scratch_shapes=[
            pltpu.VMEM((nstr * nslab, rb + 2 * SUBLANE, LANE), F32),
            pltpu.VMEM((nslab, rb + SUBLANE, LANE), F32),
            pltpu.VMEM((nlead, nheads, DN_HEAD_DIM, DN_HEAD_DIM), F32),
            pltpu.VMEM((2, nstr, rb, 3 * width), F32),
        ],
        compiler_params=_cparams("parallel", "arbitrary"),
        name="delta_chunk" if chained else "delta_step",
    )(proj3, proj3, proj3, tok3(ba), cs8, s0, conv_w, gpar, onorm_w.reshape(1, DN_HEAD_DIM))
    return y.reshape(t, width), cnew, snew


def _merge_kernel(y5_ref, ydn_ref, ga_ref, gb_ref, x_ref, wglu_ref, bglu_ref, wa_ref, wb_ref, wout_ref, gffn_ref,
                  x1_ref, h_ref):
    y = y5_ref[...]
    y = 0.5 * y * (1.0 + jnp.tanh(math.sqrt(2.0 / math.pi) * (y + 0.044715 * (y * y * y))))
    lin = jnp.dot(y.astype(BF16), wglu_ref[...], preferred_element_type=F32) + bglu_ref[...]
    glu = (y * _sigmoid(lin)).astype(BF16)
    a = jnp.dot(glu, wa_ref[...], preferred_element_type=F32)
    b = jnp.dot(ydn_ref[...].astype(BF16), wb_ref[...], preferred_element_type=F32)
    mix = (_sigmoid(ga_ref[...]) * a + _sigmoid(gb_ref[...]) * b).astype(BF16)
    x1 = x_ref[...] + jnp.dot(mix, wout_ref[...], preferred_element_type=F32)
    x1_ref[...] = x1
    h_ref[...] = _rms(x1, gffn_ref[...]).astype(BF16)


def _merge(y5, ydn, proj, x, ga_col, gb_col, w_glu, b_glu, w_a, w_b, w_out, g_ffn, tm):
    t, w5 = y5.shape
    d = x.shape[1]
    row = lambda cols: pl.BlockSpec((tm, cols), lambda i: (i, 0))
    const = lambda shape: pl.BlockSpec(shape, lambda i: (0, 0), pipeline_mode=pl.Buffered(1))
    return pl.pallas_call(
        _merge_kernel,
        grid=(t // tm,),
        in_specs=[
            row(w5), row(w5),
            pl.BlockSpec((tm, d), lambda i: (i, ga_col // d)),
            pl.BlockSpec((tm, d), lambda i: (i, gb_col // d)),
            row(d),
            const((w5, w5)), const((1, w5)), const((w5, d)), const((w5, d)), const((d, d)), const((1, d)),
        ],
        out_specs=[row(d), row(d)],
        out_shape=[jax.ShapeDtypeStruct((t, d), F32), jax.ShapeDtypeStruct((t, d), BF16)],
        compiler_params=_cparams("parallel"),
        name="merge",
    )(y5, ydn, proj, proj, x, w_glu, b_glu, w_a, w_b, w_out, g_ffn)


def _ffn_kernel(h_ref, xn_ref, wg_ref, wu_ref, wd_ref, out_ref, act_scr, *, nk, tk):
    s = pl.program_id(1)

    @pl.when(s < nk)
    def _():
        h = h_ref[...]
        gate = jnp.dot(h, wg_ref[...], preferred_element_type=F32)
        up = jnp.dot(h, wu_ref[...], preferred_element_type=F32)
        act_scr[s] = (gate * _sigmoid(gate) * up).astype(BF16)

    @pl.when(s >= nk)
    def _():
        acc = xn_ref[...]
        for kk in range(nk):
            acc = acc + jnp.dot(act_scr[kk], wd_ref[kk * tk:(kk + 1) * tk, :], preferred_element_type=F32)
        out_ref[...] = acc


def _ffn(x, h, w_gate, w_up, w_down, tm, tk, tn):
    t, d = x.shape
    dff = w_gate.shape[1]
    assert t % tm == 0 and dff % tk == 0 and d % tn == 0
    nk = dff // tk
    up_blk = lambda i, s: (0, jnp.minimum(s, nk - 1))
    down_blk = lambda i, s: (0, jnp.maximum(s - nk, 0))
    out_blk = lambda i, s: (i, jnp.maximum(s - nk, 0))
    return pl.pallas_call(
        functools.partial(_ffn_kernel, nk=nk, tk=tk),
        grid=(t // tm, nk + d // tn),
        in_specs=[
            pl.BlockSpec((tm, d), lambda i, s: (i, 0)),
            pl.BlockSpec((tm, tn), out_blk),
            pl.BlockSpec((d, tk), up_blk),
            pl.BlockSpec((d, tk), up_blk),
            pl.BlockSpec((dff, tn), down_blk),
        ],
        out_specs=pl.BlockSpec((tm, tn), out_blk),
        out_shape=jax.ShapeDtypeStruct((t, d), F32),
        scratch_shapes=[pltpu.VMEM((nk, tm, tk), BF16)],
        compiler_params=_cparams("parallel", "arbitrary"),
        name="ffn",
    )(h, x, w_gate, w_up, w_down)


def _ple_kernel(x_ref, p_ref, gple_ref, gfin_ref, wple_ref, wpg_ref, out_ref, h_scr, *, nj, tn, final):
    j = pl.program_id(1)

    @pl.when(j == 0)
    def _():
        h_scr[...] = _rms(x_ref[...], gple_ref[...]).astype(BF16)

    gate = _sigmoid(jnp.dot(h_scr[...], wpg_ref[...], preferred_element_type=F32))
    emb = jnp.dot(p_ref[...].astype(BF16), wple_ref[...], preferred_element_type=F32)
    upd = emb * gate
    for jj in range(nj):
        @pl.when(j == jj)
        def _(jj=jj):
            out_ref[:, jj * tn:(jj + 1) * tn] = x_ref[:, jj * tn:(jj + 1) * tn] + upd

    if final:
        @pl.when(j == nj - 1)
        def _():
            out_ref[...] = _rms(out_ref[...], gfin_ref[...])


def _ple_final(x, p, g_ple, g_final, w_ple, w_ple_gate, tm, tn, final):
    t, d = x.shape
    pd = p.shape[1]
    nj = d // tn
    wmode = dict(pipeline_mode=pl.Buffered(1)) if nj == 1 else {}
    return pl.pallas_call(
        functools.partial(_ple_kernel, nj=nj, tn=tn, final=final),
        grid=(t // tm, nj),
        in_specs=[
            pl.BlockSpec((tm, d), lambda i, j: (i, 0)),
            pl.BlockSpec((tm, pd), lambda i, j: (i, 0)),
            pl.BlockSpec((1, d), lambda i, j: (0, 0)),
            pl.BlockSpec((1, d), lambda i, j: (0, 0)),
            pl.BlockSpec((pd, tn), lambda i, j: (0, j), **wmode),
            pl.BlockSpec((d, tn), lambda i, j: (0, j), **wmode),
        ],
        out_specs=pl.BlockSpec((tm, d), lambda i, j: (i, 0)),
        out_shape=jax.ShapeDtypeStruct((t, d), F32),
        scratch_shapes=[pltpu.VMEM((tm, d), BF16)],
        compiler_params=_cparams("parallel", "arbitrary"),
        name="ple_final",
    )(x, p, g_ple, g_final, w_ple, w_ple_gate)


COL_Z, COL_GA, COL_GB, COL_QKV = 1024, 2048, 4096, 6144


def _prep_w_in(w_in, layer, d_model, nheads):
    w_t = jnp.swapaxes(w_in, 1, 2)
    s5w = d_model // 2
    dnw = d_model // 2
    off_u = s5w
    off_qkv = off_u + 3 * dnw
    off_z = off_qkv + dnw
    off_a = off_z + 2 * nheads
    tb = W_IN_BLOCK
    assert off_u % tb == 0 and off_qkv % tb == 0 and off_z % tb == 0 and d_model % tb == 0
    shift = off_a - off_z
    assert shift % SUBLANE == 0 and shift <= LANE
    src = ([0] + [off_qkv // tb + i for i in range(dnw // tb)] + [off_z // tb + i for i in range(2 * d_model // tb)]
           + [off_u // tb + i for i in range(3 * dnw // tb)])
    n_al = 1 + dnw // tb
    n_sh = 2 * d_model // tb
    src_tab = jnp.asarray(src, jnp.int32)
    hi_tab = jnp.asarray([(src[min(max(j, n_al), n_al + n_sh - 1)] + 1) * (tb // LANE) for j in range(len(src))],
                         jnp.int32)
    d_in = w_in.shape[1]
    return pl.pallas_call(
        functools.partial(_w_in_kernel, n_al=n_al, n_sh=n_sh, shift=shift),
        grid_spec=pltpu.PrefetchScalarGridSpec(
            num_scalar_prefetch=2,
            grid=(len(src),),
            in_specs=[pl.BlockSpec((None, tb, d_in), lambda j, lo, hi: (layer, lo[j], 0)),
                      pl.BlockSpec((None, LANE, d_in), lambda j, lo, hi: (layer, hi[j], 0))],
            out_specs=[pl.BlockSpec((tb, d_in), lambda j, lo, hi: (j, 0)),
                       pl.BlockSpec((LANE, d_in), lambda j, lo, hi: (0, 0))],
        ),
        out_shape=[jax.ShapeDtypeStruct((len(src) * tb, d_in), BF16), jax.ShapeDtypeStruct((LANE, d_in), BF16)],
        compiler_params=_cparams("arbitrary"),
        name="w_in_cast",
    )(src_tab, hi_tab, w_t, w_t)


W_IN_BLOCK = 1024


def _w_in_kernel(lo_tab, hi_tab, lo_ref, hi_ref, out_ref, ba_ref, *, n_al, n_sh, shift):
    del lo_tab, hi_tab
    j = pl.program_id(0)
    tb, d_in = out_ref.shape
    stitched = (j >= n_al) & (j < n_al + n_sh)

    @pl.when(j == n_al)
    def _():
        ba_ref[...] = jnp.concatenate([lo_ref[0:shift, :], jnp.zeros((LANE - shift, d_in), F32)],
                                      axis=0).astype(BF16)

    @pl.when(stitched)
    def _():
        out_ref[...] = jnp.concatenate([lo_ref[shift:, :], hi_ref[0:shift, :]], axis=0).astype(BF16)

    @pl.when(jnp.logical_not(stitched))
    def _():
        out_ref[...] = lo_ref[...].astype(BF16)


TOKEN_BLOCK = 1024
MERGE_ROWS = 256
INPROJ_COLS = 1536
FFN_HIDDEN_BLOCK = 512
FFN_OUT_BLOCK = 512
PLE_COLS = 2048


def _layer(x3, p3, cstate, s0, h0, lw, final):
    nseq, seqlen, d = x3.shape
    t = nseq * seqlen
    tm = min(TOKEN_BLOCK, t)
    x = x3.reshape(t, d)
    proj, ba = _inproj(x, lw['g_mix'], lw['w_main'], lw['w_ba'], tm, INPROJ_COLS)
    y5, hre, him = _s5_apply(proj, nseq, seqlen, lw['s5_ops'], lw['s5_d'], h0)
    ydn, cnew, snew = _delta_apply(proj, ba, COL_QKV, COL_Z, nseq, seqlen, cstate, s0,
                                   lw['conv_w'], lw['a_log'], lw['dt_bias'], lw['onorm_w'])
    x1, h2 = _merge(y5, ydn, proj, x, COL_GA, COL_GB, lw['w_glu'], lw['b_glu'], lw['w_a'], lw['w_b'],
                    lw['w_out'], lw['g_ffn'], min(MERGE_ROWS, tm))
    x2 = _ffn(x1, h2, lw['w_gate'], lw['w_up'], lw['w_down'], tm, FFN_HIDDEN_BLOCK, FFN_OUT_BLOCK)
    y = _ple_final(x2, p3.reshape(t, -1), lw['g_ple'], lw['g_final'], lw['w_ple'], lw['w_ple_gate'],
                   tm, PLE_COLS, final)
    ng = hre.shape[-1] // S5_STATE
    return (y.reshape(nseq, seqlen, d), cnew, snew,
            hre.reshape(nseq, ng, S5_STATE), him.reshape(nseq, ng, S5_STATE))


def kernel(x_prompt, x_sample, state_conv, state_delta, state_s5_re, state_s5_im, p_prompt, p_sample, g_mix, w_in, conv_w, a_log, dt_bias, onorm_w, s5_a_re, s5_a_im, s5_b_re, s5_b_im, s5_c_re, s5_c_im, s5_d, s5_log_dt, w_glu, b_glu, w_a, w_b, w_out, g_ffn, w_gate, w_up, w_down, g_ple, w_ple, w_ple_gate, g_final):
    depth = w_in.shape[0]
    d_model = x_prompt.shape[-1]
    nheads = state_delta.shape[2]
    nb_p = x_prompt.shape[0]
    f32z = functools.partial(jnp.zeros, dtype=F32)
    yp, ys = x_prompt, x_sample
    outs_p, outs_s = [], []
    for i in range(depth):
        w_main, w_ba = _prep_w_in(w_in, i, d_model, nheads)
        lw = dict(
            g_mix=g_mix[i][None], w_main=w_main, w_ba=w_ba,
            conv_w=conv_w[i], a_log=a_log[i], dt_bias=dt_bias[i], onorm_w=onorm_w[i],
            s5_ops=_s5_operators(s5_a_re[i], s5_a_im[i], s5_b_re[i], s5_b_im[i], s5_c_re[i], s5_c_im[i],
                                 s5_log_dt[i]),
            s5_d=s5_d[i],
            w_glu=w_glu[i].astype(BF16), b_glu=b_glu[i][None], w_a=w_a[i].astype(BF16), w_b=w_b[i].astype(BF16),
            w_out=w_out[i].astype(BF16), g_ffn=g_ffn[i][None],
            w_gate=w_gate[i].astype(BF16), w_up=w_up[i].astype(BF16), w_down=w_down[i].astype(BF16),
            g_ple=g_ple[i][None], w_ple=w_ple[i].astype(BF16), w_ple_gate=w_ple_gate[i].astype(BF16),
            g_final=g_final[None],
        )
        final = i == depth - 1
        yp, c1, d1, r1, m1 = _layer(yp, p_prompt[i], f32z((nb_p,) + state_conv.shape[2:]),
                                    f32z((nb_p,) + state_delta.shape[2:]), None, lw, final)
        ys, c2, d2, r2, m2 = _layer(ys, p_sample[i], state_conv[i], state_delta[i],
                                    (state_s5_re[i], state_s5_im[i]), lw, final)
        outs_p.append((c1, d1, r1, m1))
        outs_s.append((c2, d2, r2, m2))
    stack = lambda outs, k: jnp.stack([o[k] for o in outs])
    return (yp, ys,
            stack(outs_p, 0), stack(outs_p, 1), stack(outs_p, 2), stack(outs_p, 3),
            stack(outs_s, 0), stack(outs_s, 1), stack(outs_s, 2), stack(outs_s, 3))
```

```python
import functools
import math

import jax
import jax.numpy as jnp
from jax import lax
from jax.experimental import pallas as pl
from jax.experimental.pallas import tpu as pltpu

F32 = jnp.float32
BF16 = jnp.bfloat16

EPS = 1e-6
LANE = 128
SUBLANE = 8
VMEM_LIMIT_BYTES = 56 * 1024 * 1024

S5_GROUP = 16
S5_STATE = 64
S5_ROW = SUBLANE
S5_TILE_GROUPS = LANE // S5_GROUP
S5_TILE_STATE = S5_TILE_GROUPS * S5_STATE
DN_HEAD_DIM = 128
CONV_WIDTH = 4
DN_CHUNK = 64
DN_CHUNKS_PER_STEP = 2
DN_STEP_ROWS = 128
DN_STREAMS = 2


def _cparams(*sem):
    return pltpu.CompilerParams(dimension_semantics=sem, vmem_limit_bytes=VMEM_LIMIT_BYTES)


def _mm(a, b):
    return jnp.dot(a.astype(BF16), b.astype(BF16), preferred_element_type=F32)


def _mm_nt(a, b):
    return lax.dot_general(a.astype(BF16), b.astype(BF16), (((1,), (1,)), ((), ())),
                           preferred_element_type=F32)


def _mm_tn(a, b):
    return lax.dot_general(a.astype(BF16), b.astype(BF16), (((0,), (0,)), ((), ())),
                           preferred_element_type=F32)


def _split3(x):
    hi = x.astype(BF16)
    r1 = x - hi.astype(F32)
    mid = r1.astype(BF16)
    lo = (r1 - mid.astype(F32)).astype(BF16)
    return hi, mid, lo


def _sigmoid(x):
    return 1.0 / (1.0 + jnp.exp(-x))


def _rms(x, g):
    ms = jnp.mean(x * x, axis=-1, keepdims=True)
    return x * lax.rsqrt(ms + EPS) * g


def _inproj_kernel(x_ref, g_ref, wt_ref, wbat_ref, out_ref, ba_ref, h_scr):
    nt_dims = (((1,), (1,)), ((), ()))

    j = pl.program_id(1)

    @pl.when(j == 0)
    def _():
        h = _rms(x_ref[...], g_ref[...]).astype(BF16)
        h_scr[...] = h
        ba_ref[...] = lax.dot_general(h, wbat_ref[...], nt_dims, preferred_element_type=F32)
        out_ref[...] = lax.dot_general(h, wt_ref[...], nt_dims, preferred_element_type=F32)

    @pl.when(j > 0)
    def _():
        out_ref[...] = lax.dot_general(h_scr[...], wt_ref[...], nt_dims, preferred_element_type=F32)


def _inproj(x, g, wt_main, wt_ba, tm, tn):
    t, d = x.shape
    n = wt_main.shape[0]
    return pl.pallas_call(
        _inproj_kernel,
        grid=(t // tm, n // tn),
        in_specs=[
            pl.BlockSpec((tm, d), lambda i, j: (i, 0)),
            pl.BlockSpec((1, d), lambda i, j: (0, 0)),
            pl.BlockSpec((tn, d), lambda i, j: (j, 0)),
            pl.BlockSpec((LANE, d), lambda i, j: (0, 0)),
        ],
        out_specs=[
            pl.BlockSpec((tm, tn), lambda i, j: (i, j)),
            pl.BlockSpec((tm, LANE), lambda i, j: (i, 0)),
        ],
        out_shape=[jax.ShapeDtypeStruct((t, n), F32), jax.ShapeDtypeStruct((t, LANE), F32)],
        scratch_shapes=[pltpu.VMEM((tm, d), BF16)],
        compiler_params=_cparams("parallel", "arbitrary"),
        name="inproj",
    )(x, g, wt_main, wt_ba)


def _s5_operators(a_re, a_im, b_re, b_im, c_re, c_im, log_dt):
    g, n = a_re.shape
    nt = g // S5_TILE_GROUPS
    ns = S5_TILE_STATE
    wide = S5_ROW * LANE
    row = lambda x: x.reshape(1, g * n)
    bt = lambda x: x.transpose(2, 0, 1).reshape(S5_GROUP, g * n)
    ct = lambda x: x.transpose(1, 0, 2).reshape(S5_GROUP, g * n)
    vec = pl.BlockSpec((1, ns), lambda t: (0, t))
    mat = pl.BlockSpec((S5_GROUP, ns), lambda t: (0, t))
    return pl.pallas_call(
        _s5_ops_kernel,
        grid=(nt,),
        in_specs=[vec, vec, vec, mat, mat, mat, mat],
        out_specs=[
            pl.BlockSpec((None, wide, 2 * ns), lambda t: (t, 0, 0)),
            pl.BlockSpec((None, wide, 2 * ns), lambda t: (t, 0, 0)),
            pl.BlockSpec((None, wide, wide), lambda t: (t, 0, 0)),
            pl.BlockSpec((None, 1, 2 * ns), lambda t: (t, 0, 0)),
        ],
        out_shape=[
            jax.ShapeDtypeStruct((nt, wide, 2 * ns), BF16),
            jax.ShapeDtypeStruct((nt, wide, 2 * ns), BF16),
            jax.ShapeDtypeStruct((nt, wide, wide), BF16),
            jax.ShapeDtypeStruct((nt, 1, 2 * ns), F32),
        ],
        compiler_params=_cparams("parallel"),
        name="s5_ops",
    )(row(a_re), row(a_im), row(jnp.repeat(log_dt, n)), bt(b_re), bt(b_im), ct(c_re), ct(c_im))


def _mm_nt_split(a, b):
    ah = a.astype(BF16)
    al = (a - ah.astype(F32)).astype(BF16)
    bh = b.astype(BF16)
    bl = (b - bh.astype(F32)).astype(BF16)
    dims = (((1,), (1,)), ((), ()))
    return (lax.dot_general(ah, bh, dims, preferred_element_type=F32)
            + lax.dot_general(ah, bl, dims, preferred_element_type=F32)
            + lax.dot_general(al, bh, dims, preferred_element_type=F32))


def _s5_ops_kernel(ar_ref, ai_ref, ldt_ref, btr_ref, bti_ref, ctr_ref, cti_ref, we_ref, wyt_ref, wk_ref, lam_ref):
    ns = S5_TILE_STATE
    ar = ar_ref[...]
    ai = ai_ref[...]
    dt = jnp.exp(ldt_ref[...])
    kk = lax.broadcasted_iota(jnp.int32, (2 * SUBLANE, ns), 0).astype(F32)
    mag = jnp.exp(ar * dt * kk)
    lr = mag * jnp.cos(ai * dt * kk)
    li = mag * jnp.sin(ai * dt * kk)
    nr = lr[1:2] - 1.0
    ni = li[1:2]
    den = ar * ar + ai * ai
    cr = (nr * ar + ni * ai) / den
    ci = (ni * ar - nr * ai) / den
    btr = btr_ref[...]
    bti = bti_ref[...]
    bbr = cr * btr - ci * bti
    bbi = cr * bti + ci * btr
    ctr = ctr_ref[...]
    cti = cti_ref[...]
    same_group = (lax.broadcasted_iota(jnp.int32, (LANE, ns), 0) // S5_GROUP
                  == lax.broadcasted_iota(jnp.int32, (LANE, ns), 1) // S5_STATE)

    def blockdiag(x):
        return jnp.where(same_group, jnp.concatenate([x] * S5_TILE_GROUPS, axis=0), 0.0)

    for j in range(S5_ROW):
        k = S5_ROW - 1 - j
        er = lr[k:k + 1] * bbr - li[k:k + 1] * bbi
        ei = lr[k:k + 1] * bbi + li[k:k + 1] * bbr
        we_ref[j * LANE:(j + 1) * LANE, :] = jnp.concatenate([blockdiag(er), blockdiag(ei)], axis=1).astype(BF16)
    bq = jnp.concatenate([blockdiag(bbr), blockdiag(bbi)], axis=1)
    kblocks = []
    for k in range(S5_ROW + 1):
        mr = ctr * lr[k:k + 1] - cti * li[k:k + 1]
        mi = ctr * li[k:k + 1] + cti * lr[k:k + 1]
        wy_k = jnp.concatenate([blockdiag(mr), -blockdiag(mi)], axis=1)
        if k >= 1:
            wyt_ref[(k - 1) * LANE:k * LANE, :] = wy_k.astype(BF16)
        if k < S5_ROW:
            kblocks.append(_mm_nt_split(bq, wy_k).astype(BF16))
    zero = jnp.zeros((LANE, LANE), BF16)
    for i in range(S5_ROW):
        for j in range(S5_ROW):
            wk_ref[i * LANE:(i + 1) * LANE, j * LANE:(j + 1) * LANE] = kblocks[j - i] if j >= i else zero
    lam_ref[...] = jnp.concatenate([lr[S5_ROW:S5_ROW + 1], li[S5_ROW:S5_ROW + 1]], axis=1)


def _s5_kernel(*refs, scan, nseq, rps):
    nu = S5_ROW
    rows = nseq * rps
    u_ref, we_ref, wy_ref, wk_ref, d_ref, lam_ref = refs[:6]
    pos = 6
    if not scan:
        h0r_ref, h0i_ref = refs[pos:pos + 2]
        pos += 2
    y_ref, hre_ref, him_ref = refs[pos:pos + 3]
    scratch = refs[pos + 3:]

    ns = S5_TILE_STATE
    us = [u_ref[pl.ds(j, rows, stride=nu), :] for j in range(nu)]
    u = jnp.concatenate(us, axis=1)
    ub = u.astype(BF16)
    e = jnp.dot(ub, we_ref[...], preferred_element_type=F32)
    lam = lam_ref[...]
    lr = lam[:, :ns]
    li = lam[:, ns:]
    if scan:
        e_scr, h_scr = scratch
        nslab = 2 * ns // LANE
        for k in range(nslab):
            for b in range(nseq):
                e_scr[k, pl.ds(b, rps, stride=nseq), :] = e[b * rps:(b + 1) * rps, k * LANE:(k + 1) * LANE]

        def body(c, carry):
            hr, hi = carry
            hcat = jnp.concatenate([hr, hi], axis=1)
            for k in range(nslab):
                h_scr[k, pl.ds(c * nseq, nseq), :] = hcat[:, k * LANE:(k + 1) * LANE]
            ec = jnp.concatenate([e_scr[k, pl.ds(c * nseq, nseq), :] for k in range(nslab)], axis=1)
            return (lr * hr - li * hi + ec[:, :ns], lr * hi + li * hr + ec[:, ns:])

        zero = jnp.zeros((nseq, ns), F32)
        hr, hi = lax.fori_loop(0, rps, body, (zero, zero), unroll=4)
        hre_ref[...] = hr
        him_ref[...] = hi
        hin = jnp.concatenate(
            [jnp.concatenate([h_scr[k, pl.ds(b, rps, stride=nseq), :] for b in range(nseq)], axis=0)
             for k in range(nslab)], axis=1)
    else:
        h0r = h0r_ref[...]
        h0i = h0i_ref[...]
        hre_ref[...] = lr * h0r - li * h0i + e[:, :ns]
        him_ref[...] = lr * h0i + li * h0r + e[:, ns:]
        hin = jnp.concatenate([h0r, h0i], axis=1)
    half = nu * LANE // 2
    y_local = jnp.concatenate(
        [jnp.dot(ub[:, :half], wk_ref[:half, :half], preferred_element_type=F32),
         jnp.dot(ub, wk_ref[:, half:], preferred_element_type=F32)], axis=1)
    y = y_local + lax.dot_general(hin.astype(BF16), wy_ref[...], (((1,), (1,)), ((), ())),
                                  preferred_element_type=F32)
    d = d_ref[...]
    for j in range(nu):
        y_ref[pl.ds(j, rows, stride=nu), :] = y[:, j * LANE:(j + 1) * LANE] + d * us[j]


def _s5_apply(proj, nseq, seqlen, ops, d, h0=None):
    we_t, wy_t, wk_t, lam = ops
    nt = we_t.shape[0]
    width = nt * LANE
    t = nseq * seqlen
    scan = h0 is None
    rps = seqlen // S5_ROW
    assert seqlen % S5_ROW == 0 and (scan or rps == 1)
    ns2 = 2 * S5_TILE_STATE
    tile3 = lambda shape: pl.BlockSpec((None,) + shape, lambda tt: (tt, 0, 0))
    state = pl.BlockSpec((nseq, S5_TILE_STATE), lambda tt: (0, tt))
    in_specs = [
        pl.BlockSpec((t, LANE), lambda tt: (0, tt)),
        tile3((S5_ROW * LANE, ns2)),
        tile3((ns2, S5_ROW * LANE)),
        tile3((S5_ROW * LANE, S5_ROW * LANE)),
        tile3((1, LANE)),
        tile3((1, ns2)),
    ]
    args = [proj, we_t, wy_t, wk_t, d.reshape(nt, 1, LANE), lam]
    scratch = []
    if scan:
        scratch = [pltpu.VMEM((ns2 // LANE, nseq * rps, LANE), F32)] * 2
    else:
        in_specs += [state, state]
        args += [h0[0].reshape(nseq, nt * S5_TILE_STATE), h0[1].reshape(nseq, nt * S5_TILE_STATE)]
    return pl.pallas_call(
        functools.partial(_s5_kernel, scan=scan, nseq=nseq, rps=rps),
        grid=(nt,),
        in_specs=in_specs,
        out_specs=[pl.BlockSpec((t, LANE), lambda tt: (0, tt)), state, state],
        out_shape=[jax.ShapeDtypeStruct((t, width), F32)]
        + [jax.ShapeDtypeStruct((nseq, nt * S5_TILE_STATE), F32)] * 2,
        scratch_shapes=scratch,
        compiler_params=_cparams("parallel"),
        name="s5_scan" if scan else "s5_step",
    )(*args)


def _delta_kernel(qkv_ref, qkvn_ref, z_ref, ba_ref, cs_ref, s0_ref, convw_ref, gpar_ref, onw_ref,
                  y_ref, cnew_ref, snew_ref, ext_scr, csx_scr, s_scr, xs_scr, *, nstr, nseg, lt, chained, nheads):
    rb = nseg * lt
    nsb = 1 if chained else nseg
    hd = DN_HEAD_DIM
    width = nheads * hd
    c = pl.program_id(1)
    nc = pl.num_programs(1)
    tail = CONV_WIDTH - 1
    nslab = 3 * width // LANE
    cw = convw_ref[...]
    streams = range(nstr)
    heads = range(nheads)
    units = [(s, h) for s in streams for h in heads]

    def prepare_head(x_ref, dst_slot, s, h):
        if nsb > 1:
            tl = lax.broadcasted_iota(jnp.int32, (rb, 1), 0) % lt
        for part in range(3):
            cs = slice(part * width + h * hd, part * width + (h + 1) * hd)
            sl = s * nslab + part * nheads + h
            x = x_ref[s, :, cs]
            ext_scr[sl, SUBLANE:SUBLANE + rb, :] = x
            acc = x * cw[tail:tail + 1, cs]
            for k in range(1, CONV_WIDTH):
                xk = ext_scr[sl, SUBLANE - k:SUBLANE - k + rb, :]
                if nsb > 1:
                    xk = jnp.where(tl < k, csx_scr[sl, SUBLANE - k:SUBLANE - k + rb, :], xk)
                acc = acc + xk * cw[tail - k:tail - k + 1, cs]
            if nsb == 1:
                ext_scr[sl, 0:SUBLANE, :] = ext_scr[sl, rb:rb + SUBLANE, :]
            xh = acc * _sigmoid(acc)
            if part < 2:
                scale = hd ** -0.5 if part == 0 else 1.0
                xh = xh * (lax.rsqrt(jnp.sum(xh * xh, axis=-1, keepdims=True) + EPS) * scale)
            xs_scr[dst_slot, s, :, cs] = xh

    @pl.when(c == 0)
    def _init():
        s_scr[...] = s0_ref[...]
        if chained:
            for s in streams:
                for sl in range(nslab):
                    ext_scr[s * nslab + sl, 0:SUBLANE, :] = cs_ref[s, :, sl * LANE:(sl + 1) * LANE]
            for s, h in units:
                prepare_head(qkv_ref, 0, s, h)
        else:
            ext_scr[:, 0:SUBLANE, :] = jnp.zeros((nslab, SUBLANE, LANE), F32)

    if chained:
        slot = c % 2
        pending = list(units)

        def tick(n=1):
            for _ in range(n):
                if pending:
                    prepare_head(qkvn_ref, 1 - slot, *pending.pop(0))

        @pl.when(c == nc - 1)
        def _conv_out():
            for s in streams:
                cnew_ref[s] = qkvn_ref[s, rb - tail:rb, :]
    else:
        cs_all = cs_ref[...].reshape(rb, 3 * width)
        for sl in range(nslab):
            csx_scr[sl, 0:rb, :] = cs_all[:, sl * LANE:(sl + 1) * LANE]
        csx_scr[:, rb:rb + SUBLANE, :] = jnp.zeros((nslab, SUBLANE, LANE), F32)
        slot = 0
        for s, h in units:
            prepare_head(qkv_ref, 0, s, h)
        for sq in range(nsb):
            r0 = SUBLANE + (sq + 1) * lt - tail
            cnew_ref[sq] = jnp.concatenate([ext_scr[sl, r0:r0 + tail, :] for sl in range(nslab)], axis=1)

        def tick(n=1):
            pass

    ri = lax.broadcasted_iota(jnp.int32, (rb, rb), 0)
    ci = lax.broadcasted_iota(jnp.int32, (rb, rb), 1)
    same = (ri // lt) == (ci // lt)
    incl = (ri >= ci) & same
    strict = (ri > ci) & same
    tri = jnp.where(incl, 1.0, 0.0).astype(BF16)
    sel = jnp.where(lax.broadcasted_iota(jnp.int32, (2 * SUBLANE, LANE), 1)
                    == lax.broadcasted_iota(jnp.int32, (2 * SUBLANE, LANE), 0) + nheads, 1.0, 0.0).astype(BF16)
    lastsel = jnp.where(same & ((ci % lt) == lt - 1), 1.0, 0.0).astype(BF16)
    nt_dims = (((1,), (1,)), ((), ()))
    gpar = gpar_ref[...]
    eye = jnp.where(ri == ci, 1.0, 0.0)
    onw = onw_ref[...]
    n_sq = max(int(math.log2(lt)) - 1, 0)
    rowseq = lax.broadcasted_iota(jnp.int32, (rb, 1), 0) // lt

    def exact3(mat, parts, dims=None):
        if dims is None:
            return sum(jnp.dot(mat, p, preferred_element_type=F32) for p in parts)
        return sum(lax.dot_general(mat, p, dims, preferred_element_type=F32) for p in parts)

    q, k, v, beta, gcc, glc, decay = {}, {}, {}, {}, {}, {}, {}
    for s in streams:
        ba = ba_ref[s]
        beta_all = _sigmoid(ba)
        xg = ba + gpar[1:2, :]
        g_all = gpar[0:1, :] * (jnp.maximum(xg, 0.0) + jnp.log(1.0 + jnp.exp(-jnp.abs(xg))))
        gc_col = exact3(tri, _split3(g_all))
        csplit = _split3(gc_col)
        gc_row = exact3(sel, csplit, nt_dims)
        glast_col = exact3(lastsel, csplit)
        xs = xs_scr[slot, s]
        for h in heads:
            u_ = (s, h)
            q[u_] = xs[:, h * hd:(h + 1) * hd]
            k[u_] = xs[:, width + h * hd:width + (h + 1) * hd]
            v[u_] = xs[:, 2 * width + h * hd:2 * width + (h + 1) * hd]
            beta[u_] = beta_all[:, h:h + 1]
            gcc[u_] = gc_col[:, nheads + h:nheads + h + 1]
            glc[u_] = glast_col[:, nheads + h:nheads + h + 1]
            gcr = gc_row[h:h + 1, :]
            decay[u_] = jnp.where(incl, jnp.exp(jnp.where(incl, gcc[u_] - gcr, 0.0)), 0.0)
    per_stage = max(len(units) // (n_sq + 3), 1)
    qk_kk = {u_: _mm_nt(jnp.concatenate([q[u_], k[u_]], axis=0), k[u_]) for u_ in units}
    tick(per_stage)
    qk = {u_: qk_kk[u_][:rb] * decay[u_] for u_ in units}
    a = {u_: jnp.where(strict, beta[u_] * qk_kk[u_][rb:] * decay[u_], 0.0) for u_ in units}
    tm = {u_: eye - a[u_] for u_ in units}
    if n_sq > 0:
        bpow = {u_: _mm(a[u_], a[u_]) for u_ in units}
        tick(per_stage)
    for r in range(n_sq):
        if r == n_sq - 1:
            tm = {u_: tm[u_] + _mm(tm[u_], bpow[u_]) for u_ in units}
        else:
            nxt = {u_: _mm(jnp.concatenate([tm[u_], bpow[u_]], axis=0), bpow[u_]) for u_ in units}
            bpow = {u_: nxt[u_][rb:] for u_ in units}
            tm = {u_: tm[u_] + nxt[u_][:rb] for u_ in units}
        tick(per_stage)
    egc = {u_: jnp.exp(gcc[u_]) for u_ in units}
    uw = {u_: _mm(tm[u_], jnp.concatenate([v[u_] * beta[u_], k[u_] * (beta[u_] * egc[u_])], axis=1)) for u_ in units}
    tick(len(units))
    u = {u_: uw[u_][:, :hd] for u_ in units}
    w = {u_: uw[u_][:, hd:] for u_ in units}
    qe = {u_: q[u_] * egc[u_] for u_ in units}
    ks = {u_: k[u_] * jnp.exp(glc[u_] - gcc[u_]) for u_ in units}
    if chained:
        st = {u_: s_scr[u_[0], u_[1]] for u_ in units}
        o_parts = {u_: [] for u_ in units}
        for sg in range(nseg):
            rows = slice(sg * lt, (sg + 1) * lt)
            vnew = {u_: u[u_][rows] - _mm(w[u_][rows], st[u_]) for u_ in units}
            above = [jnp.zeros((sg * lt, hd), F32)] if sg > 0 else []
            below = [jnp.zeros(((nseg - 1 - sg) * lt, hd), F32)] if sg < nseg - 1 else []
            vpad = {u_: jnp.concatenate(above + [vnew[u_]] + below, axis=0) if nseg > 1 else vnew[u_] for u_ in units}
            for u_ in units:
                o_parts[u_].append(_mm(qe[u_][rows], st[u_]) + _mm(qk[u_][rows], vpad[u_]))
            st = {u_: st[u_] * jnp.exp(glc[u_][sg * lt:sg * lt + 1, :]) + _mm_tn(ks[u_][rows], vnew[u_])
                  for u_ in units}
        for u_ in units:
            s_scr[u_[0], u_[1]] = st[u_]
        o = {u_: jnp.concatenate(o_parts[u_], axis=0) if nseg > 1 else o_parts[u_][0] for u_ in units}
    else:
        ws, qs = {}, {}
        for u_ in units:
            h = u_[1]
            parts = [_mm(jnp.concatenate([w[u_][sq * lt:(sq + 1) * lt], qe[u_][sq * lt:(sq + 1) * lt]], axis=0),
                         s_scr[sq, h]) for sq in range(nsb)]
            ws[u_] = jnp.concatenate([p[:lt] for p in parts], axis=0)
            qs[u_] = jnp.concatenate([p[lt:] for p in parts], axis=0)
        vnew = {u_: u[u_] - ws[u_] for u_ in units}
        o = {u_: qs[u_] + _mm(qk[u_], vnew[u_]) for u_ in units}
        for u_ in units:
            h = u_[1]
            for sq in range(nsb):
                ksm = jnp.where(rowseq == sq, ks[u_], 0.0)
                s_scr[sq, h] = s_scr[sq, h] * jnp.exp(glc[u_][sq * lt:sq * lt + 1, :]) + _mm_tn(ksm, vnew[u_])
    for s, h in units:
        ou = o[(s, h)]
        oh = ou * lax.rsqrt(jnp.mean(ou * ou, axis=-1, keepdims=True) + EPS) * onw
        zh = z_ref[s, :, h * hd:(h + 1) * hd]
        y_ref[s, :, h * hd:(h + 1) * hd] = oh * (zh * _sigmoid(zh))

    @pl.when(c == nc - 1)
    def _state_out():
        snew_ref[...] = s_scr[...]


def _delta_apply(proj, ba, qkv_col, z_col, nseq, seqlen, cstate, s0, conv_w, a_log, dt_bias, onorm_w):
    nheads = s0.shape[1]
    width = nheads * DN_HEAD_DIM
    tail = CONV_WIDTH - 1
    t, ncols = proj.shape
    chained = seqlen >= DN_CHUNK
    if chained:
        lt, nseg = DN_CHUNK, DN_CHUNKS_PER_STEP
        nstr = DN_STREAMS if nseq % DN_STREAMS == 0 else 1
        nlead = nstr
        rb = nseg * lt
        assert seqlen % rb == 0
        nc = seqlen // rb
        grid = (nseq // nstr, nc)
        tok3 = lambda x: x.reshape(nseq, seqlen, x.shape[-1])
        tok_idx = lambda b, c: (b, c)
    else:
        assert seqlen == SUBLANE
        lt, nseg, nstr = seqlen, DN_STEP_ROWS // seqlen, 1
        nlead = nseg
        rb = nseg * lt
        assert nseq % nseg == 0
        nc = 1
        grid = (nseq // nseg, 1)
        tok3 = lambda x: x.reshape(1, t, x.shape[-1])
        tok_idx = lambda b, c: (0, b)
    cs8 = jnp.pad(cstate, ((0, 0), (SUBLANE - tail, 0), (0, 0)))
    gpar = jnp.zeros((SUBLANE, LANE), F32)
    gpar = gpar.at[0, nheads:2 * nheads].set(-jnp.exp(a_log))
    gpar = gpar.at[1, nheads:2 * nheads].set(dt_bias)
    proj3 = tok3(proj)
    tok = lambda cols, colblk: pl.BlockSpec((nstr, rb, cols), lambda b, c: tok_idx(b, c) + (colblk,))
    tok_next = pl.BlockSpec((nstr, rb, 3 * width),
                            lambda b, c: tok_idx(b, jnp.minimum(c + 1, nc - 1)) + (qkv_col // (3 * width),))
    lead = lambda shape: pl.BlockSpec((nlead,) + shape, lambda b, c: (b,) + (0,) * len(shape))
    const = lambda shape: pl.BlockSpec(shape, lambda b, c: (0,) * len(shape))
    nslab = 3 * width // LANE
    y, cnew, snew = pl.pallas_call(
        functools.partial(_delta_kernel, nstr=nstr, nseg=nseg, lt=lt, chained=chained, nheads=nheads),
        grid=grid,
        in_specs=[
            pl.BlockSpec((nstr, rb, 3 * width), lambda b, c: tok_idx(b, 0) + (qkv_col // (3 * width),)),
            tok_next,
            tok(width, z_col // width),
            tok(LANE, 0),
            lead((SUBLANE, 3 * width)),
            lead((nheads, DN_HEAD_DIM, DN_HEAD_DIM)),
            const((CONV_WIDTH, 3 * width)),
            const((SUBLANE, LANE)),
            const((1, DN_HEAD_DIM)),
        ],
        out_specs=[
            tok(width, 0),
            lead((tail, 3 * width)),
            lead((nheads, DN_HEAD_DIM, DN_HEAD_DIM)),
        ],
        out_shape=[
            jax.ShapeDtypeStruct(proj3.shape[:2] + (width,), F32),
            jax.ShapeDtypeStruct((nseq, tail, 3 * width), F32),
            jax.ShapeDtypeStruct(s0.shape, F32),
        ],
        scratch_shapes=[
            pltpu.VMEM((nstr * nslab, rb + 2 * SUBLANE, LANE), F32),
            pltpu.VMEM((nslab, rb + SUBLANE, LANE), F32),
            pltpu.VMEM((nlead, nheads, DN_HEAD_DIM, DN_HEAD_DIM), F32),
            pltpu.VMEM((2, nstr, rb, 3 * width), F32),
        ],
        compiler_params=_cparams("parallel", "arbitrary"),
        name="delta_chunk" if chained else "delta_step",
    )(proj3, proj3, proj3, tok3(ba), cs8, s0, conv_w, gpar, onorm_w.reshape(1, DN_HEAD_DIM))
    return y.reshape(t, width), cnew, snew


def _merge_kernel(y5_ref, ydn_ref, ga_ref, gb_ref, x_ref, wglu_ref, bglu_ref, wa_ref, wb_ref, wout_ref, gffn_ref,
                  x1_ref, h_ref):
    y = y5_ref[...]
    y = 0.5 * y * (1.0 + jnp.tanh(math.sqrt(2.0 / math.pi) * (y + 0.044715 * (y * y * y))))
    lin = jnp.dot(y.astype(BF16), wglu_ref[...], preferred_element_type=F32) + bglu_ref[...]
    glu = (y * _sigmoid(lin)).astype(BF16)
    a = jnp.dot(glu, wa_ref[...], preferred_element_type=F32)
    b = jnp.dot(ydn_ref[...].astype(BF16), wb_ref[...], preferred_element_type=F32)
    mix = (_sigmoid(ga_ref[...]) * a + _sigmoid(gb_ref[...]) * b).astype(BF16)
    x1 = x_ref[...] + jnp.dot(mix, wout_ref[...], preferred_element_type=F32)
    x1_ref[...] = x1
    h_ref[...] = _rms(x1, gffn_ref[...]).astype(BF16)


def _merge(y5, ydn, proj, x, ga_col, gb_col, w_glu, b_glu, w_a, w_b, w_out, g_ffn, tm):
    t, w5 = y5.shape
    d = x.shape[1]
    row = lambda cols: pl.BlockSpec((tm, cols), lambda i: (i, 0))
    const = lambda shape: pl.BlockSpec(shape, lambda i: (0, 0), pipeline_mode=pl.Buffered(1))
    return pl.pallas_call(
        _merge_kernel,
        grid=(t // tm,),
        in_specs=[
            row(w5), row(w5),
            pl.BlockSpec((tm, d), lambda i: (i, ga_col // d)),
            pl.BlockSpec((tm, d), lambda i: (i, gb_col // d)),
            row(d),
            const((w5, w5)), const((1, w5)), const((w5, d)), const((w5, d)), const((d, d)), const((1, d)),
        ],
        out_specs=[row(d), row(d)],
        out_shape=[jax.ShapeDtypeStruct((t, d), F32), jax.ShapeDtypeStruct((t, d), BF16)],
        compiler_params=_cparams("parallel"),
        name="merge",
    )(y5, ydn, proj, proj, x, w_glu, b_glu, w_a, w_b, w_out, g_ffn)


def _ffn_kernel(h_ref, xn_ref, wg_ref, wu_ref, wd_ref, out_ref, act_scr, *, nk, tk):
    s = pl.program_id(1)

    @pl.when(s < nk)
    def _():
        h = h_ref[...]
        gate = jnp.dot(h, wg_ref[...], preferred_element_type=F32)
        up = jnp.dot(h, wu_ref[...], preferred_element_type=F32)
        act_scr[s] = (gate * _sigmoid(gate) * up).astype(BF16)

    @pl.when(s >= nk)
    def _():
        acc = xn_ref[...]
        for kk in range(nk):
            acc = acc + jnp.dot(act_scr[kk], wd_ref[kk * tk:(kk + 1) * tk, :], preferred_element_type=F32)
        out_ref[...] = acc


def _ffn(x, h, w_gate, w_up, w_down, tm, tk, tn):
    t, d = x.shape
    dff = w_gate.shape[1]
    assert t % tm == 0 and dff % tk == 0 and d % tn == 0
    nk = dff // tk
    up_blk = lambda i, s: (0, jnp.minimum(s, nk - 1))
    down_blk = lambda i, s: (0, jnp.maximum(s - nk, 0))
    out_blk = lambda i, s: (i, jnp.maximum(s - nk, 0))
    return pl.pallas_call(
        functools.partial(_ffn_kernel, nk=nk, tk=tk),
        grid=(t // tm, nk + d // tn),
        in_specs=[
            pl.BlockSpec((tm, d), lambda i, s: (i, 0)),
            pl.BlockSpec((tm, tn), out_blk),
            pl.BlockSpec((d, tk), up_blk),
            pl.BlockSpec((d, tk), up_blk),
            pl.BlockSpec((dff, tn), down_blk),
        ],
        out_specs=pl.BlockSpec((tm, tn), out_blk),
        out_shape=jax.ShapeDtypeStruct((t, d), F32),
        scratch_shapes=[pltpu.VMEM((nk, tm, tk), BF16)],
        compiler_params=_cparams("parallel", "arbitrary"),
        name="ffn",
    )(h, x, w_gate, w_up, w_down)


def _ple_kernel(x_ref, p_ref, gple_ref, gfin_ref, wple_ref, wpg_ref, out_ref, h_scr, *, nj, tn, final):
    j = pl.program_id(1)

    @pl.when(j == 0)
    def _():
        h_scr[...] = _rms(x_ref[...], gple_ref[...]).astype(BF16)

    gate = _sigmoid(jnp.dot(h_scr[...], wpg_ref[...], preferred_element_type=F32))
    emb = jnp.dot(p_ref[...].astype(BF16), wple_ref[...], preferred_element_type=F32)
    upd = emb * gate
    for jj in range(nj):
        @pl.when(j == jj)
        def _(jj=jj):
            out_ref[:, jj * tn:(jj + 1) * tn] = x_ref[:, jj * tn:(jj + 1) * tn] + upd

    if final:
        @pl.when(j == nj - 1)
        def _():
            out_ref[...] = _rms(out_ref[...], gfin_ref[...])


def _ple_final(x, p, g_ple, g_final, w_ple, w_ple_gate, tm, tn, final):
    t, d = x.shape
    pd = p.shape[1]
    nj = d // tn
    wmode = dict(pipeline_mode=pl.Buffered(1)) if nj == 1 else {}
    return pl.pallas_call(
        functools.partial(_ple_kernel, nj=nj, tn=tn, final=final),
        grid=(t // tm, nj),
        in_specs=[
            pl.BlockSpec((tm, d), lambda i, j: (i, 0)),
            pl.BlockSpec((tm, pd), lambda i, j: (i, 0)),
            pl.BlockSpec((1, d), lambda i, j: (0, 0)),
            pl.BlockSpec((1, d), lambda i, j: (0, 0)),
            pl.BlockSpec((pd, tn), lambda i, j: (0, j), **wmode),
            pl.BlockSpec((d, tn), lambda i, j: (0, j), **wmode),
        ],
        out_specs=pl.BlockSpec((tm, d), lambda i, j: (i, 0)),
        out_shape=jax.ShapeDtypeStruct((t, d), F32),
        scratch_shapes=[pltpu.VMEM((tm, d), BF16)],
        compiler_params=_cparams("parallel", "arbitrary"),
        name="ple_final",
    )(x, p, g_ple, g_final, w_ple, w_ple_gate)


COL_Z, COL_GA, COL_GB, COL_QKV = 1024, 2048, 4096, 6144


def _prep_w_in(w_in, layer, d_model, nheads):
    w_t = jnp.swapaxes(w_in, 1, 2)
    s5w = d_model // 2
    dnw = d_model // 2
    off_u = s5w
    off_qkv = off_u + 3 * dnw
    off_z = off_qkv + dnw
    off_a = off_z + 2 * nheads
    tb = W_IN_BLOCK
    assert off_u % tb == 0 and off_qkv % tb == 0 and off_z % tb == 0 and d_model % tb == 0
    shift = off_a - off_z
    assert shift % SUBLANE == 0 and shift <= LANE
    src = ([0] + [off_qkv // tb + i for i in range(dnw // tb)] + [off_z // tb + i for i in range(2 * d_model // tb)]
           + [off_u // tb + i for i in range(3 * dnw // tb)])
    n_al = 1 + dnw // tb
    n_sh = 2 * d_model // tb
    src_tab = jnp.asarray(src, jnp.int32)
    hi_tab = jnp.asarray([(src[min(max(j, n_al), n_al + n_sh - 1)] + 1) * (tb // LANE) for j in range(len(src))],
                         jnp.int32)
    d_in = w_in.shape[1]
    return pl.pallas_call(
        functools.partial(_w_in_kernel, n_al=n_al, n_sh=n_sh, shift=shift),
        grid_spec=pltpu.PrefetchScalarGridSpec(
            num_scalar_prefetch=2,
            grid=(len(src),),
            in_specs=[pl.BlockSpec((None, tb, d_in), lambda j, lo, hi: (layer, lo[j], 0)),
                      pl.BlockSpec((None, LANE, d_in), lambda j, lo, hi: (layer, hi[j], 0))],
            out_specs=[pl.BlockSpec((tb, d_in), lambda j, lo, hi: (j, 0)),
                       pl.BlockSpec((LANE, d_in), lambda j, lo, hi: (0, 0))],
        ),
        out_shape=[jax.ShapeDtypeStruct((len(src) * tb, d_in), BF16), jax.ShapeDtypeStruct((LANE, d_in), BF16)],
        compiler_params=_cparams("arbitrary"),
        name="w_in_cast",
    )(src_tab, hi_tab, w_t, w_t)


W_IN_BLOCK = 1024


def _w_in_kernel(lo_tab, hi_tab, lo_ref, hi_ref, out_ref, ba_ref, *, n_al, n_sh, shift):
    del lo_tab, hi_tab
    j = pl.program_id(0)
    tb, d_in = out_ref.shape
    stitched = (j >= n_al) & (j < n_al + n_sh)

    @pl.when(j == n_al)
    def _():
        ba_ref[...] = jnp.concatenate([lo_ref[0:shift, :], jnp.zeros((LANE - shift, d_in), F32)],
                                      axis=0).astype(BF16)

    @pl.when(stitched)
    def _():
        out_ref[...] = jnp.concatenate([lo_ref[shift:, :], hi_ref[0:shift, :]], axis=0).astype(BF16)

    @pl.when(jnp.logical_not(stitched))
    def _():
        out_ref[...] = lo_ref[...].astype(BF16)


TOKEN_BLOCK = 1024
MERGE_ROWS = 256
INPROJ_COLS = 1536
FFN_HIDDEN_BLOCK = 512
FFN_OUT_BLOCK = 512
PLE_COLS = 2048


def _layer(x3, p3, cstate, s0, h0, lw, final):
    nseq, seqlen, d = x3.shape
    t = nseq * seqlen
    tm = min(TOKEN_BLOCK, t)
    x = x3.reshape(t, d)
    proj, ba = _inproj(x, lw['g_mix'], lw['w_main'], lw['w_ba'], tm, INPROJ_COLS)
    y5, hre, him = _s5_apply(proj, nseq, seqlen, lw['s5_ops'], lw['s5_d'], h0)
    ydn, cnew, snew = _delta_apply(proj, ba, COL_QKV, COL_Z, nseq, seqlen, cstate, s0,
                                   lw['conv_w'], lw['a_log'], lw['dt_bias'], lw['onorm_w'])
    x1, h2 = _merge(y5, ydn, proj, x, COL_GA, COL_GB, lw['w_glu'], lw['b_glu'], lw['w_a'], lw['w_b'],
                    lw['w_out'], lw['g_ffn'], min(MERGE_ROWS, tm))
    x2 = _ffn(x1, h2, lw['w_gate'], lw['w_up'], lw['w_down'], tm, FFN_HIDDEN_BLOCK, FFN_OUT_BLOCK)
    y = _ple_final(x2, p3.reshape(t, -1), lw['g_ple'], lw['g_final'], lw['w_ple'], lw['w_ple_gate'],
                   tm, PLE_COLS, final)
    ng = hre.shape[-1] // S5_STATE
    return (y.reshape(nseq, seqlen, d), cnew, snew,
            hre.reshape(nseq, ng, S5_STATE), him.reshape(nseq, ng, S5_STATE))


def kernel(x_prompt, x_sample, state_conv, state_delta, state_s5_re, state_s5_im, p_prompt, p_sample, g_mix, w_in, conv_w, a_log, dt_bias, onorm_w, s5_a_re, s5_a_im, s5_b_re, s5_b_im, s5_c_re, s5_c_im, s5_d, s5_log_dt, w_glu, b_glu, w_a, w_b, w_out, g_ffn, w_gate, w_up, w_down, g_ple, w_ple, w_ple_gate, g_final):
    depth = w_in.shape[0]
    d_model = x_prompt.shape[-1]
    nheads = state_delta.shape[2]
    nb_p = x_prompt.shape[0]
    f32z = functools.partial(jnp.zeros, dtype=F32)
    yp, ys = x_prompt, x_sample
    outs_p, outs_s = [], []
    for i in range(depth):
        w_main, w_ba = _prep_w_in(w_in, i, d_model, nheads)
        lw = dict(
            g_mix=g_mix[i][None], w_main=w_main, w_ba=w_ba,
            conv_w=conv_w[i], a_log=a_log[i], dt_bias=dt_bias[i], onorm_w=onorm_w[i],
            s5_ops=_s5_operators(s5_a_re[i], s5_a_im[i], s5_b_re[i], s5_b_im[i], s5_c_re[i], s5_c_im[i],
                                 s5_log_dt[i]),
            s5_d=s5_d[i],
            w_glu=w_glu[i].astype(BF16), b_glu=b_glu[i][None], w_a=w_a[i].astype(BF16), w_b=w_b[i].astype(BF16),
            w_out=w_out[i].astype(BF16), g_ffn=g_ffn[i][None],
            w_gate=w_gate[i].astype(BF16), w_up=w_up[i].astype(BF16), w_down=w_down[i].astype(BF16),
            g_ple=g_ple[i][None], w_ple=w_ple[i].astype(BF16), w_ple_gate=w_ple_gate[i].astype(BF16),
            g_final=g_final[None],
        )
        final = i == depth - 1
        yp, c1, d1, r1, m1 = _layer(yp, p_prompt[i], f32z((nb_p,) + state_conv.shape[2:]),
                                    f32z((nb_p,) + state_delta.shape[2:]), None, lw, final)
        ys, c2, d2, r2, m2 = _layer(ys, p_sample[i], state_conv[i], state_delta[i],
                                    (state_s5_re[i], state_s5_im[i]), lw, final)
        outs_p.append((c1, d1, r1, m1))
        outs_s.append((c2, d2, r2, m2))
    stack = lambda outs, k: jnp.stack([o[k] for o in outs])
    return (yp, ys,
            stack(outs_p, 0), stack(outs_p, 1), stack(outs_p, 2), stack(outs_p, 3),
            stack(outs_s, 0), stack(outs_s, 1), stack(outs_s, 2), stack(outs_s, 3))
```

```python
import functools
import math

import jax
import jax.numpy as jnp
from jax import lax
from jax.experimental import pallas as pl
from jax.experimental.pallas import tpu as pltpu

F32 = jnp.float32
BF16 = jnp.bfloat16

EPS = 1e-6
LANE = 128
SUBLANE = 8
VMEM_LIMIT_BYTES = 56 * 1024 * 1024

S5_GROUP = 16
S5_STATE = 64
S5_ROW = SUBLANE
S5_TILE_GROUPS = LANE // S5_GROUP
S5_TILE_STATE = S5_TILE_GROUPS * S5_STATE
DN_HEAD_DIM = 128
CONV_WIDTH = 4
DN_CHUNK = 64
DN_CHUNKS_PER_STEP = 2
DN_STEP_ROWS = 128
DN_STREAMS = 2


def _cparams(*sem):
    return pltpu.CompilerParams(dimension_semantics=sem, vmem_limit_bytes=VMEM_LIMIT_BYTES)


def _mm(a, b):
    return jnp.dot(a.astype(BF16), b.astype(BF16), preferred_element_type=F32)


def _mm_nt(a, b):
    return lax.dot_general(a.astype(BF16), b.astype(BF16), (((1,), (1,)), ((), ())),
                           preferred_element_type=F32)


def _mm_tn(a, b):
    return lax.dot_general(a.astype(BF16), b.astype(BF16), (((0,), (0,)), ((), ())),
                           preferred_element_type=F32)


def _split3(x):
    hi = x.astype(BF16)
    r1 = x - hi.astype(F32)
    mid = r1.astype(BF16)
    lo = (r1 - mid.astype(F32)).astype(BF16)
    return hi, mid, lo


def _sigmoid(x):
    return 1.0 / (1.0 + jnp.exp(-x))


def _rms(x, g):
    ms = jnp.mean(x * x, axis=-1, keepdims=True)
    return x * lax.rsqrt(ms + EPS) * g


def _inproj_kernel(x_ref, g_ref, wt_ref, wbat_ref, out_ref, ba_ref, h_scr):
    nt_dims = (((1,), (1,)), ((), ()))

    j = pl.program_id(1)

    @pl.when(j == 0)
    def _():
        h = _rms(x_ref[...], g_ref[...]).astype(BF16)
        h_scr[...] = h
        ba_ref[...] = lax.dot_general(h, wbat_ref[...], nt_dims, preferred_element_type=F32)
        out_ref[...] = lax.dot_general(h, wt_ref[...], nt_dims, preferred_element_type=F32)

    @pl.when(j > 0)
    def _():
        out_ref[...] = lax.dot_general(h_scr[...], wt_ref[...], nt_dims, preferred_element_type=F32)


def _inproj(x, g, wt_main, wt_ba, tm, tn):
    t, d = x.shape
    n = wt_main.shape[0]
    return pl.pallas_call(
        _inproj_kernel,
        grid=(t // tm, n // tn),
        in_specs=[
            pl.BlockSpec((tm, d), lambda i, j: (i, 0)),
            pl.BlockSpec((1, d), lambda i, j: (0, 0)),
            pl.BlockSpec((tn, d), lambda i, j: (j, 0)),
            pl.BlockSpec((LANE, d), lambda i, j: (0, 0)),
        ],
        out_specs=[
            pl.BlockSpec((tm, tn), lambda i, j: (i, j)),
            pl.BlockSpec((tm, LANE), lambda i, j: (i, 0)),
        ],
        out_shape=[jax.ShapeDtypeStruct((t, n), F32), jax.ShapeDtypeStruct((t, LANE), F32)],
        scratch_shapes=[pltpu.VMEM((tm, d), BF16)],
        compiler_params=_cparams("parallel", "arbitrary"),
        name="inproj",
    )(x, g, wt_main, wt_ba)


def _s5_operators(a_re, a_im, b_re, b_im, c_re, c_im, log_dt):
    g, n = a_re.shape
    nt = g // S5_TILE_GROUPS
    ns = S5_TILE_STATE
    wide = S5_ROW * LANE
    row = lambda x: x.reshape(1, g * n)
    bt = lambda x: x.transpose(2, 0, 1).reshape(S5_GROUP, g * n)
    ct = lambda x: x.transpose(1, 0, 2).reshape(S5_GROUP, g * n)
    vec = pl.BlockSpec((1, ns), lambda t: (0, t))
    mat = pl.BlockSpec((S5_GROUP, ns), lambda t: (0, t))
    return pl.pallas_call(
        _s5_ops_kernel,
        grid=(nt,),
        in_specs=[vec, vec, vec, mat, mat, mat, mat],
        out_specs=[
            pl.BlockSpec((None, wide, 2 * ns), lambda t: (t, 0, 0)),
            pl.BlockSpec((None, wide, 2 * ns), lambda t: (t, 0, 0)),
            pl.BlockSpec((None, wide, wide), lambda t: (t, 0, 0)),
            pl.BlockSpec((None, 1, 2 * ns), lambda t: (t, 0, 0)),
        ],
        out_shape=[
            jax.ShapeDtypeStruct((nt, wide, 2 * ns), BF16),
            jax.ShapeDtypeStruct((nt, wide, 2 * ns), BF16),
            jax.ShapeDtypeStruct((nt, wide, wide), BF16),
            jax.ShapeDtypeStruct((nt, 1, 2 * ns), F32),
        ],
        compiler_params=_cparams("parallel"),
        name="s5_ops",
    )(row(a_re), row(a_im), row(jnp.repeat(log_dt, n)), bt(b_re), bt(b_im), ct(c_re), ct(c_im))


def _mm_nt_split(a, b):
    ah = a.astype(BF16)
    al = (a - ah.astype(F32)).astype(BF16)
    bh = b.astype(BF16)
    bl = (b - bh.astype(F32)).astype(BF16)
    dims = (((1,), (1,)), ((), ()))
    return (lax.dot_general(ah, bh, dims, preferred_element_type=F32)
            + lax.dot_general(ah, bl, dims, preferred_element_type=F32)
            + lax.dot_general(al, bh, dims, preferred_element_type=F32))


def _s5_ops_kernel(ar_ref, ai_ref, ldt_ref, btr_ref, bti_ref, ctr_ref, cti_ref, we_ref, wyt_ref, wk_ref, lam_ref):
    ns = S5_TILE_STATE
    ar = ar_ref[...]
    ai = ai_ref[...]
    dt = jnp.exp(ldt_ref[...])
    kk = lax.broadcasted_iota(jnp.int32, (2 * SUBLANE, ns), 0).astype(F32)
    mag = jnp.exp(ar * dt * kk)
    lr = mag * jnp.cos(ai * dt * kk)
    li = mag * jnp.sin(ai * dt * kk)
    nr = lr[1:2] - 1.0
    ni = li[1:2]
    den = ar * ar + ai * ai
    cr = (nr * ar + ni * ai) / den
    ci = (ni * ar - nr * ai) / den
    btr = btr_ref[...]
    bti = bti_ref[...]
    bbr = cr * btr - ci * bti
    bbi = cr * bti + ci * btr
    ctr = ctr_ref[...]
    cti = cti_ref[...]
    same_group = (lax.broadcasted_iota(jnp.int32, (LANE, ns), 0) // S5_GROUP
                  == lax.broadcasted_iota(jnp.int32, (LANE, ns), 1) // S5_STATE)

    def blockdiag(x):
        return jnp.where(same_group, jnp.concatenate([x] * S5_TILE_GROUPS, axis=0), 0.0)

    for j in range(S5_ROW):
        k = S5_ROW - 1 - j
        er = lr[k:k + 1] * bbr - li[k:k + 1] * bbi
        ei = lr[k:k + 1] * bbi + li[k:k + 1] * bbr
        we_ref[j * LANE:(j + 1) * LANE, :] = jnp.concatenate([blockdiag(er), blockdiag(ei)], axis=1).astype(BF16)
    bq = jnp.concatenate([blockdiag(bbr), blockdiag(bbi)], axis=1)
    kblocks = []
    for k in range(S5_ROW + 1):
        mr = ctr * lr[k:k + 1] - cti * li[k:k + 1]
        mi = ctr * li[k:k + 1] + cti * lr[k:k + 1]
        wy_k = jnp.concatenate([blockdiag(mr), -blockdiag(mi)], axis=1)
        if k >= 1:
            wyt_ref[(k - 1) * LANE:k * LANE, :] = wy_k.astype(BF16)
        if k < S5_ROW:
            kblocks.append(_mm_nt_split(bq, wy_k).astype(BF16))
    zero = jnp.zeros((LANE, LANE), BF16)
    for i in range(S5_ROW):
        for j in range(S5_ROW):
            wk_ref[i * LANE:(i + 1) * LANE, j * LANE:(j + 1) * LANE] = kblocks[j - i] if j >= i else zero
    lam_ref[...] = jnp.concatenate([lr[S5_ROW:S5_ROW + 1], li[S5_ROW:S5_ROW + 1]], axis=1)


def _s5_kernel(*refs, scan, nseq, rps):
    nu = S5_ROW
    rows = nseq * rps
    u_ref, we_ref, wy_ref, wk_ref, d_ref, lam_ref = refs[:6]
    pos = 6
    if not scan:
        h0r_ref, h0i_ref = refs[pos:pos + 2]
        pos += 2
    y_ref, hre_ref, him_ref = refs[pos:pos + 3]
    scratch = refs[pos + 3:]

    ns = S5_TILE_STATE
    us = [u_ref[pl.ds(j, rows, stride=nu), :] for j in range(nu)]
    u = jnp.concatenate(us, axis=1)
    ub = u.astype(BF16)
    e = jnp.dot(ub, we_ref[...], preferred_element_type=F32)
    lam = lam_ref[...]
    lr = lam[:, :ns]
    li = lam[:, ns:]
    if scan:
        e_scr, h_scr = scratch
        nslab = 2 * ns // LANE
        for k in range(nslab):
            for b in range(nseq):
                e_scr[k, pl.ds(b, rps, stride=nseq), :] = e[b * rps:(b + 1) * rps, k * LANE:(k + 1) * LANE]

        def body(c, carry):
            hr, hi = carry
            hcat = jnp.concatenate([hr, hi], axis=1)
            for k in range(nslab):
                h_scr[k, pl.ds(c * nseq, nseq), :] = hcat[:, k * LANE:(k + 1) * LANE]
            ec = jnp.concatenate([e_scr[k, pl.ds(c * nseq, nseq), :] for k in range(nslab)], axis=1)
            return (lr * hr - li * hi + ec[:, :ns], lr * hi + li * hr + ec[:, ns:])

        zero = jnp.zeros((nseq, ns), F32)
        hr, hi = lax.fori_loop(0, rps, body, (zero, zero), unroll=4)
        hre_ref[...] = hr
        him_ref[...] = hi
        hin = jnp.concatenate(
            [jnp.concatenate([h_scr[k, pl.ds(b, rps, stride=nseq), :] for b in range(nseq)], axis=0)
             for k in range(nslab)], axis=1)
    else:
        h0r = h0r_ref[...]
        h0i = h0i_ref[...]
        hre_ref[...] = lr * h0r - li * h0i + e[:, :ns]
        him_ref[...] = lr * h0i + li * h0r + e[:, ns:]
        hin = jnp.concatenate([h0r, h0i], axis=1)
    half = nu * LANE // 2
    y_local = jnp.concatenate(
        [jnp.dot(ub[:, :half], wk_ref[:half, :half], preferred_element_type=F32),
         jnp.dot(ub, wk_ref[:, half:], preferred_element_type=F32)], axis=1)
    y = y_local + lax.dot_general(hin.astype(BF16), wy_ref[...], (((1,), (1,)), ((), ())),
                                  preferred_element_type=F32)
    d = d_ref[...]
    for j in range(nu):
        y_ref[pl.ds(j, rows, stride=nu), :] = y[:, j * LANE:(j + 1) * LANE] + d * us[j]


def _s5_apply(proj, nseq, seqlen, ops, d, h0=None):
    we_t, wy_t, wk_t, lam = ops
    nt = we_t.shape[0]
    width = nt * LANE
    t = nseq * seqlen
    scan = h0 is None
    rps = seqlen // S5_ROW
    assert seqlen % S5_ROW == 0 and (scan or rps == 1)
    ns2 = 2 * S5_TILE_STATE
    tile3 = lambda shape: pl.BlockSpec((None,) + shape, lambda tt: (tt, 0, 0))
    state = pl.BlockSpec((nseq, S5_TILE_STATE), lambda tt: (0, tt))
    in_specs = [
        pl.BlockSpec((t, LANE), lambda tt: (0, tt)),
        tile3((S5_ROW * LANE, ns2)),
        tile3((ns2, S5_ROW * LANE)),
        tile3((S5_ROW * LANE, S5_ROW * LANE)),
        tile3((1, LANE)),
        tile3((1, ns2)),
    ]
    args = [proj, we_t, wy_t, wk_t, d.reshape(nt, 1, LANE), lam]
    scratch = []
    if scan:
        scratch = [pltpu.VMEM((ns2 // LANE, nseq * rps, LANE), F32)] * 2
    else:
        in_specs += [state, state]
        args += [h0[0].reshape(nseq, nt * S5_TILE_STATE), h0[1].reshape(nseq, nt * S5_TILE_STATE)]
    return pl.pallas_call(
        functools.partial(_s5_kernel, scan=scan, nseq=nseq, rps=rps),
        grid=(nt,),
        in_specs=in_specs,
        out_specs=[pl.BlockSpec((t, LANE), lambda tt: (0, tt)), state, state],
        out_shape=[jax.ShapeDtypeStruct((t, width), F32)]
        + [jax.ShapeDtypeStruct((nseq, nt * S5_TILE_STATE), F32)] * 2,
        scratch_shapes=scratch,
        compiler_params=_cparams("parallel"),
        name="s5_scan" if scan else "s5_step",
    )(*args)


def _delta_kernel(*refs, nstr, nseg, lt, chained, nheads, ncast):
    qkv_ref, qkvn_ref, z_ref, ba_ref, cs_ref, s0_ref, convw_ref, gpar_ref, onw_ref = refs[:9]
    cast_in = refs[9:9 + ncast]
    y_ref, cnew_ref, snew_ref = refs[9 + ncast:12 + ncast]
    cast_out = refs[12 + ncast:12 + 2 * ncast]
    ext_scr, csx_scr, s_scr, xs_scr = refs[12 + 2 * ncast:]
    for src, dst in zip(cast_in, cast_out):
        dst[...] = src[...].astype(dst.dtype)
    rb = nseg * lt
    nsb = 1 if chained else nseg
    hd = DN_HEAD_DIM
    width = nheads * hd
    c = pl.program_id(1)
    nc = pl.num_programs(1)
    tail = CONV_WIDTH - 1
    nslab = 3 * width // LANE
    cw = convw_ref[...]
    streams = range(nstr)
    heads = range(nheads)
    units = [(s, h) for s in streams for h in heads]

    def prepare_head(x_ref, dst_slot, s, h):
        if nsb > 1:
            tl = lax.broadcasted_iota(jnp.int32, (rb, 1), 0) % lt
        for part in range(3):
            cs = slice(part * width + h * hd, part * width + (h + 1) * hd)
            sl = s * nslab + part * nheads + h
            x = x_ref[s, :, cs]
            ext_scr[sl, SUBLANE:SUBLANE + rb, :] = x
            acc = x * cw[tail:tail + 1, cs]
            for k in range(1, CONV_WIDTH):
                xk = ext_scr[sl, SUBLANE - k:SUBLANE - k + rb, :]
                if nsb > 1:
                    xk = jnp.where(tl < k, csx_scr[sl, SUBLANE - k:SUBLANE - k + rb, :], xk)
                acc = acc + xk * cw[tail - k:tail - k + 1, cs]
            if nsb == 1:
                ext_scr[sl, 0:SUBLANE, :] = ext_scr[sl, rb:rb + SUBLANE, :]
            xh = acc * _sigmoid(acc)
            if part < 2:
                scale = hd ** -0.5 if part == 0 else 1.0
                xh = xh * (lax.rsqrt(jnp.sum(xh * xh, axis=-1, keepdims=True) + EPS) * scale)
            xs_scr[dst_slot, s, :, cs] = xh

    @pl.when(c == 0)
    def _init():
        s_scr[...] = s0_ref[...]
        if chained:
            for s in streams:
                for sl in range(nslab):
                    ext_scr[s * nslab + sl, 0:SUBLANE, :] = cs_ref[s, :, sl * LANE:(sl + 1) * LANE]
            for s, h in units:
                prepare_head(qkv_ref, 0, s, h)
        else:
            ext_scr[:, 0:SUBLANE, :] = jnp.zeros((nslab, SUBLANE, LANE), F32)

    if chained:
        slot = c % 2
        pending = list(units)

        def tick(n=1):
            for _ in range(n):
                if pending:
                    prepare_head(qkvn_ref, 1 - slot, *pending.pop(0))

        @pl.when(c == nc - 1)
        def _conv_out():
            for s in streams:
                cnew_ref[s] = qkvn_ref[s, rb - tail:rb, :]
    else:
        cs_all = cs_ref[...].reshape(rb, 3 * width)
        for sl in range(nslab):
            csx_scr[sl, 0:rb, :] = cs_all[:, sl * LANE:(sl + 1) * LANE]
        csx_scr[:, rb:rb + SUBLANE, :] = jnp.zeros((nslab, SUBLANE, LANE), F32)
        slot = 0
        for s, h in units:
            prepare_head(qkv_ref, 0, s, h)
        for sq in range(nsb):
            r0 = SUBLANE + (sq + 1) * lt - tail
            cnew_ref[sq] = jnp.concatenate([ext_scr[sl, r0:r0 + tail, :] for sl in range(nslab)], axis=1)

        def tick(n=1):
            pass

    ri = lax.broadcasted_iota(jnp.int32, (rb, rb), 0)
    ci = lax.broadcasted_iota(jnp.int32, (rb, rb), 1)
    same = (ri // lt) == (ci // lt)
    incl = (ri >= ci) & same
    strict = (ri > ci) & same
    tri = jnp.where(incl, 1.0, 0.0).astype(BF16)
    sel = jnp.where(lax.broadcasted_iota(jnp.int32, (2 * SUBLANE, LANE), 1)
                    == lax.broadcasted_iota(jnp.int32, (2 * SUBLANE, LANE), 0) + nheads, 1.0, 0.0).astype(BF16)
    lastsel = jnp.where(same & ((ci % lt) == lt - 1), 1.0, 0.0).astype(BF16)
    nt_dims = (((1,), (1,)), ((), ()))
    gpar = gpar_ref[...]
    eye = jnp.where(ri == ci, 1.0, 0.0)
    onw = onw_ref[...]
    n_sq = max(int(math.log2(lt)) - 1, 0)
    rowseq = lax.broadcasted_iota(jnp.int32, (rb, 1), 0) // lt

    def exact3(mat, parts, dims=None):
        if dims is None:
            return sum(jnp.dot(mat, p, preferred_element_type=F32) for p in parts)
        return sum(lax.dot_general(mat, p, dims, preferred_element_type=F32) for p in parts)

    q, k, v, beta, gcc, glc, decay = {}, {}, {}, {}, {}, {}, {}
    for s in streams:
        ba = ba_ref[s]
        beta_all = _sigmoid(ba)
        xg = ba + gpar[1:2, :]
        g_all = gpar[0:1, :] * (jnp.maximum(xg, 0.0) + jnp.log(1.0 + jnp.exp(-jnp.abs(xg))))
        gc_col = exact3(tri, _split3(g_all))
        csplit = _split3(gc_col)
        gc_row = exact3(sel, csplit, nt_dims)
        glast_col = exact3(lastsel, csplit)
        xs = xs_scr[slot, s]
        for h in heads:
            u_ = (s, h)
            q[u_] = xs[:, h * hd:(h + 1) * hd]
            k[u_] = xs[:, width + h * hd:width + (h + 1) * hd]
            v[u_] = xs[:, 2 * width + h * hd:2 * width + (h + 1) * hd]
            beta[u_] = beta_all[:, h:h + 1]
            gcc[u_] = gc_col[:, nheads + h:nheads + h + 1]
            glc[u_] = glast_col[:, nheads + h:nheads + h + 1]
            gcr = gc_row[h:h + 1, :]
            decay[u_] = jnp.where(incl, jnp.exp(jnp.where(incl, gcc[u_] - gcr, 0.0)), 0.0)
    per_stage = max(len(units) // (n_sq + 3), 1)
    qk_kk = {u_: _mm_nt(jnp.concatenate([q[u_], k[u_]], axis=0), k[u_]) for u_ in units}
    tick(per_stage)
    qk = {u_: qk_kk[u_][:rb] * decay[u_] for u_ in units}
    a = {u_: jnp.where(strict, beta[u_] * qk_kk[u_][rb:] * decay[u_], 0.0) for u_ in units}
    tm = {u_: eye - a[u_] for u_ in units}
    if n_sq > 0:
        bpow = {u_: _mm(a[u_], a[u_]) for u_ in units}
        tick(per_stage)
    for r in range(n_sq):
        if r == n_sq - 1:
            tm = {u_: tm[u_] + _mm(tm[u_], bpow[u_]) for u_ in units}
        else:
            nxt = {u_: _mm(jnp.concatenate([tm[u_], bpow[u_]], axis=0), bpow[u_]) for u_ in units}
            bpow = {u_: nxt[u_][rb:] for u_ in units}
            tm = {u_: tm[u_] + nxt[u_][:rb] for u_ in units}
        tick(per_stage)
    egc = {u_: jnp.exp(gcc[u_]) for u_ in units}
    uw = {u_: _mm(tm[u_], jnp.concatenate([v[u_] * beta[u_], k[u_] * (beta[u_] * egc[u_])], axis=1)) for u_ in units}
    tick(len(units))
    u = {u_: uw[u_][:, :hd] for u_ in units}
    w = {u_: uw[u_][:, hd:] for u_ in units}
    qe = {u_: q[u_] * egc[u_] for u_ in units}
    ks = {u_: k[u_] * jnp.exp(glc[u_] - gcc[u_]) for u_ in units}
    if chained:
        st = {u_: s_scr[u_[0], u_[1]] for u_ in units}
        o_parts = {u_: [] for u_ in units}
        for sg in range(nseg):
            rows = slice(sg * lt, (sg + 1) * lt)
            vnew = {u_: u[u_][rows] - _mm(w[u_][rows], st[u_]) for u_ in units}
            above = [jnp.zeros((sg * lt, hd), F32)] if sg > 0 else []
            below = [jnp.zeros(((nseg - 1 - sg) * lt, hd), F32)] if sg < nseg - 1 else []
            vpad = {u_: jnp.concatenate(above + [vnew[u_]] + below, axis=0) if nseg > 1 else vnew[u_] for u_ in units}
            for u_ in units:
                o_parts[u_].append(_mm(qe[u_][rows], st[u_]) + _mm(qk[u_][rows], vpad[u_]))
            st = {u_: st[u_] * jnp.exp(glc[u_][sg * lt:sg * lt + 1, :]) + _mm_tn(ks[u_][rows], vnew[u_])
                  for u_ in units}
        for u_ in units:
            s_scr[u_[0], u_[1]] = st[u_]
        o = {u_: jnp.concatenate(o_parts[u_], axis=0) if nseg > 1 else o_parts[u_][0] for u_ in units}
    else:
        ws, qs = {}, {}
        for u_ in units:
            h = u_[1]
            parts = [_mm(jnp.concatenate([w[u_][sq * lt:(sq + 1) * lt], qe[u_][sq * lt:(sq + 1) * lt]], axis=0),
                         s_scr[sq, h]) for sq in range(nsb)]
            ws[u_] = jnp.concatenate([p[:lt] for p in parts], axis=0)
            qs[u_] = jnp.concatenate([p[lt:] for p in parts], axis=0)
        vnew = {u_: u[u_] - ws[u_] for u_ in units}
        o = {u_: qs[u_] + _mm(qk[u_], vnew[u_]) for u_ in units}
        for u_ in units:
            h = u_[1]
            for sq in range(nsb):
                ksm = jnp.where(rowseq == sq, ks[u_], 0.0)
                s_scr[sq, h] = s_scr[sq, h] * jnp.exp(glc[u_][sq * lt:sq * lt + 1, :]) + _mm_tn(ksm, vnew[u_])
    for s, h in units:
        ou = o[(s, h)]
        oh = ou * lax.rsqrt(jnp.mean(ou * ou, axis=-1, keepdims=True) + EPS) * onw
        zh = z_ref[s, :, h * hd:(h + 1) * hd]
        y_ref[s, :, h * hd:(h + 1) * hd] = oh * (zh * _sigmoid(zh))

    @pl.when(c == nc - 1)
    def _state_out():
        snew_ref[...] = s_scr[...]


def _delta_apply(proj, ba, qkv_col, z_col, nseq, seqlen, cstate, s0, conv_w, a_log, dt_bias, onorm_w, cast=()):
    nheads = s0.shape[1]
    width = nheads * DN_HEAD_DIM
    tail = CONV_WIDTH - 1
    t, ncols = proj.shape
    chained = seqlen >= DN_CHUNK
    if chained:
        lt, nseg = DN_CHUNK, DN_CHUNKS_PER_STEP
        nstr = DN_STREAMS if nseq % DN_STREAMS == 0 else 1
        nlead = nstr
        rb = nseg * lt
        assert seqlen % rb == 0
        nc = seqlen // rb
        grid = (nseq // nstr, nc)
        tok3 = lambda x: x.reshape(nseq, seqlen, x.shape[-1])
        tok_idx = lambda b, c: (b, c)
    else:
        assert seqlen == SUBLANE
        lt, nseg, nstr = seqlen, DN_STEP_ROWS // seqlen, 1
        nlead = nseg
        rb = nseg * lt
        assert nseq % nseg == 0
        nc = 1
        grid = (nseq // nseg, 1)
        tok3 = lambda x: x.reshape(1, t, x.shape[-1])
        tok_idx = lambda b, c: (0, b)
    cs8 = jnp.pad(cstate, ((0, 0), (SUBLANE - tail, 0), (0, 0)))
    gpar = jnp.zeros((SUBLANE, LANE), F32)
    gpar = gpar.at[0, nheads:2 * nheads].set(-jnp.exp(a_log))
    gpar = gpar.at[1, nheads:2 * nheads].set(dt_bias)
    proj3 = tok3(proj)
    tok = lambda cols, colblk: pl.BlockSpec((nstr, rb, cols), lambda b, c: tok_idx(b, c) + (colblk,))
    tok_next = pl.BlockSpec((nstr, rb, 3 * width),
                            lambda b, c: tok_idx(b, jnp.minimum(c + 1, nc - 1)) + (qkv_col // (3 * width),))
    lead = lambda shape: pl.BlockSpec((nlead,) + shape, lambda b, c: (b,) + (0,) * len(shape))
    const = lambda shape: pl.BlockSpec(shape, lambda b, c: (0,) * len(shape))
    nslab = 3 * width // LANE
    nsteps = grid[0] * grid[1]
    step = lambda b, c: b * grid[1] + c
    cast_in_specs, cast_out_specs, cast_shapes = [], [], []
    for w, layer in cast:
        rows, cols = w.shape[1:]
        assert rows % (nsteps * 2 * SUBLANE) == 0
        blk = rows // nsteps
        cast_in_specs.append(pl.BlockSpec((None, blk, cols), lambda b, c, layer=layer: (layer, step(b, c), 0)))
        cast_out_specs.append(pl.BlockSpec((blk, cols), lambda b, c: (step(b, c), 0)))
        cast_shapes.append(jax.ShapeDtypeStruct((rows, cols), BF16))
    outs = pl.pallas_call(
        functools.partial(_delta_kernel, nstr=nstr, nseg=nseg, lt=lt, chained=chained, nheads=nheads,
                          ncast=len(cast)),
        grid=grid,
        in_specs=[
            pl.BlockSpec((nstr, rb, 3 * width), lambda b, c: tok_idx(b, 0) + (qkv_col // (3 * width),)),
            tok_next,
            tok(width, z_col // width),
            tok(LANE, 0),
            lead((SUBLANE, 3 * width)),
            lead((nheads, DN_HEAD_DIM, DN_HEAD_DIM)),
            const((CONV_WIDTH, 3 * width)),
            const((SUBLANE, LANE)),
            const((1, DN_HEAD_DIM)),
        ] + cast_in_specs,
        out_specs=[
            tok(width, 0),
            lead((tail, 3 * width)),
            lead((nheads, DN_HEAD_DIM, DN_HEAD_DIM)),
        ] + cast_out_specs,
        out_shape=[
            jax.ShapeDtypeStruct(proj3.shape[:2] + (width,), F32),
            jax.ShapeDtypeStruct((nseq, tail, 3 * width), F32),
            jax.ShapeDtypeStruct(s0.shape, F32),
        ] + cast_shapes,
        scratch_shapes=[
            pltpu.VMEM((nstr * nslab, rb + 2 * SUBLANE, LANE), F32),
            pltpu.VMEM((nslab, rb + SUBLANE, LANE), F32),
            pltpu.VMEM((nlead, nheads, DN_HEAD_DIM, DN_HEAD_DIM), F32),
            pltpu.VMEM((2, nstr, rb, 3 * width), F32),
        ],
        compiler_params=_cparams("parallel", "arbitrary"),
        name="delta_chunk" if chained else "delta_step",
    )(proj3, proj3, proj3, tok3(ba), cs8, s0, conv_w, gpar, onorm_w.reshape(1, DN_HEAD_DIM),
      *[w for w, _ in cast])
    y, cnew, snew = outs[:3]
    return (y.reshape(t, width), cnew, snew) + tuple(outs[3:])


def _merge_kernel(y5_ref, ydn_ref, ga_ref, gb_ref, x_ref, wglu_ref, bglu_ref, wa_ref, wb_ref, wout_ref, gffn_ref,
                  x1_ref, h_ref):
    y = y5_ref[...]
    y = 0.5 * y * (1.0 + jnp.tanh(math.sqrt(2.0 / math.pi) * (y + 0.044715 * (y * y * y))))
    lin = jnp.dot(y.astype(BF16), wglu_ref[...], preferred_element_type=F32) + bglu_ref[...]
    glu = (y * _sigmoid(lin)).astype(BF16)
    a = jnp.dot(glu, wa_ref[...], preferred_element_type=F32)
    b = jnp.dot(ydn_ref[...].astype(BF16), wb_ref[...], preferred_element_type=F32)
    mix = (_sigmoid(ga_ref[...]) * a + _sigmoid(gb_ref[...]) * b).astype(BF16)
    x1 = x_ref[...] + jnp.dot(mix, wout_ref[...], preferred_element_type=F32)
    x1_ref[...] = x1
    h_ref[...] = _rms(x1, gffn_ref[...]).astype(BF16)


def _merge(y5, ydn, proj, x, ga_col, gb_col, w_glu, b_glu, w_a, w_b, w_out, g_ffn, tm):
    t, w5 = y5.shape
    d = x.shape[1]
    row = lambda cols: pl.BlockSpec((tm, cols), lambda i: (i, 0))
    const = lambda shape: pl.BlockSpec(shape, lambda i: (0, 0), pipeline_mode=pl.Buffered(1))
    return pl.pallas_call(
        _merge_kernel,
        grid=(t // tm,),
        in_specs=[
            row(w5), row(w5),
            pl.BlockSpec((tm, d), lambda i: (i, ga_col // d)),
            pl.BlockSpec((tm, d), lambda i: (i, gb_col // d)),
            row(d),
            const((w5, w5)), const((1, w5)), const((w5, d)), const((w5, d)), const((d, d)), const((1, d)),
        ],
        out_specs=[row(d), row(d)],
        out_shape=[jax.ShapeDtypeStruct((t, d), F32), jax.ShapeDtypeStruct((t, d), BF16)],
        compiler_params=_cparams("parallel"),
        name="merge",
    )(y5, ydn, proj, proj, x, w_glu, b_glu, w_a, w_b, w_out, g_ffn)


def _ffn_kernel(h_ref, xn_ref, wg_ref, wu_ref, wd_ref, out_ref, act_scr, *, nk, tk):
    s = pl.program_id(1)

    @pl.when(s < nk)
    def _():
        h = h_ref[...]
        gate = jnp.dot(h, wg_ref[...], preferred_element_type=F32)
        up = jnp.dot(h, wu_ref[...], preferred_element_type=F32)
        act_scr[s] = (gate * _sigmoid(gate) * up).astype(BF16)

    @pl.when(s >= nk)
    def _():
        acc = xn_ref[...]
        for kk in range(nk):
            acc = acc + jnp.dot(act_scr[kk], wd_ref[kk * tk:(kk + 1) * tk, :], preferred_element_type=F32)
        out_ref[...] = acc


def _ffn(x, h, w_gate, w_up, w_down, tm, tk, tn):
    t, d = x.shape
    dff = w_gate.shape[1]
    assert t % tm == 0 and dff % tk == 0 and d % tn == 0
    nk = dff // tk
    up_blk = lambda i, s: (0, jnp.minimum(s, nk - 1))
    down_blk = lambda i, s: (0, jnp.maximum(s - nk, 0))
    out_blk = lambda i, s: (i, jnp.maximum(s - nk, 0))
    return pl.pallas_call(
        functools.partial(_ffn_kernel, nk=nk, tk=tk),
        grid=(t // tm, nk + d // tn),
        in_specs=[
            pl.BlockSpec((tm, d), lambda i, s: (i, 0)),
            pl.BlockSpec((tm, tn), out_blk),
            pl.BlockSpec((d, tk), up_blk),
            pl.BlockSpec((d, tk), up_blk),
            pl.BlockSpec((dff, tn), down_blk),
        ],
        out_specs=pl.BlockSpec((tm, tn), out_blk),
        out_shape=jax.ShapeDtypeStruct((t, d), F32),
        scratch_shapes=[pltpu.VMEM((nk, tm, tk), BF16)],
        compiler_params=_cparams("parallel", "arbitrary"),
        name="ffn",
    )(h, x, w_gate, w_up, w_down)


def _ple_kernel(x_ref, p_ref, gple_ref, gfin_ref, wple_ref, wpg_ref, out_ref, h_scr, *, nj, tn, final):
    j = pl.program_id(1)

    @pl.when(j == 0)
    def _():
        h_scr[...] = _rms(x_ref[...], gple_ref[...]).astype(BF16)

    gate = _sigmoid(jnp.dot(h_scr[...], wpg_ref[...], preferred_element_type=F32))
    emb = jnp.dot(p_ref[...].astype(BF16), wple_ref[...], preferred_element_type=F32)
    upd = emb * gate
    for jj in range(nj):
        @pl.when(j == jj)
        def _(jj=jj):
            out_ref[:, jj * tn:(jj + 1) * tn] = x_ref[:, jj * tn:(jj + 1) * tn] + upd

    if final:
        @pl.when(j == nj - 1)
        def _():
            out_ref[...] = _rms(out_ref[...], gfin_ref[...])


def _ple_final(x, p, g_ple, g_final, w_ple, w_ple_gate, tm, tn, final):
    t, d = x.shape
    pd = p.shape[1]
    nj = d // tn
    wmode = dict(pipeline_mode=pl.Buffered(1)) if nj == 1 else {}
    return pl.pallas_call(
        functools.partial(_ple_kernel, nj=nj, tn=tn, final=final),
        grid=(t // tm, nj),
        in_specs=[
            pl.BlockSpec((tm, d), lambda i, j: (i, 0)),
            pl.BlockSpec((tm, pd), lambda i, j: (i, 0)),
            pl.BlockSpec((1, d), lambda i, j: (0, 0)),
            pl.BlockSpec((1, d), lambda i, j: (0, 0)),
            pl.BlockSpec((pd, tn), lambda i, j: (0, j), **wmode),
            pl.BlockSpec((d, tn), lambda i, j: (0, j), **wmode),
        ],
        out_specs=pl.BlockSpec((tm, d), lambda i, j: (i, 0)),
        out_shape=jax.ShapeDtypeStruct((t, d), F32),
        scratch_shapes=[pltpu.VMEM((tm, d), BF16)],
        compiler_params=_cparams("parallel", "arbitrary"),
        name="ple_final",
    )(x, p, g_ple, g_final, w_ple, w_ple_gate)


COL_Z, COL_GA, COL_GB, COL_QKV = 1024, 2048, 4096, 6144


def _prep_w_in(w_in, layer, d_model, nheads):
    w_t = jnp.swapaxes(w_in, 1, 2)
    s5w = d_model // 2
    dnw = d_model // 2
    off_u = s5w
    off_qkv = off_u + 3 * dnw
    off_z = off_qkv + dnw
    off_a = off_z + 2 * nheads
    tb = W_IN_BLOCK
    assert off_u % tb == 0 and off_qkv % tb == 0 and off_z % tb == 0 and d_model % tb == 0
    shift = off_a - off_z
    assert shift % SUBLANE == 0 and shift <= LANE
    src = ([0] + [off_qkv // tb + i for i in range(dnw // tb)] + [off_z // tb + i for i in range(2 * d_model // tb)]
           + [off_u // tb + i for i in range(3 * dnw // tb)])
    n_al = 1 + dnw // tb
    n_sh = 2 * d_model // tb
    src_tab = jnp.asarray(src, jnp.int32)
    hi_tab = jnp.asarray([(src[min(max(j, n_al), n_al + n_sh - 1)] + 1) * (tb // LANE) for j in range(len(src))],
                         jnp.int32)
    d_in = w_in.shape[1]
    return pl.pallas_call(
        functools.partial(_w_in_kernel, n_al=n_al, n_sh=n_sh, shift=shift),
        grid_spec=pltpu.PrefetchScalarGridSpec(
            num_scalar_prefetch=2,
            grid=(len(src),),
            in_specs=[pl.BlockSpec((None, tb, d_in), lambda j, lo, hi: (layer, lo[j], 0)),
                      pl.BlockSpec((None, LANE, d_in), lambda j, lo, hi: (layer, hi[j], 0))],
            out_specs=[pl.BlockSpec((tb, d_in), lambda j, lo, hi: (j, 0)),
                       pl.BlockSpec((LANE, d_in), lambda j, lo, hi: (0, 0))],
        ),
        out_shape=[jax.ShapeDtypeStruct((len(src) * tb, d_in), BF16), jax.ShapeDtypeStruct((LANE, d_in), BF16)],
        compiler_params=_cparams("arbitrary"),
        name="w_in_cast",
    )(src_tab, hi_tab, w_t, w_t)


W_IN_BLOCK = 1024


def _w_in_kernel(lo_tab, hi_tab, lo_ref, hi_ref, out_ref, ba_ref, *, n_al, n_sh, shift):
    del lo_tab, hi_tab
    j = pl.program_id(0)
    tb, d_in = out_ref.shape
    stitched = (j >= n_al) & (j < n_al + n_sh)

    @pl.when(j == n_al)
    def _():
        ba_ref[...] = jnp.concatenate([lo_ref[0:shift, :], jnp.zeros((LANE - shift, d_in), F32)],
                                      axis=0).astype(BF16)

    @pl.when(stitched)
    def _():
        out_ref[...] = jnp.concatenate([lo_ref[shift:, :], hi_ref[0:shift, :]], axis=0).astype(BF16)

    @pl.when(jnp.logical_not(stitched))
    def _():
        out_ref[...] = lo_ref[...].astype(BF16)


TOKEN_BLOCK = 1024
MERGE_ROWS = 256
INPROJ_COLS = 1536
FFN_HIDDEN_BLOCK = 512
FFN_OUT_BLOCK = 512
PLE_COLS = 2048


def _layer(x3, p3, cstate, s0, h0, lw, final):
    nseq, seqlen, d = x3.shape
    t = nseq * seqlen
    tm = min(TOKEN_BLOCK, t)
    x = x3.reshape(t, d)
    proj, ba = _inproj(x, lw['g_mix'], lw['w_main'], lw['w_ba'], tm, INPROJ_COLS)
    y5, hre, him = _s5_apply(proj, nseq, seqlen, lw['s5_ops'], lw['s5_d'], h0)
    pending = lw.pop('cast_f32', {})
    ydn, cnew, snew, *as_bf16 = _delta_apply(proj, ba, COL_QKV, COL_Z, nseq, seqlen, cstate, s0, lw['conv_w'],
                                             lw['a_log'], lw['dt_bias'], lw['onorm_w'], tuple(pending.values()))
    lw.update(zip(pending.keys(), as_bf16))
    x1, h2 = _merge(y5, ydn, proj, x, COL_GA, COL_GB, lw['w_glu'], lw['b_glu'], lw['w_a'], lw['w_b'],
                    lw['w_out'], lw['g_ffn'], min(MERGE_ROWS, tm))
    x2 = _ffn(x1, h2, lw['w_gate'], lw['w_up'], lw['w_down'], tm, FFN_HIDDEN_BLOCK, FFN_OUT_BLOCK)
    y = _ple_final(x2, p3.reshape(t, -1), lw['g_ple'], lw['g_final'], lw['w_ple'], lw['w_ple_gate'],
                   tm, PLE_COLS, final)
    ng = hre.shape[-1] // S5_STATE
    return (y.reshape(nseq, seqlen, d), cnew, snew,
            hre.reshape(nseq, ng, S5_STATE), him.reshape(nseq, ng, S5_STATE))


def kernel(x_prompt, x_sample, state_conv, state_delta, state_s5_re, state_s5_im, p_prompt, p_sample, g_mix, w_in, conv_w, a_log, dt_bias, onorm_w, s5_a_re, s5_a_im, s5_b_re, s5_b_im, s5_c_re, s5_c_im, s5_d, s5_log_dt, w_glu, b_glu, w_a, w_b, w_out, g_ffn, w_gate, w_up, w_down, g_ple, w_ple, w_ple_gate, g_final):
    depth = w_in.shape[0]
    d_model = x_prompt.shape[-1]
    nheads = state_delta.shape[2]
    nb_p = x_prompt.shape[0]
    f32z = functools.partial(jnp.zeros, dtype=F32)
    yp, ys = x_prompt, x_sample
    outs_p, outs_s = [], []
    for i in range(depth):
        w_main, w_ba = _prep_w_in(w_in, i, d_model, nheads)
        lw = dict(
            g_mix=g_mix[i][None], w_main=w_main, w_ba=w_ba,
            conv_w=conv_w[i], a_log=a_log[i], dt_bias=dt_bias[i], onorm_w=onorm_w[i],
            s5_ops=_s5_operators(s5_a_re[i], s5_a_im[i], s5_b_re[i], s5_b_im[i], s5_c_re[i], s5_c_im[i],
                                 s5_log_dt[i]),
            s5_d=s5_d[i],
            b_glu=b_glu[i][None],
            w_out=w_out[i].astype(BF16), g_ffn=g_ffn[i][None],
            cast_f32=dict(w_gate=(w_gate, i), w_up=(w_up, i), w_down=(w_down, i),
                          w_glu=(w_glu, i), w_a=(w_a, i), w_b=(w_b, i)),
            g_ple=g_ple[i][None], w_ple=w_ple[i].astype(BF16), w_ple_gate=w_ple_gate[i].astype(BF16),
            g_final=g_final[None],
        )
        final = i == depth - 1
        yp, c1, d1, r1, m1 = _layer(yp, p_prompt[i], f32z((nb_p,) + state_conv.shape[2:]),
                                    f32z((nb_p,) + state_delta.shape[2:]), None, lw, final)
        ys, c2, d2, r2, m2 = _layer(ys, p_sample[i], state_conv[i], state_delta[i],
                                    (state_s5_re[i], state_s5_im[i]), lw, final)
        outs_p.append((c1, d1, r1, m1))
        outs_s.append((c2, d2, r2, m2))
    stack = lambda outs, k: jnp.stack([o[k] for o in outs])
    return (yp, ys,
            stack(outs_p, 0), stack(outs_p, 1), stack(outs_p, 2), stack(outs_p, 3),
            stack(outs_s, 0), stack(outs_s, 1), stack(outs_s, 2), stack(outs_s, 3))
```
